```python
import jax, jax.numpy as jnp
from jax import lax
import numpy as np

D_MODEL = 1024
BATCH = 8
SEQ = 2048
DEPTH = 1

D_CONV = 1024
CONV_W = 3
N_HEADS = 16
HEAD_DIM = 64
N_KV = 4
GROUP = N_HEADS // N_KV
D_ATT = N_HEADS * HEAD_DIM
D_KV = N_KV * HEAD_DIM
CMP_LEN = 32
CMP_STRIDE = 16
CMP_HID = 128
SLC_LEN = 64
N_SEL = 8
WINDOW = 512
Q_BLOCK = 64
N_MIXERS = 2
NORM_EPS = 1e-6

SPLITS = (D_CONV, D_CONV, D_CONV, D_CONV,
          D_ATT,
          D_KV, D_KV, D_KV, D_KV, D_KV, D_KV,
          3 * N_HEADS,
          D_ATT,
          N_MIXERS * D_MODEL)
N_IN = 4 * D_CONV + 2 * D_ATT + 6 * D_KV + 3 * N_HEADS + N_MIXERS * D_MODEL

kernel_name = "hybrid_shortconv_nsa_gated_merge"


def rmsnorm(x, w):
    xf = x.astype(jnp.float32)
    r = lax.rsqrt(jnp.mean(xf * xf, axis=-1, keepdims=True) + NORM_EPS)
    return (xf * r).astype(x.dtype) * w


def alibi_slopes():
    return jnp.exp2(-8.0 * jnp.arange(1, N_HEADS + 1, dtype=jnp.float32) / N_HEADS)


def masked_softmax(s, mask):
    s = jnp.where(mask, s.astype(jnp.float32), -jnp.inf)
    m = jnp.max(s, axis=-1, keepdims=True)
    m = jnp.where(jnp.isfinite(m), m, 0.0)
    e = jnp.where(mask, jnp.exp(s - m), 0.0)
    return e / jnp.maximum(jnp.sum(e, axis=-1, keepdims=True), 1e-30)


def short_conv_mixer(h, b_gate, c_gate, z, conv_w, conv_b):
    u = c_gate * h
    T = u.shape[1]
    up = jnp.pad(u, ((0, 0), (CONV_W - 1, 0), (0, 0)))
    y = conv_b + conv_w[0] * up[:, 0:T] + conv_w[1] * up[:, 1:T + 1] + conv_w[2] * up[:, 2:T + 2]
    return b_gate * y * jax.nn.silu(z)


def compress_blocks(k, pe, w1, w2):
    B, G, T, dh = k.shape
    kd = k.reshape(B, G, T // CMP_STRIDE, CMP_STRIDE, dh)
    blocks = jnp.concatenate([kd[:, :, :-1], kd[:, :, 1:]], axis=3)
    blocks = (blocks + pe).reshape(B, G, blocks.shape[2], CMP_LEN * dh)
    return jax.nn.silu(blocks @ w1) @ w2


def nsa_attention(q, k_cmp, v_cmp, k_slc, v_slc, k_win, v_win, gates,
                  pe_k, pe_v, w1_k, w2_k, w1_v, w2_v):
    B, T = q.shape[0], q.shape[1]
    f32 = jnp.float32
    q = (q * HEAD_DIM ** -0.5).reshape(B, T, N_KV, GROUP, HEAD_DIM).transpose(0, 2, 3, 1, 4)
    tr = lambda a: a.transpose(0, 2, 1, 3)
    slopes = alibi_slopes().reshape(N_KV, GROUP)[None, :, :, None, None]
    t_pos = jnp.arange(T)

    kc = compress_blocks(tr(k_cmp), pe_k, w1_k, w2_k)
    vc = compress_blocks(tr(v_cmp), pe_v, w1_v, w2_v)
    n_cmp = kc.shape[2]
    c_start = jnp.arange(n_cmp) * CMP_STRIDE
    c_end = c_start + CMP_LEN - 1
    dist_c = t_pos[:, None] - c_end[None, :]
    s_c = jnp.einsum('bgrtd,bgcd->bgrtc', q, kc) - slopes * dist_c.astype(f32)
    p_cmp = masked_softmax(s_c, dist_c >= 0)
    o_cmp = jnp.einsum('bgrtc,bgcd->bgrtd', p_cmp.astype(vc.dtype), vc)

    n_slc = T // SLC_LEN
    s_start = jnp.arange(n_slc) * SLC_LEN
    overlap = ((c_start[:, None] <= s_start[None, :] + SLC_LEN - 1)
               & (c_end[:, None] >= s_start[None, :])).astype(f32)
    imp = jnp.einsum('bgrtc,cj->bgtj', p_cmp, overlap)
    cur = (t_pos // SLC_LEN)[:, None]
    j = jnp.arange(n_slc)[None, :]
    forced = (j == 0) | (j == cur) | (j == cur - 1)
    imp = jnp.where(forced, jnp.inf, jnp.where(j <= cur, imp, -jnp.inf))
    n_top = min(N_SEL, n_slc)
    top_val, top_idx = lax.top_k(imp, n_top)
    top_ok = top_val > -jnp.inf

    ks_blocks = tr(k_slc).reshape(B, N_KV, n_slc, SLC_LEN, HEAD_DIM)
    vs_blocks = tr(v_slc).reshape(B, N_KV, n_slc, SLC_LEN, HEAD_DIM)
    kw = jnp.pad(tr(k_win), ((0, 0), (0, 0), (WINDOW, 0), (0, 0)))
    vw = jnp.pad(tr(v_win), ((0, 0), (0, 0), (WINDOW, 0), (0, 0)))
    gather = jax.vmap(jax.vmap(lambda blk, ix: blk[ix]))
    n_sel_keys = n_top * SLC_LEN

    def block_fn(i):
        t0 = i * Q_BLOCK
        qb = lax.dynamic_slice_in_dim(q, t0, Q_BLOCK, axis=3)
        tq = t0 + jnp.arange(Q_BLOCK)
        ix = lax.dynamic_slice_in_dim(top_idx, t0, Q_BLOCK, axis=2)
        ok = lax.dynamic_slice_in_dim(top_ok, t0, Q_BLOCK, axis=2)
        ks = gather(ks_blocks, ix).reshape(B, N_KV, Q_BLOCK, n_sel_keys, HEAD_DIM)
        vs = gather(vs_blocks, ix).reshape(B, N_KV, Q_BLOCK, n_sel_keys, HEAD_DIM)
        s_pos = (ix[..., None] * SLC_LEN + jnp.arange(SLC_LEN)).reshape(B, N_KV, Q_BLOCK, n_sel_keys)
        s_ok = jnp.repeat(ok, SLC_LEN, axis=-1)
        d_s = tq[:, None] - s_pos
        s_s = jnp.einsum('bgrqd,bgqkd->bgrqk', qb, ks) - slopes * d_s[:, :, None].astype(f32)
        p_s = masked_softmax(s_s, (s_ok & (d_s >= 0))[:, :, None])
        o_s = jnp.einsum('bgrqk,bgqkd->bgrqd', p_s.astype(vs.dtype), vs)
        kwb = lax.dynamic_slice_in_dim(kw, t0, Q_BLOCK + WINDOW, axis=2)
        vwb = lax.dynamic_slice_in_dim(vw, t0, Q_BLOCK + WINDOW, axis=2)
        w_pos = t0 - WINDOW + jnp.arange(Q_BLOCK + WINDOW)
        d_w = tq[:, None] - w_pos[None, :]
        s_w = jnp.einsum('bgrqd,bgkd->bgrqk', qb, kwb) - slopes * d_w.astype(f32)
        p_w = masked_softmax(s_w, (d_w >= 0) & (d_w < WINDOW) & (w_pos[None, :] >= 0))
        o_w = jnp.einsum('bgrqk,bgkd->bgrqd', p_w.astype(vwb.dtype), vwb)
        return o_s, o_w

    o_slc, o_win = lax.map(block_fn, jnp.arange(T // Q_BLOCK))
    unblock = lambda o: o.transpose(1, 2, 3, 0, 4, 5).reshape(B, N_KV, GROUP, T, HEAD_DIM)
    o = jnp.stack([o_cmp, unblock(o_slc), unblock(o_win)], axis=-1)
    o = o.transpose(0, 3, 1, 2, 4, 5).reshape(B, T, N_HEADS, HEAD_DIM, 3)
    return jnp.einsum('bthdc,bthc->bthd', o, gates).reshape(B, T, D_ATT)


def setup_inputs(seed: int = 0) -> dict:
    key = jax.random.key(seed)
    ks = jax.random.split(key, 16)
    nrm = lambda k, shape, scale: jax.random.normal(k, shape, jnp.float32) * scale
    return {
        "x": nrm(ks[0], (BATCH, SEQ, D_MODEL), 1.0),
        "norm_w": 1.0 + nrm(ks[1], (DEPTH, D_MODEL), 0.02),
        "w_in": nrm(ks[2], (DEPTH, D_MODEL, N_IN), D_MODEL ** -0.5),
        "conv_w": nrm(ks[3], (DEPTH, CONV_W, D_CONV), CONV_W ** -0.5),
        "conv_b": nrm(ks[4], (DEPTH, D_CONV), 0.02),
        "cmp_pe_k": nrm(ks[5], (DEPTH, CMP_LEN, HEAD_DIM), 0.1),
        "cmp_pe_v": nrm(ks[6], (DEPTH, CMP_LEN, HEAD_DIM), 0.1),
        "cmp_w1_k": nrm(ks[7], (DEPTH, CMP_LEN * HEAD_DIM, CMP_HID), (CMP_LEN * HEAD_DIM) ** -0.5),
        "cmp_w2_k": nrm(ks[8], (DEPTH, CMP_HID, HEAD_DIM), CMP_HID ** -0.5),
        "cmp_w1_v": nrm(ks[9], (DEPTH, CMP_LEN * HEAD_DIM, CMP_HID), (CMP_LEN * HEAD_DIM) ** -0.5),
        "cmp_w2_v": nrm(ks[10], (DEPTH, CMP_HID, HEAD_DIM), CMP_HID ** -0.5),
        "w_proj_a": nrm(ks[11], (DEPTH, D_CONV, D_MODEL), D_CONV ** -0.5),
        "w_proj_b": nrm(ks[12], (DEPTH, D_ATT, D_MODEL), D_ATT ** -0.5),
        "w_out": nrm(ks[13], (DEPTH, D_MODEL, D_MODEL), D_MODEL ** -0.5),
        "final_norm_w": 1.0 + nrm(ks[14], (D_MODEL,), 0.02),
    }


def reference(x, norm_w, w_in, conv_w, conv_b, cmp_pe_k, cmp_pe_v, cmp_w1_k, cmp_w2_k,
              cmp_w1_v, cmp_w2_v, w_proj_a, w_proj_b, w_out, final_norm_w):
    B, T, _ = x.shape
    offsets = np.cumsum(SPLITS)[:-1].tolist()
    for l in range(DEPTH):
        h = rmsnorm(x, norm_w[l])
        proj = h @ w_in[l]
        (h_a, b_a, c_a, z_a, q, k_c, v_c, k_s, v_s, k_w, v_w,
         g_nsa, z_b, g_mix) = jnp.split(proj, offsets, axis=-1)
        y_a = short_conv_mixer(h_a, b_a, c_a, z_a, conv_w[l], conv_b[l]) @ w_proj_a[l]
        kv = lambda a: a.reshape(B, T, N_KV, HEAD_DIM)
        o_b = nsa_attention(q.reshape(B, T, N_HEADS, HEAD_DIM), kv(k_c), kv(v_c), kv(k_s), kv(v_s),
                            kv(k_w), kv(v_w), jax.nn.sigmoid(g_nsa).reshape(B, T, N_HEADS, 3),
                            cmp_pe_k[l], cmp_pe_v[l], cmp_w1_k[l], cmp_w2_k[l], cmp_w1_v[l], cmp_w2_v[l])
        y_b = (o_b * jax.nn.silu(z_b)) @ w_proj_b[l]
        g = jax.nn.sigmoid(g_mix).reshape(B, T, N_MIXERS, D_MODEL)
        mixed = g[:, :, 0] * y_a + g[:, :, 1] * y_b
        x = x + mixed @ w_out[l]
    return rmsnorm(x, final_norm_w)
```

```python
import functools

import numpy as np
import jax
import jax.numpy as jnp
from jax import lax
from jax.experimental import pallas as pl
from jax.experimental.pallas import tpu as pltpu

D_MODEL = 1024
D_CONV = 1024
CONV_W = 3
N_HEADS = 16
HEAD_DIM = 64
N_KV = 4
GROUP = N_HEADS // N_KV
D_ATT = N_HEADS * HEAD_DIM
D_KV = N_KV * HEAD_DIM
CMP_LEN = 32
CMP_STRIDE = 16
CMP_HID = 128
SLC_LEN = 64
N_SEL = 8
WINDOW = 512
Q_BLOCK = 64
NORM_EPS = 1e-6

LANES = 128
N_CMP_PAD = 128
MASK_BIG = float(2.0 ** 100)
WIN_BLOCKS = WINDOW // SLC_LEN + 2
WIN_KEYS = WIN_BLOCKS * SLC_LEN
WIN_PAD = WIN_KEYS - Q_BLOCK
SEL_CHUNK = 256
CMP_TQ = 128

OFF_CONV = 0
OFF_Q = 4096
OFF_ZB = 5120
OFF_GMIX = 6144
OFF_KVC = 8192
OFF_KSW = 8704
OFF_VSW = 9216
OFF_GATE = 9728
N_PAD = 10240

L_HI = 64
L_LO = 67
L_FLAG = 70
L_SEL = 96

VMEM_LIMIT = 48 * 1024 * 1024

f32 = jnp.float32
bf16 = jnp.bfloat16


def _column_permutation():
    idx = np.full((N_PAD,), -1, np.int64)
    idx[OFF_CONV:OFF_CONV + 4 * D_CONV] = np.arange(4 * D_CONV)
    o_q = 4 * D_CONV
    idx[OFF_Q:OFF_Q + D_ATT] = o_q + np.arange(D_ATT)
    o_kc = o_q + D_ATT
    o_vc, o_ks, o_vs, o_kw, o_vw = (o_kc + D_KV * n for n in range(1, 6))
    o_gate = o_kc + 6 * D_KV
    o_zb = o_gate + 3 * N_HEADS
    o_gmix = o_zb + D_ATT
    idx[OFF_ZB:OFF_ZB + D_ATT] = o_zb + np.arange(D_ATT)
    idx[OFF_GMIX:OFF_GMIX + 2 * D_MODEL] = o_gmix + np.arange(2 * D_MODEL)
    d = np.arange(HEAD_DIM)
    for g in range(N_KV):
        for off, lo, hi in ((OFF_KVC, o_kc, o_vc), (OFF_KSW, o_ks, o_kw), (OFF_VSW, o_vs, o_vw)):
            idx[off + g * LANES + d] = lo + HEAD_DIM * g + d
            idx[off + g * LANES + HEAD_DIM + d] = hi + HEAD_DIM * g + d
        idx[OFF_GATE + g * LANES + np.arange(3 * GROUP)] = o_gate + 3 * GROUP * g + np.arange(3 * GROUP)
    return idx


def _sigmoid(x):
    return 1.0 / (1.0 + jnp.exp(-x))


def _dot_nt(a, b):
    return lax.dot_general(a, b, (((1,), (1,)), ((), ())), preferred_element_type=f32)


def _inproj_kernel(x_ref, nw_ref, w_ref, o_ref, h_ref):
    @pl.when(pl.program_id(1) == 0)
    def _():
        x = x_ref[...]
        r = lax.rsqrt(jnp.mean(x * x, axis=-1, keepdims=True) + NORM_EPS)
        h_ref[...] = ((x * r) * nw_ref[...]).astype(bf16)

    o_ref[...] = jnp.dot(h_ref[...], w_ref[...], preferred_element_type=f32).astype(bf16)


def _inproj(x2, norm_w, w_p, tm=1024, tn=1024):
    n_rows = x2.shape[0]
    return pl.pallas_call(
        _inproj_kernel,
        grid=(n_rows // tm, N_PAD // tn),
        in_specs=[
            pl.BlockSpec((tm, D_MODEL), lambda i, j: (i, 0)),
            pl.BlockSpec((1, D_MODEL), lambda i, j: (0, 0)),
            pl.BlockSpec((D_MODEL, tn), lambda i, j: (0, j)),
        ],
        out_specs=pl.BlockSpec((tm, tn), lambda i, j: (i, j)),
        out_shape=jax.ShapeDtypeStruct((n_rows, N_PAD), bf16),
        scratch_shapes=[pltpu.VMEM((tm, D_MODEL), bf16)],
        compiler_params=pltpu.CompilerParams(
            dimension_semantics=("parallel", "arbitrary"), vmem_limit_bytes=VMEM_LIMIT),
        name="inproj",
    )(x2, norm_w, w_p)


def _softmax_step(s, v, carry):
    m, l, acc = carry
    m_new = jnp.maximum(m, jnp.max(s, axis=-1, keepdims=True))
    alpha = jnp.exp(m - m_new)
    p = jnp.exp(s - m_new)
    l = alpha * l + jnp.sum(p, axis=-1, keepdims=True)
    acc = alpha * acc + jnp.dot(p.astype(bf16), v, preferred_element_type=f32)
    return m_new, l, acc


def _stack_heads(q_tile, aug_rows, extra):
    lane = lax.broadcasted_iota(jnp.int32, (1, LANES), 1)
    pairs = (q_tile[:, :LANES], q_tile[:, LANES:])
    parts = []
    for r in range(GROUP):
        qh = pairs[r // 2]
        if r % 2 == 1:
            qh = pltpu.roll(qh, HEAD_DIM, 1)
        aug = aug_rows[r:r + 1, :]
        if extra is not None:
            aug = aug + extra
        parts.append(jnp.where(lane < HEAD_DIM, qh, aug).astype(bf16))
    return jnp.concatenate(parts, axis=0)


def _merge_heads(vals, lane):
    return (jnp.where(lane < HEAD_DIM, vals[0], vals[1]), jnp.where(lane < HEAD_DIM, vals[2], vals[3]))


def _nsa_kernel(q_ref, ksw_ref, vsw_ref, gate_ref, kvc_ref, wlo_ref, whi_ref, pelo_ref, pehi_ref, w2_ref,
                kaug_s_ref, kaug_w_ref, kaug_c_ref, qaug_ref, tri_ref, band_ref, ovl_ref,
                o_ref,
                kcvc_ref, ks_ref, kw_ref, vw_ref, ocmp_ref, imp_ref, selq_ref):
    seq = q_ref.shape[0]
    g = pl.program_id(1)
    lane = lax.broadcasted_iota(jnp.int32, (1, LANES), 1)
    low = lane < HEAD_DIM

    @pl.when(g == 0)
    def _():
        for gg in range(N_KV):
            xcat = jnp.concatenate(
                [kvc_ref[0, :, l * (N_KV * LANES) + gg * LANES: l * (N_KV * LANES) + (gg + 1) * LANES]
                 for l in range(CMP_STRIDE)], axis=1).astype(f32)
            xa = (xcat + pelo_ref[...]).astype(bf16)
            xb = (xcat + pehi_ref[...]).astype(bf16)
            a = jnp.dot(xa, wlo_ref[...], preferred_element_type=f32)
            bm = jnp.dot(xb, whi_ref[...], preferred_element_type=f32)
            h = a + pltpu.roll(bm, N_CMP_PAD - 1, 0)
            hid = h * _sigmoid(h)
            kcvc_ref[gg] = jnp.dot(hid.astype(bf16), w2_ref[...], preferred_element_type=f32)

    ksw = ksw_ref[...]
    ks_ref[...] = jnp.where(low, ksw, kaug_s_ref[...])
    kwin = pltpu.roll(ksw.astype(f32), HEAD_DIM, 1).astype(bf16)
    kw_ref[0:WIN_PAD, :] = kaug_w_ref[0:WIN_PAD, :]
    kw_ref[WIN_PAD:, :] = jnp.where(low, kwin, kaug_w_ref[WIN_PAD:, :])
    vw_ref[0:WIN_PAD, :] = jnp.zeros((WIN_PAD, LANES), bf16)
    vw_ref[WIN_PAD:, :] = vsw_ref[...]

    qaug = qaug_ref[0]
    kcvc = kcvc_ref[g]
    kc_aug = jnp.where(low, kcvc, kaug_c_ref[...]).astype(bf16)
    kcvc_b = kcvc.astype(bf16)
    ovl = ovl_ref[...]

    def cmp_body(i, _):
        t0 = pl.multiple_of(i * CMP_TQ, CMP_TQ)
        qt = q_ref[pl.ds(t0, CMP_TQ), :].astype(f32) * (HEAD_DIM ** -0.5)
        qp = _stack_heads(qt, qaug[0:GROUP], None)
        s = _dot_nt(qp, kc_aug)
        shp_c = (GROUP * CMP_TQ, N_CMP_PAD)
        t_idx = t0 + (lax.broadcasted_iota(jnp.int32, shp_c, 0) & (CMP_TQ - 1))
        c_end = lax.broadcasted_iota(jnp.int32, shp_c, 1) * CMP_STRIDE + (CMP_LEN - 1)
        ok = c_end <= t_idx
        sm = jnp.where(ok, s, -jnp.inf)
        m = jnp.max(sm, axis=-1, keepdims=True)
        m = jnp.where(m > -jnp.inf, m, 0.0)
        e = jnp.where(ok, jnp.exp(sm - m), 0.0)
        p = e / jnp.maximum(jnp.sum(e, axis=-1, keepdims=True), 1e-30)
        o = jnp.dot(p.astype(bf16), kcvc_b, preferred_element_type=f32)
        sig = _sigmoid(gate_ref[pl.ds(t0, CMP_TQ), :].astype(f32))
        vals = []
        for r in range(GROUP):
            og = o[r * CMP_TQ:(r + 1) * CMP_TQ, :] * sig[:, 3 * r:3 * r + 1]
            vals.append(pltpu.roll(og, HEAD_DIM, 1) if r % 2 == 0 else og)
        c0, c1 = _merge_heads(vals, lane)
        ocmp_ref[pl.ds(t0, CMP_TQ), 0:LANES] = c0
        ocmp_ref[pl.ds(t0, CMP_TQ), LANES:2 * LANES] = c1
        ps = p[0:CMP_TQ] + p[CMP_TQ:2 * CMP_TQ] + p[2 * CMP_TQ:3 * CMP_TQ] + p[3 * CMP_TQ:4 * CMP_TQ]
        p1 = ps.astype(bf16)
        r1 = ps - p1.astype(f32)
        p2 = r1.astype(bf16)
        p3 = (r1 - p2.astype(f32)).astype(bf16)
        imp_ref[i] = _dot_nt(ovl, p1) + _dot_nt(ovl, p2) + _dot_nt(ovl, p3)
        return 0

    lax.fori_loop(0, seq // CMP_TQ, cmp_body, 0)

    n_slc = seq // SLC_LEN
    n_tt = seq // LANES
    shp = (n_tt, n_slc, LANES)
    j_idx = lax.broadcasted_iota(jnp.int32, shp, 1)
    t_idx = lax.broadcasted_iota(jnp.int32, shp, 0) * LANES + lax.broadcasted_iota(jnp.int32, shp, 2)
    cur = t_idx // SLC_LEN
    forced = (j_idx == 0) | (j_idx == cur) | (j_idx == cur - 1)
    val = jnp.where(forced, jnp.inf, jnp.where(j_idx <= cur, imp_ref[...], -jnp.inf))
    rank = jnp.zeros(shp, jnp.int32)
    for jp in range(n_slc):
        vj = val[:, jp:jp + 1, :]
        beats = (vj > val) | ((vj == val) & (j_idx > jp))
        rank = rank + beats.astype(jnp.int32)
    sel = (rank < N_SEL) & (val > -jnp.inf)
    notsel = jnp.where(sel, 0.0, 1.0)
    eye = (lax.broadcasted_iota(jnp.int32, (LANES, LANES), 0)
           == lax.broadcasted_iota(jnp.int32, (LANES, LANES), 1)).astype(bf16)
    zpad = jnp.zeros((L_SEL, LANES), f32)
    for tt in range(n_tt):
        z = jnp.concatenate([zpad, notsel[tt]], axis=0).astype(bf16)
        selq_ref[tt * LANES:(tt + 1) * LANES, :] = _dot_nt(eye, z) * (-MASK_BIG)

    band = jnp.concatenate([band_ref[...]] * GROUP, axis=0)
    rows = GROUP * Q_BLOCK

    def blk_body(i, _):
        t0 = pl.multiple_of(i * Q_BLOCK, Q_BLOCK)
        qt = q_ref[pl.ds(t0, Q_BLOCK), :].astype(f32) * (HEAD_DIM ** -0.5)
        qs = _stack_heads(qt, qaug[0:GROUP], selq_ref[pl.ds(t0, Q_BLOCK), :])
        qw = _stack_heads(qt, qaug[GROUP:2 * GROUP], None)

        def chunk(c, carry):
            k0 = pl.multiple_of(c * SEL_CHUNK, SEL_CHUNK)
            s = _dot_nt(qs, ks_ref[pl.ds(k0, SEL_CHUNK), :])
            return _softmax_step(s, vsw_ref[pl.ds(k0, SEL_CHUNK), :], carry)

        n_full = i // (SEL_CHUNK // SLC_LEN)
        init = (jnp.full((rows, 1), -jnp.inf, f32), jnp.zeros((rows, 1), f32), jnp.zeros((rows, LANES), f32))
        carry = lax.fori_loop(0, n_full, chunk, init)
        k0 = pl.multiple_of(n_full * SEL_CHUNK, SEL_CHUNK)
        tri = tri_ref[i % (SEL_CHUNK // SLC_LEN)]
        s = _dot_nt(qs, ks_ref[pl.ds(k0, SEL_CHUNK), :]) + jnp.concatenate([tri] * GROUP, axis=0)
        _, l_s, acc_s = _softmax_step(s, vsw_ref[pl.ds(k0, SEL_CHUNK), :], carry)
        o_s = acc_s / jnp.maximum(l_s, 1e-30)

        s = _dot_nt(qw, kw_ref[pl.ds(t0, WIN_KEYS), :]) + band
        m = jnp.max(s, axis=-1, keepdims=True)
        p = jnp.exp(s - m)
        l_w = jnp.sum(p, axis=-1, keepdims=True)
        acc_w = jnp.dot(p.astype(bf16), vw_ref[pl.ds(t0, WIN_KEYS), :], preferred_element_type=f32)
        o_w = acc_w / jnp.maximum(l_w, 1e-30)

        sig = _sigmoid(gate_ref[pl.ds(t0, Q_BLOCK), :].astype(f32))
        vals = []
        for r in range(GROUP):
            a_s = o_s[r * Q_BLOCK:(r + 1) * Q_BLOCK, :] * sig[:, 3 * r + 1:3 * r + 2]
            a_w = o_w[r * Q_BLOCK:(r + 1) * Q_BLOCK, :] * sig[:, 3 * r + 2:3 * r + 3]
            if r % 2 == 0:
                vals.append(a_s + pltpu.roll(a_w, HEAD_DIM, 1))
            else:
                vals.append(pltpu.roll(a_s, HEAD_DIM, 1) + a_w)
        c0, c1 = _merge_heads(vals, lane)
        o_ref[pl.ds(t0, Q_BLOCK), 0:LANES] = (ocmp_ref[pl.ds(t0, Q_BLOCK), 0:LANES] + c0).astype(bf16)
        o_ref[pl.ds(t0, Q_BLOCK), LANES:2 * LANES] = (
            ocmp_ref[pl.ds(t0, Q_BLOCK), LANES:2 * LANES] + c1).astype(bf16)
        return 0

    lax.fori_loop(0, seq // Q_BLOCK, blk_body, 0)


def _position_lanes(pos):
    out = np.zeros((pos.shape[0], LANES), np.float32)
    out[:, L_HI:L_HI + 3] = ((pos // 64) * 64)[:, None]
    out[:, L_LO:L_LO + 3] = (pos % 64)[:, None]
    return out


def _nsa_tables(seq):
    pos = np.arange(seq)
    kaug_s = _position_lanes(pos)
    kaug_s[pos, L_SEL + pos // SLC_LEN] = 1.0
    kaug_w = np.zeros((seq + WIN_PAD, LANES), np.float32)
    kaug_w[WIN_PAD:] = _position_lanes(pos)
    kaug_w[:WIN_PAD, L_FLAG] = 1.0
    kaug_c = _position_lanes(np.arange(N_CMP_PAD) * CMP_STRIDE + CMP_LEN - 1)
    per_chunk = SEL_CHUNK // SLC_LEN
    tl = np.arange(Q_BLOCK)[:, None]
    kk = np.arange(SEL_CHUNK)[None, :]
    tri = np.stack([np.where((kk // SLC_LEN < mm) | ((kk // SLC_LEN == mm) & (kk % SLC_LEN <= tl)), 0.0, -MASK_BIG)
                    for mm in range(per_chunk)]).astype(np.float32)
    kk = np.arange(WIN_KEYS)[None, :]
    dist = tl + WIN_PAD - kk
    band = np.where((dist >= 0) & (dist < WINDOW), 0.0, -MASK_BIG).astype(np.float32)
    c = np.arange(N_CMP_PAD)[None, :]
    j = np.arange(seq // SLC_LEN)[:, None]
    c_start, c_end, s_start = c * CMP_STRIDE, c * CMP_STRIDE + CMP_LEN - 1, j * SLC_LEN
    ovl = ((c_start <= s_start + SLC_LEN - 1) & (c_end >= s_start) & (c < seq // CMP_STRIDE - 1)).astype(np.float32)
    return (jnp.asarray(kaug_s, bf16), jnp.asarray(kaug_w, bf16), jnp.asarray(kaug_c, f32),
            jnp.asarray(tri), jnp.asarray(band), jnp.asarray(ovl, bf16))


def _query_aug():
    slopes = jnp.exp2(-8.0 * jnp.arange(1, N_HEADS + 1, dtype=f32) / N_HEADS).reshape(N_KV, GROUP)
    hi = slopes.astype(bf16).astype(f32)
    mid = (slopes - hi).astype(bf16).astype(f32)
    lo = (slopes - hi - mid).astype(bf16).astype(f32)
    pieces = jnp.stack([hi, mid, lo, hi, mid, lo], axis=-1)
    base = jnp.zeros((N_KV, GROUP, LANES), f32).at[:, :, L_HI:L_HI + 6].set(pieces)
    return jnp.concatenate([base, base.at[:, :, L_FLAG].set(-MASK_BIG)], axis=1)


def _nsa(proj, kvc, wlo, whi, pelo, pehi, w2bd, batch, seq):
    kaug_s, kaug_w, kaug_c, tri, band, ovl = _nsa_tables(seq)
    qaug = _query_aug()
    const2 = lambda b, g: (0, 0)
    const3 = lambda b, g: (0, 0, 0)
    in_specs = [
        pl.BlockSpec((seq, 2 * LANES), lambda b, g: (b, OFF_Q // (2 * LANES) + g)),
        pl.BlockSpec((seq, LANES), lambda b, g: (b, OFF_KSW // LANES + g)),
        pl.BlockSpec((seq, LANES), lambda b, g: (b, OFF_VSW // LANES + g)),
        pl.BlockSpec((seq, LANES), lambda b, g: (b, OFF_GATE // LANES + g)),
        pl.BlockSpec((1,) + kvc.shape[1:], lambda b, g: (b, 0, 0)),
        pl.BlockSpec(wlo.shape, const2),
        pl.BlockSpec(whi.shape, const2),
        pl.BlockSpec(pelo.shape, const2),
        pl.BlockSpec(pehi.shape, const2),
        pl.BlockSpec(w2bd.shape, const2),
        pl.BlockSpec(kaug_s.shape, const2),
        pl.BlockSpec(kaug_w.shape, const2),
        pl.BlockSpec(kaug_c.shape, const2),
        pl.BlockSpec((1, 2 * GROUP, LANES), lambda b, g: (g, 0, 0)),
        pl.BlockSpec(tri.shape, const3),
        pl.BlockSpec(band.shape, const2),
        pl.BlockSpec(ovl.shape, const2),
    ]
    return pl.pallas_call(
        _nsa_kernel,
        grid=(batch, N_KV),
        in_specs=in_specs,
        out_specs=pl.BlockSpec((seq, 2 * LANES), lambda b, g: (b, g)),
        out_shape=jax.ShapeDtypeStruct((batch * seq, D_ATT), bf16),
        scratch_shapes=[
            pltpu.VMEM((N_KV, N_CMP_PAD, LANES), f32),
            pltpu.VMEM((seq, LANES), bf16),
            pltpu.VMEM((seq + WIN_PAD, LANES), bf16),
            pltpu.VMEM((seq + WIN_PAD, LANES), bf16),
            pltpu.VMEM((seq, 2 * LANES), f32),
            pltpu.VMEM((seq // LANES, seq // SLC_LEN, LANES), f32),
            pltpu.VMEM((seq, LANES), f32),
        ],
        compiler_params=pltpu.CompilerParams(
            dimension_semantics=("parallel", "arbitrary"), vmem_limit_bytes=VMEM_LIMIT),
        name="nsa",
    )(proj, proj, proj, proj, kvc, wlo, whi, pelo, pehi, w2bd, kaug_s, kaug_w, kaug_c, qaug, tri, band, ovl)


def _out_kernel(x_ref, ha_ref, bg_ref, cg_ref, za_ref, hap_ref, cgp_ref, zb_ref, gm_ref, ob_ref,
                cw_ref, cb_ref, wa_ref, wb_ref, wo_ref, fnw_ref, o_ref, *, tiles_per_seq):
    tm = x_ref.shape[0]
    halo = hap_ref.shape[0]
    not_first = (pl.program_id(0) % tiles_per_seq != 0).astype(f32)
    u = cg_ref[...].astype(f32) * ha_ref[...].astype(f32)
    u_prev = cgp_ref[...].astype(f32) * hap_ref[...].astype(f32) * not_first
    row = lax.broadcasted_iota(jnp.int32, (tm, 1), 0)
    p1 = u_prev[halo - 1:halo, :]
    p2 = u_prev[halo - 2:halo - 1, :]
    u1 = jnp.where(row == 0, p1, pltpu.roll(u, 1, 0))
    u2 = jnp.where(row == 0, p2, jnp.where(row == 1, p1, pltpu.roll(u, 2, 0)))
    cw = cw_ref[...]
    y = cb_ref[...] + cw[0:1, :] * u2 + cw[1:2, :] * u1 + cw[2:3, :] * u
    za = za_ref[...].astype(f32)
    ya_in = bg_ref[...].astype(f32) * y * (za * _sigmoid(za))
    y_a = jnp.dot(ya_in.astype(bf16), wa_ref[...], preferred_element_type=f32)
    zb = zb_ref[...].astype(f32)
    yb_in = ob_ref[...].astype(f32) * (zb * _sigmoid(zb))
    y_b = jnp.dot(yb_in.astype(bf16), wb_ref[...], preferred_element_type=f32)
    mixed = (_sigmoid(gm_ref[:, 0:D_MODEL].astype(f32)) * y_a
             + _sigmoid(gm_ref[:, D_MODEL:2 * D_MODEL].astype(f32)) * y_b)
    xo = x_ref[...] + jnp.dot(mixed.astype(bf16), wo_ref[...], preferred_element_type=f32)
    r = lax.rsqrt(jnp.mean(xo * xo, axis=-1, keepdims=True) + NORM_EPS)
    o_ref[...] = (xo * r) * fnw_ref[...]


def _out(x2, proj, ob, conv_w, conv_b, wa, wb, wo, fnw, seq, tm=512, halo=16):
    n_rows = x2.shape[0]
    blk = lambda c: pl.BlockSpec((tm, D_MODEL), lambda i, c=c: (i, c))
    prev = lambda c: pl.BlockSpec((halo, D_MODEL), lambda i, c=c: (jnp.maximum(i * (tm // halo) - 1, 0), c))
    full = lambda a: pl.BlockSpec(a.shape, lambda i: (0, 0))
    return pl.pallas_call(
        functools.partial(_out_kernel, tiles_per_seq=seq // tm),
        grid=(n_rows // tm,),
        in_specs=[
            blk(0),
            blk(0), blk(1), blk(2), blk(3),
            prev(0), prev(2),
            blk(OFF_ZB // D_MODEL),
            pl.BlockSpec((tm, 2 * D_MODEL), lambda i: (i, OFF_GMIX // (2 * D_MODEL))),
            blk(0),
            full(conv_w), full(conv_b), full(wa), full(wb), full(wo), full(fnw),
        ],
        out_specs=pl.BlockSpec((tm, D_MODEL), lambda i: (i, 0)),
        out_shape=jax.ShapeDtypeStruct((n_rows, D_MODEL), f32),
        compiler_params=pltpu.CompilerParams(
            dimension_semantics=("parallel",), vmem_limit_bytes=VMEM_LIMIT),
        name="merge_out",
    )(x2, proj, proj, proj, proj, proj, proj, proj, proj, ob, conv_w, conv_b, wa, wb, wo, fnw)


def _blockdiag_w1(w1_k, w1_v, lo):
    half = CMP_STRIDE * HEAD_DIM
    wk = w1_k[lo * half:(lo + 1) * half].reshape(CMP_STRIDE, HEAD_DIM, CMP_HID)
    wv = w1_v[lo * half:(lo + 1) * half].reshape(CMP_STRIDE, HEAD_DIM, CMP_HID)
    z = jnp.zeros_like(wk)
    top = jnp.concatenate([wk, z], axis=-1)
    bot = jnp.concatenate([z, wv], axis=-1)
    return jnp.concatenate([top, bot], axis=1).reshape(CMP_STRIDE * 2 * HEAD_DIM, 2 * CMP_HID).astype(bf16)


def _pe_row(pe_k, pe_v, lo):
    sl = slice(lo * CMP_STRIDE, (lo + 1) * CMP_STRIDE)
    return jnp.concatenate([pe_k[sl], pe_v[sl]], axis=-1).reshape(1, CMP_STRIDE * 2 * HEAD_DIM)


def kernel(x, norm_w, w_in, conv_w, conv_b, cmp_pe_k, cmp_pe_v, cmp_w1_k, cmp_w2_k, cmp_w1_v, cmp_w2_v,
           w_proj_a, w_proj_b, w_out, final_norm_w):
    batch, seq, _ = x.shape
    assert norm_w.shape[0] == 1 and seq % SEL_CHUNK == 0 and seq // CMP_STRIDE == N_CMP_PAD
    idx = _column_permutation()
    w_p = jnp.where(jnp.asarray(idx >= 0)[None, :], jnp.take(w_in[0], jnp.asarray(np.maximum(idx, 0)), axis=1),
                    0.0).astype(bf16)
    x2 = x.reshape(batch * seq, D_MODEL)
    proj = _inproj(x2, norm_w, w_p)

    kvc = proj[:, OFF_KVC:OFF_KVC + N_KV * LANES].reshape(batch, seq // CMP_STRIDE, CMP_STRIDE * N_KV * LANES)
    wlo = _blockdiag_w1(cmp_w1_k[0], cmp_w1_v[0], 0)
    whi = _blockdiag_w1(cmp_w1_k[0], cmp_w1_v[0], 1)
    pelo = _pe_row(cmp_pe_k[0], cmp_pe_v[0], 0)
    pehi = _pe_row(cmp_pe_k[0], cmp_pe_v[0], 1)
    zk = jnp.zeros_like(cmp_w2_k[0])
    w2bd = jnp.concatenate([jnp.concatenate([cmp_w2_k[0], zk], axis=1),
                            jnp.concatenate([zk, cmp_w2_v[0]], axis=1)], axis=0).astype(bf16)
    ob = _nsa(proj, kvc, wlo, whi, pelo, pehi, w2bd, batch, seq)

    out = _out(x2, proj, ob, conv_w[0], conv_b, w_proj_a[0].astype(bf16), w_proj_b[0].astype(bf16),
               w_out[0].astype(bf16), final_norm_w.reshape(1, D_MODEL), seq)
    return out.reshape(batch, seq, D_MODEL)
```

```python
import functools

import numpy as np
import jax
import jax.numpy as jnp
from jax import lax
from jax.experimental import pallas as pl
from jax.experimental.pallas import tpu as pltpu

D_MODEL = 1024
D_CONV = 1024
CONV_W = 3
N_HEADS = 16
HEAD_DIM = 64
N_KV = 4
GROUP = N_HEADS // N_KV
D_ATT = N_HEADS * HEAD_DIM
D_KV = N_KV * HEAD_DIM
CMP_LEN = 32
CMP_STRIDE = 16
CMP_HID = 128
SLC_LEN = 64
N_SEL = 8
WINDOW = 512
NORM_EPS = 1e-6

LANES = 128
N_CMP_PAD = 128
MASK_BIG = float(2.0 ** 100)
SEL_CHUNK = 512
WIN_TQ = 128
WIN_KEYS = WINDOW + WIN_TQ
WIN_PAD = WINDOW
CMP_TQ = 256

OFF_CONV = 0
OFF_Q = 4096
OFF_ZB = 5120
OFF_GMIX = 6144
OFF_KVC = 8192
OFF_KSW = 8704
OFF_VSW = 9216
OFF_GATE = 9728
N_PAD = 10240

L_HI = 64
L_LO = 67
L_FLAG = 70
L_SEL = 96

VMEM_LIMIT = 56 * 1024 * 1024

f32 = jnp.float32
bf16 = jnp.bfloat16


def _column_permutation():
    idx = np.full((N_PAD,), -1, np.int64)
    idx[OFF_CONV:OFF_CONV + 4 * D_CONV] = np.arange(4 * D_CONV)
    o_q = 4 * D_CONV
    idx[OFF_Q:OFF_Q + D_ATT] = o_q + np.arange(D_ATT)
    o_kc = o_q + D_ATT
    o_vc, o_ks, o_vs, o_kw, o_vw = (o_kc + D_KV * n for n in range(1, 6))
    o_gate = o_kc + 6 * D_KV
    o_zb = o_gate + 3 * N_HEADS
    o_gmix = o_zb + D_ATT
    idx[OFF_ZB:OFF_ZB + D_ATT] = o_zb + np.arange(D_ATT)
    idx[OFF_GMIX:OFF_GMIX + 2 * D_MODEL] = o_gmix + np.arange(2 * D_MODEL)
    d = np.arange(HEAD_DIM)
    for g in range(N_KV):
        for off, lo, hi in ((OFF_KVC, o_kc, o_vc), (OFF_KSW, o_ks, o_kw), (OFF_VSW, o_vs, o_vw)):
            idx[off + g * LANES + d] = lo + HEAD_DIM * g + d
            idx[off + g * LANES + HEAD_DIM + d] = hi + HEAD_DIM * g + d
        idx[OFF_GATE + g * LANES + np.arange(3 * GROUP)] = o_gate + 3 * GROUP * g + np.arange(3 * GROUP)
    return idx


def _sigmoid(x):
    return 1.0 / (1.0 + jnp.exp(-x))


def _dot_nt(a, b):
    return lax.dot_general(a, b, (((1,), (1,)), ((), ())), preferred_element_type=f32)


def _inproj_kernel(x_ref, nw_ref, w_ref, o_ref, h_ref):
    @pl.when(pl.program_id(1) == 0)
    def _():
        x = x_ref[...]
        r = lax.rsqrt(jnp.mean(x * x, axis=-1, keepdims=True) + NORM_EPS)
        h_ref[...] = ((x * r) * nw_ref[...]).astype(bf16)

    o_ref[...] = jnp.dot(h_ref[...], w_ref[...], preferred_element_type=f32).astype(bf16)


def _inproj(x2, norm_w, w_p, tm=1024, tn=1024):
    n_rows = x2.shape[0]
    return pl.pallas_call(
        _inproj_kernel,
        grid=(n_rows // tm, N_PAD // tn),
        in_specs=[
            pl.BlockSpec((tm, D_MODEL), lambda i, j: (i, 0)),
            pl.BlockSpec((1, D_MODEL), lambda i, j: (0, 0)),
            pl.BlockSpec((D_MODEL, tn), lambda i, j: (0, j)),
        ],
        out_specs=pl.BlockSpec((tm, tn), lambda i, j: (i, j)),
        out_shape=jax.ShapeDtypeStruct((n_rows, N_PAD), bf16),
        scratch_shapes=[pltpu.VMEM((tm, D_MODEL), bf16)],
        compiler_params=pltpu.CompilerParams(
            dimension_semantics=("parallel", "arbitrary"), vmem_limit_bytes=VMEM_LIMIT),
        name="inproj",
    )(x2, norm_w, w_p)


def _softmax_step(s, v, carry):
    m, acc = carry
    m_new = jnp.maximum(m, jnp.max(s, axis=-1, keepdims=True))
    alpha = jnp.exp(m - m_new)
    p = jnp.exp(s - m_new)
    acc = alpha * acc + jnp.dot(p.astype(bf16), v, preferred_element_type=f32)
    return m_new, acc


def _stack_heads(q_tile, aug_rows, extra):
    lane = lax.broadcasted_iota(jnp.int32, (1, LANES), 1)
    pairs = (q_tile[:, :LANES], q_tile[:, LANES:])
    parts = []
    for r in range(GROUP):
        qh = pairs[r // 2]
        if r % 2 == 1:
            qh = pltpu.roll(qh, HEAD_DIM, 1)
        aug = aug_rows[r:r + 1, :]
        if extra is not None:
            aug = aug + extra
        parts.append(jnp.where(lane < HEAD_DIM, qh, aug).astype(bf16))
    return jnp.concatenate(parts, axis=0)


def _merge_heads(vals, lane):
    return (jnp.where(lane < HEAD_DIM, vals[0], vals[1]), jnp.where(lane < HEAD_DIM, vals[2], vals[3]))


def _nsa_kernel(q_ref, ksw_ref, vsw_ref, gate_ref, kvc_ref, wlo_ref, whi_ref, pelo_ref, pehi_ref, w2_ref,
                kaug_s_ref, kaug_w_ref, kaug_c_ref, qaug_ref, tri_ref, band_ref, ovl_ref,
                o_ref,
                kcvc_ref, ks_ref, kw_ref, vs_ref, vw_ref, ocw_ref, imp_ref, selq_ref):
    seq = q_ref.shape[0]
    g = pl.program_id(1)
    lane = lax.broadcasted_iota(jnp.int32, (1, LANES), 1)
    low = lane < HEAD_DIM

    @pl.when(g == 0)
    def _():
        for gg in range(N_KV):
            xcat = jnp.concatenate(
                [kvc_ref[0, :, l * (N_KV * LANES) + gg * LANES: l * (N_KV * LANES) + (gg + 1) * LANES]
                 for l in range(CMP_STRIDE)], axis=1).astype(f32)
            xa = (xcat + pelo_ref[...]).astype(bf16)
            xb = (xcat + pehi_ref[...]).astype(bf16)
            a = jnp.dot(xa, wlo_ref[...], preferred_element_type=f32)
            bm = jnp.dot(xb, whi_ref[...], preferred_element_type=f32)
            h = a + pltpu.roll(bm, N_CMP_PAD - 1, 0)
            hid = h * _sigmoid(h)
            kcvc_ref[gg] = jnp.dot(hid.astype(bf16), w2_ref[...], preferred_element_type=f32)

    ksw = ksw_ref[...]
    vsw = vsw_ref[...]
    one = jnp.ones((1, LANES), bf16)
    ks_ref[...] = jnp.where(low, ksw, kaug_s_ref[...])
    kwin = pltpu.roll(ksw.astype(f32), HEAD_DIM, 1).astype(bf16)
    kw_ref[0:WIN_PAD, :] = kaug_w_ref[0:WIN_PAD, :]
    kw_ref[WIN_PAD:, :] = jnp.where(low, kwin, kaug_w_ref[WIN_PAD:, :])
    vs_ref[...] = jnp.where(low, vsw, one)
    vw_ref[0:WIN_PAD, :] = jnp.zeros((WIN_PAD, LANES), bf16)
    vw_ref[WIN_PAD:, :] = jnp.where(low, one, vsw)

    qaug = qaug_ref[0]
    kcvc = kcvc_ref[g]
    kc_aug = jnp.where(low, kcvc, kaug_c_ref[...]).astype(bf16)
    kcvc_b = kcvc.astype(bf16)
    ovl = ovl_ref[...]

    def cw_body(i, _):
        t0 = pl.multiple_of(i * CMP_TQ, CMP_TQ)
        qt = q_ref[pl.ds(t0, CMP_TQ), :].astype(f32) * (HEAD_DIM ** -0.5)
        sig = _sigmoid(gate_ref[pl.ds(t0, CMP_TQ), :].astype(f32))

        qp = _stack_heads(qt, qaug[0:GROUP], None)
        s = _dot_nt(qp, kc_aug)
        shp_c = (GROUP * CMP_TQ, N_CMP_PAD)
        t_idx = t0 + (lax.broadcasted_iota(jnp.int32, shp_c, 0) & (CMP_TQ - 1))
        c_end = lax.broadcasted_iota(jnp.int32, shp_c, 1) * CMP_STRIDE + (CMP_LEN - 1)
        ok = c_end <= t_idx
        sm = jnp.where(ok, s, -jnp.inf)
        m = jnp.max(sm, axis=-1, keepdims=True)
        m = jnp.where(m > -jnp.inf, m, 0.0)
        e = jnp.where(ok, jnp.exp(sm - m), 0.0)
        p = e / jnp.maximum(jnp.sum(e, axis=-1, keepdims=True), 1e-30)
        o_c = jnp.dot(p.astype(bf16), kcvc_b, preferred_element_type=f32)
        ps = p[0:CMP_TQ] + p[CMP_TQ:2 * CMP_TQ] + p[2 * CMP_TQ:3 * CMP_TQ] + p[3 * CMP_TQ:4 * CMP_TQ]
        p1 = ps.astype(bf16)
        r1 = ps - p1.astype(f32)
        p2 = r1.astype(bf16)
        p3 = (r1 - p2.astype(f32)).astype(bf16)
        imp = _dot_nt(ovl, p1) + _dot_nt(ovl, p2) + _dot_nt(ovl, p3)
        for h in range(CMP_TQ // LANES):
            imp_ref[i * (CMP_TQ // LANES) + h] = imp[:, h * LANES:(h + 1) * LANES]

        o_w = []
        for h in range(CMP_TQ // WIN_TQ):
            tw = pl.multiple_of(t0 + h * WIN_TQ, WIN_TQ)
            qw = _stack_heads(qt[h * WIN_TQ:(h + 1) * WIN_TQ], qaug[GROUP:2 * GROUP], None)
            s = _dot_nt(qw, kw_ref[pl.ds(tw, WIN_KEYS), :]) + band_ref[...]
            m = jnp.max(s, axis=-1, keepdims=True)
            p = jnp.exp(s - m)
            acc = jnp.dot(p.astype(bf16), vw_ref[pl.ds(tw, WIN_KEYS), :], preferred_element_type=f32)
            o_w.append(acc / jnp.maximum(acc[:, 0:1], 1e-30))

        vals = []
        for r in range(GROUP):
            a_c = o_c[r * CMP_TQ:(r + 1) * CMP_TQ, :] * sig[:, 3 * r:3 * r + 1]
            a_w = (jnp.concatenate([o[r * WIN_TQ:(r + 1) * WIN_TQ, :] for o in o_w], axis=0)
                   * sig[:, 3 * r + 2:3 * r + 3])
            a = a_c + a_w
            vals.append(pltpu.roll(a, HEAD_DIM, 1) if r % 2 == 0 else a)
        c0, c1 = _merge_heads(vals, lane)
        ocw_ref[pl.ds(t0, CMP_TQ), 0:LANES] = c0
        ocw_ref[pl.ds(t0, CMP_TQ), LANES:2 * LANES] = c1
        return 0

    lax.fori_loop(0, seq // CMP_TQ, cw_body, 0)

    n_slc = seq // SLC_LEN
    n_tt = seq // LANES
    shp = (n_tt, n_slc, LANES)
    j_idx = lax.broadcasted_iota(jnp.int32, shp, 1)
    t_idx = lax.broadcasted_iota(jnp.int32, shp, 0) * LANES + lax.broadcasted_iota(jnp.int32, shp, 2)
    cur = t_idx // SLC_LEN
    forced = (j_idx == 0) | (j_idx == cur) | (j_idx == cur - 1)
    val = jnp.where(forced, jnp.inf, jnp.where(j_idx <= cur, imp_ref[...], -jnp.inf))
    rank = jnp.zeros(shp, jnp.int32)
    for jp in range(n_slc):
        vj = val[:, jp:jp + 1, :]
        beats = (vj > val) | ((vj == val) & (j_idx > jp))
        rank = rank + beats.astype(jnp.int32)
    sel = (rank < N_SEL) & (val > -jnp.inf)
    notsel = jnp.where(sel, 0.0, 1.0)
    eye = (lax.broadcasted_iota(jnp.int32, (LANES, LANES), 0)
           == lax.broadcasted_iota(jnp.int32, (LANES, LANES), 1)).astype(bf16)
    zpad = jnp.zeros((L_SEL, LANES), f32)
    for tt in range(n_tt):
        z = jnp.concatenate([zpad, notsel[tt]], axis=0).astype(bf16)
        selq_ref[tt * LANES:(tt + 1) * LANES, :] = _dot_nt(eye, z) * (-MASK_BIG)

    rows = GROUP * SEL_CHUNK

    def sel_body(i, _):
        t0 = pl.multiple_of(i * SEL_CHUNK, SEL_CHUNK)
        qt = q_ref[pl.ds(t0, SEL_CHUNK), :].astype(f32) * (HEAD_DIM ** -0.5)
        qs = _stack_heads(qt, qaug[0:GROUP], selq_ref[pl.ds(t0, SEL_CHUNK), :])

        def chunk(c, carry):
            k0 = pl.multiple_of(c * SEL_CHUNK, SEL_CHUNK)
            s = _dot_nt(qs, ks_ref[pl.ds(k0, SEL_CHUNK), :])
            return _softmax_step(s, vs_ref[pl.ds(k0, SEL_CHUNK), :], carry)

        init = (jnp.full((rows, 1), -jnp.inf, f32), jnp.zeros((rows, LANES), f32))
        carry = lax.fori_loop(0, i, chunk, init)
        s = _dot_nt(qs, ks_ref[pl.ds(t0, SEL_CHUNK), :])
        tri = tri_ref[...]
        s = jnp.concatenate([s[r * SEL_CHUNK:(r + 1) * SEL_CHUNK, :] + tri for r in range(GROUP)], axis=0)
        _, acc = _softmax_step(s, vs_ref[pl.ds(t0, SEL_CHUNK), :], carry)
        o_s = acc / jnp.maximum(acc[:, HEAD_DIM:HEAD_DIM + 1], 1e-30)

        sig = _sigmoid(gate_ref[pl.ds(t0, SEL_CHUNK), :].astype(f32))
        vals = []
        for r in range(GROUP):
            a_s = o_s[r * SEL_CHUNK:(r + 1) * SEL_CHUNK, :] * sig[:, 3 * r + 1:3 * r + 2]
            vals.append(a_s if r % 2 == 0 else pltpu.roll(a_s, HEAD_DIM, 1))
        c0, c1 = _merge_heads(vals, lane)
        o_ref[pl.ds(t0, SEL_CHUNK), 0:LANES] = (ocw_ref[pl.ds(t0, SEL_CHUNK), 0:LANES] + c0).astype(bf16)
        o_ref[pl.ds(t0, SEL_CHUNK), LANES:2 * LANES] = (
            ocw_ref[pl.ds(t0, SEL_CHUNK), LANES:2 * LANES] + c1).astype(bf16)
        return 0

    lax.fori_loop(0, seq // SEL_CHUNK, sel_body, 0)


def _position_lanes(pos):
    out = np.zeros((pos.shape[0], LANES), np.float32)
    out[:, L_HI:L_HI + 3] = ((pos // 64) * 64)[:, None]
    out[:, L_LO:L_LO + 3] = (pos % 64)[:, None]
    return out


def _nsa_tables(seq):
    pos = np.arange(seq)
    kaug_s = _position_lanes(pos)
    kaug_s[pos, L_SEL + pos // SLC_LEN] = 1.0
    kaug_w = np.zeros((seq + WIN_PAD, LANES), np.float32)
    kaug_w[WIN_PAD:] = _position_lanes(pos)
    kaug_w[:WIN_PAD, L_FLAG] = 1.0
    kaug_c = _position_lanes(np.arange(N_CMP_PAD) * CMP_STRIDE + CMP_LEN - 1)
    tl = np.arange(SEL_CHUNK)[:, None]
    kk = np.arange(SEL_CHUNK)[None, :]
    tri = np.where(kk <= tl, 0.0, -MASK_BIG).astype(np.float32)
    tl = np.arange(WIN_TQ)[:, None]
    kk = np.arange(WIN_KEYS)[None, :]
    dist = tl + WIN_PAD - kk
    band = np.tile(np.where((dist >= 0) & (dist < WINDOW), 0.0, -MASK_BIG).astype(np.float32), (GROUP, 1))
    c = np.arange(N_CMP_PAD)[None, :]
    j = np.arange(seq // SLC_LEN)[:, None]
    c_start, c_end, s_start = c * CMP_STRIDE, c * CMP_STRIDE + CMP_LEN - 1, j * SLC_LEN
    ovl = ((c_start <= s_start + SLC_LEN - 1) & (c_end >= s_start) & (c < seq // CMP_STRIDE - 1)).astype(np.float32)
    return (jnp.asarray(kaug_s, bf16), jnp.asarray(kaug_w, bf16), jnp.asarray(kaug_c, f32),
            jnp.asarray(tri), jnp.asarray(band), jnp.asarray(ovl, bf16))


def _query_aug():
    slopes = jnp.exp2(-8.0 * jnp.arange(1, N_HEADS + 1, dtype=f32) / N_HEADS).reshape(N_KV, GROUP)
    hi = slopes.astype(bf16).astype(f32)
    mid = (slopes - hi).astype(bf16).astype(f32)
    lo = (slopes - hi - mid).astype(bf16).astype(f32)
    pieces = jnp.stack([hi, mid, lo, hi, mid, lo], axis=-1)
    base = jnp.zeros((N_KV, GROUP, LANES), f32).at[:, :, L_HI:L_HI + 6].set(pieces)
    return jnp.concatenate([base, base.at[:, :, L_FLAG].set(-MASK_BIG)], axis=1)


def _nsa(proj, kvc, wlo, whi, pelo, pehi, w2bd, batch, seq):
    kaug_s, kaug_w, kaug_c, tri, band, ovl = _nsa_tables(seq)
    qaug = _query_aug()
    const2 = lambda b, g: (0, 0)
    in_specs = [
        pl.BlockSpec((seq, 2 * LANES), lambda b, g: (b, OFF_Q // (2 * LANES) + g)),
        pl.BlockSpec((seq, LANES), lambda b, g: (b, OFF_KSW // LANES + g)),
        pl.BlockSpec((seq, LANES), lambda b, g: (b, OFF_VSW // LANES + g)),
        pl.BlockSpec((seq, LANES), lambda b, g: (b, OFF_GATE // LANES + g)),
        pl.BlockSpec((1,) + kvc.shape[1:], lambda b, g: (b, 0, 0)),
        pl.BlockSpec(wlo.shape, const2),
        pl.BlockSpec(whi.shape, const2),
        pl.BlockSpec(pelo.shape, const2),
        pl.BlockSpec(pehi.shape, const2),
        pl.BlockSpec(w2bd.shape, const2),
        pl.BlockSpec(kaug_s.shape, const2),
        pl.BlockSpec(kaug_w.shape, const2),
        pl.BlockSpec(kaug_c.shape, const2),
        pl.BlockSpec((1, 2 * GROUP, LANES), lambda b, g: (g, 0, 0)),
        pl.BlockSpec(tri.shape, const2),
        pl.BlockSpec(band.shape, const2),
        pl.BlockSpec(ovl.shape, const2),
    ]
    return pl.pallas_call(
        _nsa_kernel,
        grid=(batch, N_KV),
        in_specs=in_specs,
        out_specs=pl.BlockSpec((seq, 2 * LANES), lambda b, g: (b, g)),
        out_shape=jax.ShapeDtypeStruct((batch * seq, D_ATT), bf16),
        scratch_shapes=[
            pltpu.VMEM((N_KV, N_CMP_PAD, LANES), f32),
            pltpu.VMEM((seq, LANES), bf16),
            pltpu.VMEM((seq + WIN_PAD, LANES), bf16),
            pltpu.VMEM((seq, LANES), bf16),
            pltpu.VMEM((seq + WIN_PAD, LANES), bf16),
            pltpu.VMEM((seq, 2 * LANES), f32),
            pltpu.VMEM((seq // LANES, seq // SLC_LEN, LANES), f32),
            pltpu.VMEM((seq, LANES), f32),
        ],
        compiler_params=pltpu.CompilerParams(
            dimension_semantics=("parallel", "arbitrary"), vmem_limit_bytes=VMEM_LIMIT),
        name="nsa",
    )(proj, proj, proj, proj, kvc, wlo, whi, pelo, pehi, w2bd, kaug_s, kaug_w, kaug_c, qaug, tri, band, ovl)


def _out_kernel(x_ref, ha_ref, bg_ref, cg_ref, za_ref, hap_ref, cgp_ref, zb_ref, gm_ref, ob_ref,
                cw_ref, cb_ref, wa_ref, wb_ref, wo_ref, fnw_ref, o_ref, *, tiles_per_seq):
    tm = x_ref.shape[0]
    halo = hap_ref.shape[0]
    not_first = (pl.program_id(0) % tiles_per_seq != 0).astype(f32)
    u = cg_ref[...].astype(f32) * ha_ref[...].astype(f32)
    u_prev = cgp_ref[...].astype(f32) * hap_ref[...].astype(f32) * not_first
    row = lax.broadcasted_iota(jnp.int32, (tm, 1), 0)
    p1 = u_prev[halo - 1:halo, :]
    p2 = u_prev[halo - 2:halo - 1, :]
    u1 = jnp.where(row == 0, p1, pltpu.roll(u, 1, 0))
    u2 = jnp.where(row == 0, p2, jnp.where(row == 1, p1, pltpu.roll(u, 2, 0)))
    cw = cw_ref[...]
    y = cb_ref[...] + cw[0:1, :] * u2 + cw[1:2, :] * u1 + cw[2:3, :] * u
    za = za_ref[...].astype(f32)
    ya_in = bg_ref[...].astype(f32) * y * (za * _sigmoid(za))
    y_a = jnp.dot(ya_in.astype(bf16), wa_ref[...], preferred_element_type=f32)
    zb = zb_ref[...].astype(f32)
    yb_in = ob_ref[...].astype(f32) * (zb * _sigmoid(zb))
    y_b = jnp.dot(yb_in.astype(bf16), wb_ref[...], preferred_element_type=f32)
    mixed = (_sigmoid(gm_ref[:, 0:D_MODEL].astype(f32)) * y_a
             + _sigmoid(gm_ref[:, D_MODEL:2 * D_MODEL].astype(f32)) * y_b)
    xo = x_ref[...] + jnp.dot(mixed.astype(bf16), wo_ref[...], preferred_element_type=f32)
    r = lax.rsqrt(jnp.mean(xo * xo, axis=-1, keepdims=True) + NORM_EPS)
    o_ref[...] = (xo * r) * fnw_ref[...]


def _out(x2, proj, ob, conv_w, conv_b, wa, wb, wo, fnw, seq, tm=512, halo=16):
    n_rows = x2.shape[0]
    blk = lambda c: pl.BlockSpec((tm, D_MODEL), lambda i, c=c: (i, c))
    prev = lambda c: pl.BlockSpec((halo, D_MODEL), lambda i, c=c: (jnp.maximum(i * (tm // halo) - 1, 0), c))
    full = lambda a: pl.BlockSpec(a.shape, lambda i: (0, 0))
    return pl.pallas_call(
        functools.partial(_out_kernel, tiles_per_seq=seq // tm),
        grid=(n_rows // tm,),
        in_specs=[
            blk(0),
            blk(0), blk(1), blk(2), blk(3),
            prev(0), prev(2),
            blk(OFF_ZB // D_MODEL),
            pl.BlockSpec((tm, 2 * D_MODEL), lambda i: (i, OFF_GMIX // (2 * D_MODEL))),
            blk(0),
            full(conv_w), full(conv_b), full(wa), full(wb), full(wo), full(fnw),
        ],
        out_specs=pl.BlockSpec((tm, D_MODEL), lambda i: (i, 0)),
        out_shape=jax.ShapeDtypeStruct((n_rows, D_MODEL), f32),
        compiler_params=pltpu.CompilerParams(
            dimension_semantics=("parallel",), vmem_limit_bytes=VMEM_LIMIT),
        name="merge_out",
    )(x2, proj, proj, proj, proj, proj, proj, proj, proj, ob, conv_w, conv_b, wa, wb, wo, fnw)


def _blockdiag_w1(w1_k, w1_v, lo):
    half = CMP_STRIDE * HEAD_DIM
    wk = w1_k[lo * half:(lo + 1) * half].reshape(CMP_STRIDE, HEAD_DIM, CMP_HID)
    wv = w1_v[lo * half:(lo + 1) * half].reshape(CMP_STRIDE, HEAD_DIM, CMP_HID)
    z = jnp.zeros_like(wk)
    top = jnp.concatenate([wk, z], axis=-1)
    bot = jnp.concatenate([z, wv], axis=-1)
    return jnp.concatenate([top, bot], axis=1).reshape(CMP_STRIDE * 2 * HEAD_DIM, 2 * CMP_HID).astype(bf16)


def _pe_row(pe_k, pe_v, lo):
    sl = slice(lo * CMP_STRIDE, (lo + 1) * CMP_STRIDE)
    return jnp.concatenate([pe_k[sl], pe_v[sl]], axis=-1).reshape(1, CMP_STRIDE * 2 * HEAD_DIM)


def kernel(x, norm_w, w_in, conv_w, conv_b, cmp_pe_k, cmp_pe_v, cmp_w1_k, cmp_w2_k, cmp_w1_v, cmp_w2_v,
           w_proj_a, w_proj_b, w_out, final_norm_w):
    batch, seq, _ = x.shape
    assert norm_w.shape[0] == 1 and seq % SEL_CHUNK == 0 and seq // CMP_STRIDE == N_CMP_PAD
    idx = _column_permutation()
    w_p = jnp.where(jnp.asarray(idx >= 0)[None, :], jnp.take(w_in[0], jnp.asarray(np.maximum(idx, 0)), axis=1),
                    0.0).astype(bf16)
    x2 = x.reshape(batch * seq, D_MODEL)
    proj = _inproj(x2, norm_w, w_p)

    kvc = proj[:, OFF_KVC:OFF_KVC + N_KV * LANES].reshape(batch, seq // CMP_STRIDE, CMP_STRIDE * N_KV * LANES)
    wlo = _blockdiag_w1(cmp_w1_k[0], cmp_w1_v[0], 0)
    whi = _blockdiag_w1(cmp_w1_k[0], cmp_w1_v[0], 1)
    pelo = _pe_row(cmp_pe_k[0], cmp_pe_v[0], 0)
    pehi = _pe_row(cmp_pe_k[0], cmp_pe_v[0], 1)
    zk = jnp.zeros_like(cmp_w2_k[0])
    w2bd = jnp.concatenate([jnp.concatenate([cmp_w2_k[0], zk], axis=1),
                            jnp.concatenate([zk, cmp_w2_v[0]], axis=1)], axis=0).astype(bf16)
    ob = _nsa(proj, kvc, wlo, whi, pelo, pehi, w2bd, batch, seq)

    out = _out(x2, proj, ob, conv_w[0], conv_b, w_proj_a[0].astype(bf16), w_proj_b[0].astype(bf16),
               w_out[0].astype(bf16), final_norm_w.reshape(1, D_MODEL), seq)
    return out.reshape(batch, seq, D_MODEL)
```

```python
import functools

import numpy as np
import jax
import jax.numpy as jnp
from jax import lax
from jax.experimental import pallas as pl
from jax.experimental.pallas import tpu as pltpu

D_MODEL = 1024
D_CONV = 1024
CONV_W = 3
N_HEADS = 16
HEAD_DIM = 64
N_KV = 4
GROUP = N_HEADS // N_KV
D_ATT = N_HEADS * HEAD_DIM
D_KV = N_KV * HEAD_DIM
CMP_LEN = 32
CMP_STRIDE = 16
CMP_HID = 128
SLC_LEN = 64
N_SEL = 8
WINDOW = 512
NORM_EPS = 1e-6

LANES = 128
N_CMP_PAD = 128
MASK_BIG = float(2.0 ** 100)
LOG2E = float(np.log2(np.e))
SEL_CHUNK = 512
WIN_TQ = 128
WIN_KEYS = WINDOW + WIN_TQ
WIN_PAD = WINDOW
CW_TQ = 256

OFF_CONV = 0
OFF_Q = 4096
OFF_ZB = 5120
OFF_GMIX = 6144
OFF_KVC = 8192
OFF_KSW = 8704
OFF_VSW = 9216
OFF_GATE = 9728
N_PAD = 10240

L_HI = 64
L_LO = 67
L_FLAG = 70
L_SEL = 96

VMEM_LIMIT = 56 * 1024 * 1024

f32 = jnp.float32
bf16 = jnp.bfloat16


def _reordered_w_in(w):
    o_q = 4 * D_CONV
    o_kv = o_q + D_ATT
    o_gate = o_kv + 6 * D_KV
    o_zb = o_gate + 3 * N_HEADS
    o_gmix = o_zb + D_ATT
    kv = w[:, o_kv:o_gate].reshape(D_MODEL, 6, N_KV, HEAD_DIM)
    pair = lambda a, b: jnp.stack([kv[:, a], kv[:, b]], axis=2).reshape(D_MODEL, N_KV * LANES)
    gates = jnp.pad(w[:, o_gate:o_zb].reshape(D_MODEL, N_KV, 3 * GROUP),
                    ((0, 0), (0, 0), (0, LANES - 3 * GROUP))).reshape(D_MODEL, N_KV * LANES)
    cols = [w[:, :o_q], w[:, o_q:o_kv] * (HEAD_DIM ** -0.5 * LOG2E), w[:, o_zb:o_gmix], w[:, o_gmix:],
            pair(0, 1), pair(2, 4), pair(3, 5), gates]
    return jnp.concatenate(cols, axis=1).astype(bf16)


def _sigmoid(x):
    return 1.0 / (1.0 + jnp.exp(-x))


def _dot_nt(a, b):
    return lax.dot_general(a, b, (((1,), (1,)), ((), ())), preferred_element_type=f32)


def _inproj_kernel(x_ref, nw_ref, w_ref, o_ref, h_ref):
    @pl.when(pl.program_id(1) == 0)
    def _():
        x = x_ref[...]
        r = lax.rsqrt(jnp.mean(x * x, axis=-1, keepdims=True) + NORM_EPS)
        h_ref[...] = ((x * r) * nw_ref[...]).astype(bf16)

    o_ref[...] = jnp.dot(h_ref[...], w_ref[...], preferred_element_type=f32).astype(bf16)


def _inproj(x2, norm_w, w_p, tm=1024, tn=1024):
    n_rows = x2.shape[0]
    return pl.pallas_call(
        _inproj_kernel,
        grid=(n_rows // tm, N_PAD // tn),
        in_specs=[
            pl.BlockSpec((tm, D_MODEL), lambda i, j: (i, 0)),
            pl.BlockSpec((1, D_MODEL), lambda i, j: (0, 0)),
            pl.BlockSpec((D_MODEL, tn), lambda i, j: (0, j)),
        ],
        out_specs=pl.BlockSpec((tm, tn), lambda i, j: (i, j)),
        out_shape=jax.ShapeDtypeStruct((n_rows, N_PAD), bf16),
        scratch_shapes=[pltpu.VMEM((tm, D_MODEL), bf16)],
        compiler_params=pltpu.CompilerParams(
            dimension_semantics=("parallel", "arbitrary"), vmem_limit_bytes=VMEM_LIMIT),
        name="inproj",
    )(x2, norm_w, w_p)


def _softmax_step(s, v, carry):
    m, acc = carry
    m_new = jnp.maximum(m, jnp.max(s, axis=-1, keepdims=True))
    alpha = jnp.exp2(m - m_new)
    p = jnp.exp2(s - m_new)
    acc = alpha * acc + jnp.dot(p.astype(bf16), v, preferred_element_type=f32)
    return m_new, acc


def _stack_heads(q_tile, aug_rows, extra):
    lane = lax.broadcasted_iota(jnp.int32, (1, LANES), 1)
    pairs = (q_tile[:, :LANES], q_tile[:, LANES:])
    parts = []
    for r in range(GROUP):
        qh = pairs[r // 2]
        if r % 2 == 1:
            qh = pltpu.roll(qh, HEAD_DIM, 1)
        aug = aug_rows[r:r + 1, :]
        if extra is not None:
            aug = aug + extra
        parts.append(jnp.where(lane < HEAD_DIM, qh, aug).astype(bf16))
    return jnp.concatenate(parts, axis=0)


def _merge_heads(vals, lane):
    return (jnp.where(lane < HEAD_DIM, vals[0], vals[1]), jnp.where(lane < HEAD_DIM, vals[2], vals[3]))


def _nsa_kernel(q_ref, ksw_ref, vsw_ref, gate_ref, kvc_ref, wlo_ref, whi_ref, pelo_ref, pehi_ref, w2_ref,
                kaug_s_ref, kaug_w_ref, kaug_c_ref, qaug_ref, tri_ref, band_ref, ovl_ref,
                o_ref,
                kvcf_ref, ks_ref, kw_ref, vs_ref, vw_ref, ocw_ref, imp_ref, selq_ref):
    seq = q_ref.shape[0]
    lane = lax.broadcasted_iota(jnp.int32, (1, LANES), 1)
    low = lane < HEAD_DIM

    kvcf_ref[...] = kvc_ref[...].astype(f32)
    xcat = jnp.concatenate([kvcf_ref[pl.ds(l, N_CMP_PAD, stride=CMP_STRIDE), :] for l in range(CMP_STRIDE)],
                           axis=1)
    xa = (xcat + pelo_ref[...]).astype(bf16)
    xb = (xcat + pehi_ref[...]).astype(bf16)
    a = jnp.dot(xa, wlo_ref[...], preferred_element_type=f32)
    bm = jnp.dot(xb, whi_ref[...], preferred_element_type=f32)
    h = a + pltpu.roll(bm, N_CMP_PAD - 1, 0)
    hid = h * _sigmoid(h)
    kcvc = jnp.dot(hid.astype(bf16), w2_ref[...], preferred_element_type=f32)

    ksw = ksw_ref[...]
    vsw = vsw_ref[...]
    one = jnp.ones((1, LANES), bf16)
    ks_ref[...] = jnp.where(low, ksw, kaug_s_ref[...])
    kwin = pltpu.roll(ksw.astype(f32), HEAD_DIM, 1).astype(bf16)
    kw_ref[0:WIN_PAD, :] = kaug_w_ref[0:WIN_PAD, :]
    kw_ref[WIN_PAD:, :] = jnp.where(low, kwin, kaug_w_ref[WIN_PAD:, :])
    vs_ref[...] = jnp.where(low, vsw, one)
    vw_ref[0:WIN_PAD, :] = jnp.zeros((WIN_PAD, LANES), bf16)
    vw_ref[WIN_PAD:, :] = jnp.where(low, one, vsw)

    qaug = qaug_ref[0]
    kc_aug = jnp.where(low, kcvc, kaug_c_ref[...]).astype(bf16)
    vc_rows = jnp.concatenate([jnp.zeros((N_CMP_PAD, LANES), bf16), kcvc.astype(bf16)], axis=1)
    zeros_w = jnp.zeros((WIN_KEYS, LANES), bf16)
    ovl = ovl_ref[...]

    def cw_body(i, _):
        t0 = pl.multiple_of(i * CW_TQ, CW_TQ)
        sig = _sigmoid(gate_ref[pl.ds(t0, CW_TQ), :].astype(f32))
        o_w, o_c = [], []
        for h in range(CW_TQ // WIN_TQ):
            tw = pl.multiple_of(t0 + h * WIN_TQ, WIN_TQ)
            qw = _stack_heads(q_ref[pl.ds(tw, WIN_TQ), :].astype(f32), qaug[GROUP:2 * GROUP], None)
            k_all = jnp.concatenate([kw_ref[pl.ds(tw, WIN_KEYS), :], kc_aug], axis=0)
            s = _dot_nt(qw, k_all)

            sw = s[:, :WIN_KEYS] + band_ref[...]
            p_w = jnp.exp2(sw - jnp.max(sw, axis=-1, keepdims=True))

            shp_c = (GROUP * WIN_TQ, N_CMP_PAD)
            t_idx = tw + (lax.broadcasted_iota(jnp.int32, shp_c, 0) & (WIN_TQ - 1))
            c_end = lax.broadcasted_iota(jnp.int32, shp_c, 1) * CMP_STRIDE + (CMP_LEN - 1)
            ok = c_end <= t_idx
            sm = jnp.where(ok, s[:, WIN_KEYS:], -jnp.inf)
            m = jnp.max(sm, axis=-1, keepdims=True)
            m = jnp.where(m > -jnp.inf, m, 0.0)
            e = jnp.where(ok, jnp.exp2(sm - m), 0.0)
            p_c = e / jnp.maximum(jnp.sum(e, axis=-1, keepdims=True), 1e-30)

            p_all = jnp.concatenate([p_w.astype(bf16), p_c.astype(bf16)], axis=1)
            v_all = jnp.concatenate(
                [jnp.concatenate([vw_ref[pl.ds(tw, WIN_KEYS), :], zeros_w], axis=1), vc_rows], axis=0)
            acc = jnp.dot(p_all, v_all, preferred_element_type=f32)
            acc_w = acc[:, :LANES]
            o_w.append(acc_w / jnp.maximum(acc_w[:, 0:1], 1e-30))
            o_c.append(acc[:, LANES:])

            ps = p_c[0:WIN_TQ] + p_c[WIN_TQ:2 * WIN_TQ] + p_c[2 * WIN_TQ:3 * WIN_TQ] + p_c[3 * WIN_TQ:4 * WIN_TQ]
            p1 = ps.astype(bf16)
            r1 = ps - p1.astype(f32)
            p2 = r1.astype(bf16)
            p3 = (r1 - p2.astype(f32)).astype(bf16)
            imp_ref[i * (CW_TQ // WIN_TQ) + h] = _dot_nt(ovl, p1) + _dot_nt(ovl, p2) + _dot_nt(ovl, p3)

        vals = []
        for r in range(GROUP):
            head = lambda parts: jnp.concatenate([o[r * WIN_TQ:(r + 1) * WIN_TQ, :] for o in parts], axis=0)
            a = head(o_c) * sig[:, 3 * r:3 * r + 1] + head(o_w) * sig[:, 3 * r + 2:3 * r + 3]
            vals.append(pltpu.roll(a, HEAD_DIM, 1) if r % 2 == 0 else a)
        c0, c1 = _merge_heads(vals, lane)
        ocw_ref[pl.ds(t0, CW_TQ), 0:LANES] = c0
        ocw_ref[pl.ds(t0, CW_TQ), LANES:2 * LANES] = c1
        return 0

    lax.fori_loop(0, seq // CW_TQ, cw_body, 0)

    n_slc = seq // SLC_LEN
    n_tt = seq // LANES
    shp = (n_tt, n_slc, LANES)
    j_idx = lax.broadcasted_iota(jnp.int32, shp, 1)
    t_idx = lax.broadcasted_iota(jnp.int32, shp, 0) * LANES + lax.broadcasted_iota(jnp.int32, shp, 2)
    cur = t_idx // SLC_LEN
    forced = (j_idx == 0) | (j_idx == cur) | (j_idx == cur - 1)
    val = jnp.where(forced, jnp.inf, jnp.where(j_idx <= cur, imp_ref[...], -jnp.inf))
    rank = jnp.zeros(shp, jnp.int32)
    for jp in range(n_slc):
        vj = val[:, jp:jp + 1, :]
        beats = (vj > val) | ((vj == val) & (j_idx > jp))
        rank = rank + beats.astype(jnp.int32)
    sel = (rank < N_SEL) & (val > -jnp.inf)
    notsel = jnp.where(sel, 0.0, 1.0)
    eye = (lax.broadcasted_iota(jnp.int32, (LANES, LANES), 0)
           == lax.broadcasted_iota(jnp.int32, (LANES, LANES), 1)).astype(bf16)
    zpad = jnp.zeros((L_SEL, LANES), f32)
    for tt in range(n_tt):
        z = jnp.concatenate([zpad, notsel[tt]], axis=0).astype(bf16)
        selq_ref[tt * LANES:(tt + 1) * LANES, :] = _dot_nt(eye, z) * (-MASK_BIG)

    def stacked_q(i):
        tok = slice(i * SEL_CHUNK, (i + 1) * SEL_CHUNK)
        return _stack_heads(q_ref[tok, :].astype(f32), qaug[0:GROUP], selq_ref[tok, :])

    def scores(qs, i, c):
        s = _dot_nt(qs, ks_ref[c * SEL_CHUNK:(c + 1) * SEL_CHUNK, :])
        if c == i:
            tri = tri_ref[...]
            s = jnp.concatenate([s[r * SEL_CHUNK:(r + 1) * SEL_CHUNK, :] + tri for r in range(GROUP)], axis=0)
        return s

    def finish(i, acc):
        tok = slice(i * SEL_CHUNK, (i + 1) * SEL_CHUNK)
        o_s = acc / jnp.maximum(acc[:, HEAD_DIM:HEAD_DIM + 1], 1e-30)
        sig = _sigmoid(gate_ref[tok, :].astype(f32))
        vals = []
        for r in range(GROUP):
            a_s = o_s[r * SEL_CHUNK:(r + 1) * SEL_CHUNK, :] * sig[:, 3 * r + 1:3 * r + 2]
            vals.append(a_s if r % 2 == 0 else pltpu.roll(a_s, HEAD_DIM, 1))
        c0, c1 = _merge_heads(vals, lane)
        o_ref[tok, 0:LANES] = (ocw_ref[tok, 0:LANES] + c0).astype(bf16)
        o_ref[tok, LANES:2 * LANES] = (ocw_ref[tok, LANES:2 * LANES] + c1).astype(bf16)

    steps = [(i, c) for i in range(seq // SEL_CHUNK) for c in range(i + 1)]
    qs = stacked_q(0)
    s_next = scores(qs, 0, 0)
    carry = None
    for k, (i, c) in enumerate(steps):
        s = s_next
        if k + 1 < len(steps):
            i2, c2 = steps[k + 1]
            if i2 != i:
                qs = stacked_q(i2)
            s_next = scores(qs, i2, c2)
        v = vs_ref[c * SEL_CHUNK:(c + 1) * SEL_CHUNK, :]
        if c == 0:
            m = jnp.max(s, axis=-1, keepdims=True)
            carry = (m, jnp.dot(jnp.exp2(s - m).astype(bf16), v, preferred_element_type=f32))
        else:
            carry = _softmax_step(s, v, carry)
        if c == i:
            finish(i, carry[1])


def _position_lanes(pos):
    out = np.zeros((pos.shape[0], LANES), np.float32)
    out[:, L_HI:L_HI + 3] = ((pos // 64) * 64)[:, None]
    out[:, L_LO:L_LO + 3] = (pos % 64)[:, None]
    return out


def _nsa_tables(seq):
    pos = np.arange(seq)
    kaug_s = _position_lanes(pos)
    kaug_s[pos, L_SEL + pos // SLC_LEN] = 1.0
    kaug_w = np.zeros((seq + WIN_PAD, LANES), np.float32)
    kaug_w[WIN_PAD:] = _position_lanes(pos)
    kaug_w[:WIN_PAD, L_FLAG] = 1.0
    kaug_c = _position_lanes(np.arange(N_CMP_PAD) * CMP_STRIDE + CMP_LEN - 1)
    tl = np.arange(SEL_CHUNK)[:, None]
    kk = np.arange(SEL_CHUNK)[None, :]
    tri = np.where(kk <= tl, 0.0, -MASK_BIG).astype(np.float32)
    tl = np.arange(WIN_TQ)[:, None]
    kk = np.arange(WIN_KEYS)[None, :]
    dist = tl + WIN_PAD - kk
    band = np.tile(np.where((dist >= 0) & (dist < WINDOW), 0.0, -MASK_BIG).astype(np.float32), (GROUP, 1))
    c = np.arange(N_CMP_PAD)[None, :]
    j = np.arange(seq // SLC_LEN)[:, None]
    c_start, c_end, s_start = c * CMP_STRIDE, c * CMP_STRIDE + CMP_LEN - 1, j * SLC_LEN
    ovl = ((c_start <= s_start + SLC_LEN - 1) & (c_end >= s_start) & (c < seq // CMP_STRIDE - 1)).astype(np.float32)
    return (jnp.asarray(kaug_s, bf16), jnp.asarray(kaug_w, bf16), jnp.asarray(kaug_c, f32),
            jnp.asarray(tri), jnp.asarray(band), jnp.asarray(ovl, bf16))


def _query_aug():
    slopes = jnp.exp2(-8.0 * jnp.arange(1, N_HEADS + 1, dtype=f32) / N_HEADS).reshape(N_KV, GROUP) * LOG2E
    hi =slopes.astype(bf16).astype(f32)
    mid = (slopes - hi).astype(bf16).astype(f32)
    lo = (slopes - hi - mid).astype(bf16).astype(f32)
    pieces = jnp.stack([hi, mid, lo, hi, mid, lo], axis=-1)
    base = jnp.zeros((N_KV, GROUP, LANES), f32).at[:, :, L_HI:L_HI + 6].set(pieces)
    return jnp.concatenate([base, base.at[:, :, L_FLAG].set(-MASK_BIG)], axis=1)


def _nsa(proj, wlo, whi, pelo, pehi, w2bd, batch, seq):
    kaug_s, kaug_w, kaug_c, tri, band, ovl = _nsa_tables(seq)
    qaug = _query_aug()
    const2 = lambda b, g: (0, 0)
    in_specs = [
        pl.BlockSpec((seq, 2 * LANES), lambda b, g: (b, OFF_Q // (2 * LANES) + g)),
        pl.BlockSpec((seq, LANES), lambda b, g: (b, OFF_KSW // LANES + g)),
        pl.BlockSpec((seq, LANES), lambda b, g: (b, OFF_VSW // LANES + g)),
        pl.BlockSpec((seq, LANES), lambda b, g: (b, OFF_GATE // LANES + g)),
        pl.BlockSpec((seq, LANES), lambda b, g: (b, OFF_KVC // LANES + g)),
        pl.BlockSpec(wlo.shape, const2),
        pl.BlockSpec(whi.shape, const2),
        pl.BlockSpec(pelo.shape, const2),
        pl.BlockSpec(pehi.shape, const2),
        pl.BlockSpec(w2bd.shape, const2),
        pl.BlockSpec(kaug_s.shape, const2),
        pl.BlockSpec(kaug_w.shape, const2),
        pl.BlockSpec(kaug_c.shape, const2),
        pl.BlockSpec((1, 2 * GROUP, LANES), lambda b, g: (g, 0, 0)),
        pl.BlockSpec(tri.shape, const2),
        pl.BlockSpec(band.shape, const2),
        pl.BlockSpec(ovl.shape, const2),
    ]
    return pl.pallas_call(
        _nsa_kernel,
        grid=(batch, N_KV),
        in_specs=in_specs,
        out_specs=pl.BlockSpec((seq, 2 * LANES), lambda b, g: (b, g)),
        out_shape=jax.ShapeDtypeStruct((batch * seq, D_ATT), bf16),
        scratch_shapes=[
            pltpu.VMEM((seq, LANES), f32),
            pltpu.VMEM((seq, LANES), bf16),
            pltpu.VMEM((seq + WIN_PAD, LANES), bf16),
            pltpu.VMEM((seq, LANES), bf16),
            pltpu.VMEM((seq + WIN_PAD, LANES), bf16),
            pltpu.VMEM((seq, 2 * LANES), f32),
            pltpu.VMEM((seq // LANES, seq // SLC_LEN, LANES), f32),
            pltpu.VMEM((seq, LANES), f32),
        ],
        compiler_params=pltpu.CompilerParams(
            dimension_semantics=("parallel", "parallel"), vmem_limit_bytes=VMEM_LIMIT),
        name="nsa",
    )(proj, proj, proj, proj, proj, wlo, whi, pelo, pehi, w2bd, kaug_s, kaug_w, kaug_c, qaug, tri, band, ovl)


def _out_kernel(x_ref, ha_ref, bg_ref, cg_ref, za_ref, hap_ref, cgp_ref, zb_ref, gm_ref, ob_ref,
                cw_ref, cb_ref, wa_ref, wb_ref, wo_ref, fnw_ref, o_ref, *, tiles_per_seq):
    tm = x_ref.shape[0]
    halo = hap_ref.shape[0]
    not_first = (pl.program_id(0) % tiles_per_seq != 0).astype(f32)
    u = cg_ref[...].astype(f32) * ha_ref[...].astype(f32)
    u_prev = cgp_ref[...].astype(f32) * hap_ref[...].astype(f32) * not_first
    row = lax.broadcasted_iota(jnp.int32, (tm, 1), 0)
    p1 = u_prev[halo - 1:halo, :]
    p2 = u_prev[halo - 2:halo - 1, :]
    u1 = jnp.where(row == 0, p1, pltpu.roll(u, 1, 0))
    u2 = jnp.where(row == 0, p2, jnp.where(row == 1, p1, pltpu.roll(u, 2, 0)))
    cw = cw_ref[...]
    y = cb_ref[...] + cw[0:1, :] * u2 + cw[1:2, :] * u1 + cw[2:3, :] * u
    za = za_ref[...].astype(f32)
    ya_in = bg_ref[...].astype(f32) * y * (za * _sigmoid(za))
    y_a = jnp.dot(ya_in.astype(bf16), wa_ref[...], preferred_element_type=f32)
    zb = zb_ref[...].astype(f32)
    yb_in = ob_ref[...].astype(f32) * (zb * _sigmoid(zb))
    y_b = jnp.dot(yb_in.astype(bf16), wb_ref[...], preferred_element_type=f32)
    mixed = (_sigmoid(gm_ref[:, 0:D_MODEL].astype(f32)) * y_a
             + _sigmoid(gm_ref[:, D_MODEL:2 * D_MODEL].astype(f32)) * y_b)
    xo = x_ref[...] + jnp.dot(mixed.astype(bf16), wo_ref[...], preferred_element_type=f32)
    r = lax.rsqrt(jnp.mean(xo * xo, axis=-1, keepdims=True) + NORM_EPS)
    o_ref[...] = (xo * r) * fnw_ref[...]


def _out(x2, proj, ob, conv_w, conv_b, wa, wb, wo, fnw, seq, tm=512, halo=16):
    n_rows = x2.shape[0]
    blk = lambda c: pl.BlockSpec((tm, D_MODEL), lambda i, c=c: (i, c))
    prev = lambda c: pl.BlockSpec((halo, D_MODEL), lambda i, c=c: (jnp.maximum(i * (tm // halo) - 1, 0), c))
    full = lambda a: pl.BlockSpec(a.shape, lambda i: (0, 0))
    return pl.pallas_call(
        functools.partial(_out_kernel, tiles_per_seq=seq // tm),
        grid=(n_rows // tm,),
        in_specs=[
            blk(0),
            blk(0), blk(1), blk(2), blk(3),
            prev(0), prev(2),
            blk(OFF_ZB // D_MODEL),
            pl.BlockSpec((tm, 2 * D_MODEL), lambda i: (i, OFF_GMIX // (2 * D_MODEL))),
            blk(0),
            full(conv_w), full(conv_b), full(wa), full(wb), full(wo), full(fnw),
        ],
        out_specs=pl.BlockSpec((tm, D_MODEL), lambda i: (i, 0)),
        out_shape=jax.ShapeDtypeStruct((n_rows, D_MODEL), f32),
        compiler_params=pltpu.CompilerParams(
            dimension_semantics=("parallel",), vmem_limit_bytes=VMEM_LIMIT),
        name="merge_out",
    )(x2, proj, proj, proj, proj, proj, proj, proj, proj, ob, conv_w, conv_b, wa, wb, wo, fnw)


def _blockdiag_w1(w1_k, w1_v, lo):
    half = CMP_STRIDE * HEAD_DIM
    wk = w1_k[lo * half:(lo + 1) * half].reshape(CMP_STRIDE, HEAD_DIM, CMP_HID)
    wv = w1_v[lo * half:(lo + 1) * half].reshape(CMP_STRIDE, HEAD_DIM, CMP_HID)
    z = jnp.zeros_like(wk)
    top = jnp.concatenate([wk, z], axis=-1)
    bot = jnp.concatenate([z, wv], axis=-1)
    return jnp.concatenate([top, bot], axis=1).reshape(CMP_STRIDE * 2 * HEAD_DIM, 2 * CMP_HID).astype(bf16)


def _pe_row(pe_k, pe_v, lo):
    sl = slice(lo * CMP_STRIDE, (lo + 1) * CMP_STRIDE)
    return jnp.concatenate([pe_k[sl], pe_v[sl]], axis=-1).reshape(1, CMP_STRIDE * 2 * HEAD_DIM)


def kernel(x, norm_w, w_in, conv_w, conv_b, cmp_pe_k, cmp_pe_v, cmp_w1_k, cmp_w2_k, cmp_w1_v, cmp_w2_v,
           w_proj_a, w_proj_b, w_out, final_norm_w):
    batch, seq, _ = x.shape
    assert norm_w.shape[0] == 1 and seq % SEL_CHUNK == 0 and seq // CMP_STRIDE == N_CMP_PAD
    w_p = _reordered_w_in(w_in[0])
    assert w_p.shape[1] == N_PAD
    x2 = x.reshape(batch * seq, D_MODEL)
    proj = _inproj(x2, norm_w, w_p)

    wlo = _blockdiag_w1(cmp_w1_k[0], cmp_w1_v[0], 0)
    whi = _blockdiag_w1(cmp_w1_k[0], cmp_w1_v[0], 1)
    pelo = _pe_row(cmp_pe_k[0], cmp_pe_v[0], 0)
    pehi = _pe_row(cmp_pe_k[0], cmp_pe_v[0], 1)
    zk = jnp.zeros_like(cmp_w2_k[0])
    w2bd = jnp.concatenate([jnp.concatenate([cmp_w2_k[0], zk], axis=1),
                            jnp.concatenate([zk, cmp_w2_v[0]], axis=1)], axis=0).astype(bf16)
    ob = _nsa(proj, wlo, whi, pelo, pehi, w2bd, batch, seq)

    out = _out(x2, proj, ob, conv_w[0], conv_b, w_proj_a[0].astype(bf16), w_proj_b[0].astype(bf16),
               w_out[0].astype(bf16), final_norm_w.reshape(1, D_MODEL), seq)
    return out.reshape(batch, seq, D_MODEL)
```

```python
import functools

import numpy as np
import jax
import jax.numpy as jnp
from jax import lax
from jax.experimental import pallas as pl
from jax.experimental.pallas import tpu as pltpu

D_MODEL = 1024
D_CONV = 1024
CONV_W = 3
N_HEADS = 16
HEAD_DIM = 64
N_KV = 4
GROUP = N_HEADS // N_KV
D_ATT = N_HEADS * HEAD_DIM
D_KV = N_KV * HEAD_DIM
CMP_LEN = 32
CMP_STRIDE = 16
CMP_HID = 128
SLC_LEN = 64
N_SEL = 8
WINDOW = 512
NORM_EPS = 1e-6

LANES = 128
N_CMP_PAD = 128
MASK_BIG = float(2.0 ** 100)
LOG2E = float(np.log2(np.e))
SEL_CHUNK = 512
WIN_TQ = 128
WIN_KEYS = WINDOW + WIN_TQ
WIN_PAD = WINDOW
CW_TQ = 512

OFF_CONV = 0
OFF_Q = 4096
OFF_ZB = 5120
OFF_GMIX = 6144
OFF_KVC = 8192
OFF_KSW = 8704
OFF_VSW = 9216
OFF_GATE = 9728
N_PAD = 10240

L_HI = 64
L_LO = 67
L_FLAG = 70
L_SEL = 96

VMEM_LIMIT = 56 * 1024 * 1024

f32 = jnp.float32
bf16 = jnp.bfloat16


def _reordered_w_in(w):
    o_q = 4 * D_CONV
    o_kv = o_q + D_ATT
    o_gate = o_kv + 6 * D_KV
    o_zb = o_gate + 3 * N_HEADS
    o_gmix = o_zb + D_ATT
    kv = w[:, o_kv:o_gate].reshape(D_MODEL, 6, N_KV, HEAD_DIM)
    pair = lambda a, b: jnp.stack([kv[:, a], kv[:, b]], axis=2).reshape(D_MODEL, N_KV * LANES)
    gates = jnp.pad(w[:, o_gate:o_zb].reshape(D_MODEL, N_KV, 3 * GROUP),
                    ((0, 0), (0, 0), (0, LANES - 3 * GROUP))).reshape(D_MODEL, N_KV * LANES)
    cols = [w[:, :o_q], w[:, o_q:o_kv] * (HEAD_DIM ** -0.5 * LOG2E), w[:, o_zb:o_gmix], w[:, o_gmix:],
            pair(0, 1), pair(2, 4), pair(3, 5), gates]
    return jnp.concatenate(cols, axis=1).astype(bf16)


def _sigmoid(x):
    return 1.0 / (1.0 + jnp.exp(-x))


def _dot_nt(a, b):
    return lax.dot_general(a, b, (((1,), (1,)), ((), ())), preferred_element_type=f32)


def _inproj_kernel(x_ref, nw_ref, w_ref, o_ref, h_ref):
    @pl.when(pl.program_id(1) == 0)
    def _():
        x = x_ref[...]
        r = lax.rsqrt(jnp.mean(x * x, axis=-1, keepdims=True) + NORM_EPS)
        h_ref[...] = ((x * r) * nw_ref[...]).astype(bf16)

    acc = jnp.dot(h_ref[...], w_ref[...], preferred_element_type=f32)
    tn = o_ref.shape[1]
    j = pl.program_id(1)
    is_silu = (j == (OFF_CONV + 3 * D_CONV) // tn) | (j == OFF_ZB // tn)
    is_sigmoid = (j == OFF_GMIX // tn) | (j == OFF_GMIX // tn + 1)
    is_tail = j == OFF_GATE // tn
    split = OFF_GATE % tn

    @pl.when(is_silu)
    def _():
        o_ref[...] = (acc * _sigmoid(acc)).astype(bf16)

    @pl.when(is_sigmoid)
    def _():
        o_ref[...] = _sigmoid(acc).astype(bf16)

    @pl.when(is_tail)
    def _():
        o_ref[:, :split] = acc[:, :split].astype(bf16)
        o_ref[:, split:] = _sigmoid(acc[:, split:]).astype(bf16)

    @pl.when(jnp.logical_not(is_silu | is_sigmoid | is_tail))
    def _():
        o_ref[...] = acc.astype(bf16)


def _inproj(x2, norm_w, w_p, tm=2048, tn=1024):
    n_rows = x2.shape[0]
    assert D_CONV % tn == 0 and OFF_ZB % tn == 0 and OFF_GMIX % tn == 0 and D_MODEL == tn and N_PAD - OFF_GATE < tn
    return pl.pallas_call(
        _inproj_kernel,
        grid=(n_rows // tm, N_PAD // tn),
        in_specs=[
            pl.BlockSpec((tm, D_MODEL), lambda i, j: (i, 0)),
            pl.BlockSpec((1, D_MODEL), lambda i, j: (0, 0)),
            pl.BlockSpec((D_MODEL, tn), lambda i, j: (0, j)),
        ],
        out_specs=pl.BlockSpec((tm, tn), lambda i, j: (i, j)),
        out_shape=jax.ShapeDtypeStruct((n_rows, N_PAD), bf16),
        scratch_shapes=[pltpu.VMEM((tm, D_MODEL), bf16)],
        compiler_params=pltpu.CompilerParams(
            dimension_semantics=("parallel", "arbitrary"), vmem_limit_bytes=VMEM_LIMIT),
        name="inproj",
    )(x2, norm_w, w_p)


def _softmax_step(s, v, carry):
    m, acc = carry
    m_new = jnp.maximum(m, jnp.max(s, axis=-1, keepdims=True))
    alpha = jnp.exp2(m - m_new)
    p = jnp.exp2(s - m_new)
    acc = alpha * acc + jnp.dot(p.astype(bf16), v, preferred_element_type=f32)
    return m_new, acc


def _stack_heads(q_tile, aug_rows, extra):
    lane = lax.broadcasted_iota(jnp.int32, (1, LANES), 1)
    pairs = (q_tile[:, :LANES], q_tile[:, LANES:])
    parts = []
    for r in range(GROUP):
        qh = pairs[r // 2]
        if r % 2 == 1:
            qh = pltpu.roll(qh, HEAD_DIM, 1)
        aug = aug_rows[r:r + 1, :]
        if extra is not None:
            aug = aug + extra
        parts.append(jnp.where(lane < HEAD_DIM, qh, aug).astype(bf16))
    return jnp.concatenate(parts, axis=0)


def _merge_heads(vals, lane):
    return (jnp.where(lane < HEAD_DIM, vals[0], vals[1]), jnp.where(lane < HEAD_DIM, vals[2], vals[3]))


def _nsa_kernel(q_ref, ksw_ref, vsw_ref, gate_ref, kvc_ref, wlo_ref, whi_ref, pelo_ref, pehi_ref, w2_ref,
                kaug_s_ref, kaug_w_ref, kaug_c_ref, qaug_ref, tri_ref, band_ref, ovl_ref,
                o_ref,
                kvcf_ref, ks_ref, kw_ref, vs_ref, vw_ref, ocw_ref, imp_ref, selq_ref):
    seq = q_ref.shape[0]
    lane = lax.broadcasted_iota(jnp.int32, (1, LANES), 1)
    low = lane < HEAD_DIM

    kvcf_ref[...] = kvc_ref[...].astype(f32)
    xcat = jnp.concatenate([kvcf_ref[pl.ds(l, N_CMP_PAD, stride=CMP_STRIDE), :] for l in range(CMP_STRIDE)],
                           axis=1)
    xa = (xcat + pelo_ref[...]).astype(bf16)
    xb = (xcat + pehi_ref[...]).astype(bf16)
    a = jnp.dot(xa, wlo_ref[...], preferred_element_type=f32)
    bm = jnp.dot(xb, whi_ref[...], preferred_element_type=f32)
    h = a + pltpu.roll(bm, N_CMP_PAD - 1, 0)
    hid = h * _sigmoid(h)
    kcvc = jnp.dot(hid.astype(bf16), w2_ref[...], preferred_element_type=f32)

    ksw = ksw_ref[...]
    vsw = vsw_ref[...]
    one = jnp.ones((1, LANES), bf16)
    ks_ref[...] = jnp.where(low, ksw, kaug_s_ref[...])
    kwin = pltpu.roll(ksw.astype(f32), HEAD_DIM, 1).astype(bf16)
    kw_ref[0:WIN_PAD, :] = kaug_w_ref[0:WIN_PAD, :]
    kw_ref[WIN_PAD:, :] = jnp.where(low, kwin, kaug_w_ref[WIN_PAD:, :])
    vs_ref[...] = jnp.where(low, vsw, one)
    vw_ref[0:WIN_PAD, :] = jnp.zeros((WIN_PAD, LANES), bf16)
    vw_ref[WIN_PAD:, :] = jnp.where(low, one, vsw)

    qaug = qaug_ref[0]
    kc_aug = jnp.where(low, kcvc, kaug_c_ref[...]).astype(bf16)
    vc_rows = jnp.concatenate([jnp.zeros((N_CMP_PAD, LANES), bf16), kcvc.astype(bf16)], axis=1)
    zeros_w = jnp.zeros((WIN_KEYS, LANES), bf16)
    ovl = ovl_ref[...]

    def cw_body(i, _):
        t0 = pl.multiple_of(i * CW_TQ, CW_TQ)
        sig = gate_ref[pl.ds(t0, CW_TQ), :].astype(f32)
        o_w, o_c = [], []
        for h in range(CW_TQ // WIN_TQ):
            tw = pl.multiple_of(t0 + h * WIN_TQ, WIN_TQ)
            qw = _stack_heads(q_ref[pl.ds(tw, WIN_TQ), :].astype(f32), qaug[GROUP:2 * GROUP], None)
            k_all = jnp.concatenate([kw_ref[pl.ds(tw, WIN_KEYS), :], kc_aug], axis=0)
            s = _dot_nt(qw, k_all)

            sw = s[:, :WIN_KEYS] + band_ref[...]
            p_w = jnp.exp2(sw - jnp.max(sw, axis=-1, keepdims=True))

            shp_c = (GROUP * WIN_TQ, N_CMP_PAD)
            t_idx = tw + (lax.broadcasted_iota(jnp.int32, shp_c, 0) & (WIN_TQ - 1))
            c_end = lax.broadcasted_iota(jnp.int32, shp_c, 1) * CMP_STRIDE + (CMP_LEN - 1)
            ok = c_end <= t_idx
            sm = jnp.where(ok, s[:, WIN_KEYS:], -jnp.inf)
            m = jnp.max(sm, axis=-1, keepdims=True)
            m = jnp.where(m > -jnp.inf, m, 0.0)
            e = jnp.where(ok, jnp.exp2(sm - m), 0.0)
            p_c = e / jnp.maximum(jnp.sum(e, axis=-1, keepdims=True), 1e-30)

            p_all = jnp.concatenate([p_w.astype(bf16), p_c.astype(bf16)], axis=1)
            v_all = jnp.concatenate(
                [jnp.concatenate([vw_ref[pl.ds(tw, WIN_KEYS), :], zeros_w], axis=1), vc_rows], axis=0)
            acc = jnp.dot(p_all, v_all, preferred_element_type=f32)
            acc_w = acc[:, :LANES]
            o_w.append(acc_w / jnp.maximum(acc_w[:, 0:1], 1e-30))
            o_c.append(acc[:, LANES:])

            ps = p_c[0:WIN_TQ] + p_c[WIN_TQ:2 * WIN_TQ] + p_c[2 * WIN_TQ:3 * WIN_TQ] + p_c[3 * WIN_TQ:4 * WIN_TQ]
            p1 = ps.astype(bf16)
            r1 = ps - p1.astype(f32)
            p2 = r1.astype(bf16)
            p3 = (r1 - p2.astype(f32)).astype(bf16)
            imp_ref[i * (CW_TQ // WIN_TQ) + h] = _dot_nt(ovl, p1) + _dot_nt(ovl, p2) + _dot_nt(ovl, p3)

        vals = []
        for r in range(GROUP):
            head = lambda parts: jnp.concatenate([o[r * WIN_TQ:(r + 1) * WIN_TQ, :] for o in parts], axis=0)
            a = head(o_c) * sig[:, 3 * r:3 * r + 1] + head(o_w) * sig[:, 3 * r + 2:3 * r + 3]
            vals.append(pltpu.roll(a, HEAD_DIM, 1) if r % 2 == 0 else a)
        c0, c1 = _merge_heads(vals, lane)
        ocw_ref[pl.ds(t0, CW_TQ), 0:LANES] = c0
        ocw_ref[pl.ds(t0, CW_TQ), LANES:2 * LANES] = c1
        return 0

    lax.fori_loop(0, seq // CW_TQ, cw_body, 0)

    n_slc = seq // SLC_LEN
    n_tt = seq // LANES
    shp = (n_tt, n_slc, LANES)
    j_idx = lax.broadcasted_iota(jnp.int32, shp, 1)
    t_idx = lax.broadcasted_iota(jnp.int32, shp, 0) * LANES + lax.broadcasted_iota(jnp.int32, shp, 2)
    cur = t_idx // SLC_LEN
    forced = (j_idx == 0) | (j_idx == cur) | (j_idx == cur - 1)
    val = jnp.where(forced, jnp.inf, jnp.where(j_idx <= cur, imp_ref[...], -jnp.inf))
    rank = jnp.zeros(shp, jnp.int32)
    for jp in range(n_slc):
        vj = val[:, jp:jp + 1, :]
        beats = (vj > val) | ((vj == val) & (j_idx > jp))
        rank = rank + beats.astype(jnp.int32)
    sel = (rank < N_SEL) & (val > -jnp.inf)
    notsel = jnp.where(sel, 0.0, 1.0)
    eye = (lax.broadcasted_iota(jnp.int32, (LANES, LANES), 0)
           == lax.broadcasted_iota(jnp.int32, (LANES, LANES), 1)).astype(bf16)
    zpad = jnp.zeros((L_SEL, LANES), f32)
    for tt in range(n_tt):
        z = jnp.concatenate([zpad, notsel[tt]], axis=0).astype(bf16)
        selq_ref[tt * LANES:(tt + 1) * LANES, :] = _dot_nt(eye, z) * (-MASK_BIG)

    half = SEL_CHUNK // 2
    half_rows = GROUP * half

    def stacked_q(i):
        parts = []
        for h in range(2):
            tok = slice(i * SEL_CHUNK + h * half, i * SEL_CHUNK + (h + 1) * half)
            parts.append(_stack_heads(q_ref[tok, :].astype(f32), qaug[0:GROUP], selq_ref[tok, :]))
        return parts

    def causal(s):
        tri = tri_ref[...]
        return jnp.concatenate([s[r * half:(r + 1) * half, :] + tri for r in range(GROUP)], axis=0)

    def scores(qs, i, c):
        k0 = c * SEL_CHUNK
        if c < i:
            return (_dot_nt(jnp.concatenate(qs, axis=0), ks_ref[k0:k0 + SEL_CHUNK, :]),)
        s_a = causal(_dot_nt(qs[0], ks_ref[k0:k0 + half, :]))
        s_b = _dot_nt(qs[1], ks_ref[k0:k0 + SEL_CHUNK, :])
        return (s_a, jnp.concatenate([s_b[:, :half], causal(s_b[:, half:])], axis=1))

    def softmax_step(s, v, carry):
        if carry is None:
            m = jnp.max(s, axis=-1, keepdims=True)
            return m, jnp.dot(jnp.exp2(s - m).astype(bf16), v, preferred_element_type=f32)
        return _softmax_step(s, v, carry)

    def finish(i, acc):
        tok = slice(i * SEL_CHUNK, (i + 1) * SEL_CHUNK)
        o_s = acc / jnp.maximum(acc[:, HEAD_DIM:HEAD_DIM + 1], 1e-30)
        sig = gate_ref[tok, :].astype(f32)
        vals = []
        for r in range(GROUP):
            a_s = jnp.concatenate([o_s[h * half_rows + r * half:h * half_rows + (r + 1) * half, :]
                                   for h in range(2)], axis=0) * sig[:, 3 * r + 1:3 * r + 2]
            vals.append(a_s if r % 2 == 0 else pltpu.roll(a_s, HEAD_DIM, 1))
        c0, c1 = _merge_heads(vals, lane)
        o_ref[tok, 0:LANES] = (ocw_ref[tok, 0:LANES] + c0).astype(bf16)
        o_ref[tok, LANES:2 * LANES] = (ocw_ref[tok, LANES:2 * LANES] + c1).astype(bf16)

    steps = [(i, c) for i in range(seq // SEL_CHUNK) for c in range(i + 1)]
    qs = stacked_q(0)
    s_next = scores(qs, 0, 0)
    carry = None
    for k, (i, c) in enumerate(steps):
        s = s_next
        if k + 1 < len(steps):
            i2, c2 = steps[k + 1]
            if i2 != i:
                qs = stacked_q(i2)
            s_next = scores(qs, i2, c2)
        k0 = c * SEL_CHUNK
        if c < i:
            carry = softmax_step(s[0], vs_ref[k0:k0 + SEL_CHUNK, :], carry)
        else:
            rows_a = None if carry is None else tuple(a[:half_rows] for a in carry)
            rows_b = None if carry is None else tuple(a[half_rows:] for a in carry)
            _, acc_a = softmax_step(s[0], vs_ref[k0:k0 + half, :], rows_a)
            _, acc_b = softmax_step(s[1], vs_ref[k0:k0 + SEL_CHUNK, :], rows_b)
            finish(i, jnp.concatenate([acc_a, acc_b], axis=0))
            carry = None


def _position_lanes(pos):
    out = np.zeros((pos.shape[0], LANES), np.float32)
    out[:, L_HI:L_HI + 3] = ((pos // 64) * 64)[:, None]
    out[:, L_LO:L_LO + 3] = (pos % 64)[:, None]
    return out


def _nsa_tables(seq):
    pos = np.arange(seq)
    kaug_s = _position_lanes(pos)
    kaug_s[pos, L_SEL + pos // SLC_LEN] = 1.0
    kaug_w = np.zeros((seq + WIN_PAD, LANES), np.float32)
    kaug_w[WIN_PAD:] = _position_lanes(pos)
    kaug_w[:WIN_PAD, L_FLAG] = 1.0
    kaug_c = _position_lanes(np.arange(N_CMP_PAD) * CMP_STRIDE + CMP_LEN - 1)
    tl = np.arange(SEL_CHUNK // 2)[:, None]
    kk = np.arange(SEL_CHUNK // 2)[None, :]
    tri = np.where(kk <= tl, 0.0, -MASK_BIG).astype(np.float32)
    tl = np.arange(WIN_TQ)[:, None]
    kk = np.arange(WIN_KEYS)[None, :]
    dist = tl + WIN_PAD - kk
    band = np.tile(np.where((dist >= 0) & (dist < WINDOW), 0.0, -MASK_BIG).astype(np.float32), (GROUP, 1))
    c = np.arange(N_CMP_PAD)[None, :]
    j = np.arange(seq // SLC_LEN)[:, None]
    c_start, c_end, s_start = c * CMP_STRIDE, c * CMP_STRIDE + CMP_LEN - 1, j * SLC_LEN
    ovl = ((c_start <= s_start + SLC_LEN - 1) & (c_end >= s_start) & (c < seq // CMP_STRIDE - 1)).astype(np.float32)
    return (jnp.asarray(kaug_s, bf16), jnp.asarray(kaug_w, bf16), jnp.asarray(kaug_c, f32),
            jnp.asarray(tri), jnp.asarray(band), jnp.asarray(ovl, bf16))


def _query_aug():
    slopes = jnp.exp2(-8.0 * jnp.arange(1, N_HEADS + 1, dtype=f32) / N_HEADS).reshape(N_KV, GROUP) * LOG2E
    hi =slopes.astype(bf16).astype(f32)
    mid = (slopes - hi).astype(bf16).astype(f32)
    lo = (slopes - hi - mid).astype(bf16).astype(f32)
    pieces = jnp.stack([hi, mid, lo, hi, mid, lo], axis=-1)
    base = jnp.zeros((N_KV, GROUP, LANES), f32).at[:, :, L_HI:L_HI + 6].set(pieces)
    return jnp.concatenate([base, base.at[:, :, L_FLAG].set(-MASK_BIG)], axis=1)


def _nsa(proj, wlo, whi, pelo, pehi, w2bd, batch, seq):
    kaug_s, kaug_w, kaug_c, tri, band, ovl = _nsa_tables(seq)
    qaug = _query_aug()
    const2 = lambda b, g: (0, 0)
    in_specs = [
        pl.BlockSpec((seq, 2 * LANES), lambda b, g: (b, OFF_Q // (2 * LANES) + g)),
        pl.BlockSpec((seq, LANES), lambda b, g: (b, OFF_KSW // LANES + g)),
        pl.BlockSpec((seq, LANES), lambda b, g: (b, OFF_VSW // LANES + g)),
        pl.BlockSpec((seq, LANES), lambda b, g: (b, OFF_GATE // LANES + g)),
        pl.BlockSpec((seq, LANES), lambda b, g: (b, OFF_KVC // LANES + g)),
        pl.BlockSpec(wlo.shape, const2),
        pl.BlockSpec(whi.shape, const2),
        pl.BlockSpec(pelo.shape, const2),
        pl.BlockSpec(pehi.shape, const2),
        pl.BlockSpec(w2bd.shape, const2),
        pl.BlockSpec(kaug_s.shape, const2),
        pl.BlockSpec(kaug_w.shape, const2),
        pl.BlockSpec(kaug_c.shape, const2),
        pl.BlockSpec((1, 2 * GROUP, LANES), lambda b, g: (g, 0, 0)),
        pl.BlockSpec(tri.shape, const2),
        pl.BlockSpec(band.shape, const2),
        pl.BlockSpec(ovl.shape, const2),
    ]
    return pl.pallas_call(
        _nsa_kernel,
        grid=(batch, N_KV),
        in_specs=in_specs,
        out_specs=pl.BlockSpec((seq, 2 * LANES), lambda b, g: (b, g)),
        out_shape=jax.ShapeDtypeStruct((batch * seq, D_ATT), bf16),
        scratch_shapes=[
            pltpu.VMEM((seq, LANES), f32),
            pltpu.VMEM((seq, LANES), bf16),
            pltpu.VMEM((seq + WIN_PAD, LANES), bf16),
            pltpu.VMEM((seq, LANES), bf16),
            pltpu.VMEM((seq + WIN_PAD, LANES), bf16),
            pltpu.VMEM((seq, 2 * LANES), f32),
            pltpu.VMEM((seq // LANES, seq // SLC_LEN, LANES), f32),
            pltpu.VMEM((seq, LANES), f32),
        ],
        compiler_params=pltpu.CompilerParams(
            dimension_semantics=("parallel", "parallel"), vmem_limit_bytes=VMEM_LIMIT),
        name="nsa",
    )(proj, proj, proj, proj, proj, wlo, whi, pelo, pehi, w2bd, kaug_s, kaug_w, kaug_c, qaug, tri, band, ovl)


def _out_kernel(x_ref, ha_ref, bg_ref, cg_ref, za_ref, hap_ref, cgp_ref, zb_ref, gm_ref, ob_ref,
                cw_ref, cb_ref, wa_ref, wb_ref, wo_ref, fnw_ref, o_ref, *, tiles_per_seq):
    tm = x_ref.shape[0]
    halo = hap_ref.shape[0]
    not_first = (pl.program_id(0) % tiles_per_seq != 0).astype(f32)
    u = cg_ref[...].astype(f32) * ha_ref[...].astype(f32)
    u_prev = cgp_ref[...].astype(f32) * hap_ref[...].astype(f32) * not_first
    row = lax.broadcasted_iota(jnp.int32, (tm, 1), 0)
    p1 = u_prev[halo - 1:halo, :]
    p2 = u_prev[halo - 2:halo - 1, :]
    u1 = jnp.where(row == 0, p1, pltpu.roll(u, 1, 0))
    u2 = jnp.where(row == 0, p2, jnp.where(row == 1, p1, pltpu.roll(u, 2, 0)))
    cw = cw_ref[...]
    y = cb_ref[...] + cw[0:1, :] * u2 + cw[1:2, :] * u1 + cw[2:3, :] * u
    ya_in = bg_ref[...].astype(f32) * y * za_ref[...].astype(f32)
    y_a = jnp.dot(ya_in.astype(bf16), wa_ref[...], preferred_element_type=f32)
    yb_in = ob_ref[...].astype(f32) * zb_ref[...].astype(f32)
    y_b = jnp.dot(yb_in.astype(bf16), wb_ref[...], preferred_element_type=f32)
    mixed = (gm_ref[:, 0:D_MODEL].astype(f32) * y_a + gm_ref[:, D_MODEL:2 * D_MODEL].astype(f32) * y_b)
    xo = x_ref[...] + jnp.dot(mixed.astype(bf16), wo_ref[...], preferred_element_type=f32)
    r = lax.rsqrt(jnp.mean(xo * xo, axis=-1, keepdims=True) + NORM_EPS)
    o_ref[...] = (xo * r) * fnw_ref[...]


def _out(x2, proj, ob, conv_w, conv_b, wa, wb, wo, fnw, seq, tm=512, halo=16):
    n_rows = x2.shape[0]
    blk = lambda c: pl.BlockSpec((tm, D_MODEL), lambda i, c=c: (i, c))
    prev = lambda c: pl.BlockSpec((halo, D_MODEL), lambda i, c=c: (jnp.maximum(i * (tm // halo) - 1, 0), c))
    full = lambda a: pl.BlockSpec(a.shape, lambda i: (0, 0))
    return pl.pallas_call(
        functools.partial(_out_kernel, tiles_per_seq=seq // tm),
        grid=(n_rows // tm,),
        in_specs=[
            blk(0),
            blk(0), blk(1), blk(2), blk(3),
            prev(0), prev(2),
            blk(OFF_ZB // D_MODEL),
            pl.BlockSpec((tm, 2 * D_MODEL), lambda i: (i, OFF_GMIX // (2 * D_MODEL))),
            blk(0),
            full(conv_w), full(conv_b), full(wa), full(wb), full(wo), full(fnw),
        ],
        out_specs=pl.BlockSpec((tm, D_MODEL), lambda i: (i, 0)),
        out_shape=jax.ShapeDtypeStruct((n_rows, D_MODEL), f32),
        compiler_params=pltpu.CompilerParams(
            dimension_semantics=("parallel",), vmem_limit_bytes=VMEM_LIMIT),
        name="merge_out",
    )(x2, proj, proj, proj, proj, proj, proj, proj, proj, ob, conv_w, conv_b, wa, wb, wo, fnw)


def _blockdiag_w1(w1_k, w1_v, lo):
    half = CMP_STRIDE * HEAD_DIM
    wk = w1_k[lo * half:(lo + 1) * half].reshape(CMP_STRIDE, HEAD_DIM, CMP_HID)
    wv = w1_v[lo * half:(lo + 1) * half].reshape(CMP_STRIDE, HEAD_DIM, CMP_HID)
    z = jnp.zeros_like(wk)
    top = jnp.concatenate([wk, z], axis=-1)
    bot = jnp.concatenate([z, wv], axis=-1)
    return jnp.concatenate([top, bot], axis=1).reshape(CMP_STRIDE * 2 * HEAD_DIM, 2 * CMP_HID).astype(bf16)


def _pe_row(pe_k, pe_v, lo):
    sl = slice(lo * CMP_STRIDE, (lo + 1) * CMP_STRIDE)
    return jnp.concatenate([pe_k[sl], pe_v[sl]], axis=-1).reshape(1, CMP_STRIDE * 2 * HEAD_DIM)


def kernel(x, norm_w, w_in, conv_w, conv_b, cmp_pe_k, cmp_pe_v, cmp_w1_k, cmp_w2_k, cmp_w1_v, cmp_w2_v,
           w_proj_a, w_proj_b, w_out, final_norm_w):
    batch, seq, _ = x.shape
    assert norm_w.shape[0] == 1 and seq % SEL_CHUNK == 0 and seq // CMP_STRIDE == N_CMP_PAD
    w_p = _reordered_w_in(w_in[0])
    assert w_p.shape[1] == N_PAD
    x2 = x.reshape(batch * seq, D_MODEL)
    proj = _inproj(x2, norm_w, w_p)

    wlo = _blockdiag_w1(cmp_w1_k[0], cmp_w1_v[0], 0)
    whi = _blockdiag_w1(cmp_w1_k[0], cmp_w1_v[0], 1)
    pelo = _pe_row(cmp_pe_k[0], cmp_pe_v[0], 0)
    pehi = _pe_row(cmp_pe_k[0], cmp_pe_v[0], 1)
    zk = jnp.zeros_like(cmp_w2_k[0])
    w2bd = jnp.concatenate([jnp.concatenate([cmp_w2_k[0], zk], axis=1),
                            jnp.concatenate([zk, cmp_w2_v[0]], axis=1)], axis=0).astype(bf16)
    ob = _nsa(proj, wlo, whi, pelo, pehi, w2bd, batch, seq)

    out = _out(x2, proj, ob, conv_w[0], conv_b, w_proj_a[0].astype(bf16), w_proj_b[0].astype(bf16),
               w_out[0].astype(bf16), final_norm_w.reshape(1, D_MODEL), seq)
    return out.reshape(batch, seq, D_MODEL)
```

```python
import functools

import numpy as np
import jax
import jax.numpy as jnp
from jax import lax
from jax.experimental import pallas as pl
from jax.experimental.pallas import tpu as pltpu

D_MODEL = 1024
D_CONV = 1024
CONV_W = 3
N_HEADS = 16
HEAD_DIM = 64
N_KV = 4
GROUP = N_HEADS // N_KV
D_ATT = N_HEADS * HEAD_DIM
D_KV = N_KV * HEAD_DIM
CMP_LEN = 32
CMP_STRIDE = 16
CMP_HID = 128
SLC_LEN = 64
N_SEL = 8
WINDOW = 512
NORM_EPS = 1e-6

LANES = 128
N_CMP_PAD = 128
MASK_BIG = float(2.0 ** 100)
LOG2E = float(np.log2(np.e))
SEL_CHUNK = 512
WIN_TQ = 128
WIN_KEYS = WINDOW + WIN_TQ
WIN_PAD = WINDOW
CW_TQ = 512

OFF_CONV = 0
OFF_Q = 4096
OFF_ZB = 5120
OFF_GMIX = 6144
OFF_KVC = 8192
OFF_KSW = 8704
OFF_VSW = 9216
OFF_GATE = 9728
N_PAD = 10240

L_HI = 64
L_LO = 67
L_FLAG = 70
L_SEL = 96

VMEM_LIMIT = 56 * 1024 * 1024

f32 = jnp.float32
bf16 = jnp.bfloat16


def _reordered_w_in(w):
    o_q = 4 * D_CONV
    o_kv = o_q + D_ATT
    o_gate = o_kv + 6 * D_KV
    o_zb = o_gate + 3 * N_HEADS
    o_gmix = o_zb + D_ATT
    kv = w[:, o_kv:o_gate].reshape(D_MODEL, 6, N_KV, HEAD_DIM)
    pair = lambda a, b: jnp.stack([kv[:, a], kv[:, b]], axis=2).reshape(D_MODEL, N_KV * LANES)
    gates = jnp.pad(w[:, o_gate:o_zb].reshape(D_MODEL, N_KV, 3 * GROUP),
                    ((0, 0), (0, 0), (0, LANES - 3 * GROUP))).reshape(D_MODEL, N_KV * LANES)
    cols = [w[:, :o_q], w[:, o_q:o_kv] * (HEAD_DIM ** -0.5 * LOG2E), w[:, o_zb:o_gmix], w[:, o_gmix:],
            pair(0, 1), pair(2, 4), pair(3, 5), gates]
    return jnp.concatenate(cols, axis=1).astype(bf16)


def _sigmoid(x):
    return 1.0 / (1.0 + jnp.exp(-x))


def _dot_nt(a, b):
    return lax.dot_general(a, b, (((1,), (1,)), ((), ())), preferred_element_type=f32)


def _inproj_kernel(x_ref, nw_ref, w_ref, o_ref, h_ref):
    @pl.when(pl.program_id(1) == 0)
    def _():
        x = x_ref[...]
        r = lax.rsqrt(jnp.mean(x * x, axis=-1, keepdims=True) + NORM_EPS)
        h_ref[...] = ((x * r) * nw_ref[...]).astype(bf16)

    product = lambda: jnp.dot(h_ref[...], w_ref[...], preferred_element_type=f32)
    tn = o_ref.shape[1]
    j = pl.program_id(1)
    is_silu = (j == (OFF_CONV + 3 * D_CONV) // tn) | (j == OFF_ZB // tn)
    is_sigmoid = (j == OFF_GMIX // tn) | (j == OFF_GMIX // tn + 1)
    is_tail = j == OFF_GATE // tn
    split = OFF_GATE % tn

    @pl.when(is_silu)
    def _():
        acc = product()
        o_ref[...] = (acc * _sigmoid(acc)).astype(bf16)

    @pl.when(is_sigmoid)
    def _():
        o_ref[...] = _sigmoid(product()).astype(bf16)

    @pl.when(is_tail)
    def _():
        acc = product()
        o_ref[:, :split] = acc[:, :split].astype(bf16)
        o_ref[:, split:] = _sigmoid(acc[:, split:]).astype(bf16)

    @pl.when(jnp.logical_not(is_silu | is_sigmoid | is_tail))
    def _():
        o_ref[...] = product().astype(bf16)


def _inproj(x2, norm_w, w_p, tm=2048, tn=1024):
    n_rows = x2.shape[0]
    assert D_CONV % tn == 0 and OFF_ZB % tn == 0 and OFF_GMIX % tn == 0 and D_MODEL == tn and N_PAD - OFF_GATE < tn
    return pl.pallas_call(
        _inproj_kernel,
        grid=(n_rows // tm, N_PAD // tn),
        in_specs=[
            pl.BlockSpec((tm, D_MODEL), lambda i, j: (i, 0)),
            pl.BlockSpec((1, D_MODEL), lambda i, j: (0, 0)),
            pl.BlockSpec((D_MODEL, tn), lambda i, j: (0, j)),
        ],
        out_specs=pl.BlockSpec((tm, tn), lambda i, j: (i, j)),
        out_shape=jax.ShapeDtypeStruct((n_rows, N_PAD), bf16),
        scratch_shapes=[pltpu.VMEM((tm, D_MODEL), bf16)],
        compiler_params=pltpu.CompilerParams(
            dimension_semantics=("parallel", "arbitrary"), vmem_limit_bytes=VMEM_LIMIT),
        name="inproj",
    )(x2, norm_w, w_p)


def _softmax_step(s, v, carry):
    m, acc = carry
    m_new = jnp.maximum(m, jnp.max(s, axis=-1, keepdims=True))
    alpha = jnp.exp2(m - m_new)
    p = jnp.exp2(s - m_new)
    acc = alpha * acc + jnp.dot(p.astype(bf16), v, preferred_element_type=f32)
    return m_new, acc


def _stack_heads(q_tile, aug_rows, extra):
    lane = lax.broadcasted_iota(jnp.int32, (1, LANES), 1)
    pairs = (q_tile[:, :LANES], q_tile[:, LANES:])
    parts = []
    for r in range(GROUP):
        qh = pairs[r // 2]
        if r % 2 == 1:
            qh = pltpu.roll(qh, HEAD_DIM, 1)
        aug = aug_rows[r:r + 1, :]
        if extra is not None:
            aug = aug + extra
        parts.append(jnp.where(lane < HEAD_DIM, qh, aug).astype(bf16))
    return jnp.concatenate(parts, axis=0)


def _merge_heads(vals, lane):
    return (jnp.where(lane < HEAD_DIM, vals[0], vals[1]), jnp.where(lane < HEAD_DIM, vals[2], vals[3]))


def _nsa_kernel(q_ref, ksw_ref, vsw_ref, gate_ref, kvc_ref, wlo_ref, whi_ref, pelo_ref, pehi_ref, w2_ref,
                kaug_s_ref, kaug_w_ref, kaug_c_ref, qaug_ref, tri_ref, band_ref, ovl_ref,
                o_ref,
                kvcf_ref, ks_ref, kw_ref, vs_ref, vw_ref, ocw_ref, imp_ref, selq_ref):
    seq = q_ref.shape[0]
    lane = lax.broadcasted_iota(jnp.int32, (1, LANES), 1)
    low = lane < HEAD_DIM

    kvcf_ref[...] = kvc_ref[...].astype(f32)
    xcat = jnp.concatenate([kvcf_ref[pl.ds(l, N_CMP_PAD, stride=CMP_STRIDE), :] for l in range(CMP_STRIDE)],
                           axis=1)
    xa = (xcat + pelo_ref[...]).astype(bf16)
    xb = (xcat + pehi_ref[...]).astype(bf16)
    a = jnp.dot(xa, wlo_ref[...], preferred_element_type=f32)
    bm = jnp.dot(xb, whi_ref[...], preferred_element_type=f32)
    h = a + pltpu.roll(bm, N_CMP_PAD - 1, 0)
    hid = h * _sigmoid(h)
    kcvc = jnp.dot(hid.astype(bf16), w2_ref[...], preferred_element_type=f32)

    ksw = ksw_ref[...]
    vsw = vsw_ref[...]
    one = jnp.ones((1, LANES), bf16)
    ks_ref[...] = jnp.where(low, ksw, kaug_s_ref[...])
    kwin = pltpu.roll(ksw.astype(f32), HEAD_DIM, 1).astype(bf16)
    kw_ref[0:WIN_PAD, :] = kaug_w_ref[0:WIN_PAD, :]
    kw_ref[WIN_PAD:, :] = jnp.where(low, kwin, kaug_w_ref[WIN_PAD:, :])
    vs_ref[...] = jnp.where(low, vsw, one)
    vw_ref[0:WIN_PAD, :] = jnp.zeros((WIN_PAD, LANES), bf16)
    vw_ref[WIN_PAD:, :] = jnp.where(low, one, vsw)

    qaug = qaug_ref[0]
    kc_aug = jnp.where(low, kcvc, kaug_c_ref[...]).astype(bf16)
    vc_rows = jnp.concatenate([jnp.zeros((N_CMP_PAD, LANES), bf16), kcvc.astype(bf16)], axis=1)
    zeros_w = jnp.zeros((WIN_KEYS, LANES), bf16)
    ovl = ovl_ref[...]

    def cw_body(i, _):
        t0 = pl.multiple_of(i * CW_TQ, CW_TQ)
        sig = gate_ref[pl.ds(t0, CW_TQ), :].astype(f32)
        o_w, o_c = [], []
        n_tiles = CW_TQ // WIN_TQ

        def tile_scores(h):
            tw = pl.multiple_of(t0 + h * WIN_TQ, WIN_TQ)
            qw = _stack_heads(q_ref[pl.ds(tw, WIN_TQ), :].astype(f32), qaug[GROUP:2 * GROUP], None)
            k_all = jnp.concatenate([kw_ref[pl.ds(tw, WIN_KEYS), :], kc_aug], axis=0)
            return _dot_nt(qw, k_all)

        s_next = tile_scores(0)
        for h in range(n_tiles):
            tw = pl.multiple_of(t0 + h * WIN_TQ, WIN_TQ)
            s = s_next
            if h + 1 < n_tiles:
                s_next = tile_scores(h + 1)

            sw = jnp.concatenate([s[:, :WIN_TQ] + band_ref[:, :WIN_TQ], s[:, WIN_TQ:WINDOW],
                                  s[:, WINDOW:WIN_KEYS] + band_ref[:, WINDOW:]], axis=1)
            p_w = jnp.exp2(sw - jnp.max(sw, axis=-1, keepdims=True))

            shp_c = (GROUP * WIN_TQ, N_CMP_PAD)
            t_idx = tw + (lax.broadcasted_iota(jnp.int32, shp_c, 0) & (WIN_TQ - 1))
            c_end = lax.broadcasted_iota(jnp.int32, shp_c, 1) * CMP_STRIDE + (CMP_LEN - 1)
            ok = c_end <= t_idx
            sm = jnp.where(ok, s[:, WIN_KEYS:], -jnp.inf)
            m = jnp.max(sm, axis=-1, keepdims=True)
            m = jnp.where(m > -jnp.inf, m, 0.0)
            e = jnp.where(ok, jnp.exp2(sm - m), 0.0)
            p_c = e / jnp.maximum(jnp.sum(e, axis=-1, keepdims=True), 1e-30)

            p_all = jnp.concatenate([p_w.astype(bf16), p_c.astype(bf16)], axis=1)
            v_all = jnp.concatenate(
                [jnp.concatenate([vw_ref[pl.ds(tw, WIN_KEYS), :], zeros_w], axis=1), vc_rows], axis=0)
            acc = jnp.dot(p_all, v_all, preferred_element_type=f32)
            acc_w = acc[:, :LANES]
            o_w.append(acc_w / jnp.maximum(acc_w[:, 0:1], 1e-30))
            o_c.append(acc[:, LANES:])

            ps = p_c[0:WIN_TQ] + p_c[WIN_TQ:2 * WIN_TQ] + p_c[2 * WIN_TQ:3 * WIN_TQ] + p_c[3 * WIN_TQ:4 * WIN_TQ]
            p1 = ps.astype(bf16)
            r1 = ps - p1.astype(f32)
            p2 = r1.astype(bf16)
            p3 = (r1 - p2.astype(f32)).astype(bf16)
            imp_ref[i * (CW_TQ // WIN_TQ) + h] = _dot_nt(ovl, p1) + _dot_nt(ovl, p2) + _dot_nt(ovl, p3)

        vals = []
        for r in range(GROUP):
            head = lambda parts: jnp.concatenate([o[r * WIN_TQ:(r + 1) * WIN_TQ, :] for o in parts], axis=0)
            a = head(o_c) * sig[:, 3 * r:3 * r + 1] + head(o_w) * sig[:, 3 * r + 2:3 * r + 3]
            vals.append(pltpu.roll(a, HEAD_DIM, 1) if r % 2 == 0 else a)
        c0, c1 = _merge_heads(vals, lane)
        ocw_ref[pl.ds(t0, CW_TQ), 0:LANES] = c0
        ocw_ref[pl.ds(t0, CW_TQ), LANES:2 * LANES] = c1
        return 0

    lax.fori_loop(0, seq // CW_TQ, cw_body, 0)

    n_slc = seq // SLC_LEN
    n_tt = seq // LANES
    shp = (n_tt, n_slc, LANES)
    j_idx = lax.broadcasted_iota(jnp.int32, shp, 1)
    t_idx = lax.broadcasted_iota(jnp.int32, shp, 0) * LANES + lax.broadcasted_iota(jnp.int32, shp, 2)
    cur = t_idx // SLC_LEN
    forced = (j_idx == 0) | (j_idx == cur) | (j_idx == cur - 1)
    val = jnp.where(forced, jnp.inf, jnp.where(j_idx <= cur, imp_ref[...], -jnp.inf))
    rank = jnp.zeros(shp, jnp.int32)
    for jp in range(n_slc):
        vj = val[:, jp:jp + 1, :]
        beats = (vj > val) | ((vj == val) & (j_idx > jp))
        rank = rank + beats.astype(jnp.int32)
    sel = (rank < N_SEL) & (val > -jnp.inf)
    notsel = jnp.where(sel, 0.0, 1.0)
    eye = (lax.broadcasted_iota(jnp.int32, (LANES, LANES), 0)
           == lax.broadcasted_iota(jnp.int32, (LANES, LANES), 1)).astype(bf16)
    zpad = jnp.zeros((L_SEL, LANES), f32)
    for tt in range(n_tt):
        z = jnp.concatenate([zpad, notsel[tt]], axis=0).astype(bf16)
        selq_ref[tt * LANES:(tt + 1) * LANES, :] = _dot_nt(eye, z) * (-MASK_BIG)

    half = SEL_CHUNK // 2
    half_rows = GROUP * half

    def stacked_q(i):
        parts = []
        for h in range(2):
            tok = slice(i * SEL_CHUNK + h * half, i * SEL_CHUNK + (h + 1) * half)
            parts.append(_stack_heads(q_ref[tok, :].astype(f32), qaug[0:GROUP], selq_ref[tok, :]))
        return parts

    def causal(s):
        tri = tri_ref[...]
        return jnp.concatenate([s[r * half:(r + 1) * half, :] + tri for r in range(GROUP)], axis=0)

    def scores(qs, i, c):
        k0 = c * SEL_CHUNK
        if c < i:
            return (_dot_nt(jnp.concatenate(qs, axis=0), ks_ref[k0:k0 + SEL_CHUNK, :]),)
        s_a = causal(_dot_nt(qs[0], ks_ref[k0:k0 + half, :]))
        s_b = _dot_nt(qs[1], ks_ref[k0:k0 + SEL_CHUNK, :])
        return (s_a, jnp.concatenate([s_b[:, :half], causal(s_b[:, half:])], axis=1))

    def softmax_step(s, v, carry):
        if carry is None:
            m = jnp.max(s, axis=-1, keepdims=True)
            return m, jnp.dot(jnp.exp2(s - m).astype(bf16), v, preferred_element_type=f32)
        return _softmax_step(s, v, carry)

    def finish(i, acc):
        tok = slice(i * SEL_CHUNK, (i + 1) * SEL_CHUNK)
        o_s = acc / jnp.maximum(acc[:, HEAD_DIM:HEAD_DIM + 1], 1e-30)
        sig = gate_ref[tok, :].astype(f32)
        vals = []
        for r in range(GROUP):
            a_s = jnp.concatenate([o_s[h * half_rows + r * half:h * half_rows + (r + 1) * half, :]
                                   for h in range(2)], axis=0) * sig[:, 3 * r + 1:3 * r + 2]
            vals.append(a_s if r % 2 == 0 else pltpu.roll(a_s, HEAD_DIM, 1))
        c0, c1 = _merge_heads(vals, lane)
        o_ref[tok, 0:LANES] = (ocw_ref[tok, 0:LANES] + c0).astype(bf16)
        o_ref[tok, LANES:2 * LANES] = (ocw_ref[tok, LANES:2 * LANES] + c1).astype(bf16)

    steps = [(i, c) for i in range(seq // SEL_CHUNK) for c in range(i + 1)]
    qs = stacked_q(0)
    s_next = scores(qs, 0, 0)
    carry = None
    for k, (i, c) in enumerate(steps):
        s = s_next
        if k + 1 < len(steps):
            i2, c2 = steps[k + 1]
            if i2 != i:
                qs = stacked_q(i2)
            s_next = scores(qs, i2, c2)
        k0 = c * SEL_CHUNK
        if c < i:
            carry = softmax_step(s[0], vs_ref[k0:k0 + SEL_CHUNK, :], carry)
        else:
            rows_a = None if carry is None else tuple(a[:half_rows] for a in carry)
            rows_b = None if carry is None else tuple(a[half_rows:] for a in carry)
            _, acc_a = softmax_step(s[0], vs_ref[k0:k0 + half, :], rows_a)
            _, acc_b = softmax_step(s[1], vs_ref[k0:k0 + SEL_CHUNK, :], rows_b)
            finish(i, jnp.concatenate([acc_a, acc_b], axis=0))
            carry = None


def _position_lanes(pos):
    out = np.zeros((pos.shape[0], LANES), np.float32)
    out[:, L_HI:L_HI + 3] = ((pos // 64) * 64)[:, None]
    out[:, L_LO:L_LO + 3] = (pos % 64)[:, None]
    return out


def _nsa_tables(seq):
    pos = np.arange(seq)
    kaug_s = _position_lanes(pos)
    kaug_s[pos, L_SEL + pos // SLC_LEN] = 1.0
    kaug_w = np.zeros((seq + WIN_PAD, LANES), np.float32)
    kaug_w[WIN_PAD:] = _position_lanes(pos)
    kaug_w[:WIN_PAD, L_FLAG] = 1.0
    kaug_c = _position_lanes(np.arange(N_CMP_PAD) * CMP_STRIDE + CMP_LEN - 1)
    tl = np.arange(SEL_CHUNK // 2)[:, None]
    kk = np.arange(SEL_CHUNK // 2)[None, :]
    tri = np.where(kk <= tl, 0.0, -MASK_BIG).astype(np.float32)
    tl = np.arange(WIN_TQ)[:, None]
    kk = np.arange(WIN_KEYS)[None, :]
    dist = tl + WIN_PAD - kk
    band = np.tile(np.where((dist >= 0) & (dist < WINDOW), 0.0, -MASK_BIG).astype(np.float32), (GROUP, 1))
    c = np.arange(N_CMP_PAD)[None, :]
    j = np.arange(seq // SLC_LEN)[:, None]
    c_start, c_end, s_start = c * CMP_STRIDE, c * CMP_STRIDE + CMP_LEN - 1, j * SLC_LEN
    ovl = ((c_start <= s_start + SLC_LEN - 1) & (c_end >= s_start) & (c < seq // CMP_STRIDE - 1)).astype(np.float32)
    return (jnp.asarray(kaug_s, bf16), jnp.asarray(kaug_w, bf16), jnp.asarray(kaug_c, f32),
            jnp.asarray(tri), jnp.asarray(band), jnp.asarray(ovl, bf16))


def _query_aug():
    slopes = jnp.exp2(-8.0 * jnp.arange(1, N_HEADS + 1, dtype=f32) / N_HEADS).reshape(N_KV, GROUP) * LOG2E
    hi =slopes.astype(bf16).astype(f32)
    mid = (slopes - hi).astype(bf16).astype(f32)
    lo = (slopes - hi - mid).astype(bf16).astype(f32)
    pieces = jnp.stack([hi, mid, lo, hi, mid, lo], axis=-1)
    base = jnp.zeros((N_KV, GROUP, LANES), f32).at[:, :, L_HI:L_HI + 6].set(pieces)
    return jnp.concatenate([base, base.at[:, :, L_FLAG].set(-MASK_BIG)], axis=1)


def _nsa(proj, wlo, whi, pelo, pehi, w2bd, batch, seq):
    kaug_s, kaug_w, kaug_c, tri, band, ovl = _nsa_tables(seq)
    qaug = _query_aug()
    const2 = lambda b, g: (0, 0)
    in_specs = [
        pl.BlockSpec((seq, 2 * LANES), lambda b, g: (b, OFF_Q // (2 * LANES) + g)),
        pl.BlockSpec((seq, LANES), lambda b, g: (b, OFF_KSW // LANES + g)),
        pl.BlockSpec((seq, LANES), lambda b, g: (b, OFF_VSW // LANES + g)),
        pl.BlockSpec((seq, LANES), lambda b, g: (b, OFF_GATE // LANES + g)),
        pl.BlockSpec((seq, LANES), lambda b, g: (b, OFF_KVC // LANES + g)),
        pl.BlockSpec(wlo.shape, const2),
        pl.BlockSpec(whi.shape, const2),
        pl.BlockSpec(pelo.shape, const2),
        pl.BlockSpec(pehi.shape, const2),
        pl.BlockSpec(w2bd.shape, const2),
        pl.BlockSpec(kaug_s.shape, const2),
        pl.BlockSpec(kaug_w.shape, const2),
        pl.BlockSpec(kaug_c.shape, const2),
        pl.BlockSpec((1, 2 * GROUP, LANES), lambda b, g: (g, 0, 0)),
        pl.BlockSpec(tri.shape, const2),
        pl.BlockSpec(band.shape, const2),
        pl.BlockSpec(ovl.shape, const2),
    ]
    return pl.pallas_call(
        _nsa_kernel,
        grid=(batch, N_KV),
        in_specs=in_specs,
        out_specs=pl.BlockSpec((seq, 2 * LANES), lambda b, g: (b, g)),
        out_shape=jax.ShapeDtypeStruct((batch * seq, D_ATT), bf16),
        scratch_shapes=[
            pltpu.VMEM((seq, LANES), f32),
            pltpu.VMEM((seq, LANES), bf16),
            pltpu.VMEM((seq + WIN_PAD, LANES), bf16),
            pltpu.VMEM((seq, LANES), bf16),
            pltpu.VMEM((seq + WIN_PAD, LANES), bf16),
            pltpu.VMEM((seq, 2 * LANES), f32),
            pltpu.VMEM((seq // LANES, seq // SLC_LEN, LANES), f32),
            pltpu.VMEM((seq, LANES), f32),
        ],
        compiler_params=pltpu.CompilerParams(
            dimension_semantics=("parallel", "parallel"), vmem_limit_bytes=VMEM_LIMIT),
        name="nsa",
    )(proj, proj, proj, proj, proj, wlo, whi, pelo, pehi, w2bd, kaug_s, kaug_w, kaug_c, qaug, tri, band, ovl)


def _out_kernel(x_ref, ha_ref, bg_ref, cg_ref, za_ref, hap_ref, cgp_ref, zb_ref, gm_ref, ob_ref,
                cw_ref, cb_ref, wa_ref, wb_ref, wo_ref, fnw_ref, o_ref, *, tiles_per_seq):
    tm = x_ref.shape[0]
    halo = hap_ref.shape[0]
    not_first = (pl.program_id(0) % tiles_per_seq != 0).astype(f32)
    u = cg_ref[...].astype(f32) * ha_ref[...].astype(f32)
    u_prev = cgp_ref[...].astype(f32) * hap_ref[...].astype(f32) * not_first
    row = lax.broadcasted_iota(jnp.int32, (tm, 1), 0)
    p1 = u_prev[halo - 1:halo, :]
    p2 = u_prev[halo - 2:halo - 1, :]
    u1 = jnp.where(row == 0, p1, pltpu.roll(u, 1, 0))
    u2 = jnp.where(row == 0, p2, jnp.where(row == 1, p1, pltpu.roll(u, 2, 0)))
    cw = cw_ref[...]
    y = cb_ref[...] + cw[0:1, :] * u2 + cw[1:2, :] * u1 + cw[2:3, :] * u
    ya_in = bg_ref[...].astype(f32) * y * za_ref[...].astype(f32)
    y_a = jnp.dot(ya_in.astype(bf16), wa_ref[...], preferred_element_type=f32)
    yb_in = ob_ref[...].astype(f32) * zb_ref[...].astype(f32)
    y_b = jnp.dot(yb_in.astype(bf16), wb_ref[...], preferred_element_type=f32)
    mixed = (gm_ref[:, 0:D_MODEL].astype(f32) * y_a + gm_ref[:, D_MODEL:2 * D_MODEL].astype(f32) * y_b)
    xo = x_ref[...] + jnp.dot(mixed.astype(bf16), wo_ref[...], preferred_element_type=f32)
    r = lax.rsqrt(jnp.mean(xo * xo, axis=-1, keepdims=True) + NORM_EPS)
    o_ref[...] = (xo * r) * fnw_ref[...]


def _out(x2, proj, ob, conv_w, conv_b, wa, wb, wo, fnw, seq, tm=512, halo=16):
    n_rows = x2.shape[0]
    blk = lambda c: pl.BlockSpec((tm, D_MODEL), lambda i, c=c: (i, c))
    prev = lambda c: pl.BlockSpec((halo, D_MODEL), lambda i, c=c: (jnp.maximum(i * (tm // halo) - 1, 0), c))
    full = lambda a: pl.BlockSpec(a.shape, lambda i: (0, 0))
    return pl.pallas_call(
        functools.partial(_out_kernel, tiles_per_seq=seq // tm),
        grid=(n_rows // tm,),
        in_specs=[
            blk(0),
            blk(0), blk(1), blk(2), blk(3),
            prev(0), prev(2),
            blk(OFF_ZB // D_MODEL),
            pl.BlockSpec((tm, 2 * D_MODEL), lambda i: (i, OFF_GMIX // (2 * D_MODEL))),
            blk(0),
            full(conv_w), full(conv_b), full(wa), full(wb), full(wo), full(fnw),
        ],
        out_specs=pl.BlockSpec((tm, D_MODEL), lambda i: (i, 0)),
        out_shape=jax.ShapeDtypeStruct((n_rows, D_MODEL), f32),
        compiler_params=pltpu.CompilerParams(
            dimension_semantics=("parallel",), vmem_limit_bytes=VMEM_LIMIT),
        name="merge_out",
    )(x2, proj, proj, proj, proj, proj, proj, proj, proj, ob, conv_w, conv_b, wa, wb, wo, fnw)


def _blockdiag_w1(w1_k, w1_v, lo):
    half = CMP_STRIDE * HEAD_DIM
    wk = w1_k[lo * half:(lo + 1) * half].reshape(CMP_STRIDE, HEAD_DIM, CMP_HID)
    wv = w1_v[lo * half:(lo + 1) * half].reshape(CMP_STRIDE, HEAD_DIM, CMP_HID)
    z = jnp.zeros_like(wk)
    top = jnp.concatenate([wk, z], axis=-1)
    bot = jnp.concatenate([z, wv], axis=-1)
    return jnp.concatenate([top, bot], axis=1).reshape(CMP_STRIDE * 2 * HEAD_DIM, 2 * CMP_HID).astype(bf16)


def _pe_row(pe_k, pe_v, lo):
    sl = slice(lo * CMP_STRIDE, (lo + 1) * CMP_STRIDE)
    return jnp.concatenate([pe_k[sl], pe_v[sl]], axis=-1).reshape(1, CMP_STRIDE * 2 * HEAD_DIM)


def kernel(x, norm_w, w_in, conv_w, conv_b, cmp_pe_k, cmp_pe_v, cmp_w1_k, cmp_w2_k, cmp_w1_v, cmp_w2_v,
           w_proj_a, w_proj_b, w_out, final_norm_w):
    batch, seq, _ = x.shape
    assert norm_w.shape[0] == 1 and seq % SEL_CHUNK == 0 and seq // CMP_STRIDE == N_CMP_PAD
    w_p = _reordered_w_in(w_in[0])
    assert w_p.shape[1] == N_PAD
    x2 = x.reshape(batch * seq, D_MODEL)
    proj = _inproj(x2, norm_w, w_p)

    wlo = _blockdiag_w1(cmp_w1_k[0], cmp_w1_v[0], 0)
    whi = _blockdiag_w1(cmp_w1_k[0], cmp_w1_v[0], 1)
    pelo = _pe_row(cmp_pe_k[0], cmp_pe_v[0], 0)
    pehi = _pe_row(cmp_pe_k[0], cmp_pe_v[0], 1)
    zk = jnp.zeros_like(cmp_w2_k[0])
    w2bd = jnp.concatenate([jnp.concatenate([cmp_w2_k[0], zk], axis=1),
                            jnp.concatenate([zk, cmp_w2_v[0]], axis=1)], axis=0).astype(bf16)
    ob = _nsa(proj, wlo, whi, pelo, pehi, w2bd, batch, seq)

    out = _out(x2, proj, ob, conv_w[0], conv_b, w_proj_a[0].astype(bf16), w_proj_b[0].astype(bf16),
               w_out[0].astype(bf16), final_norm_w.reshape(1, D_MODEL), seq)
    return out.reshape(batch, seq, D_MODEL)
```

```python
import functools

import numpy as np
import jax
import jax.numpy as jnp
from jax import lax
from jax.experimental import pallas as pl
from jax.experimental.pallas import tpu as pltpu

D_MODEL = 1024
D_CONV = 1024
CONV_W = 3
N_HEADS = 16
HEAD_DIM = 64
N_KV = 4
GROUP = N_HEADS // N_KV
D_ATT = N_HEADS * HEAD_DIM
D_KV = N_KV * HEAD_DIM
CMP_LEN = 32
CMP_STRIDE = 16
CMP_HID = 128
SLC_LEN = 64
N_SEL = 8
WINDOW = 512
NORM_EPS = 1e-6

LANES = 128
N_CMP_PAD = 128
MASK_BIG = float(2.0 ** 100)
LOG2E = float(np.log2(np.e))
SEL_CHUNK = 512
WIN_TQ = 128
WIN_KEYS = WINDOW + WIN_TQ
WIN_PAD = WINDOW
CW_TQ = 1024

OFF_CONV = 0
OFF_Q = 4096
OFF_ZB = 5120
OFF_GMIX = 6144
OFF_KVC = 8192
OFF_KSW = 8704
OFF_VSW = 9216
OFF_GATE = 9728
N_PAD = 10240

L_HI = 64
L_LO = 67
L_FLAG = 70
L_SEL = 96

VMEM_LIMIT = 56 * 1024 * 1024

f32 = jnp.float32
bf16 = jnp.bfloat16


def _reordered_w_in(w):
    o_q = 4 * D_CONV
    o_kv = o_q + D_ATT
    o_gate = o_kv + 6 * D_KV
    o_zb = o_gate + 3 * N_HEADS
    o_gmix = o_zb + D_ATT
    kv = w[:, o_kv:o_gate].reshape(D_MODEL, 6, N_KV, HEAD_DIM)
    pair = lambda a, b: jnp.stack([kv[:, a], kv[:, b]], axis=2).reshape(D_MODEL, N_KV * LANES)
    gates = jnp.pad(w[:, o_gate:o_zb].reshape(D_MODEL, N_KV, 3 * GROUP),
                    ((0, 0), (0, 0), (0, LANES - 3 * GROUP))).reshape(D_MODEL, N_KV * LANES)
    cols = [w[:, :o_q], w[:, o_q:o_kv] * (HEAD_DIM ** -0.5 * LOG2E), w[:, o_zb:o_gmix], w[:, o_gmix:],
            pair(0, 1), pair(2, 4), pair(3, 5), gates]
    return jnp.concatenate(cols, axis=1).astype(bf16)


def _sigmoid(x):
    return 1.0 / (1.0 + jnp.exp(-x))


def _dot_nt(a, b):
    return lax.dot_general(a, b, (((1,), (1,)), ((), ())), preferred_element_type=f32)


def _inproj_kernel(x_ref, nw_ref, w_ref, o_ref, h_ref):
    @pl.when(pl.program_id(1) == 0)
    def _():
        x = x_ref[...]
        r = lax.rsqrt(jnp.mean(x * x, axis=-1, keepdims=True) + NORM_EPS)
        h_ref[...] = ((x * r) * nw_ref[...]).astype(bf16)

    product = lambda: jnp.dot(h_ref[...], w_ref[...], preferred_element_type=f32)
    tn = o_ref.shape[1]
    j = pl.program_id(1)
    is_silu = (j == (OFF_CONV + 3 * D_CONV) // tn) | (j == OFF_ZB // tn)
    is_sigmoid = (j == OFF_GMIX // tn) | (j == OFF_GMIX // tn + 1)
    is_tail = j == OFF_GATE // tn
    split = OFF_GATE % tn

    @pl.when(is_silu)
    def _():
        acc = product()
        o_ref[...] = (acc * _sigmoid(acc)).astype(bf16)

    @pl.when(is_sigmoid)
    def _():
        o_ref[...] = _sigmoid(product()).astype(bf16)

    @pl.when(is_tail)
    def _():
        acc = product()
        o_ref[:, :split] = acc[:, :split].astype(bf16)
        o_ref[:, split:] = _sigmoid(acc[:, split:]).astype(bf16)

    @pl.when(jnp.logical_not(is_silu | is_sigmoid | is_tail))
    def _():
        o_ref[...] = product().astype(bf16)


def _inproj(x2, norm_w, w_p, tm=2048, tn=1024):
    n_rows = x2.shape[0]
    assert D_CONV % tn == 0 and OFF_ZB % tn == 0 and OFF_GMIX % tn == 0 and D_MODEL == tn and N_PAD - OFF_GATE < tn
    return pl.pallas_call(
        _inproj_kernel,
        grid=(n_rows // tm, N_PAD // tn),
        in_specs=[
            pl.BlockSpec((tm, D_MODEL), lambda i, j: (i, 0)),
            pl.BlockSpec((1, D_MODEL), lambda i, j: (0, 0)),
            pl.BlockSpec((D_MODEL, tn), lambda i, j: (0, j)),
        ],
        out_specs=pl.BlockSpec((tm, tn), lambda i, j: (i, j)),
        out_shape=jax.ShapeDtypeStruct((n_rows, N_PAD), bf16),
        scratch_shapes=[pltpu.VMEM((tm, D_MODEL), bf16)],
        compiler_params=pltpu.CompilerParams(
            dimension_semantics=("parallel", "arbitrary"), vmem_limit_bytes=VMEM_LIMIT),
        name="inproj",
    )(x2, norm_w, w_p)


def _softmax_step(s, v, carry):
    m, acc = carry
    m_new = jnp.maximum(m, jnp.max(s, axis=-1, keepdims=True))
    alpha = jnp.exp2(m - m_new)
    p = jnp.exp2(s - m_new)
    acc = alpha * acc + jnp.dot(p.astype(bf16), v, preferred_element_type=f32)
    return m_new, acc


def _stack_heads(q_tile, aug_rows, extra):
    lane = lax.broadcasted_iota(jnp.int32, (1, LANES), 1)
    pairs = (q_tile[:, :LANES], q_tile[:, LANES:])
    parts = []
    for r in range(GROUP):
        qh = pairs[r // 2]
        if r % 2 == 1:
            qh = pltpu.roll(qh, HEAD_DIM, 1)
        aug = aug_rows[r:r + 1, :]
        if extra is not None:
            aug = aug + extra
        parts.append(jnp.where(lane < HEAD_DIM, qh, aug).astype(bf16))
    return jnp.concatenate(parts, axis=0)


def _merge_heads(vals, lane):
    return (jnp.where(lane < HEAD_DIM, vals[0], vals[1]), jnp.where(lane < HEAD_DIM, vals[2], vals[3]))


def _nsa_kernel(q_ref, ksw_ref, vsw_ref, gate_ref, kvc_ref, wlo_ref, whi_ref, pelo_ref, pehi_ref, w2_ref,
                kaug_s_ref, kaug_w_ref, kaug_c_ref, qaug_ref, tri_ref, band_ref, ovl_ref, cend_ref,
                o_ref,
                kvcf_ref, ks_ref, kw_ref, vs_ref, vw_ref, ocw_ref, imp_ref, ns_ref, selq_ref):
    seq = q_ref.shape[0]
    lane = lax.broadcasted_iota(jnp.int32, (1, LANES), 1)
    low = lane < HEAD_DIM

    kvcf_ref[...] = kvc_ref[...].astype(f32)
    xcat = jnp.concatenate([kvcf_ref[pl.ds(l, N_CMP_PAD, stride=CMP_STRIDE), :] for l in range(CMP_STRIDE)],
                           axis=1)
    xa = (xcat + pelo_ref[...]).astype(bf16)
    xb = (xcat + pehi_ref[...]).astype(bf16)
    a = jnp.dot(xa, wlo_ref[...], preferred_element_type=f32)
    bm = jnp.dot(xb, whi_ref[...], preferred_element_type=f32)
    h = a + pltpu.roll(bm, N_CMP_PAD - 1, 0)
    hid = h * _sigmoid(h)
    kcvc = jnp.dot(hid.astype(bf16), w2_ref[...], preferred_element_type=f32)

    ksw = ksw_ref[...]
    vsw = vsw_ref[...]
    one = jnp.ones((1, LANES), bf16)
    ks_ref[...] = jnp.where(low, ksw, kaug_s_ref[...])
    kwin = pltpu.roll(ksw.astype(f32), HEAD_DIM, 1).astype(bf16)
    kw_ref[0:WIN_PAD, :] = kaug_w_ref[0:WIN_PAD, :]
    kw_ref[WIN_PAD:, :] = jnp.where(low, kwin, kaug_w_ref[WIN_PAD:, :])
    vs_ref[...] = jnp.where(low, vsw, one)
    vw_ref[0:WIN_PAD, :] = jnp.zeros((WIN_PAD, LANES), bf16)
    vw_ref[WIN_PAD:, :] = jnp.where(low, one, vsw)

    qaug = qaug_ref[0]
    kc_aug = jnp.where(low, kcvc, kaug_c_ref[...]).astype(bf16)
    vc_rows = jnp.concatenate([jnp.zeros((N_CMP_PAD, LANES), bf16), kcvc.astype(bf16)], axis=1)
    zeros_w = jnp.zeros((WIN_KEYS, LANES), bf16)
    ovl = ovl_ref[...]

    def cw_body(i, _):
        t0 = pl.multiple_of(i * CW_TQ, CW_TQ)
        sig = gate_ref[pl.ds(t0, CW_TQ), :].astype(f32)
        o_w, o_c = [], []
        n_tiles = CW_TQ // WIN_TQ

        def tile_scores(h):
            tw = pl.multiple_of(t0 + h * WIN_TQ, WIN_TQ)
            qw = _stack_heads(q_ref[pl.ds(tw, WIN_TQ), :].astype(f32), qaug[GROUP:2 * GROUP], None)
            k_all = jnp.concatenate([kw_ref[pl.ds(tw, WIN_KEYS), :], kc_aug], axis=0)
            return _dot_nt(qw, k_all)

        s_next = tile_scores(0)
        for h in range(n_tiles):
            tw = pl.multiple_of(t0 + h * WIN_TQ, WIN_TQ)
            s = s_next
            if h + 1 < n_tiles:
                s_next = tile_scores(h + 1)

            sw = jnp.concatenate([s[:, :WIN_TQ] + band_ref[:, :WIN_TQ], s[:, WIN_TQ:WINDOW],
                                  s[:, WINDOW:WIN_KEYS] + band_ref[:, WINDOW:]], axis=1)
            p_w = jnp.exp2(sw - jnp.max(sw, axis=-1, keepdims=True))

            ok = cend_ref[...] <= tw
            sm = jnp.where(ok, s[:, WIN_KEYS:], -jnp.inf)
            m = jnp.max(sm, axis=-1, keepdims=True)
            m = jnp.where(m > -jnp.inf, m, 0.0)
            e = jnp.where(ok, jnp.exp2(sm - m), 0.0)
            p_c = e / jnp.maximum(jnp.sum(e, axis=-1, keepdims=True), 1e-30)

            p_all = jnp.concatenate([p_w.astype(bf16), p_c.astype(bf16)], axis=1)
            v_all = jnp.concatenate(
                [jnp.concatenate([vw_ref[pl.ds(tw, WIN_KEYS), :], zeros_w], axis=1), vc_rows], axis=0)
            acc = jnp.dot(p_all, v_all, preferred_element_type=f32)
            acc_w = acc[:, :LANES]
            o_w.append(acc_w / jnp.maximum(acc_w[:, 0:1], 1e-30))
            o_c.append(acc[:, LANES:])

            ps = p_c[0:WIN_TQ] + p_c[WIN_TQ:2 * WIN_TQ] + p_c[2 * WIN_TQ:3 * WIN_TQ] + p_c[3 * WIN_TQ:4 * WIN_TQ]
            p1 = ps.astype(bf16)
            r1 = ps - p1.astype(f32)
            p2 = r1.astype(bf16)
            p3 = (r1 - p2.astype(f32)).astype(bf16)
            imp_ref[i * (CW_TQ // WIN_TQ) + h] = _dot_nt(ovl, p1) + _dot_nt(ovl, p2) + _dot_nt(ovl, p3)

        vals = []
        for r in range(GROUP):
            head = lambda parts: jnp.concatenate([o[r * WIN_TQ:(r + 1) * WIN_TQ, :] for o in parts], axis=0)
            a = head(o_c) * sig[:, 3 * r:3 * r + 1] + head(o_w) * sig[:, 3 * r + 2:3 * r + 3]
            vals.append(pltpu.roll(a, HEAD_DIM, 1) if r % 2 == 0 else a)
        c0, c1 = _merge_heads(vals, lane)
        ocw_ref[pl.ds(t0, CW_TQ), 0:LANES] = c0
        ocw_ref[pl.ds(t0, CW_TQ), LANES:2 * LANES] = c1
        return 0

    lax.fori_loop(0, seq // CW_TQ, cw_body, 0)

    n_slc = seq // SLC_LEN
    n_tt = seq // LANES
    shp = (n_tt, LANES)
    cur = (lax.broadcasted_iota(jnp.int32, shp, 0) * LANES + lax.broadcasted_iota(jnp.int32, shp, 1)) // SLC_LEN
    vals = []
    for j in range(n_slc):
        forced = (cur == j) | (cur == j + 1)
        vj = jnp.where(cur >= j, imp_ref[:, j, :], -jnp.inf)
        vals.append(jnp.full(shp, jnp.inf, f32) if j == 0 else jnp.where(forced, jnp.inf, vj))
    wins = [jnp.zeros(shp, f32) for _ in range(n_slc)]
    losses = [jnp.zeros(shp, f32) for _ in range(n_slc)]
    for a in range(n_slc):
        for b in range(a + 1, n_slc):
            a_first = jnp.where(vals[a] >= vals[b], 1.0, 0.0)
            wins[a] = wins[a] + a_first
            losses[b] = losses[b] + a_first
    for j in range(n_slc):
        rank = losses[j] + (float(n_slc - 1 - j) - wins[j])
        sel = (rank < float(N_SEL)) & (vals[j] > -jnp.inf)
        ns_ref[j] = jnp.where(sel, 0.0, 1.0)
    eye = (lax.broadcasted_iota(jnp.int32, (LANES, LANES), 0)
           == lax.broadcasted_iota(jnp.int32, (LANES, LANES), 1)).astype(bf16)
    zpad = jnp.zeros((L_SEL, LANES), f32)
    for tt in range(n_tt):
        z = jnp.concatenate([zpad, ns_ref[:, tt, :]], axis=0).astype(bf16)
        selq_ref[tt * LANES:(tt + 1) * LANES, :] = _dot_nt(eye, z) * (-MASK_BIG)

    half = SEL_CHUNK // 2
    half_rows = GROUP * half

    def stacked_q(i):
        parts = []
        for h in range(2):
            tok = slice(i * SEL_CHUNK + h * half, i * SEL_CHUNK + (h + 1) * half)
            parts.append(_stack_heads(q_ref[tok, :].astype(f32), qaug[0:GROUP], selq_ref[tok, :]))
        return parts

    def causal(s):
        tri = tri_ref[...]
        return jnp.concatenate([s[r * half:(r + 1) * half, :] + tri for r in range(GROUP)], axis=0)

    def scores(qs, i, c):
        k0 = c * SEL_CHUNK
        if c < i:
            return (_dot_nt(jnp.concatenate(qs, axis=0), ks_ref[k0:k0 + SEL_CHUNK, :]),)
        s_a = causal(_dot_nt(qs[0], ks_ref[k0:k0 + half, :]))
        s_b = _dot_nt(qs[1], ks_ref[k0:k0 + SEL_CHUNK, :])
        return (s_a, jnp.concatenate([s_b[:, :half], causal(s_b[:, half:])], axis=1))

    def softmax_step(s, v, carry):
        if carry is None:
            m = jnp.max(s, axis=-1, keepdims=True)
            return m, jnp.dot(jnp.exp2(s - m).astype(bf16), v, preferred_element_type=f32)
        return _softmax_step(s, v, carry)

    def finish(i, acc):
        tok = slice(i * SEL_CHUNK, (i + 1) * SEL_CHUNK)
        o_s = acc / jnp.maximum(acc[:, HEAD_DIM:HEAD_DIM + 1], 1e-30)
        sig = gate_ref[tok, :].astype(f32)
        vals = []
        for r in range(GROUP):
            a_s = jnp.concatenate([o_s[h * half_rows + r * half:h * half_rows + (r + 1) * half, :]
                                   for h in range(2)], axis=0) * sig[:, 3 * r + 1:3 * r + 2]
            vals.append(a_s if r % 2 == 0 else pltpu.roll(a_s, HEAD_DIM, 1))
        c0, c1 = _merge_heads(vals, lane)
        o_ref[tok, 0:LANES] = (ocw_ref[tok, 0:LANES] + c0).astype(bf16)
        o_ref[tok, LANES:2 * LANES] = (ocw_ref[tok, LANES:2 * LANES] + c1).astype(bf16)

    steps = [(i, c) for i in range(seq // SEL_CHUNK) for c in range(i + 1)]
    qs = stacked_q(0)
    s_next = scores(qs, 0, 0)
    carry = None
    for k, (i, c) in enumerate(steps):
        s = s_next
        if k + 1 < len(steps):
            i2, c2 = steps[k + 1]
            if i2 != i:
                qs = stacked_q(i2)
            s_next = scores(qs, i2, c2)
        k0 = c * SEL_CHUNK
        if c < i:
            carry = softmax_step(s[0], vs_ref[k0:k0 + SEL_CHUNK, :], carry)
        else:
            rows_a = None if carry is None else tuple(a[:half_rows] for a in carry)
            rows_b = None if carry is None else tuple(a[half_rows:] for a in carry)
            _, acc_a = softmax_step(s[0], vs_ref[k0:k0 + half, :], rows_a)
            _, acc_b = softmax_step(s[1], vs_ref[k0:k0 + SEL_CHUNK, :], rows_b)
            finish(i, jnp.concatenate([acc_a, acc_b], axis=0))
            carry = None


def _position_lanes(pos):
    out = np.zeros((pos.shape[0], LANES), np.float32)
    out[:, L_HI:L_HI + 3] = ((pos // 64) * 64)[:, None]
    out[:, L_LO:L_LO + 3] = (pos % 64)[:, None]
    return out


def _nsa_tables(seq):
    pos = np.arange(seq)
    kaug_s = _position_lanes(pos)
    kaug_s[pos, L_SEL + pos // SLC_LEN] = 1.0
    kaug_w = np.zeros((seq + WIN_PAD, LANES), np.float32)
    kaug_w[WIN_PAD:] = _position_lanes(pos)
    kaug_w[:WIN_PAD, L_FLAG] = 1.0
    kaug_c = _position_lanes(np.arange(N_CMP_PAD) * CMP_STRIDE + CMP_LEN - 1)
    tl = np.arange(SEL_CHUNK // 2)[:, None]
    kk = np.arange(SEL_CHUNK // 2)[None, :]
    tri = np.where(kk <= tl, 0.0, -MASK_BIG).astype(np.float32)
    tl = np.arange(WIN_TQ)[:, None]
    kk = np.arange(WIN_KEYS)[None, :]
    dist = tl + WIN_PAD - kk
    band = np.tile(np.where((dist >= 0) & (dist < WINDOW), 0.0, -MASK_BIG).astype(np.float32), (GROUP, 1))
    c = np.arange(N_CMP_PAD)[None, :]
    j = np.arange(seq // SLC_LEN)[:, None]
    c_start, c_end, s_start = c * CMP_STRIDE, c * CMP_STRIDE + CMP_LEN - 1, j * SLC_LEN
    ovl = ((c_start <= s_start + SLC_LEN - 1) & (c_end >= s_start) & (c < seq // CMP_STRIDE - 1)).astype(np.float32)
    cend = np.tile((c * CMP_STRIDE + CMP_LEN - 1) - tl, (GROUP, 1)).astype(np.int32)
    return (jnp.asarray(kaug_s, bf16), jnp.asarray(kaug_w, bf16), jnp.asarray(kaug_c, f32),
            jnp.asarray(tri), jnp.asarray(band), jnp.asarray(ovl, bf16), jnp.asarray(cend))


def _query_aug():
    slopes = jnp.exp2(-8.0 * jnp.arange(1, N_HEADS + 1, dtype=f32) / N_HEADS).reshape(N_KV, GROUP) * LOG2E
    hi =slopes.astype(bf16).astype(f32)
    mid = (slopes - hi).astype(bf16).astype(f32)
    lo = (slopes - hi - mid).astype(bf16).astype(f32)
    pieces = jnp.stack([hi, mid, lo, hi, mid, lo], axis=-1)
    base = jnp.zeros((N_KV, GROUP, LANES), f32).at[:, :, L_HI:L_HI + 6].set(pieces)
    return jnp.concatenate([base, base.at[:, :, L_FLAG].set(-MASK_BIG)], axis=1)


def _nsa(proj, wlo, whi, pelo, pehi, w2bd, batch, seq):
    kaug_s, kaug_w, kaug_c, tri, band, ovl, cend = _nsa_tables(seq)
    qaug = _query_aug()
    const2 = lambda b, g: (0, 0)
    in_specs = [
        pl.BlockSpec((seq, 2 * LANES), lambda b, g: (b, OFF_Q // (2 * LANES) + g)),
        pl.BlockSpec((seq, LANES), lambda b, g: (b, OFF_KSW // LANES + g)),
        pl.BlockSpec((seq, LANES), lambda b, g: (b, OFF_VSW // LANES + g)),
        pl.BlockSpec((seq, LANES), lambda b, g: (b, OFF_GATE // LANES + g)),
        pl.BlockSpec((seq, LANES), lambda b, g: (b, OFF_KVC // LANES + g)),
        pl.BlockSpec(wlo.shape, const2),
        pl.BlockSpec(whi.shape, const2),
        pl.BlockSpec(pelo.shape, const2),
        pl.BlockSpec(pehi.shape, const2),
        pl.BlockSpec(w2bd.shape, const2),
        pl.BlockSpec(kaug_s.shape, const2),
        pl.BlockSpec(kaug_w.shape, const2),
        pl.BlockSpec(kaug_c.shape, const2),
        pl.BlockSpec((1, 2 * GROUP, LANES), lambda b, g: (g, 0, 0)),
        pl.BlockSpec(tri.shape, const2),
        pl.BlockSpec(band.shape, const2),
        pl.BlockSpec(ovl.shape, const2),
        pl.BlockSpec(cend.shape, const2),
    ]
    return pl.pallas_call(
        _nsa_kernel,
        grid=(batch, N_KV),
        in_specs=in_specs,
        out_specs=pl.BlockSpec((seq, 2 * LANES), lambda b, g: (b, g)),
        out_shape=jax.ShapeDtypeStruct((batch * seq, D_ATT), bf16),
        scratch_shapes=[
            pltpu.VMEM((seq, LANES), f32),
            pltpu.VMEM((seq, LANES), bf16),
            pltpu.VMEM((seq + WIN_PAD, LANES), bf16),
            pltpu.VMEM((seq, LANES), bf16),
            pltpu.VMEM((seq + WIN_PAD, LANES), bf16),
            pltpu.VMEM((seq, 2 * LANES), f32),
            pltpu.VMEM((seq // LANES, seq // SLC_LEN, LANES), f32),
            pltpu.VMEM((seq // SLC_LEN, seq // LANES, LANES), f32),
            pltpu.VMEM((seq, LANES), f32),
        ],
        compiler_params=pltpu.CompilerParams(
            dimension_semantics=("parallel", "parallel"), vmem_limit_bytes=VMEM_LIMIT),
        name="nsa",
    )(proj, proj, proj, proj, proj, wlo, whi, pelo, pehi, w2bd, kaug_s, kaug_w, kaug_c, qaug, tri, band, ovl, cend)


def _out_kernel(x_ref, ha_ref, bg_ref, cg_ref, za_ref, hap_ref, cgp_ref, zb_ref, gm_ref, ob_ref,
                cw_ref, cb_ref, wa_ref, wb_ref, wo_ref, fnw_ref, o_ref, *, tiles_per_seq):
    tm = x_ref.shape[0]
    halo = hap_ref.shape[0]
    not_first = (pl.program_id(0) % tiles_per_seq != 0).astype(f32)
    u = cg_ref[...].astype(f32) * ha_ref[...].astype(f32)
    u_prev = cgp_ref[...].astype(f32) * hap_ref[...].astype(f32) * not_first
    row = lax.broadcasted_iota(jnp.int32, (tm, 1), 0)
    p1 = u_prev[halo - 1:halo, :]
    p2 = u_prev[halo - 2:halo - 1, :]
    u1 = jnp.where(row == 0, p1, pltpu.roll(u, 1, 0))
    u2 = jnp.where(row == 0, p2, jnp.where(row == 1, p1, pltpu.roll(u, 2, 0)))
    cw = cw_ref[...]
    y = cb_ref[...] + cw[0:1, :] * u2 + cw[1:2, :] * u1 + cw[2:3, :] * u
    ya_in = bg_ref[...].astype(f32) * y * za_ref[...].astype(f32)
    y_a = jnp.dot(ya_in.astype(bf16), wa_ref[...], preferred_element_type=f32)
    yb_in = ob_ref[...].astype(f32) * zb_ref[...].astype(f32)
    y_b = jnp.dot(yb_in.astype(bf16), wb_ref[...], preferred_element_type=f32)
    mixed = (gm_ref[:, 0:D_MODEL].astype(f32) * y_a + gm_ref[:, D_MODEL:2 * D_MODEL].astype(f32) * y_b)
    xo = x_ref[...] + jnp.dot(mixed.astype(bf16), wo_ref[...], preferred_element_type=f32)
    r = lax.rsqrt(jnp.mean(xo * xo, axis=-1, keepdims=True) + NORM_EPS)
    o_ref[...] = (xo * r) * fnw_ref[...]


def _out(x2, proj, ob, conv_w, conv_b, wa, wb, wo, fnw, seq, tm=512, halo=16):
    n_rows = x2.shape[0]
    blk = lambda c: pl.BlockSpec((tm, D_MODEL), lambda i, c=c: (i, c))
    prev = lambda c: pl.BlockSpec((halo, D_MODEL), lambda i, c=c: (jnp.maximum(i * (tm // halo) - 1, 0), c))
    full = lambda a: pl.BlockSpec(a.shape, lambda i: (0, 0))
    return pl.pallas_call(
        functools.partial(_out_kernel, tiles_per_seq=seq // tm),
        grid=(n_rows // tm,),
        in_specs=[
            blk(0),
            blk(0), blk(1), blk(2), blk(3),
            prev(0), prev(2),
            blk(OFF_ZB // D_MODEL),
            pl.BlockSpec((tm, 2 * D_MODEL), lambda i: (i, OFF_GMIX // (2 * D_MODEL))),
            blk(0),
            full(conv_w), full(conv_b), full(wa), full(wb), full(wo), full(fnw),
        ],
        out_specs=pl.BlockSpec((tm, D_MODEL), lambda i: (i, 0)),
        out_shape=jax.ShapeDtypeStruct((n_rows, D_MODEL), f32),
        compiler_params=pltpu.CompilerParams(
            dimension_semantics=("parallel",), vmem_limit_bytes=VMEM_LIMIT),
        name="merge_out",
    )(x2, proj, proj, proj, proj, proj, proj, proj, proj, ob, conv_w, conv_b, wa, wb, wo, fnw)


def _blockdiag_w1(w1_k, w1_v, lo):
    half = CMP_STRIDE * HEAD_DIM
    wk = w1_k[lo * half:(lo + 1) * half].reshape(CMP_STRIDE, HEAD_DIM, CMP_HID)
    wv = w1_v[lo * half:(lo + 1) * half].reshape(CMP_STRIDE, HEAD_DIM, CMP_HID)
    z = jnp.zeros_like(wk)
    top = jnp.concatenate([wk, z], axis=-1)
    bot = jnp.concatenate([z, wv], axis=-1)
    return jnp.concatenate([top, bot], axis=1).reshape(CMP_STRIDE * 2 * HEAD_DIM, 2 * CMP_HID).astype(bf16)


def _pe_row(pe_k, pe_v, lo):
    sl = slice(lo * CMP_STRIDE, (lo + 1) * CMP_STRIDE)
    return jnp.concatenate([pe_k[sl], pe_v[sl]], axis=-1).reshape(1, CMP_STRIDE * 2 * HEAD_DIM)


def kernel(x, norm_w, w_in, conv_w, conv_b, cmp_pe_k, cmp_pe_v, cmp_w1_k, cmp_w2_k, cmp_w1_v, cmp_w2_v,
           w_proj_a, w_proj_b, w_out, final_norm_w):
    batch, seq, _ = x.shape
    assert norm_w.shape[0] == 1 and seq % SEL_CHUNK == 0 and seq // CMP_STRIDE == N_CMP_PAD
    w_p = _reordered_w_in(w_in[0])
    assert w_p.shape[1] == N_PAD
    x2 = x.reshape(batch * seq, D_MODEL)
    proj = _inproj(x2, norm_w, w_p)

    wlo = _blockdiag_w1(cmp_w1_k[0], cmp_w1_v[0], 0)
    whi = _blockdiag_w1(cmp_w1_k[0], cmp_w1_v[0], 1)
    pelo = _pe_row(cmp_pe_k[0], cmp_pe_v[0], 0)
    pehi = _pe_row(cmp_pe_k[0], cmp_pe_v[0], 1)
    zk = jnp.zeros_like(cmp_w2_k[0])
    w2bd = jnp.concatenate([jnp.concatenate([cmp_w2_k[0], zk], axis=1),
                            jnp.concatenate([zk, cmp_w2_v[0]], axis=1)], axis=0).astype(bf16)
    ob = _nsa(proj, wlo, whi, pelo, pehi, w2bd, batch, seq)

    out = _out(x2, proj, ob, conv_w[0], conv_b, w_proj_a[0].astype(bf16), w_proj_b[0].astype(bf16),
               w_out[0].astype(bf16), final_norm_w.reshape(1, D_MODEL), seq)
    return out.reshape(batch, seq, D_MODEL)
```

```python
import functools

import numpy as np
import jax
import jax.numpy as jnp
from jax import lax
from jax.experimental import pallas as pl
from jax.experimental.pallas import tpu as pltpu

D_MODEL = 1024
D_CONV = 1024
CONV_W = 3
N_HEADS = 16
HEAD_DIM = 64
N_KV = 4
GROUP = N_HEADS // N_KV
D_ATT = N_HEADS * HEAD_DIM
D_KV = N_KV * HEAD_DIM
CMP_LEN = 32
CMP_STRIDE = 16
CMP_HID = 128
SLC_LEN = 64
N_SEL = 8
WINDOW = 512
NORM_EPS = 1e-6

LANES = 128
N_CMP_PAD = 128
MASK_BIG = float(2.0 ** 100)
LOG2E = float(np.log2(np.e))
SEL_CHUNK = 512
WIN_TQ = 128
WIN_KEYS = WINDOW + WIN_TQ
WIN_PAD = WINDOW
CW_TQ = 512

OFF_CONV = 0
OFF_Q = 4096
OFF_ZB = 5120
OFF_GMIX = 6144
OFF_KVC = 8192
OFF_KSW = 8704
OFF_VSW = 9216
OFF_GATE = 9728
N_PAD = 10240

L_HI = 64
L_LO = 67
L_FLAG = 70
L_SEL = 96

VMEM_LIMIT = 56 * 1024 * 1024

f32 = jnp.float32
bf16 = jnp.bfloat16


def _reordered_w_in(w):
    o_q = 4 * D_CONV
    o_kv = o_q + D_ATT
    o_gate = o_kv + 6 * D_KV
    o_zb = o_gate + 3 * N_HEADS
    o_gmix = o_zb + D_ATT
    kv = w[:, o_kv:o_gate].reshape(D_MODEL, 6, N_KV, HEAD_DIM)
    pair = lambda a, b: jnp.stack([kv[:, a], kv[:, b]], axis=2).reshape(D_MODEL, N_KV * LANES)
    gates = jnp.pad(w[:, o_gate:o_zb].reshape(D_MODEL, N_KV, 3 * GROUP),
                    ((0, 0), (0, 0), (0, LANES - 3 * GROUP))).reshape(D_MODEL, N_KV * LANES)
    cols = [w[:, :o_q], w[:, o_q:o_kv] * (HEAD_DIM ** -0.5 * LOG2E), w[:, o_zb:o_gmix], w[:, o_gmix:],
            pair(0, 1), pair(2, 4), pair(3, 5), gates]
    return jnp.concatenate(cols, axis=1).astype(bf16)


def _sigmoid(x):
    return 1.0 / (1.0 + jnp.exp(-x))


def _dot_nt(a, b):
    return lax.dot_general(a, b, (((1,), (1,)), ((), ())), preferred_element_type=f32)


def _inproj_kernel(x_ref, nw_ref, w_ref, o_ref, h_ref):
    @pl.when(pl.program_id(1) == 0)
    def _():
        x = x_ref[...]
        r = lax.rsqrt(jnp.mean(x * x, axis=-1, keepdims=True) + NORM_EPS)
        h_ref[...] = ((x * r) * nw_ref[...]).astype(bf16)

    product = lambda: jnp.dot(h_ref[...], w_ref[...], preferred_element_type=f32)
    tn = o_ref.shape[1]
    j = pl.program_id(1)
    is_silu = (j == (OFF_CONV + 3 * D_CONV) // tn) | (j == OFF_ZB // tn)
    is_sigmoid = (j == OFF_GMIX // tn) | (j == OFF_GMIX // tn + 1)
    is_tail = j == OFF_GATE // tn
    split = OFF_GATE % tn

    @pl.when(is_silu)
    def _():
        acc = product()
        o_ref[...] = (acc * _sigmoid(acc)).astype(bf16)

    @pl.when(is_sigmoid)
    def _():
        o_ref[...] = _sigmoid(product()).astype(bf16)

    @pl.when(is_tail)
    def _():
        acc = product()
        o_ref[:, :split] = acc[:, :split].astype(bf16)
        o_ref[:, split:] = _sigmoid(acc[:, split:]).astype(bf16)

    @pl.when(jnp.logical_not(is_silu | is_sigmoid | is_tail))
    def _():
        o_ref[...] = product().astype(bf16)


def _inproj(x2, norm_w, w_p, tm=2048, tn=1024):
    n_rows = x2.shape[0]
    assert D_CONV % tn == 0 and OFF_ZB % tn == 0 and OFF_GMIX % tn == 0 and D_MODEL == tn and N_PAD - OFF_GATE < tn
    return pl.pallas_call(
        _inproj_kernel,
        grid=(n_rows // tm, N_PAD // tn),
        in_specs=[
            pl.BlockSpec((tm, D_MODEL), lambda i, j: (i, 0)),
            pl.BlockSpec((1, D_MODEL), lambda i, j: (0, 0)),
            pl.BlockSpec((D_MODEL, tn), lambda i, j: (0, j)),
        ],
        out_specs=pl.BlockSpec((tm, tn), lambda i, j: (i, j)),
        out_shape=jax.ShapeDtypeStruct((n_rows, N_PAD), bf16),
        scratch_shapes=[pltpu.VMEM((tm, D_MODEL), bf16)],
        compiler_params=pltpu.CompilerParams(
            dimension_semantics=("parallel", "arbitrary"), vmem_limit_bytes=VMEM_LIMIT),
        name="inproj",
    )(x2, norm_w, w_p)


def _softmax_step(s, v, carry):
    m, acc = carry
    m_new = jnp.maximum(m, jnp.max(s, axis=-1, keepdims=True))
    alpha = jnp.exp2(m - m_new)
    p = jnp.exp2(s - m_new)
    acc = alpha * acc + jnp.dot(p.astype(bf16), v, preferred_element_type=f32)
    return m_new, acc


def _stack_heads(q_tile, aug_rows, extra):
    lane = lax.broadcasted_iota(jnp.int32, (1, LANES), 1)
    pairs = (q_tile[:, :LANES], q_tile[:, LANES:])
    parts = []
    for r in range(GROUP):
        qh = pairs[r // 2]
        if r % 2 == 1:
            qh = pltpu.roll(qh, HEAD_DIM, 1)
        aug = aug_rows[r:r + 1, :]
        if extra is not None:
            aug = aug + extra
        parts.append(jnp.where(lane < HEAD_DIM, qh, aug).astype(bf16))
    return jnp.concatenate(parts, axis=0)


def _merge_heads(vals, lane):
    return (jnp.where(lane < HEAD_DIM, vals[0], vals[1]), jnp.where(lane < HEAD_DIM, vals[2], vals[3]))


def _nsa_kernel(q_ref, ksw_ref, vsw_ref, gate_ref, kvc_ref, wlo_ref, whi_ref, pelo_ref, pehi_ref, w2_ref,
                kaug_s_ref, kaug_w_ref, kaug_c_ref, qaug_ref, tri_ref, band_ref, ovl_ref, cend_ref,
                o_ref,
                kvcf_ref, ks_ref, kw_ref, vs_ref, vw_ref, ocw_ref, imp_ref, ns_ref, selq_ref):
    seq = q_ref.shape[0]
    lane = lax.broadcasted_iota(jnp.int32, (1, LANES), 1)
    low = lane < HEAD_DIM

    kvcf_ref[...] = kvc_ref[...].astype(f32)
    xcat = jnp.concatenate([kvcf_ref[pl.ds(l, N_CMP_PAD, stride=CMP_STRIDE), :] for l in range(CMP_STRIDE)],
                           axis=1)
    xa = (xcat + pelo_ref[...]).astype(bf16)
    xb = (xcat + pehi_ref[...]).astype(bf16)
    a = jnp.dot(xa, wlo_ref[...], preferred_element_type=f32)
    bm = jnp.dot(xb, whi_ref[...], preferred_element_type=f32)
    h = a + pltpu.roll(bm, N_CMP_PAD - 1, 0)
    hid = h * _sigmoid(h)
    kcvc = jnp.dot(hid.astype(bf16), w2_ref[...], preferred_element_type=f32)

    ksw = ksw_ref[...]
    vsw = vsw_ref[...]
    one = jnp.ones((1, LANES), bf16)
    ks_ref[...] = jnp.where(low, ksw, kaug_s_ref[...])
    kwin = pltpu.roll(ksw.astype(f32), HEAD_DIM, 1).astype(bf16)
    kw_ref[0:WIN_PAD, :] = kaug_w_ref[0:WIN_PAD, :]
    kw_ref[WIN_PAD:, :] = jnp.where(low, kwin, kaug_w_ref[WIN_PAD:, :])
    vs_ref[...] = jnp.where(low, vsw, one)
    vw_ref[0:WIN_PAD, :] = jnp.zeros((WIN_PAD, LANES), bf16)
    vw_ref[WIN_PAD:, :] = jnp.where(low, one, vsw)

    qaug = qaug_ref[0]
    kc_aug = jnp.where(low, kcvc, kaug_c_ref[...]).astype(bf16)
    vc_rows = jnp.concatenate([jnp.zeros((N_CMP_PAD, LANES), bf16), kcvc.astype(bf16)], axis=1)
    zeros_w = jnp.zeros((WIN_KEYS, LANES), bf16)
    ovl = ovl_ref[...]

    def cw_body(i, _):
        t0 = pl.multiple_of(i * CW_TQ, CW_TQ)
        sig = gate_ref[pl.ds(t0, CW_TQ), :].astype(f32)
        o_w, o_c = [], []
        n_tiles = CW_TQ // WIN_TQ

        def tile_scores(h):
            tw = pl.multiple_of(t0 + h * WIN_TQ, WIN_TQ)
            qw = _stack_heads(q_ref[pl.ds(tw, WIN_TQ), :].astype(f32), qaug[GROUP:2 * GROUP], None)
            k_all = jnp.concatenate([kw_ref[pl.ds(tw, WIN_KEYS), :], kc_aug], axis=0)
            return _dot_nt(qw, k_all)

        s_next = tile_scores(0)
        for h in range(n_tiles):
            tw = pl.multiple_of(t0 + h * WIN_TQ, WIN_TQ)
            s = s_next
            if h + 1 < n_tiles:
                s_next = tile_scores(h + 1)

            sw = jnp.concatenate([s[:, :WIN_TQ] + band_ref[:, :WIN_TQ], s[:, WIN_TQ:WINDOW],
                                  s[:, WINDOW:WIN_KEYS] + band_ref[:, WINDOW:]], axis=1)
            p_w = jnp.exp2(sw - jnp.max(sw, axis=-1, keepdims=True))

            ok = cend_ref[...] <= tw
            sm = jnp.where(ok, s[:, WIN_KEYS:], -jnp.inf)
            m = jnp.max(sm, axis=-1, keepdims=True)
            m = jnp.where(m > -jnp.inf, m, 0.0)
            e = jnp.where(ok, jnp.exp2(sm - m), 0.0)
            p_c = e / jnp.maximum(jnp.sum(e, axis=-1, keepdims=True), 1e-30)

            p_all = jnp.concatenate([p_w.astype(bf16), p_c.astype(bf16)], axis=1)
            v_all = jnp.concatenate(
                [jnp.concatenate([vw_ref[pl.ds(tw, WIN_KEYS), :], zeros_w], axis=1), vc_rows], axis=0)
            acc = jnp.dot(p_all, v_all, preferred_element_type=f32)
            acc_w = acc[:, :LANES]
            o_w.append(acc_w / jnp.maximum(acc_w[:, 0:1], 1e-30))
            o_c.append(acc[:, LANES:])

            ps = p_c[0:WIN_TQ] + p_c[WIN_TQ:2 * WIN_TQ] + p_c[2 * WIN_TQ:3 * WIN_TQ] + p_c[3 * WIN_TQ:4 * WIN_TQ]
            p1 = ps.astype(bf16)
            r1 = ps - p1.astype(f32)
            p2 = r1.astype(bf16)
            p3 = (r1 - p2.astype(f32)).astype(bf16)
            imp_ref[i * (CW_TQ // WIN_TQ) + h] = _dot_nt(ovl, p1) + _dot_nt(ovl, p2) + _dot_nt(ovl, p3)

        vals = []
        for r in range(GROUP):
            head = lambda parts: jnp.concatenate([o[r * WIN_TQ:(r + 1) * WIN_TQ, :] for o in parts], axis=0)
            a = head(o_c) * sig[:, 3 * r:3 * r + 1] + head(o_w) * sig[:, 3 * r + 2:3 * r + 3]
            vals.append(pltpu.roll(a, HEAD_DIM, 1) if r % 2 == 0 else a)
        c0, c1 = _merge_heads(vals, lane)
        ocw_ref[pl.ds(t0, CW_TQ), 0:LANES] = c0
        ocw_ref[pl.ds(t0, CW_TQ), LANES:2 * LANES] = c1
        return 0

    lax.fori_loop(0, seq // CW_TQ, cw_body, 0)

    n_slc = seq // SLC_LEN
    n_tt = seq // LANES
    shp = (n_tt, LANES)
    cur = (lax.broadcasted_iota(jnp.int32, shp, 0) * LANES + lax.broadcasted_iota(jnp.int32, shp, 1)) // SLC_LEN
    vals = []
    for j in range(n_slc):
        forced = (cur == j) | (cur == j + 1)
        vj = jnp.where(cur >= j, imp_ref[:, j, :], -jnp.inf)
        vals.append(jnp.full(shp, jnp.inf, f32) if j == 0 else jnp.where(forced, jnp.inf, vj))
    wins = [jnp.zeros(shp, f32) for _ in range(n_slc)]
    losses = [jnp.zeros(shp, f32) for _ in range(n_slc)]
    for a in range(n_slc):
        for b in range(a + 1, n_slc):
            a_first = jnp.where(vals[a] >= vals[b], 1.0, 0.0)
            wins[a] = wins[a] + a_first
            losses[b] = losses[b] + a_first
    for j in range(n_slc):
        rank = losses[j] + (float(n_slc - 1 - j) - wins[j])
        sel = (rank < float(N_SEL)) & (vals[j] > -jnp.inf)
        ns_ref[j] = jnp.where(sel, 0.0, 1.0)
    eye = (lax.broadcasted_iota(jnp.int32, (LANES, LANES), 0)
           == lax.broadcasted_iota(jnp.int32, (LANES, LANES), 1)).astype(bf16)
    zpad = jnp.zeros((L_SEL, LANES), f32)
    for tt in range(n_tt):
        z = jnp.concatenate([zpad, ns_ref[:, tt, :]], axis=0).astype(bf16)
        selq_ref[tt * LANES:(tt + 1) * LANES, :] = _dot_nt(eye, z) * (-MASK_BIG)

    half = SEL_CHUNK // 2
    half_rows = GROUP * half

    def stacked_q(i):
        parts = []
        for h in range(2):
            tok = slice(i * SEL_CHUNK + h * half, i * SEL_CHUNK + (h + 1) * half)
            parts.append(_stack_heads(q_ref[tok, :].astype(f32), qaug[0:GROUP], selq_ref[tok, :]))
        return parts

    def causal(s):
        tri = tri_ref[...]
        return jnp.concatenate([s[r * half:(r + 1) * half, :] + tri for r in range(GROUP)], axis=0)

    def scores(qs, i, c):
        k0 = c * SEL_CHUNK
        if c < i:
            return (_dot_nt(jnp.concatenate(qs, axis=0), ks_ref[k0:k0 + SEL_CHUNK, :]),)
        s_a = causal(_dot_nt(qs[0], ks_ref[k0:k0 + half, :]))
        s_b = _dot_nt(qs[1], ks_ref[k0:k0 + SEL_CHUNK, :])
        return (s_a, jnp.concatenate([s_b[:, :half], causal(s_b[:, half:])], axis=1))

    def softmax_step(s, v, carry):
        if carry is None:
            m = jnp.max(s, axis=-1, keepdims=True)
            return m, jnp.dot(jnp.exp2(s - m).astype(bf16), v, preferred_element_type=f32)
        return _softmax_step(s, v, carry)

    def finish(i, acc):
        tok = slice(i * SEL_CHUNK, (i + 1) * SEL_CHUNK)
        o_s = acc / jnp.maximum(acc[:, HEAD_DIM:HEAD_DIM + 1], 1e-30)
        sig = gate_ref[tok, :].astype(f32)
        vals = []
        for r in range(GROUP):
            a_s = jnp.concatenate([o_s[h * half_rows + r * half:h * half_rows + (r + 1) * half, :]
                                   for h in range(2)], axis=0) * sig[:, 3 * r + 1:3 * r + 2]
            vals.append(a_s if r % 2 == 0 else pltpu.roll(a_s, HEAD_DIM, 1))
        c0, c1 = _merge_heads(vals, lane)
        o_ref[tok, 0:LANES] = (ocw_ref[tok, 0:LANES] + c0).astype(bf16)
        o_ref[tok, LANES:2 * LANES] = (ocw_ref[tok, LANES:2 * LANES] + c1).astype(bf16)

    steps = [(i, c) for i in range(seq // SEL_CHUNK) for c in range(i + 1)]
    qs = stacked_q(0)
    s_next = scores(qs, 0, 0)
    carry = None
    for k, (i, c) in enumerate(steps):
        s = s_next
        if k + 1 < len(steps):
            i2, c2 = steps[k + 1]
            if i2 != i:
                qs = stacked_q(i2)
            s_next = scores(qs, i2, c2)
        k0 = c * SEL_CHUNK
        if c < i:
            carry = softmax_step(s[0], vs_ref[k0:k0 + SEL_CHUNK, :], carry)
        else:
            rows_a = None if carry is None else tuple(a[:half_rows] for a in carry)
            rows_b = None if carry is None else tuple(a[half_rows:] for a in carry)
            _, acc_a = softmax_step(s[0], vs_ref[k0:k0 + half, :], rows_a)
            _, acc_b = softmax_step(s[1], vs_ref[k0:k0 + SEL_CHUNK, :], rows_b)
            finish(i, jnp.concatenate([acc_a, acc_b], axis=0))
            carry = None


def _position_lanes(pos):
    out = np.zeros((pos.shape[0], LANES), np.float32)
    out[:, L_HI:L_HI + 3] = ((pos // 64) * 64)[:, None]
    out[:, L_LO:L_LO + 3] = (pos % 64)[:, None]
    return out


def _nsa_tables(seq):
    pos = np.arange(seq)
    kaug_s = _position_lanes(pos)
    kaug_s[pos, L_SEL + pos // SLC_LEN] = 1.0
    kaug_w = np.zeros((seq + WIN_PAD, LANES), np.float32)
    kaug_w[WIN_PAD:] = _position_lanes(pos)
    kaug_w[:WIN_PAD, L_FLAG] = 1.0
    kaug_c = _position_lanes(np.arange(N_CMP_PAD) * CMP_STRIDE + CMP_LEN - 1)
    tl = np.arange(SEL_CHUNK // 2)[:, None]
    kk = np.arange(SEL_CHUNK // 2)[None, :]
    tri = np.where(kk <= tl, 0.0, -MASK_BIG).astype(np.float32)
    tl = np.arange(WIN_TQ)[:, None]
    kk = np.arange(WIN_KEYS)[None, :]
    dist = tl + WIN_PAD - kk
    band = np.tile(np.where((dist >= 0) & (dist < WINDOW), 0.0, -MASK_BIG).astype(np.float32), (GROUP, 1))
    c = np.arange(N_CMP_PAD)[None, :]
    j = np.arange(seq // SLC_LEN)[:, None]
    c_start, c_end, s_start = c * CMP_STRIDE, c * CMP_STRIDE + CMP_LEN - 1, j * SLC_LEN
    ovl = ((c_start <= s_start + SLC_LEN - 1) & (c_end >= s_start) & (c < seq // CMP_STRIDE - 1)).astype(np.float32)
    cend = np.tile((c * CMP_STRIDE + CMP_LEN - 1) - tl, (GROUP, 1)).astype(np.int32)
    return (jnp.asarray(kaug_s, bf16), jnp.asarray(kaug_w, bf16), jnp.asarray(kaug_c, f32),
            jnp.asarray(tri), jnp.asarray(band), jnp.asarray(ovl, bf16), jnp.asarray(cend))


def _query_aug():
    slopes = jnp.exp2(-8.0 * jnp.arange(1, N_HEADS + 1, dtype=f32) / N_HEADS).reshape(N_KV, GROUP) * LOG2E
    hi =slopes.astype(bf16).astype(f32)
    mid = (slopes - hi).astype(bf16).astype(f32)
    lo = (slopes - hi - mid).astype(bf16).astype(f32)
    pieces = jnp.stack([hi, mid, lo, hi, mid, lo], axis=-1)
    base = jnp.zeros((N_KV, GROUP, LANES), f32).at[:, :, L_HI:L_HI + 6].set(pieces)
    return jnp.concatenate([base, base.at[:, :, L_FLAG].set(-MASK_BIG)], axis=1)


def _nsa(proj, wlo, whi, pelo, pehi, w2bd, batch, seq):
    kaug_s, kaug_w, kaug_c, tri, band, ovl, cend = _nsa_tables(seq)
    qaug = _query_aug()
    const2 = lambda b, g: (0, 0)
    in_specs = [
        pl.BlockSpec((seq, 2 * LANES), lambda b, g: (b, OFF_Q // (2 * LANES) + g)),
        pl.BlockSpec((seq, LANES), lambda b, g: (b, OFF_KSW // LANES + g)),
        pl.BlockSpec((seq, LANES), lambda b, g: (b, OFF_VSW // LANES + g)),
        pl.BlockSpec((seq, LANES), lambda b, g: (b, OFF_GATE // LANES + g)),
        pl.BlockSpec((seq, LANES), lambda b, g: (b, OFF_KVC // LANES + g)),
        pl.BlockSpec(wlo.shape, const2),
        pl.BlockSpec(whi.shape, const2),
        pl.BlockSpec(pelo.shape, const2),
        pl.BlockSpec(pehi.shape, const2),
        pl.BlockSpec(w2bd.shape, const2),
        pl.BlockSpec(kaug_s.shape, const2),
        pl.BlockSpec(kaug_w.shape, const2),
        pl.BlockSpec(kaug_c.shape, const2),
        pl.BlockSpec((1, 2 * GROUP, LANES), lambda b, g: (g, 0, 0)),
        pl.BlockSpec(tri.shape, const2),
        pl.BlockSpec(band.shape, const2),
        pl.BlockSpec(ovl.shape, const2),
        pl.BlockSpec(cend.shape, const2),
    ]
    return pl.pallas_call(
        _nsa_kernel,
        grid=(batch, N_KV),
        in_specs=in_specs,
        out_specs=pl.BlockSpec((seq, 2 * LANES), lambda b, g: (b, g)),
        out_shape=jax.ShapeDtypeStruct((batch * seq, D_ATT), bf16),
        scratch_shapes=[
            pltpu.VMEM((seq, LANES), f32),
            pltpu.VMEM((seq, LANES), bf16),
            pltpu.VMEM((seq + WIN_PAD, LANES), bf16),
            pltpu.VMEM((seq, LANES), bf16),
            pltpu.VMEM((seq + WIN_PAD, LANES), bf16),
            pltpu.VMEM((seq, 2 * LANES), f32),
            pltpu.VMEM((seq // LANES, seq // SLC_LEN, LANES), f32),
            pltpu.VMEM((seq // SLC_LEN, seq // LANES, LANES), f32),
            pltpu.VMEM((seq, LANES), f32),
        ],
        compiler_params=pltpu.CompilerParams(
            dimension_semantics=("parallel", "parallel"), vmem_limit_bytes=VMEM_LIMIT),
        name="nsa",
    )(proj, proj, proj, proj, proj, wlo, whi, pelo, pehi, w2bd, kaug_s, kaug_w, kaug_c, qaug, tri, band, ovl, cend)


def _out_kernel(x_ref, ha_ref, bg_ref, cg_ref, za_ref, hap_ref, cgp_ref, zb_ref, gm_ref, ob_ref,
                cw_ref, cb_ref, wa_ref, wb_ref, wo_ref, fnw_ref, o_ref, ya_ref, *, tiles_per_seq, n_tiles):
    tm = x_ref.shape[0]
    halo = hap_ref.shape[0]
    s = pl.program_id(0)

    @pl.when(s == 0)
    def _():
        ya_ref[...] = jnp.zeros_like(ya_ref)

    y_a = jnp.dot(ya_ref[...], wa_ref[...], preferred_element_type=f32)
    yb_in = ob_ref[...].astype(f32) * zb_ref[...].astype(f32)
    y_b = jnp.dot(yb_in.astype(bf16), wb_ref[...], preferred_element_type=f32)
    mixed = (gm_ref[:, 0:D_MODEL].astype(f32) * y_a + gm_ref[:, D_MODEL:2 * D_MODEL].astype(f32) * y_b)
    xo = x_ref[...] + jnp.dot(mixed.astype(bf16), wo_ref[...], preferred_element_type=f32)
    r = lax.rsqrt(jnp.mean(xo * xo, axis=-1, keepdims=True) + NORM_EPS)
    o_ref[...] = (xo * r) * fnw_ref[...]

    tile = jnp.minimum(s, n_tiles - 1)
    not_first = (tile % tiles_per_seq != 0).astype(f32)
    u = cg_ref[...].astype(f32) * ha_ref[...].astype(f32)
    u_prev = cgp_ref[...].astype(f32) * hap_ref[...].astype(f32) * not_first
    row = lax.broadcasted_iota(jnp.int32, (tm, 1), 0)
    p1 = u_prev[halo - 1:halo, :]
    p2 = u_prev[halo - 2:halo - 1, :]
    u1 = jnp.where(row == 0, p1, pltpu.roll(u, 1, 0))
    u2 = jnp.where(row == 0, p2, jnp.where(row == 1, p1, pltpu.roll(u, 2, 0)))
    cw = cw_ref[...]
    y = cb_ref[...] + cw[0:1, :] * u2 + cw[1:2, :] * u1 + cw[2:3, :] * u
    ya_ref[...] = (bg_ref[...].astype(f32) * y * za_ref[...].astype(f32)).astype(bf16)


def _out(x2, proj, ob, conv_w, conv_b, wa, wb, wo, fnw, seq, tm=512, halo=16):
    n_rows = x2.shape[0]
    n_tiles = n_rows // tm
    cur = lambda i: jnp.minimum(i, n_tiles - 1)
    done = lambda i: jnp.maximum(i - 1, 0)
    conv_blk = lambda c: pl.BlockSpec((tm, D_MODEL), lambda i, c=c: (cur(i), c))
    prev_rows = lambda c: pl.BlockSpec(
        (halo, D_MODEL), lambda i, c=c: (jnp.maximum(cur(i) * (tm // halo) - 1, 0), c))
    blk = lambda c: pl.BlockSpec((tm, D_MODEL), lambda i, c=c: (done(i), c))
    full = lambda a: pl.BlockSpec(a.shape, lambda i: (0, 0))
    return pl.pallas_call(
        functools.partial(_out_kernel, tiles_per_seq=seq // tm, n_tiles=n_tiles),
        grid=(n_tiles + 1,),
        in_specs=[
            blk(0),
            conv_blk(0), conv_blk(1), conv_blk(2), conv_blk(3),
            prev_rows(0), prev_rows(2),
            blk(OFF_ZB // D_MODEL),
            pl.BlockSpec((tm, 2 * D_MODEL), lambda i: (done(i), OFF_GMIX // (2 * D_MODEL))),
            blk(0),
            full(conv_w), full(conv_b), full(wa), full(wb), full(wo), full(fnw),
        ],
        out_specs=pl.BlockSpec((tm, D_MODEL), lambda i: (done(i), 0)),
        out_shape=jax.ShapeDtypeStruct((n_rows, D_MODEL), f32),
        scratch_shapes=[pltpu.VMEM((tm, D_MODEL), bf16)],
        compiler_params=pltpu.CompilerParams(
            dimension_semantics=("arbitrary",), vmem_limit_bytes=VMEM_LIMIT),
        name="merge_out",
    )(x2, proj, proj, proj, proj, proj, proj, proj, proj, ob, conv_w, conv_b, wa, wb, wo, fnw)


def _blockdiag_w1(w1_k, w1_v, lo):
    half = CMP_STRIDE * HEAD_DIM
    wk = w1_k[lo * half:(lo + 1) * half].reshape(CMP_STRIDE, HEAD_DIM, CMP_HID)
    wv = w1_v[lo * half:(lo + 1) * half].reshape(CMP_STRIDE, HEAD_DIM, CMP_HID)
    z = jnp.zeros_like(wk)
    top = jnp.concatenate([wk, z], axis=-1)
    bot = jnp.concatenate([z, wv], axis=-1)
    return jnp.concatenate([top, bot], axis=1).reshape(CMP_STRIDE * 2 * HEAD_DIM, 2 * CMP_HID).astype(bf16)


def _pe_row(pe_k, pe_v, lo):
    sl = slice(lo * CMP_STRIDE, (lo + 1) * CMP_STRIDE)
    return jnp.concatenate([pe_k[sl], pe_v[sl]], axis=-1).reshape(1, CMP_STRIDE * 2 * HEAD_DIM)


def kernel(x, norm_w, w_in, conv_w, conv_b, cmp_pe_k, cmp_pe_v, cmp_w1_k, cmp_w2_k, cmp_w1_v, cmp_w2_v,
           w_proj_a, w_proj_b, w_out, final_norm_w):
    batch, seq, _ = x.shape
    assert norm_w.shape[0] == 1 and seq % SEL_CHUNK == 0 and seq // CMP_STRIDE == N_CMP_PAD
    w_p = _reordered_w_in(w_in[0])
    assert w_p.shape[1] == N_PAD
    x2 = x.reshape(batch * seq, D_MODEL)
    proj = _inproj(x2, norm_w, w_p)

    wlo = _blockdiag_w1(cmp_w1_k[0], cmp_w1_v[0], 0)
    whi = _blockdiag_w1(cmp_w1_k[0], cmp_w1_v[0], 1)
    pelo = _pe_row(cmp_pe_k[0], cmp_pe_v[0], 0)
    pehi = _pe_row(cmp_pe_k[0], cmp_pe_v[0], 1)
    zk = jnp.zeros_like(cmp_w2_k[0])
    w2bd = jnp.concatenate([jnp.concatenate([cmp_w2_k[0], zk], axis=1),
                            jnp.concatenate([zk, cmp_w2_v[0]], axis=1)], axis=0).astype(bf16)
    ob = _nsa(proj, wlo, whi, pelo, pehi, w2bd, batch, seq)

    out = _out(x2, proj, ob, conv_w[0], conv_b, w_proj_a[0].astype(bf16), w_proj_b[0].astype(bf16),
               w_out[0].astype(bf16), final_norm_w.reshape(1, D_MODEL), seq)
    return out.reshape(batch, seq, D_MODEL)
```

```python
import functools

import numpy as np
import jax
import jax.numpy as jnp
from jax import lax
from jax.experimental import pallas as pl
from jax.experimental.pallas import tpu as pltpu

D_MODEL = 1024
D_CONV = 1024
CONV_W = 3
N_HEADS = 16
HEAD_DIM = 64
N_KV = 4
GROUP = N_HEADS // N_KV
D_ATT = N_HEADS * HEAD_DIM
D_KV = N_KV * HEAD_DIM
CMP_LEN = 32
CMP_STRIDE = 16
CMP_HID = 128
SLC_LEN = 64
N_SEL = 8
WINDOW = 512
NORM_EPS = 1e-6

LANES = 128
N_CMP_PAD = 128
MASK_BIG = float(2.0 ** 100)
LOG2E = float(np.log2(np.e))
SEL_CHUNK = 512
WIN_TQ = 128
WIN_KEYS = WINDOW + WIN_TQ
WIN_PAD = WINDOW
CW_TQ = 512

OFF_Q = 0
OFF_ZB = 1024
OFF_GMIX = 2048
OFF_KVC = 4096
OFF_KSW = 4608
OFF_VSW = 5120
OFF_GATE = 5632
N_PROJ = 6144

L_HI = 64
L_LO = 67
L_FLAG = 70
L_SEL = 96

VMEM_LIMIT = 56 * 1024 * 1024

f32 = jnp.float32
bf16 = jnp.bfloat16


def _reordered_w_in(w):
    o_q = 4 * D_CONV
    o_kv = o_q + D_ATT
    o_gate = o_kv + 6 * D_KV
    o_zb = o_gate + 3 * N_HEADS
    o_gmix = o_zb + D_ATT
    kv = w[:, o_kv:o_gate].reshape(D_MODEL, 6, N_KV, HEAD_DIM)
    pair = lambda a, b: jnp.stack([kv[:, a], kv[:, b]], axis=2).reshape(D_MODEL, N_KV * LANES)
    gates = jnp.pad(w[:, o_gate:o_zb].reshape(D_MODEL, N_KV, 3 * GROUP),
                    ((0, 0), (0, 0), (0, LANES - 3 * GROUP))).reshape(D_MODEL, N_KV * LANES)
    cols = [w[:, :o_q], w[:, o_q:o_kv] * (HEAD_DIM ** -0.5 * LOG2E), w[:, o_zb:o_gmix], w[:, o_gmix:],
            pair(0, 1), pair(2, 4), pair(3, 5), gates]
    return jnp.concatenate(cols, axis=1).astype(bf16)


def _sigmoid(x):
    return 1.0 / (1.0 + jnp.exp(-x))


def _dot_nt(a, b):
    return lax.dot_general(a, b, (((1,), (1,)), ((), ())), preferred_element_type=f32)


CONV_STEPS = 2
CARRY_ROWS = 8


def _inproj_kernel(x_ref, nw_ref, w1_ref, w2_ref, cw_ref, cb_ref, o_ref, ya_ref, h_ref, u_ref, carry_ref,
                   *, tiles_per_seq):
    i = pl.program_id(0)
    j = pl.program_id(1)
    tm = x_ref.shape[0]
    tn = o_ref.shape[1]
    product = lambda w_ref: jnp.dot(h_ref[...], w_ref[...], preferred_element_type=f32)

    @pl.when((i == 0) & (j == 0))
    def _():
        carry_ref[...] = jnp.zeros_like(carry_ref)

    @pl.when(j == 0)
    def _():
        x = x_ref[...]
        r = lax.rsqrt(jnp.mean(x * x, axis=-1, keepdims=True) + NORM_EPS)
        h_ref[...] = ((x * r) * nw_ref[...]).astype(bf16)
        u_ref[...] = product(w2_ref) * product(w1_ref)

    @pl.when(j == 1)
    def _():
        u = u_ref[...]
        prev = carry_ref[...] * (i % tiles_per_seq != 0).astype(f32)
        p1 = prev[CARRY_ROWS - 1:CARRY_ROWS, :]
        p2 = prev[CARRY_ROWS - 2:CARRY_ROWS - 1, :]
        row = lax.broadcasted_iota(jnp.int32, (tm, 1), 0)
        u1 = jnp.where(row == 0, p1, pltpu.roll(u, 1, 0))
        u2 = jnp.where(row == 0, p2, jnp.where(row == 1, p1, pltpu.roll(u, 2, 0)))
        cw = cw_ref[...]
        y = cb_ref[...] + cw[0:1, :] * u2 + cw[1:2, :] * u1 + cw[2:3, :] * u
        carry_ref[...] = u[tm - CARRY_ROWS:tm, :]
        z = product(w2_ref)
        ya_ref[...] = (product(w1_ref) * y * (z * _sigmoid(z))).astype(bf16)

    t = j - CONV_STEPS
    is_silu = t == OFF_ZB // tn
    is_sigmoid = (t == OFF_GMIX // tn) | (t == OFF_GMIX // tn + 1)
    is_tail = t == OFF_GATE // tn
    split = OFF_GATE % tn

    @pl.when(is_silu)
    def _():
        acc = product(w1_ref)
        o_ref[...] = (acc * _sigmoid(acc)).astype(bf16)

    @pl.when(is_sigmoid)
    def _():
        o_ref[...] = _sigmoid(product(w1_ref)).astype(bf16)

    @pl.when(is_tail)
    def _():
        acc = product(w1_ref)
        o_ref[:, :split] = acc[:, :split].astype(bf16)
        o_ref[:, split:] = _sigmoid(acc[:, split:]).astype(bf16)

    @pl.when((t >= 0) & jnp.logical_not(is_silu | is_sigmoid | is_tail))
    def _():
        o_ref[...] = product(w1_ref).astype(bf16)


def _inproj(x2, norm_w, w_p, conv_w, conv_b, seq, tm=1024, tn=1024):
    n_rows = x2.shape[0]
    assert tn == D_CONV == D_MODEL and OFF_ZB % tn == 0 and OFF_GMIX % tn == 0 and N_PROJ - OFF_GATE < tn
    assert seq % tm == 0 and w_p.shape[1] == 4 * D_CONV + N_PROJ
    first = lambda i, j: (0, jnp.where(j < CONV_STEPS, j, j + 2))
    second = lambda i, j: (0, jnp.minimum(j, 1) + 2)
    return pl.pallas_call(
        functools.partial(_inproj_kernel, tiles_per_seq=seq // tm),
        grid=(n_rows // tm, CONV_STEPS + N_PROJ // tn),
        in_specs=[
            pl.BlockSpec((tm, D_MODEL), lambda i, j: (i, 0)),
            pl.BlockSpec((1, D_MODEL), lambda i, j: (0, 0)),
            pl.BlockSpec((D_MODEL, tn), first),
            pl.BlockSpec((D_MODEL, tn), second),
            pl.BlockSpec(conv_w.shape, lambda i, j: (0, 0)),
            pl.BlockSpec(conv_b.shape, lambda i, j: (0, 0)),
        ],
        out_specs=[
            pl.BlockSpec((tm, tn), lambda i, j: (i, jnp.maximum(j - CONV_STEPS, 0))),
            pl.BlockSpec((tm, D_CONV), lambda i, j: (i, 0)),
        ],
        out_shape=[jax.ShapeDtypeStruct((n_rows, N_PROJ), bf16), jax.ShapeDtypeStruct((n_rows, D_CONV), bf16)],
        scratch_shapes=[
            pltpu.VMEM((tm, D_MODEL), bf16),
            pltpu.VMEM((tm, D_CONV), f32),
            pltpu.VMEM((CARRY_ROWS, D_CONV), f32),
        ],
        compiler_params=pltpu.CompilerParams(
            dimension_semantics=("arbitrary", "arbitrary"), vmem_limit_bytes=VMEM_LIMIT),
        name="inproj",
    )(x2, norm_w, w_p, w_p, conv_w, conv_b)


def _softmax_step(s, v, carry):
    m, acc = carry
    m_new = jnp.maximum(m, jnp.max(s, axis=-1, keepdims=True))
    alpha = jnp.exp2(m - m_new)
    p = jnp.exp2(s - m_new)
    acc = alpha * acc + jnp.dot(p.astype(bf16), v, preferred_element_type=f32)
    return m_new, acc


def _stack_heads(q_tile, aug_rows, extra):
    lane = lax.broadcasted_iota(jnp.int32, (1, LANES), 1)
    pairs = (q_tile[:, :LANES], q_tile[:, LANES:])
    parts = []
    for r in range(GROUP):
        qh = pairs[r // 2]
        if r % 2 == 1:
            qh = pltpu.roll(qh, HEAD_DIM, 1)
        aug = aug_rows[r:r + 1, :]
        if extra is not None:
            aug = aug + extra
        parts.append(jnp.where(lane < HEAD_DIM, qh, aug).astype(bf16))
    return jnp.concatenate(parts, axis=0)


def _merge_heads(vals, lane):
    return (jnp.where(lane < HEAD_DIM, vals[0], vals[1]), jnp.where(lane < HEAD_DIM, vals[2], vals[3]))


def _nsa_kernel(q_ref, ksw_ref, vsw_ref, gate_ref, kvc_ref, wlo_ref, whi_ref, pelo_ref, pehi_ref, w2_ref,
                kaug_s_ref, kaug_w_ref, kaug_c_ref, qaug_ref, tri_ref, band_ref, ovl_ref, cend_ref,
                o_ref,
                kvcf_ref, ks_ref, kw_ref, vs_ref, vw_ref, ocw_ref, imp_ref, ns_ref, selq_ref):
    seq = q_ref.shape[0]
    lane = lax.broadcasted_iota(jnp.int32, (1, LANES), 1)
    low = lane < HEAD_DIM

    kvcf_ref[...] = kvc_ref[...].astype(f32)
    xcat = jnp.concatenate([kvcf_ref[pl.ds(l, N_CMP_PAD, stride=CMP_STRIDE), :] for l in range(CMP_STRIDE)],
                           axis=1)
    xa = (xcat + pelo_ref[...]).astype(bf16)
    xb = (xcat + pehi_ref[...]).astype(bf16)
    a = jnp.dot(xa, wlo_ref[...], preferred_element_type=f32)
    bm = jnp.dot(xb, whi_ref[...], preferred_element_type=f32)
    h = a + pltpu.roll(bm, N_CMP_PAD - 1, 0)
    hid = h * _sigmoid(h)
    kcvc = jnp.dot(hid.astype(bf16), w2_ref[...], preferred_element_type=f32)

    ksw = ksw_ref[...]
    vsw = vsw_ref[...]
    one = jnp.ones((1, LANES), bf16)
    ks_ref[...] = jnp.where(low, ksw, kaug_s_ref[...])
    kwin = pltpu.roll(ksw.astype(f32), HEAD_DIM, 1).astype(bf16)
    kw_ref[0:WIN_PAD, :] = kaug_w_ref[0:WIN_PAD, :]
    kw_ref[WIN_PAD:, :] = jnp.where(low, kwin, kaug_w_ref[WIN_PAD:, :])
    vs_ref[...] = jnp.where(low, vsw, one)
    vw_ref[0:WIN_PAD, :] = jnp.zeros((WIN_PAD, LANES), bf16)
    vw_ref[WIN_PAD:, :] = jnp.where(low, one, vsw)

    qaug = qaug_ref[0]
    kc_aug = jnp.where(low, kcvc, kaug_c_ref[...]).astype(bf16)
    vc_rows = jnp.concatenate([jnp.zeros((N_CMP_PAD, LANES), bf16), kcvc.astype(bf16)], axis=1)
    zeros_w = jnp.zeros((WIN_KEYS, LANES), bf16)
    ovl = ovl_ref[...]

    def cw_body(i, _):
        t0 = pl.multiple_of(i * CW_TQ, CW_TQ)
        sig = gate_ref[pl.ds(t0, CW_TQ), :].astype(f32)
        o_w, o_c = [], []
        n_tiles = CW_TQ // WIN_TQ

        def tile_scores(h):
            tw = pl.multiple_of(t0 + h * WIN_TQ, WIN_TQ)
            qw = _stack_heads(q_ref[pl.ds(tw, WIN_TQ), :].astype(f32), qaug[GROUP:2 * GROUP], None)
            k_all = jnp.concatenate([kw_ref[pl.ds(tw, WIN_KEYS), :], kc_aug], axis=0)
            return _dot_nt(qw, k_all)

        s_next = tile_scores(0)
        for h in range(n_tiles):
            tw = pl.multiple_of(t0 + h * WIN_TQ, WIN_TQ)
            s = s_next
            if h + 1 < n_tiles:
                s_next = tile_scores(h + 1)

            sw = jnp.concatenate([s[:, :WIN_TQ] + band_ref[:, :WIN_TQ], s[:, WIN_TQ:WINDOW],
                                  s[:, WINDOW:WIN_KEYS] + band_ref[:, WINDOW:]], axis=1)
            p_w = jnp.exp2(sw - jnp.max(sw, axis=-1, keepdims=True))

            ok = cend_ref[...] <= tw
            sm = jnp.where(ok, s[:, WIN_KEYS:], -jnp.inf)
            m = jnp.max(sm, axis=-1, keepdims=True)
            m = jnp.where(m > -jnp.inf, m, 0.0)
            e = jnp.where(ok, jnp.exp2(sm - m), 0.0)
            p_c = e / jnp.maximum(jnp.sum(e, axis=-1, keepdims=True), 1e-30)

            p_all = jnp.concatenate([p_w.astype(bf16), p_c.astype(bf16)], axis=1)
            v_all = jnp.concatenate(
                [jnp.concatenate([vw_ref[pl.ds(tw, WIN_KEYS), :], zeros_w], axis=1), vc_rows], axis=0)
            acc = jnp.dot(p_all, v_all, preferred_element_type=f32)
            acc_w = acc[:, :LANES]
            o_w.append(acc_w / jnp.maximum(acc_w[:, 0:1], 1e-30))
            o_c.append(acc[:, LANES:])

            ps = p_c[0:WIN_TQ] + p_c[WIN_TQ:2 * WIN_TQ] + p_c[2 * WIN_TQ:3 * WIN_TQ] + p_c[3 * WIN_TQ:4 * WIN_TQ]
            p1 = ps.astype(bf16)
            r1 = ps - p1.astype(f32)
            p2 = r1.astype(bf16)
            p3 = (r1 - p2.astype(f32)).astype(bf16)
            imp_ref[i * (CW_TQ // WIN_TQ) + h] = _dot_nt(ovl, p1) + _dot_nt(ovl, p2) + _dot_nt(ovl, p3)

        vals = []
        for r in range(GROUP):
            head = lambda parts: jnp.concatenate([o[r * WIN_TQ:(r + 1) * WIN_TQ, :] for o in parts], axis=0)
            a = head(o_c) * sig[:, 3 * r:3 * r + 1] + head(o_w) * sig[:, 3 * r + 2:3 * r + 3]
            vals.append(pltpu.roll(a, HEAD_DIM, 1) if r % 2 == 0 else a)
        c0, c1 = _merge_heads(vals, lane)
        ocw_ref[pl.ds(t0, CW_TQ), 0:LANES] = c0
        ocw_ref[pl.ds(t0, CW_TQ), LANES:2 * LANES] = c1
        return 0

    lax.fori_loop(0, seq // CW_TQ, cw_body, 0)

    n_slc = seq // SLC_LEN
    n_tt = seq // LANES
    shp = (n_tt, LANES)
    cur = (lax.broadcasted_iota(jnp.int32, shp, 0) * LANES + lax.broadcasted_iota(jnp.int32, shp, 1)) // SLC_LEN
    vals = []
    for j in range(n_slc):
        forced = (cur == j) | (cur == j + 1)
        vj = jnp.where(cur >= j, imp_ref[:, j, :], -jnp.inf)
        vals.append(jnp.full(shp, jnp.inf, f32) if j == 0 else jnp.where(forced, jnp.inf, vj))
    wins = [jnp.zeros(shp, f32) for _ in range(n_slc)]
    losses = [jnp.zeros(shp, f32) for _ in range(n_slc)]
    for a in range(n_slc):
        for b in range(a + 1, n_slc):
            a_first = jnp.where(vals[a] >= vals[b], 1.0, 0.0)
            wins[a] = wins[a] + a_first
            losses[b] = losses[b] + a_first
    for j in range(n_slc):
        rank = losses[j] + (float(n_slc - 1 - j) - wins[j])
        sel = (rank < float(N_SEL)) & (vals[j] > -jnp.inf)
        ns_ref[j] = jnp.where(sel, 0.0, 1.0)
    eye = (lax.broadcasted_iota(jnp.int32, (LANES, LANES), 0)
           == lax.broadcasted_iota(jnp.int32, (LANES, LANES), 1)).astype(bf16)
    zpad = jnp.zeros((L_SEL, LANES), f32)
    for tt in range(n_tt):
        z = jnp.concatenate([zpad, ns_ref[:, tt, :]], axis=0).astype(bf16)
        selq_ref[tt * LANES:(tt + 1) * LANES, :] = _dot_nt(eye, z) * (-MASK_BIG)

    half = SEL_CHUNK // 2
    half_rows = GROUP * half

    def stacked_q(i):
        parts = []
        for h in range(2):
            tok = slice(i * SEL_CHUNK + h * half, i * SEL_CHUNK + (h + 1) * half)
            parts.append(_stack_heads(q_ref[tok, :].astype(f32), qaug[0:GROUP], selq_ref[tok, :]))
        return parts

    def causal(s):
        tri = tri_ref[...]
        return jnp.concatenate([s[r * half:(r + 1) * half, :] + tri for r in range(GROUP)], axis=0)

    def scores(qs, i, c):
        k0 = c * SEL_CHUNK
        if c < i:
            return (_dot_nt(jnp.concatenate(qs, axis=0), ks_ref[k0:k0 + SEL_CHUNK, :]),)
        s_a = causal(_dot_nt(qs[0], ks_ref[k0:k0 + half, :]))
        s_b = _dot_nt(qs[1], ks_ref[k0:k0 + SEL_CHUNK, :])
        return (s_a, jnp.concatenate([s_b[:, :half], causal(s_b[:, half:])], axis=1))

    def softmax_step(s, v, carry):
        if carry is None:
            m = jnp.max(s, axis=-1, keepdims=True)
            return m, jnp.dot(jnp.exp2(s - m).astype(bf16), v, preferred_element_type=f32)
        return _softmax_step(s, v, carry)

    def finish(i, acc):
        tok = slice(i * SEL_CHUNK, (i + 1) * SEL_CHUNK)
        o_s = acc / jnp.maximum(acc[:, HEAD_DIM:HEAD_DIM + 1], 1e-30)
        sig = gate_ref[tok, :].astype(f32)
        vals = []
        for r in range(GROUP):
            a_s = jnp.concatenate([o_s[h * half_rows + r * half:h * half_rows + (r + 1) * half, :]
                                   for h in range(2)], axis=0) * sig[:, 3 * r + 1:3 * r + 2]
            vals.append(a_s if r % 2 == 0 else pltpu.roll(a_s, HEAD_DIM, 1))
        c0, c1 = _merge_heads(vals, lane)
        o_ref[tok, 0:LANES] = (ocw_ref[tok, 0:LANES] + c0).astype(bf16)
        o_ref[tok, LANES:2 * LANES] = (ocw_ref[tok, LANES:2 * LANES] + c1).astype(bf16)

    steps = [(i, c) for i in range(seq // SEL_CHUNK) for c in range(i + 1)]
    qs = stacked_q(0)
    s_next = scores(qs, 0, 0)
    carry = None
    for k, (i, c) in enumerate(steps):
        s = s_next
        if k + 1 < len(steps):
            i2, c2 = steps[k + 1]
            if i2 != i:
                qs = stacked_q(i2)
            s_next = scores(qs, i2, c2)
        k0 = c * SEL_CHUNK
        if c < i:
            carry = softmax_step(s[0], vs_ref[k0:k0 + SEL_CHUNK, :], carry)
        else:
            rows_a = None if carry is None else tuple(a[:half_rows] for a in carry)
            rows_b = None if carry is None else tuple(a[half_rows:] for a in carry)
            _, acc_a = softmax_step(s[0], vs_ref[k0:k0 + half, :], rows_a)
            _, acc_b = softmax_step(s[1], vs_ref[k0:k0 + SEL_CHUNK, :], rows_b)
            finish(i, jnp.concatenate([acc_a, acc_b], axis=0))
            carry = None


def _position_lanes(pos):
    out = np.zeros((pos.shape[0], LANES), np.float32)
    out[:, L_HI:L_HI + 3] = ((pos // 64) * 64)[:, None]
    out[:, L_LO:L_LO + 3] = (pos % 64)[:, None]
    return out


def _nsa_tables(seq):
    pos = np.arange(seq)
    kaug_s = _position_lanes(pos)
    kaug_s[pos, L_SEL + pos // SLC_LEN] = 1.0
    kaug_w = np.zeros((seq + WIN_PAD, LANES), np.float32)
    kaug_w[WIN_PAD:] = _position_lanes(pos)
    kaug_w[:WIN_PAD, L_FLAG] = 1.0
    kaug_c = _position_lanes(np.arange(N_CMP_PAD) * CMP_STRIDE + CMP_LEN - 1)
    tl = np.arange(SEL_CHUNK // 2)[:, None]
    kk = np.arange(SEL_CHUNK // 2)[None, :]
    tri = np.where(kk <= tl, 0.0, -MASK_BIG).astype(np.float32)
    tl = np.arange(WIN_TQ)[:, None]
    kk = np.arange(WIN_KEYS)[None, :]
    dist = tl + WIN_PAD - kk
    band = np.tile(np.where((dist >= 0) & (dist < WINDOW), 0.0, -MASK_BIG).astype(np.float32), (GROUP, 1))
    c = np.arange(N_CMP_PAD)[None, :]
    j = np.arange(seq // SLC_LEN)[:, None]
    c_start, c_end, s_start = c * CMP_STRIDE, c * CMP_STRIDE + CMP_LEN - 1, j * SLC_LEN
    ovl = ((c_start <= s_start + SLC_LEN - 1) & (c_end >= s_start) & (c < seq // CMP_STRIDE - 1)).astype(np.float32)
    cend = np.tile((c * CMP_STRIDE + CMP_LEN - 1) - tl, (GROUP, 1)).astype(np.int32)
    return (jnp.asarray(kaug_s, bf16), jnp.asarray(kaug_w, bf16), jnp.asarray(kaug_c, f32),
            jnp.asarray(tri), jnp.asarray(band), jnp.asarray(ovl, bf16), jnp.asarray(cend))


def _query_aug():
    slopes = jnp.exp2(-8.0 * jnp.arange(1, N_HEADS + 1, dtype=f32) / N_HEADS).reshape(N_KV, GROUP) * LOG2E
    hi =slopes.astype(bf16).astype(f32)
    mid = (slopes - hi).astype(bf16).astype(f32)
    lo = (slopes - hi - mid).astype(bf16).astype(f32)
    pieces = jnp.stack([hi, mid, lo, hi, mid, lo], axis=-1)
    base = jnp.zeros((N_KV, GROUP, LANES), f32).at[:, :, L_HI:L_HI + 6].set(pieces)
    return jnp.concatenate([base, base.at[:, :, L_FLAG].set(-MASK_BIG)], axis=1)


def _nsa(proj, wlo, whi, pelo, pehi, w2bd, batch, seq):
    kaug_s, kaug_w, kaug_c, tri, band, ovl, cend = _nsa_tables(seq)
    qaug = _query_aug()
    const2 = lambda b, g: (0, 0)
    in_specs = [
        pl.BlockSpec((seq, 2 * LANES), lambda b, g: (b, OFF_Q // (2 * LANES) + g)),
        pl.BlockSpec((seq, LANES), lambda b, g: (b, OFF_KSW // LANES + g)),
        pl.BlockSpec((seq, LANES), lambda b, g: (b, OFF_VSW // LANES + g)),
        pl.BlockSpec((seq, LANES), lambda b, g: (b, OFF_GATE // LANES + g)),
        pl.BlockSpec((seq, LANES), lambda b, g: (b, OFF_KVC // LANES + g)),
        pl.BlockSpec(wlo.shape, const2),
        pl.BlockSpec(whi.shape, const2),
        pl.BlockSpec(pelo.shape, const2),
        pl.BlockSpec(pehi.shape, const2),
        pl.BlockSpec(w2bd.shape, const2),
        pl.BlockSpec(kaug_s.shape, const2),
        pl.BlockSpec(kaug_w.shape, const2),
        pl.BlockSpec(kaug_c.shape, const2),
        pl.BlockSpec((1, 2 * GROUP, LANES), lambda b, g: (g, 0, 0)),
        pl.BlockSpec(tri.shape, const2),
        pl.BlockSpec(band.shape, const2),
        pl.BlockSpec(ovl.shape, const2),
        pl.BlockSpec(cend.shape, const2),
    ]
    return pl.pallas_call(
        _nsa_kernel,
        grid=(batch, N_KV),
        in_specs=in_specs,
        out_specs=pl.BlockSpec((seq, 2 * LANES), lambda b, g: (b, g)),
        out_shape=jax.ShapeDtypeStruct((batch * seq, D_ATT), bf16),
        scratch_shapes=[
            pltpu.VMEM((seq, LANES), f32),
            pltpu.VMEM((seq, LANES), bf16),
            pltpu.VMEM((seq + WIN_PAD, LANES), bf16),
            pltpu.VMEM((seq, LANES), bf16),
            pltpu.VMEM((seq + WIN_PAD, LANES), bf16),
            pltpu.VMEM((seq, 2 * LANES), f32),
            pltpu.VMEM((seq // LANES, seq // SLC_LEN, LANES), f32),
            pltpu.VMEM((seq // SLC_LEN, seq // LANES, LANES), f32),
            pltpu.VMEM((seq, LANES), f32),
        ],
        compiler_params=pltpu.CompilerParams(
            dimension_semantics=("parallel", "parallel"), vmem_limit_bytes=VMEM_LIMIT),
        name="nsa",
    )(proj, proj, proj, proj, proj, wlo, whi, pelo, pehi, w2bd, kaug_s, kaug_w, kaug_c, qaug, tri, band, ovl, cend)


def _out_kernel(x_ref, ya_ref, zb_ref, gm_ref, ob_ref, wa_ref, wb_ref, wo_ref, fnw_ref, o_ref):
    y_a = jnp.dot(ya_ref[...], wa_ref[...], preferred_element_type=f32)
    yb_in = ob_ref[...].astype(f32) * zb_ref[...].astype(f32)
    y_b = jnp.dot(yb_in.astype(bf16), wb_ref[...], preferred_element_type=f32)
    mixed = (gm_ref[:, 0:D_MODEL].astype(f32) * y_a + gm_ref[:, D_MODEL:2 * D_MODEL].astype(f32) * y_b)
    xo = x_ref[...] + jnp.dot(mixed.astype(bf16), wo_ref[...], preferred_element_type=f32)
    r = lax.rsqrt(jnp.mean(xo * xo, axis=-1, keepdims=True) + NORM_EPS)
    o_ref[...] = (xo * r) * fnw_ref[...]


def _out(x2, ya, proj, ob, wa, wb, wo, fnw, tm=512):
    n_rows = x2.shape[0]
    blk = lambda c: pl.BlockSpec((tm, D_MODEL), lambda i, c=c: (i, c))
    full = lambda a: pl.BlockSpec(a.shape, lambda i: (0, 0))
    return pl.pallas_call(
        _out_kernel,
        grid=(n_rows // tm,),
        in_specs=[
            blk(0),
            blk(0),
            blk(OFF_ZB // D_MODEL),
            pl.BlockSpec((tm, 2 * D_MODEL), lambda i: (i, OFF_GMIX // (2 * D_MODEL))),
            blk(0),
            full(wa), full(wb), full(wo), full(fnw),
        ],
        out_specs=pl.BlockSpec((tm, D_MODEL), lambda i: (i, 0)),
        out_shape=jax.ShapeDtypeStruct((n_rows, D_MODEL), f32),
        compiler_params=pltpu.CompilerParams(
            dimension_semantics=("parallel",), vmem_limit_bytes=VMEM_LIMIT),
        name="merge_out",
    )(x2, ya, proj, proj, ob, wa, wb, wo, fnw)


def _blockdiag_w1(w1_k, w1_v, lo):
    half = CMP_STRIDE * HEAD_DIM
    wk = w1_k[lo * half:(lo + 1) * half].reshape(CMP_STRIDE, HEAD_DIM, CMP_HID)
    wv = w1_v[lo * half:(lo + 1) * half].reshape(CMP_STRIDE, HEAD_DIM, CMP_HID)
    z = jnp.zeros_like(wk)
    top = jnp.concatenate([wk, z], axis=-1)
    bot = jnp.concatenate([z, wv], axis=-1)
    return jnp.concatenate([top, bot], axis=1).reshape(CMP_STRIDE * 2 * HEAD_DIM, 2 * CMP_HID).astype(bf16)


def _pe_row(pe_k, pe_v, lo):
    sl = slice(lo * CMP_STRIDE, (lo + 1) * CMP_STRIDE)
    return jnp.concatenate([pe_k[sl], pe_v[sl]], axis=-1).reshape(1, CMP_STRIDE * 2 * HEAD_DIM)


def kernel(x, norm_w, w_in, conv_w, conv_b, cmp_pe_k, cmp_pe_v, cmp_w1_k, cmp_w2_k, cmp_w1_v, cmp_w2_v,
           w_proj_a, w_proj_b, w_out, final_norm_w):
    batch, seq, _ = x.shape
    assert norm_w.shape[0] == 1 and seq % SEL_CHUNK == 0 and seq // CMP_STRIDE == N_CMP_PAD
    w_p = _reordered_w_in(w_in[0])
    x2 = x.reshape(batch * seq, D_MODEL)
    proj, ya = _inproj(x2, norm_w, w_p, conv_w[0], conv_b, seq)

    wlo = _blockdiag_w1(cmp_w1_k[0], cmp_w1_v[0], 0)
    whi = _blockdiag_w1(cmp_w1_k[0], cmp_w1_v[0], 1)
    pelo = _pe_row(cmp_pe_k[0], cmp_pe_v[0], 0)
    pehi = _pe_row(cmp_pe_k[0], cmp_pe_v[0], 1)
    zk = jnp.zeros_like(cmp_w2_k[0])
    w2bd = jnp.concatenate([jnp.concatenate([cmp_w2_k[0], zk], axis=1),
                            jnp.concatenate([zk, cmp_w2_v[0]], axis=1)], axis=0).astype(bf16)
    ob = _nsa(proj, wlo, whi, pelo, pehi, w2bd, batch, seq)

    out = _out(x2, ya, proj, ob, w_proj_a[0].astype(bf16), w_proj_b[0].astype(bf16),
               w_out[0].astype(bf16), final_norm_w.reshape(1, D_MODEL))
    return out.reshape(batch, seq, D_MODEL)
```

```python
import functools

import numpy as np
import jax
import jax.numpy as jnp
from jax import lax
from jax.experimental import pallas as pl
from jax.experimental.pallas import tpu as pltpu

D_MODEL = 1024
D_CONV = 1024
CONV_W = 3
N_HEADS = 16
HEAD_DIM = 64
N_KV = 4
GROUP = N_HEADS // N_KV
D_ATT = N_HEADS * HEAD_DIM
D_KV = N_KV * HEAD_DIM
CMP_LEN = 32
CMP_STRIDE = 16
CMP_HID = 128
SLC_LEN = 64
N_SEL = 8
WINDOW = 512
NORM_EPS = 1e-6

LANES = 128
N_CMP_PAD = 128
MASK_BIG = float(2.0 ** 100)
LOG2E = float(np.log2(np.e))
SEL_CHUNK = 512
WIN_TQ = 128
WIN_KEYS = WINDOW + WIN_TQ
WIN_PAD = WINDOW
CW_TQ = 512

OFF_Q = 0
OFF_ZB = 1024
OFF_GMIX = 2048
OFF_KVC = 4096
OFF_KSW = 4608
OFF_VSW = 5120
OFF_GATE = 5632
N_PROJ = 6144

L_HI = 64
L_LO = 67
L_FLAG = 70
L_SEL = 96

VMEM_LIMIT = 56 * 1024 * 1024

f32 = jnp.float32
bf16 = jnp.bfloat16


def _reordered_w_in(w):
    o_q = 4 * D_CONV
    o_kv = o_q + D_ATT
    o_gate = o_kv + 6 * D_KV
    o_zb = o_gate + 3 * N_HEADS
    o_gmix = o_zb + D_ATT
    kv = w[:, o_kv:o_gate].reshape(D_MODEL, 6, N_KV, HEAD_DIM)
    pair = lambda a, b: jnp.stack([kv[:, a], kv[:, b]], axis=2).reshape(D_MODEL, N_KV * LANES)
    gates = jnp.pad(w[:, o_gate:o_zb].reshape(D_MODEL, N_KV, 3 * GROUP),
                    ((0, 0), (0, 0), (0, LANES - 3 * GROUP))).reshape(D_MODEL, N_KV * LANES)
    cols = [w[:, :o_q], w[:, o_q:o_kv] * (HEAD_DIM ** -0.5 * LOG2E), w[:, o_zb:o_gmix], w[:, o_gmix:],
            pair(0, 1), pair(2, 4), pair(3, 5), gates]
    return jnp.concatenate(cols, axis=1).astype(bf16)


def _sigmoid(x):
    return 1.0 / (1.0 + jnp.exp(-x))


def _dot_nt(a, b):
    return lax.dot_general(a, b, (((1,), (1,)), ((), ())), preferred_element_type=f32)


CONV_STEPS = 2
CARRY_ROWS = 8
ROW_BLOCK = 256


def _inproj_kernel(x_ref, nw_ref, w1_ref, w2_ref, cw_ref, cb_ref, o_ref, ya_ref, h_ref, u_ref, carry_ref,
                   *, tiles_per_seq):
    i = pl.program_id(0)
    j = pl.program_id(1)
    tm = x_ref.shape[0]
    tn = o_ref.shape[1]
    n_stored = N_PROJ // tn
    blocks = [slice(k * ROW_BLOCK, (k + 1) * ROW_BLOCK) for k in range(tm // ROW_BLOCK)]
    product = lambda rows, w_ref: jnp.dot(h_ref[rows, :], w_ref[...], preferred_element_type=f32)

    @pl.when((i == 0) & (j == 0))
    def _():
        carry_ref[...] = jnp.zeros_like(carry_ref)

    @pl.when(j == 0)
    def _():
        x = x_ref[...]
        r = lax.rsqrt(jnp.mean(x * x, axis=-1, keepdims=True) + NORM_EPS)
        h_ref[...] = ((x * r) * nw_ref[...]).astype(bf16)

    @pl.when(j == n_stored)
    def _():
        for rows in blocks:
            u_ref[rows, :] = product(rows, w2_ref) * product(rows, w1_ref)

    @pl.when(j == n_stored + 1)
    def _():
        lead = carry_ref[...] * (i % tiles_per_seq != 0).astype(f32)
        cw = cw_ref[...]
        for rows in blocks:
            win = jnp.concatenate([lead, u_ref[rows, :]], axis=0)
            u1 = pltpu.roll(win, 1, 0)[CARRY_ROWS:, :]
            u2 = pltpu.roll(win, 2, 0)[CARRY_ROWS:, :]
            y = cb_ref[...] + cw[0:1, :] * u2 + cw[1:2, :] * u1 + cw[2:3, :] * win[CARRY_ROWS:, :]
            z = product(rows, w2_ref)
            ya_ref[rows, :] = (product(rows, w1_ref) * y * (z * _sigmoid(z))).astype(bf16)
            lead = u_ref[rows.stop - CARRY_ROWS:rows.stop, :]
        carry_ref[...] = lead

    t = j
    is_silu = t == OFF_ZB // tn
    is_sigmoid = (t == OFF_GMIX // tn) | (t == OFF_GMIX // tn + 1)
    is_tail = t == OFF_GATE // tn
    split = OFF_GATE % tn

    @pl.when(is_silu)
    def _():
        for rows in blocks:
            acc = product(rows, w1_ref)
            o_ref[rows, :] = (acc * _sigmoid(acc)).astype(bf16)

    @pl.when(is_sigmoid)
    def _():
        for rows in blocks:
            o_ref[rows, :] = _sigmoid(product(rows, w1_ref)).astype(bf16)

    @pl.when(is_tail)
    def _():
        for rows in blocks:
            acc = product(rows, w1_ref)
            o_ref[rows, :split] = acc[:, :split].astype(bf16)
            o_ref[rows, split:] = _sigmoid(acc[:, split:]).astype(bf16)

    @pl.when((t < n_stored) & jnp.logical_not(is_silu | is_sigmoid | is_tail))
    def _():
        for rows in blocks:
            o_ref[rows, :] = product(rows, w1_ref).astype(bf16)


def _inproj(x2, norm_w, w_p, conv_w, conv_b, seq, tm=1024, tn=1024):
    n_rows = x2.shape[0]
    assert tn == D_CONV == D_MODEL and OFF_ZB % tn == 0 and OFF_GMIX % tn == 0 and N_PROJ - OFF_GATE < tn
    assert seq % tm == 0 and w_p.shape[1] == 4 * D_CONV + N_PROJ
    n_stored = N_PROJ // tn
    first = lambda i, j: (0, jnp.where(j < n_stored, j + 4, j - n_stored))
    second = lambda i, j: (0, jnp.where(j <= n_stored, 2, 3))
    return pl.pallas_call(
        functools.partial(_inproj_kernel, tiles_per_seq=seq // tm),
        grid=(n_rows // tm, CONV_STEPS + N_PROJ // tn),
        in_specs=[
            pl.BlockSpec((tm, D_MODEL), lambda i, j: (i, 0)),
            pl.BlockSpec((1, D_MODEL), lambda i, j: (0, 0)),
            pl.BlockSpec((D_MODEL, tn), first),
            pl.BlockSpec((D_MODEL, tn), second),
            pl.BlockSpec(conv_w.shape, lambda i, j: (0, 0)),
            pl.BlockSpec(conv_b.shape, lambda i, j: (0, 0)),
        ],
        out_specs=[
            pl.BlockSpec((tm, tn), lambda i, j: (i, jnp.minimum(j, n_stored - 1))),
            pl.BlockSpec((tm, D_CONV), lambda i, j: (i, 0)),
        ],
        out_shape=[jax.ShapeDtypeStruct((n_rows, N_PROJ), bf16), jax.ShapeDtypeStruct((n_rows, D_CONV), bf16)],
        scratch_shapes=[
            pltpu.VMEM((tm, D_MODEL), bf16),
            pltpu.VMEM((tm, D_CONV), f32),
            pltpu.VMEM((CARRY_ROWS, D_CONV), f32),
        ],
        compiler_params=pltpu.CompilerParams(
            dimension_semantics=("arbitrary", "arbitrary"), vmem_limit_bytes=VMEM_LIMIT),
        name="inproj",
    )(x2, norm_w, w_p, w_p, conv_w, conv_b)


def _softmax_step(s, v, carry):
    m, acc = carry
    m_new = jnp.maximum(m, jnp.max(s, axis=-1, keepdims=True))
    alpha = jnp.exp2(m - m_new)
    p = jnp.exp2(s - m_new)
    acc = alpha * acc + jnp.dot(p.astype(bf16), v, preferred_element_type=f32)
    return m_new, acc


def _stack_heads(q_tile, aug_rows, extra):
    lane = lax.broadcasted_iota(jnp.int32, (1, LANES), 1)
    pairs = (q_tile[:, :LANES], q_tile[:, LANES:])
    parts = []
    for r in range(GROUP):
        qh = pairs[r // 2]
        if r % 2 == 1:
            qh = pltpu.roll(qh, HEAD_DIM, 1)
        aug = aug_rows[r:r + 1, :]
        if extra is not None:
            aug = aug + extra
        parts.append(jnp.where(lane < HEAD_DIM, qh, aug).astype(bf16))
    return jnp.concatenate(parts, axis=0)


def _merge_heads(vals, lane):
    return (jnp.where(lane < HEAD_DIM, vals[0], vals[1]), jnp.where(lane < HEAD_DIM, vals[2], vals[3]))


def _nsa_kernel(q_ref, ksw_ref, vsw_ref, gate_ref, kvc_ref, wlo_ref, whi_ref, pelo_ref, pehi_ref, w2_ref,
                kaug_s_ref, kaug_w_ref, kaug_c_ref, qaug_ref, tri_ref, band_ref, ovl_ref, cend_ref,
                o_ref,
                kvcf_ref, ks_ref, kw_ref, vs_ref, vw_ref, ocw_ref, imp_ref, ns_ref, selq_ref):
    seq = q_ref.shape[0]
    lane = lax.broadcasted_iota(jnp.int32, (1, LANES), 1)
    low = lane < HEAD_DIM

    kvcf_ref[...] = kvc_ref[...].astype(f32)
    xcat = jnp.concatenate([kvcf_ref[pl.ds(l, N_CMP_PAD, stride=CMP_STRIDE), :] for l in range(CMP_STRIDE)],
                           axis=1)
    xa = (xcat + pelo_ref[...]).astype(bf16)
    xb = (xcat + pehi_ref[...]).astype(bf16)
    a = jnp.dot(xa, wlo_ref[...], preferred_element_type=f32)
    bm = jnp.dot(xb, whi_ref[...], preferred_element_type=f32)
    h = a + pltpu.roll(bm, N_CMP_PAD - 1, 0)
    hid = h * _sigmoid(h)
    kcvc = jnp.dot(hid.astype(bf16), w2_ref[...], preferred_element_type=f32)

    ksw = ksw_ref[...]
    vsw = vsw_ref[...]
    one = jnp.ones((1, LANES), bf16)
    ks_ref[...] = jnp.where(low, ksw, kaug_s_ref[...])
    kwin = pltpu.roll(ksw.astype(f32), HEAD_DIM, 1).astype(bf16)
    kw_ref[0:WIN_PAD, :] = kaug_w_ref[0:WIN_PAD, :]
    kw_ref[WIN_PAD:, :] = jnp.where(low, kwin, kaug_w_ref[WIN_PAD:, :])
    vs_ref[...] = jnp.where(low, vsw, one)
    vw_ref[0:WIN_PAD, :] = jnp.zeros((WIN_PAD, LANES), bf16)
    vw_ref[WIN_PAD:, :] = jnp.where(low, one, vsw)

    qaug = qaug_ref[0]
    kc_aug = jnp.where(low, kcvc, kaug_c_ref[...]).astype(bf16)
    vc_rows = jnp.concatenate([jnp.zeros((N_CMP_PAD, LANES), bf16), kcvc.astype(bf16)], axis=1)
    zeros_w = jnp.zeros((WIN_KEYS, LANES), bf16)
    ovl = ovl_ref[...]

    def cw_body(i, _):
        t0 = pl.multiple_of(i * CW_TQ, CW_TQ)
        sig = gate_ref[pl.ds(t0, CW_TQ), :].astype(f32)
        o_w, o_c = [], []
        n_tiles = CW_TQ // WIN_TQ

        def tile_scores(h):
            tw = pl.multiple_of(t0 + h * WIN_TQ, WIN_TQ)
            qw = _stack_heads(q_ref[pl.ds(tw, WIN_TQ), :].astype(f32), qaug[GROUP:2 * GROUP], None)
            k_all = jnp.concatenate([kw_ref[pl.ds(tw, WIN_KEYS), :], kc_aug], axis=0)
            return _dot_nt(qw, k_all)

        s_next = tile_scores(0)
        for h in range(n_tiles):
            tw = pl.multiple_of(t0 + h * WIN_TQ, WIN_TQ)
            s = s_next
            if h + 1 < n_tiles:
                s_next = tile_scores(h + 1)

            sw = jnp.concatenate([s[:, :WIN_TQ] + band_ref[:, :WIN_TQ], s[:, WIN_TQ:WINDOW],
                                  s[:, WINDOW:WIN_KEYS] + band_ref[:, WINDOW:]], axis=1)
            p_w = jnp.exp2(sw - jnp.max(sw, axis=-1, keepdims=True))

            ok = cend_ref[...] <= tw
            sm = jnp.where(ok, s[:, WIN_KEYS:], -jnp.inf)
            m = jnp.max(sm, axis=-1, keepdims=True)
            m = jnp.where(m > -jnp.inf, m, 0.0)
            e = jnp.where(ok, jnp.exp2(sm - m), 0.0)
            p_c = e / jnp.maximum(jnp.sum(e, axis=-1, keepdims=True), 1e-30)

            p_all = jnp.concatenate([p_w.astype(bf16), p_c.astype(bf16)], axis=1)
            v_all = jnp.concatenate(
                [jnp.concatenate([vw_ref[pl.ds(tw, WIN_KEYS), :], zeros_w], axis=1), vc_rows], axis=0)
            acc = jnp.dot(p_all, v_all, preferred_element_type=f32)
            acc_w = acc[:, :LANES]
            o_w.append(acc_w / jnp.maximum(acc_w[:, 0:1], 1e-30))
            o_c.append(acc[:, LANES:])

            ps = p_c[0:WIN_TQ] + p_c[WIN_TQ:2 * WIN_TQ] + p_c[2 * WIN_TQ:3 * WIN_TQ] + p_c[3 * WIN_TQ:4 * WIN_TQ]
            p1 = ps.astype(bf16)
            r1 = ps - p1.astype(f32)
            p2 = r1.astype(bf16)
            p3 = (r1 - p2.astype(f32)).astype(bf16)
            imp_ref[i * (CW_TQ // WIN_TQ) + h] = _dot_nt(ovl, p1) + _dot_nt(ovl, p2) + _dot_nt(ovl, p3)

        vals = []
        for r in range(GROUP):
            head = lambda parts: jnp.concatenate([o[r * WIN_TQ:(r + 1) * WIN_TQ, :] for o in parts], axis=0)
            a = head(o_c) * sig[:, 3 * r:3 * r + 1] + head(o_w) * sig[:, 3 * r + 2:3 * r + 3]
            vals.append(pltpu.roll(a, HEAD_DIM, 1) if r % 2 == 0 else a)
        c0, c1 = _merge_heads(vals, lane)
        ocw_ref[pl.ds(t0, CW_TQ), 0:LANES] = c0
        ocw_ref[pl.ds(t0, CW_TQ), LANES:2 * LANES] = c1
        return 0

    lax.fori_loop(0, seq // CW_TQ, cw_body, 0)

    n_slc = seq // SLC_LEN
    n_tt = seq // LANES
    shp = (n_tt, LANES)
    cur = (lax.broadcasted_iota(jnp.int32, shp, 0) * LANES + lax.broadcasted_iota(jnp.int32, shp, 1)) // SLC_LEN
    vals = []
    for j in range(n_slc):
        forced = (cur == j) | (cur == j + 1)
        vj = jnp.where(cur >= j, imp_ref[:, j, :], -jnp.inf)
        vals.append(jnp.full(shp, jnp.inf, f32) if j == 0 else jnp.where(forced, jnp.inf, vj))
    wins = [jnp.zeros(shp, f32) for _ in range(n_slc)]
    losses = [jnp.zeros(shp, f32) for _ in range(n_slc)]
    for a in range(n_slc):
        for b in range(a + 1, n_slc):
            a_first = jnp.where(vals[a] >= vals[b], 1.0, 0.0)
            wins[a] = wins[a] + a_first
            losses[b] = losses[b] + a_first
    for j in range(n_slc):
        rank = losses[j] + (float(n_slc - 1 - j) - wins[j])
        sel = (rank < float(N_SEL)) & (vals[j] > -jnp.inf)
        ns_ref[j] = jnp.where(sel, 0.0, 1.0)
    eye = (lax.broadcasted_iota(jnp.int32, (LANES, LANES), 0)
           == lax.broadcasted_iota(jnp.int32, (LANES, LANES), 1)).astype(bf16)
    zpad = jnp.zeros((L_SEL, LANES), f32)
    for tt in range(n_tt):
        z = jnp.concatenate([zpad, ns_ref[:, tt, :]], axis=0).astype(bf16)
        selq_ref[tt * LANES:(tt + 1) * LANES, :] = _dot_nt(eye, z) * (-MASK_BIG)

    half = SEL_CHUNK // 2
    half_rows = GROUP * half

    def stacked_q(i):
        parts = []
        for h in range(2):
            tok = slice(i * SEL_CHUNK + h * half, i * SEL_CHUNK + (h + 1) * half)
            parts.append(_stack_heads(q_ref[tok, :].astype(f32), qaug[0:GROUP], selq_ref[tok, :]))
        return parts

    def causal(s):
        tri = tri_ref[...]
        return jnp.concatenate([s[r * half:(r + 1) * half, :] + tri for r in range(GROUP)], axis=0)

    def scores(qs, i, c):
        k0 = c * SEL_CHUNK
        if c < i:
            return (_dot_nt(jnp.concatenate(qs, axis=0), ks_ref[k0:k0 + SEL_CHUNK, :]),)
        s_a = causal(_dot_nt(qs[0], ks_ref[k0:k0 + half, :]))
        s_b = _dot_nt(qs[1], ks_ref[k0:k0 + SEL_CHUNK, :])
        return (s_a, jnp.concatenate([s_b[:, :half], causal(s_b[:, half:])], axis=1))

    def softmax_step(s, v, carry):
        if carry is None:
            m = jnp.max(s, axis=-1, keepdims=True)
            return m, jnp.dot(jnp.exp2(s - m).astype(bf16), v, preferred_element_type=f32)
        return _softmax_step(s, v, carry)

    def finish(i, acc):
        tok = slice(i * SEL_CHUNK, (i + 1) * SEL_CHUNK)
        o_s = acc / jnp.maximum(acc[:, HEAD_DIM:HEAD_DIM + 1], 1e-30)
        sig = gate_ref[tok, :].astype(f32)
        vals = []
        for r in range(GROUP):
            a_s = jnp.concatenate([o_s[h * half_rows + r * half:h * half_rows + (r + 1) * half, :]
                                   for h in range(2)], axis=0) * sig[:, 3 * r + 1:3 * r + 2]
            vals.append(a_s if r % 2 == 0 else pltpu.roll(a_s, HEAD_DIM, 1))
        c0, c1 = _merge_heads(vals, lane)
        o_ref[tok, 0:LANES] = (ocw_ref[tok, 0:LANES] + c0).astype(bf16)
        o_ref[tok, LANES:2 * LANES] = (ocw_ref[tok, LANES:2 * LANES] + c1).astype(bf16)

    steps = [(i, c) for i in range(seq // SEL_CHUNK) for c in range(i + 1)]
    qs = stacked_q(0)
    s_next = scores(qs, 0, 0)
    carry = None
    for k, (i, c) in enumerate(steps):
        s = s_next
        if k + 1 < len(steps):
            i2, c2 = steps[k + 1]
            if i2 != i:
                qs = stacked_q(i2)
            s_next = scores(qs, i2, c2)
        k0 = c * SEL_CHUNK
        if c < i:
            carry = softmax_step(s[0], vs_ref[k0:k0 + SEL_CHUNK, :], carry)
        else:
            rows_a = None if carry is None else tuple(a[:half_rows] for a in carry)
            rows_b = None if carry is None else tuple(a[half_rows:] for a in carry)
            _, acc_a = softmax_step(s[0], vs_ref[k0:k0 + half, :], rows_a)
            _, acc_b = softmax_step(s[1], vs_ref[k0:k0 + SEL_CHUNK, :], rows_b)
            finish(i, jnp.concatenate([acc_a, acc_b], axis=0))
            carry = None


def _position_lanes(pos):
    out = np.zeros((pos.shape[0], LANES), np.float32)
    out[:, L_HI:L_HI + 3] = ((pos // 64) * 64)[:, None]
    out[:, L_LO:L_LO + 3] = (pos % 64)[:, None]
    return out


def _nsa_tables(seq):
    pos = np.arange(seq)
    kaug_s = _position_lanes(pos)
    kaug_s[pos, L_SEL + pos // SLC_LEN] = 1.0
    kaug_w = np.zeros((seq + WIN_PAD, LANES), np.float32)
    kaug_w[WIN_PAD:] = _position_lanes(pos)
    kaug_w[:WIN_PAD, L_FLAG] = 1.0
    kaug_c = _position_lanes(np.arange(N_CMP_PAD) * CMP_STRIDE + CMP_LEN - 1)
    tl = np.arange(SEL_CHUNK // 2)[:, None]
    kk = np.arange(SEL_CHUNK // 2)[None, :]
    tri = np.where(kk <= tl, 0.0, -MASK_BIG).astype(np.float32)
    tl = np.arange(WIN_TQ)[:, None]
    kk = np.arange(WIN_KEYS)[None, :]
    dist = tl + WIN_PAD - kk
    band = np.tile(np.where((dist >= 0) & (dist < WINDOW), 0.0, -MASK_BIG).astype(np.float32), (GROUP, 1))
    c = np.arange(N_CMP_PAD)[None, :]
    j = np.arange(seq // SLC_LEN)[:, None]
    c_start, c_end, s_start = c * CMP_STRIDE, c * CMP_STRIDE + CMP_LEN - 1, j * SLC_LEN
    ovl = ((c_start <= s_start + SLC_LEN - 1) & (c_end >= s_start) & (c < seq // CMP_STRIDE - 1)).astype(np.float32)
    cend = np.tile((c * CMP_STRIDE + CMP_LEN - 1) - tl, (GROUP, 1)).astype(np.int32)
    return (jnp.asarray(kaug_s, bf16), jnp.asarray(kaug_w, bf16), jnp.asarray(kaug_c, f32),
            jnp.asarray(tri), jnp.asarray(band), jnp.asarray(ovl, bf16), jnp.asarray(cend))


def _query_aug():
    slopes = jnp.exp2(-8.0 * jnp.arange(1, N_HEADS + 1, dtype=f32) / N_HEADS).reshape(N_KV, GROUP) * LOG2E
    hi =slopes.astype(bf16).astype(f32)
    mid = (slopes - hi).astype(bf16).astype(f32)
    lo = (slopes - hi - mid).astype(bf16).astype(f32)
    pieces = jnp.stack([hi, mid, lo, hi, mid, lo], axis=-1)
    base = jnp.zeros((N_KV, GROUP, LANES), f32).at[:, :, L_HI:L_HI + 6].set(pieces)
    return jnp.concatenate([base, base.at[:, :, L_FLAG].set(-MASK_BIG)], axis=1)


def _nsa(proj, wlo, whi, pelo, pehi, w2bd, batch, seq):
    kaug_s, kaug_w, kaug_c, tri, band, ovl, cend = _nsa_tables(seq)
    qaug = _query_aug()
    const2 = lambda b, g: (0, 0)
    in_specs = [
        pl.BlockSpec((seq, 2 * LANES), lambda b, g: (b, OFF_Q // (2 * LANES) + g)),
        pl.BlockSpec((seq, LANES), lambda b, g: (b, OFF_KSW // LANES + g)),
        pl.BlockSpec((seq, LANES), lambda b, g: (b, OFF_VSW // LANES + g)),
        pl.BlockSpec((seq, LANES), lambda b, g: (b, OFF_GATE // LANES + g)),
        pl.BlockSpec((seq, LANES), lambda b, g: (b, OFF_KVC // LANES + g)),
        pl.BlockSpec(wlo.shape, const2),
        pl.BlockSpec(whi.shape, const2),
        pl.BlockSpec(pelo.shape, const2),
        pl.BlockSpec(pehi.shape, const2),
        pl.BlockSpec(w2bd.shape, const2),
        pl.BlockSpec(kaug_s.shape, const2),
        pl.BlockSpec(kaug_w.shape, const2),
        pl.BlockSpec(kaug_c.shape, const2),
        pl.BlockSpec((1, 2 * GROUP, LANES), lambda b, g: (g, 0, 0)),
        pl.BlockSpec(tri.shape, const2),
        pl.BlockSpec(band.shape, const2),
        pl.BlockSpec(ovl.shape, const2),
        pl.BlockSpec(cend.shape, const2),
    ]
    return pl.pallas_call(
        _nsa_kernel,
        grid=(batch, N_KV),
        in_specs=in_specs,
        out_specs=pl.BlockSpec((seq, 2 * LANES), lambda b, g: (b, g)),
        out_shape=jax.ShapeDtypeStruct((batch * seq, D_ATT), bf16),
        scratch_shapes=[
            pltpu.VMEM((seq, LANES), f32),
            pltpu.VMEM((seq, LANES), bf16),
            pltpu.VMEM((seq + WIN_PAD, LANES), bf16),
            pltpu.VMEM((seq, LANES), bf16),
            pltpu.VMEM((seq + WIN_PAD, LANES), bf16),
            pltpu.VMEM((seq, 2 * LANES), f32),
            pltpu.VMEM((seq // LANES, seq // SLC_LEN, LANES), f32),
            pltpu.VMEM((seq // SLC_LEN, seq // LANES, LANES), f32),
            pltpu.VMEM((seq, LANES), f32),
        ],
        compiler_params=pltpu.CompilerParams(
            dimension_semantics=("parallel", "parallel"), vmem_limit_bytes=VMEM_LIMIT),
        name="nsa",
    )(proj, proj, proj, proj, proj, wlo, whi, pelo, pehi, w2bd, kaug_s, kaug_w, kaug_c, qaug, tri, band, ovl, cend)


def _out_kernel(x_ref, ya_ref, zb_ref, gm_ref, ob_ref, wa_ref, wb_ref, wo_ref, fnw_ref, o_ref):
    y_a = jnp.dot(ya_ref[...], wa_ref[...], preferred_element_type=f32)
    yb_in = ob_ref[...].astype(f32) * zb_ref[...].astype(f32)
    y_b = jnp.dot(yb_in.astype(bf16), wb_ref[...], preferred_element_type=f32)
    mixed = (gm_ref[:, 0:D_MODEL].astype(f32) * y_a + gm_ref[:, D_MODEL:2 * D_MODEL].astype(f32) * y_b)
    xo = x_ref[...] + jnp.dot(mixed.astype(bf16), wo_ref[...], preferred_element_type=f32)
    r = lax.rsqrt(jnp.mean(xo * xo, axis=-1, keepdims=True) + NORM_EPS)
    o_ref[...] = (xo * r) * fnw_ref[...]


def _out(x2, ya, proj, ob, wa, wb, wo, fnw, tm=512):
    n_rows = x2.shape[0]
    blk = lambda c: pl.BlockSpec((tm, D_MODEL), lambda i, c=c: (i, c))
    full = lambda a: pl.BlockSpec(a.shape, lambda i: (0, 0))
    return pl.pallas_call(
        _out_kernel,
        grid=(n_rows // tm,),
        in_specs=[
            blk(0),
            blk(0),
            blk(OFF_ZB // D_MODEL),
            pl.BlockSpec((tm, 2 * D_MODEL), lambda i: (i, OFF_GMIX // (2 * D_MODEL))),
            blk(0),
            full(wa), full(wb), full(wo), full(fnw),
        ],
        out_specs=pl.BlockSpec((tm, D_MODEL), lambda i: (i, 0)),
        out_shape=jax.ShapeDtypeStruct((n_rows, D_MODEL), f32),
        compiler_params=pltpu.CompilerParams(
            dimension_semantics=("parallel",), vmem_limit_bytes=VMEM_LIMIT),
        name="merge_out",
    )(x2, ya, proj, proj, ob, wa, wb, wo, fnw)


def _blockdiag_w1(w1_k, w1_v, lo):
    half = CMP_STRIDE * HEAD_DIM
    wk = w1_k[lo * half:(lo + 1) * half].reshape(CMP_STRIDE, HEAD_DIM, CMP_HID)
    wv = w1_v[lo * half:(lo + 1) * half].reshape(CMP_STRIDE, HEAD_DIM, CMP_HID)
    z = jnp.zeros_like(wk)
    top = jnp.concatenate([wk, z], axis=-1)
    bot = jnp.concatenate([z, wv], axis=-1)
    return jnp.concatenate([top, bot], axis=1).reshape(CMP_STRIDE * 2 * HEAD_DIM, 2 * CMP_HID).astype(bf16)


def _pe_row(pe_k, pe_v, lo):
    sl = slice(lo * CMP_STRIDE, (lo + 1) * CMP_STRIDE)
    return jnp.concatenate([pe_k[sl], pe_v[sl]], axis=-1).reshape(1, CMP_STRIDE * 2 * HEAD_DIM)


def kernel(x, norm_w, w_in, conv_w, conv_b, cmp_pe_k, cmp_pe_v, cmp_w1_k, cmp_w2_k, cmp_w1_v, cmp_w2_v,
           w_proj_a, w_proj_b, w_out, final_norm_w):
    batch, seq, _ = x.shape
    assert norm_w.shape[0] == 1 and seq % SEL_CHUNK == 0 and seq // CMP_STRIDE == N_CMP_PAD
    w_p = _reordered_w_in(w_in[0])
    x2 = x.reshape(batch * seq, D_MODEL)
    proj, ya = _inproj(x2, norm_w, w_p, conv_w[0], conv_b, seq)

    wlo = _blockdiag_w1(cmp_w1_k[0], cmp_w1_v[0], 0)
    whi = _blockdiag_w1(cmp_w1_k[0], cmp_w1_v[0], 1)
    pelo = _pe_row(cmp_pe_k[0], cmp_pe_v[0], 0)
    pehi = _pe_row(cmp_pe_k[0], cmp_pe_v[0], 1)
    zk = jnp.zeros_like(cmp_w2_k[0])
    w2bd = jnp.concatenate([jnp.concatenate([cmp_w2_k[0], zk], axis=1),
                            jnp.concatenate([zk, cmp_w2_v[0]], axis=1)], axis=0).astype(bf16)
    ob = _nsa(proj, wlo, whi, pelo, pehi, w2bd, batch, seq)

    out = _out(x2, ya, proj, ob, w_proj_a[0].astype(bf16), w_proj_b[0].astype(bf16),
               w_out[0].astype(bf16), final_norm_w.reshape(1, D_MODEL))
    return out.reshape(batch, seq, D_MODEL)
```

```python
import functools

import numpy as np
import jax
import jax.numpy as jnp
from jax import lax
from jax.experimental import pallas as pl
from jax.experimental.pallas import tpu as pltpu

D_MODEL = 1024
D_CONV = 1024
CONV_W = 3
N_HEADS = 16
HEAD_DIM = 64
N_KV = 4
GROUP = N_HEADS // N_KV
D_ATT = N_HEADS * HEAD_DIM
D_KV = N_KV * HEAD_DIM
CMP_LEN = 32
CMP_STRIDE = 16
CMP_HID = 128
SLC_LEN = 64
N_SEL = 8
WINDOW = 512
NORM_EPS = 1e-6

LANES = 128
N_CMP_PAD = 128
MASK_BIG = float(2.0 ** 100)
LOG2E = float(np.log2(np.e))
SEL_CHUNK = 512
WIN_TQ = 128
WIN_KEYS = WINDOW + WIN_TQ
WIN_PAD = WINDOW
CW_TQ = 512

OFF_Q = 0
OFF_ZB = 1024
OFF_GMIX = 2048
OFF_KVC = 4096
OFF_KSW = 4608
OFF_VSW = 5120
OFF_GATE = 5632
N_PROJ = 6144

L_HI = 64
L_LO = 67
L_FLAG = 70
L_SEL = 96

VMEM_LIMIT = 56 * 1024 * 1024

f32 = jnp.float32
bf16 = jnp.bfloat16


def _reordered_w_in(w):
    o_q = 4 * D_CONV
    o_kv = o_q + D_ATT
    o_gate = o_kv + 6 * D_KV
    o_zb = o_gate + 3 * N_HEADS
    o_gmix = o_zb + D_ATT
    q = (w[:, o_q:o_kv] * (HEAD_DIM ** -0.5 * LOG2E)).astype(bf16)
    w = w.astype(bf16)
    kv = w[:, o_kv:o_gate].reshape(D_MODEL, 6, N_KV, HEAD_DIM)
    pair = lambda a, b: jnp.stack([kv[:, a], kv[:, b]], axis=2).reshape(D_MODEL, N_KV * LANES)
    gates = jnp.pad(w[:, o_gate:o_zb].reshape(D_MODEL, N_KV, 3 * GROUP),
                    ((0, 0), (0, 0), (0, LANES - 3 * GROUP))).reshape(D_MODEL, N_KV * LANES)
    cols = [w[:, :o_q], q, w[:, o_zb:o_gmix], w[:, o_gmix:], pair(0, 1), pair(2, 4), pair(3, 5), gates]
    return jnp.concatenate(cols, axis=1)


def _sigmoid(x):
    return 1.0 / (1.0 + jnp.exp(-x))


def _dot_nt(a, b):
    return lax.dot_general(a, b, (((1,), (1,)), ((), ())), preferred_element_type=f32)


CARRY_ROWS = 8
ROW_BLOCK = 256


def _column_activations(lo, hi):
    silu = lambda a: a * _sigmoid(a)
    keep = lambda a: a
    segments = ((OFF_Q, OFF_ZB, keep), (OFF_ZB, OFF_GMIX, silu), (OFF_GMIX, OFF_KVC, _sigmoid),
                (OFF_KVC, OFF_GATE, keep), (OFF_GATE, N_PROJ, _sigmoid))
    return [(max(a, lo) - lo, min(b, hi) - lo, f) for a, b, f in segments if max(a, lo) < min(b, hi)]


def _inproj_kernel(x_ref, nw_ref, w1_ref, w2_ref, cw_ref, cb_ref, o_ref, ya_ref, h_ref, carry_ref,
                   *, tiles_per_seq):
    i = pl.program_id(0)
    j = pl.program_id(1)
    tm = x_ref.shape[0]
    tn = o_ref.shape[1]
    n_stored = N_PROJ // tn
    blocks = [slice(k * ROW_BLOCK, (k + 1) * ROW_BLOCK) for k in range(tm // ROW_BLOCK)]
    product = lambda rows, w: jnp.dot(h_ref[rows, :], w, preferred_element_type=f32)

    @pl.when((i == 0) & (j == 0))
    def _():
        carry_ref[...] = jnp.zeros_like(carry_ref)

    @pl.when(j == 0)
    def _():
        x = x_ref[...]
        r = lax.rsqrt(jnp.mean(x * x, axis=-1, keepdims=True) + NORM_EPS)
        h_ref[...] = ((x * r) * nw_ref[...]).astype(bf16)

    for t in range(n_stored):
        @pl.when(j == t)
        def _(t=t):
            for rows in blocks:
                acc = product(rows, w1_ref[...])
                for lo, hi, act in _column_activations(t * tn, (t + 1) * tn):
                    o_ref[rows, lo:hi] = act(acc[:, lo:hi]).astype(bf16)

    @pl.when(j == n_stored)
    def _():
        lead = carry_ref[...] * (i % tiles_per_seq != 0).astype(f32)
        cw = cw_ref[...]
        for rows in blocks:
            u = product(rows, w2_ref[:, :D_CONV]) * product(rows, w1_ref[:, :D_CONV])
            win = jnp.concatenate([lead, u], axis=0)
            u1 = pltpu.roll(win, 1, 0)[CARRY_ROWS:, :]
            u2 = pltpu.roll(win, 2, 0)[CARRY_ROWS:, :]
            y = cb_ref[...] + cw[0:1, :] * u2 + cw[1:2, :] * u1 + cw[2:3, :] * u
            z = product(rows, w2_ref[:, D_CONV:])
            ya_ref[rows, :] = (product(rows, w1_ref[:, D_CONV:]) * y * (z * _sigmoid(z))).astype(bf16)
            lead = u[ROW_BLOCK - CARRY_ROWS:, :]
        carry_ref[...] = lead


def _inproj(x2, norm_w, w_p, conv_w, conv_b, seq, tm=1024):
    n_rows = x2.shape[0]
    tn = 2 * D_CONV
    assert N_PROJ % tn == 0 and seq % tm == 0 and tm % ROW_BLOCK == 0 and w_p.shape[1] == 2 * tn + N_PROJ
    n_stored = N_PROJ // tn
    first = lambda i, j: (0, jnp.where(j < n_stored, j + 2, 0))
    second = lambda i, j: (0, 1)
    return pl.pallas_call(
        functools.partial(_inproj_kernel, tiles_per_seq=seq // tm),
        grid=(n_rows // tm, n_stored + 1),
        in_specs=[
            pl.BlockSpec((tm, D_MODEL), lambda i, j: (i, 0)),
            pl.BlockSpec((1, D_MODEL), lambda i, j: (0, 0)),
            pl.BlockSpec((D_MODEL, tn), first),
            pl.BlockSpec((D_MODEL, tn), second),
            pl.BlockSpec(conv_w.shape, lambda i, j: (0, 0)),
            pl.BlockSpec(conv_b.shape, lambda i, j: (0, 0)),
        ],
        out_specs=[
            pl.BlockSpec((tm, tn), lambda i, j: (i, jnp.minimum(j, n_stored - 1))),
            pl.BlockSpec((tm, D_CONV), lambda i, j: (i, 0)),
        ],
        out_shape=[jax.ShapeDtypeStruct((n_rows, N_PROJ), bf16), jax.ShapeDtypeStruct((n_rows, D_CONV), bf16)],
        scratch_shapes=[
            pltpu.VMEM((tm, D_MODEL), bf16),
            pltpu.VMEM((CARRY_ROWS, D_CONV), f32),
        ],
        compiler_params=pltpu.CompilerParams(
            dimension_semantics=("arbitrary", "arbitrary"), vmem_limit_bytes=VMEM_LIMIT),
        name="inproj",
    )(x2, norm_w, w_p, w_p, conv_w, conv_b)


def _softmax_step(s, v, carry):
    m, acc = carry
    m_new = jnp.maximum(m, jnp.max(s, axis=-1, keepdims=True))
    alpha = jnp.exp2(m - m_new)
    p = jnp.exp2(s - m_new)
    acc = alpha * acc + jnp.dot(p.astype(bf16), v, preferred_element_type=f32)
    return m_new, acc


def _stack_heads(q_tile, aug_rows, extra):
    lane = lax.broadcasted_iota(jnp.int32, (1, LANES), 1)
    pairs = (q_tile[:, :LANES], q_tile[:, LANES:])
    parts = []
    for r in range(GROUP):
        qh = pairs[r // 2]
        if r % 2 == 1:
            qh = pltpu.roll(qh, HEAD_DIM, 1)
        aug = aug_rows[r:r + 1, :]
        if extra is not None:
            aug = aug + extra
        parts.append(jnp.where(lane < HEAD_DIM, qh, aug).astype(bf16))
    return jnp.concatenate(parts, axis=0)


def _merge_heads(vals, lane):
    return (jnp.where(lane < HEAD_DIM, vals[0], vals[1]), jnp.where(lane < HEAD_DIM, vals[2], vals[3]))


def _nsa_kernel(q_ref, ksw_ref, vsw_ref, gate_ref, kvc_ref, wlo_ref, whi_ref, pelo_ref, pehi_ref, w2_ref,
                kaug_s_ref, kaug_w_ref, kaug_c_ref, qaug_ref, tri_ref, band_ref, ovl_ref, cend_ref,
                o_ref,
                kvcf_ref, ks_ref, kw_ref, vs_ref, vw_ref, ocw_ref, imp_ref, ns_ref, selq_ref):
    seq = q_ref.shape[0]
    lane = lax.broadcasted_iota(jnp.int32, (1, LANES), 1)
    low = lane < HEAD_DIM

    kvcf_ref[...] = kvc_ref[...].astype(f32)
    xcat = jnp.concatenate([kvcf_ref[pl.ds(l, N_CMP_PAD, stride=CMP_STRIDE), :] for l in range(CMP_STRIDE)],
                           axis=1)
    xa = (xcat + pelo_ref[...]).astype(bf16)
    xb = (xcat + pehi_ref[...]).astype(bf16)
    a = jnp.dot(xa, wlo_ref[...], preferred_element_type=f32)
    bm = jnp.dot(xb, whi_ref[...], preferred_element_type=f32)
    h = a + pltpu.roll(bm, N_CMP_PAD - 1, 0)
    hid = h * _sigmoid(h)
    kcvc = jnp.dot(hid.astype(bf16), w2_ref[...], preferred_element_type=f32)

    ksw = ksw_ref[...]
    vsw = vsw_ref[...]
    one = jnp.ones((1, LANES), bf16)
    ks_ref[...] = jnp.where(low, ksw, kaug_s_ref[...])
    kwin = pltpu.roll(ksw.astype(f32), HEAD_DIM, 1).astype(bf16)
    kw_ref[0:WIN_PAD, :] = kaug_w_ref[0:WIN_PAD, :]
    kw_ref[WIN_PAD:, :] = jnp.where(low, kwin, kaug_w_ref[WIN_PAD:, :])
    vs_ref[...] = jnp.where(low, vsw, one)
    vw_ref[0:WIN_PAD, :] = jnp.zeros((WIN_PAD, LANES), bf16)
    vw_ref[WIN_PAD:, :] = jnp.where(low, one, vsw)

    qaug = qaug_ref[0]
    kc_aug = jnp.where(low, kcvc, kaug_c_ref[...]).astype(bf16)
    vc_rows = jnp.concatenate([jnp.zeros((N_CMP_PAD, LANES), bf16), kcvc.astype(bf16)], axis=1)
    zeros_w = jnp.zeros((WIN_KEYS, LANES), bf16)
    ovl = ovl_ref[...]

    def cw_body(i, _):
        t0 = pl.multiple_of(i * CW_TQ, CW_TQ)
        sig = gate_ref[pl.ds(t0, CW_TQ), :].astype(f32)
        o_w, o_c = [], []
        n_tiles = CW_TQ // WIN_TQ

        def tile_scores(h):
            tw = pl.multiple_of(t0 + h * WIN_TQ, WIN_TQ)
            qw = _stack_heads(q_ref[pl.ds(tw, WIN_TQ), :].astype(f32), qaug[GROUP:2 * GROUP], None)
            k_all = jnp.concatenate([kw_ref[pl.ds(tw, WIN_KEYS), :], kc_aug], axis=0)
            return _dot_nt(qw, k_all)

        s_next = tile_scores(0)
        for h in range(n_tiles):
            tw = pl.multiple_of(t0 + h * WIN_TQ, WIN_TQ)
            s = s_next
            if h + 1 < n_tiles:
                s_next = tile_scores(h + 1)

            sw = jnp.concatenate([s[:, :WIN_TQ] + band_ref[:, :WIN_TQ], s[:, WIN_TQ:WINDOW],
                                  s[:, WINDOW:WIN_KEYS] + band_ref[:, WINDOW:]], axis=1)
            p_w = jnp.exp2(sw - jnp.max(sw, axis=-1, keepdims=True))

            ok = cend_ref[...] <= tw
            sm = jnp.where(ok, s[:, WIN_KEYS:], -jnp.inf)
            m = jnp.max(sm, axis=-1, keepdims=True)
            m = jnp.where(m > -jnp.inf, m, 0.0)
            e = jnp.where(ok, jnp.exp2(sm - m), 0.0)
            p_c = e / jnp.maximum(jnp.sum(e, axis=-1, keepdims=True), 1e-30)

            p_all = jnp.concatenate([p_w.astype(bf16), p_c.astype(bf16)], axis=1)
            v_all = jnp.concatenate(
                [jnp.concatenate([vw_ref[pl.ds(tw, WIN_KEYS), :], zeros_w], axis=1), vc_rows], axis=0)
            acc = jnp.dot(p_all, v_all, preferred_element_type=f32)
            acc_w = acc[:, :LANES]
            o_w.append(acc_w / jnp.maximum(acc_w[:, 0:1], 1e-30))
            o_c.append(acc[:, LANES:])

            ps = p_c[0:WIN_TQ] + p_c[WIN_TQ:2 * WIN_TQ] + p_c[2 * WIN_TQ:3 * WIN_TQ] + p_c[3 * WIN_TQ:4 * WIN_TQ]
            p1 = ps.astype(bf16)
            r1 = ps - p1.astype(f32)
            p2 = r1.astype(bf16)
            p3 = (r1 - p2.astype(f32)).astype(bf16)
            imp_ref[i * (CW_TQ // WIN_TQ) + h] = _dot_nt(ovl, p1) + _dot_nt(ovl, p2) + _dot_nt(ovl, p3)

        vals = []
        for r in range(GROUP):
            head = lambda parts: jnp.concatenate([o[r * WIN_TQ:(r + 1) * WIN_TQ, :] for o in parts], axis=0)
            a = head(o_c) * sig[:, 3 * r:3 * r + 1] + head(o_w) * sig[:, 3 * r + 2:3 * r + 3]
            vals.append(pltpu.roll(a, HEAD_DIM, 1) if r % 2 == 0 else a)
        c0, c1 = _merge_heads(vals, lane)
        ocw_ref[pl.ds(t0, CW_TQ), 0:LANES] = c0
        ocw_ref[pl.ds(t0, CW_TQ), LANES:2 * LANES] = c1
        return 0

    lax.fori_loop(0, seq // CW_TQ, cw_body, 0)

    n_slc = seq // SLC_LEN
    n_tt = seq // LANES
    shp = (n_tt, LANES)
    cur = (lax.broadcasted_iota(jnp.int32, shp, 0) * LANES + lax.broadcasted_iota(jnp.int32, shp, 1)) // SLC_LEN
    vals = []
    for j in range(n_slc):
        forced = (cur == j) | (cur == j + 1)
        vj = jnp.where(cur >= j, imp_ref[:, j, :], -jnp.inf)
        vals.append(jnp.full(shp, jnp.inf, f32) if j == 0 else jnp.where(forced, jnp.inf, vj))
    wins = [jnp.zeros(shp, f32) for _ in range(n_slc)]
    losses = [jnp.zeros(shp, f32) for _ in range(n_slc)]
    for a in range(n_slc):
        for b in range(a + 1, n_slc):
            a_first = jnp.where(vals[a] >= vals[b], 1.0, 0.0)
            wins[a] = wins[a] + a_first
            losses[b] = losses[b] + a_first
    for j in range(n_slc):
        rank = losses[j] + (float(n_slc - 1 - j) - wins[j])
        sel = (rank < float(N_SEL)) & (vals[j] > -jnp.inf)
        ns_ref[j] = jnp.where(sel, 0.0, 1.0)
    eye = (lax.broadcasted_iota(jnp.int32, (LANES, LANES), 0)
           == lax.broadcasted_iota(jnp.int32, (LANES, LANES), 1)).astype(bf16)
    zpad = jnp.zeros((L_SEL, LANES), f32)
    for tt in range(n_tt):
        z = jnp.concatenate([zpad, ns_ref[:, tt, :]], axis=0).astype(bf16)
        selq_ref[tt * LANES:(tt + 1) * LANES, :] = _dot_nt(eye, z) * (-MASK_BIG)

    half = SEL_CHUNK // 2
    half_rows = GROUP * half

    def stacked_q(i):
        parts = []
        for h in range(2):
            tok = slice(i * SEL_CHUNK + h * half, i * SEL_CHUNK + (h + 1) * half)
            parts.append(_stack_heads(q_ref[tok, :].astype(f32), qaug[0:GROUP], selq_ref[tok, :]))
        return parts

    def causal(s):
        tri = tri_ref[...]
        return jnp.concatenate([s[r * half:(r + 1) * half, :] + tri for r in range(GROUP)], axis=0)

    def scores(qs, i, c):
        k0 = c * SEL_CHUNK
        if c < i:
            return (_dot_nt(jnp.concatenate(qs, axis=0), ks_ref[k0:k0 + SEL_CHUNK, :]),)
        s_a = causal(_dot_nt(qs[0], ks_ref[k0:k0 + half, :]))
        s_b = _dot_nt(qs[1], ks_ref[k0:k0 + SEL_CHUNK, :])
        return (s_a, jnp.concatenate([s_b[:, :half], causal(s_b[:, half:])], axis=1))

    def softmax_step(s, v, carry):
        if carry is None:
            m = jnp.max(s, axis=-1, keepdims=True)
            return m, jnp.dot(jnp.exp2(s - m).astype(bf16), v, preferred_element_type=f32)
        return _softmax_step(s, v, carry)

    def finish(i, acc):
        tok = slice(i * SEL_CHUNK, (i + 1) * SEL_CHUNK)
        o_s = acc / jnp.maximum(acc[:, HEAD_DIM:HEAD_DIM + 1], 1e-30)
        sig = gate_ref[tok, :].astype(f32)
        vals = []
        for r in range(GROUP):
            a_s = jnp.concatenate([o_s[h * half_rows + r * half:h * half_rows + (r + 1) * half, :]
                                   for h in range(2)], axis=0) * sig[:, 3 * r + 1:3 * r + 2]
            vals.append(a_s if r % 2 == 0 else pltpu.roll(a_s, HEAD_DIM, 1))
        c0, c1 = _merge_heads(vals, lane)
        o_ref[tok, 0:LANES] = (ocw_ref[tok, 0:LANES] + c0).astype(bf16)
        o_ref[tok, LANES:2 * LANES] = (ocw_ref[tok, LANES:2 * LANES] + c1).astype(bf16)

    steps = [(i, c) for i in range(seq // SEL_CHUNK) for c in range(i + 1)]
    qs = stacked_q(0)
    s_next = scores(qs, 0, 0)
    carry = None
    for k, (i, c) in enumerate(steps):
        s = s_next
        if k + 1 < len(steps):
            i2, c2 = steps[k + 1]
            if i2 != i:
                qs = stacked_q(i2)
            s_next = scores(qs, i2, c2)
        k0 = c * SEL_CHUNK
        if c < i:
            carry = softmax_step(s[0], vs_ref[k0:k0 + SEL_CHUNK, :], carry)
        else:
            rows_a = None if carry is None else tuple(a[:half_rows] for a in carry)
            rows_b = None if carry is None else tuple(a[half_rows:] for a in carry)
            _, acc_a = softmax_step(s[0], vs_ref[k0:k0 + half, :], rows_a)
            _, acc_b = softmax_step(s[1], vs_ref[k0:k0 + SEL_CHUNK, :], rows_b)
            finish(i, jnp.concatenate([acc_a, acc_b], axis=0))
            carry = None


def _position_lanes(pos):
    out = np.zeros((pos.shape[0], LANES), np.float32)
    out[:, L_HI:L_HI + 3] = ((pos // 64) * 64)[:, None]
    out[:, L_LO:L_LO + 3] = (pos % 64)[:, None]
    return out


def _nsa_tables(seq):
    pos = np.arange(seq)
    kaug_s = _position_lanes(pos)
    kaug_s[pos, L_SEL + pos // SLC_LEN] = 1.0
    kaug_w = np.zeros((seq + WIN_PAD, LANES), np.float32)
    kaug_w[WIN_PAD:] = _position_lanes(pos)
    kaug_w[:WIN_PAD, L_FLAG] = 1.0
    kaug_c = _position_lanes(np.arange(N_CMP_PAD) * CMP_STRIDE + CMP_LEN - 1)
    tl = np.arange(SEL_CHUNK // 2)[:, None]
    kk = np.arange(SEL_CHUNK // 2)[None, :]
    tri = np.where(kk <= tl, 0.0, -MASK_BIG).astype(np.float32)
    tl = np.arange(WIN_TQ)[:, None]
    kk = np.arange(WIN_KEYS)[None, :]
    dist = tl + WIN_PAD - kk
    band = np.tile(np.where((dist >= 0) & (dist < WINDOW), 0.0, -MASK_BIG).astype(np.float32), (GROUP, 1))
    c = np.arange(N_CMP_PAD)[None, :]
    j = np.arange(seq // SLC_LEN)[:, None]
    c_start, c_end, s_start = c * CMP_STRIDE, c * CMP_STRIDE + CMP_LEN - 1, j * SLC_LEN
    ovl = ((c_start <= s_start + SLC_LEN - 1) & (c_end >= s_start) & (c < seq // CMP_STRIDE - 1)).astype(np.float32)
    cend = np.tile((c * CMP_STRIDE + CMP_LEN - 1) - tl, (GROUP, 1)).astype(np.int32)
    return (jnp.asarray(kaug_s, bf16), jnp.asarray(kaug_w, bf16), jnp.asarray(kaug_c, f32),
            jnp.asarray(tri), jnp.asarray(band), jnp.asarray(ovl, bf16), jnp.asarray(cend))


def _query_aug():
    slopes = jnp.exp2(-8.0 * jnp.arange(1, N_HEADS + 1, dtype=f32) / N_HEADS).reshape(N_KV, GROUP) * LOG2E
    hi =slopes.astype(bf16).astype(f32)
    mid = (slopes - hi).astype(bf16).astype(f32)
    lo = (slopes - hi - mid).astype(bf16).astype(f32)
    pieces = jnp.stack([hi, mid, lo, hi, mid, lo], axis=-1)
    base = jnp.zeros((N_KV, GROUP, LANES), f32).at[:, :, L_HI:L_HI + 6].set(pieces)
    return jnp.concatenate([base, base.at[:, :, L_FLAG].set(-MASK_BIG)], axis=1)


def _nsa(proj, wlo, whi, pelo, pehi, w2bd, batch, seq):
    kaug_s, kaug_w, kaug_c, tri, band, ovl, cend = _nsa_tables(seq)
    qaug = _query_aug()
    const2 = lambda b, g: (0, 0)
    in_specs = [
        pl.BlockSpec((seq, 2 * LANES), lambda b, g: (b, OFF_Q // (2 * LANES) + g)),
        pl.BlockSpec((seq, LANES), lambda b, g: (b, OFF_KSW // LANES + g)),
        pl.BlockSpec((seq, LANES), lambda b, g: (b, OFF_VSW // LANES + g)),
        pl.BlockSpec((seq, LANES), lambda b, g: (b, OFF_GATE // LANES + g)),
        pl.BlockSpec((seq, LANES), lambda b, g: (b, OFF_KVC // LANES + g)),
        pl.BlockSpec(wlo.shape, const2),
        pl.BlockSpec(whi.shape, const2),
        pl.BlockSpec(pelo.shape, const2),
        pl.BlockSpec(pehi.shape, const2),
        pl.BlockSpec(w2bd.shape, const2),
        pl.BlockSpec(kaug_s.shape, const2),
        pl.BlockSpec(kaug_w.shape, const2),
        pl.BlockSpec(kaug_c.shape, const2),
        pl.BlockSpec((1, 2 * GROUP, LANES), lambda b, g: (g, 0, 0)),
        pl.BlockSpec(tri.shape, const2),
        pl.BlockSpec(band.shape, const2),
        pl.BlockSpec(ovl.shape, const2),
        pl.BlockSpec(cend.shape, const2),
    ]
    return pl.pallas_call(
        _nsa_kernel,
        grid=(batch, N_KV),
        in_specs=in_specs,
        out_specs=pl.BlockSpec((seq, 2 * LANES), lambda b, g: (b, g)),
        out_shape=jax.ShapeDtypeStruct((batch * seq, D_ATT), bf16),
        scratch_shapes=[
            pltpu.VMEM((seq, LANES), f32),
            pltpu.VMEM((seq, LANES), bf16),
            pltpu.VMEM((seq + WIN_PAD, LANES), bf16),
            pltpu.VMEM((seq, LANES), bf16),
            pltpu.VMEM((seq + WIN_PAD, LANES), bf16),
            pltpu.VMEM((seq, 2 * LANES), f32),
            pltpu.VMEM((seq // LANES, seq // SLC_LEN, LANES), f32),
            pltpu.VMEM((seq // SLC_LEN, seq // LANES, LANES), f32),
            pltpu.VMEM((seq, LANES), f32),
        ],
        compiler_params=pltpu.CompilerParams(
            dimension_semantics=("parallel", "parallel"), vmem_limit_bytes=VMEM_LIMIT),
        name="nsa",
    )(proj, proj, proj, proj, proj, wlo, whi, pelo, pehi, w2bd, kaug_s, kaug_w, kaug_c, qaug, tri, band, ovl, cend)


def _out_kernel(x_ref, ya_ref, zb_ref, gm_ref, ob_ref, wa_ref, wb_ref, wo_ref, fnw_ref, o_ref):
    y_a = jnp.dot(ya_ref[...], wa_ref[...], preferred_element_type=f32)
    yb_in = ob_ref[...].astype(f32) * zb_ref[...].astype(f32)
    y_b = jnp.dot(yb_in.astype(bf16), wb_ref[...], preferred_element_type=f32)
    mixed = (gm_ref[:, 0:D_MODEL].astype(f32) * y_a + gm_ref[:, D_MODEL:2 * D_MODEL].astype(f32) * y_b)
    xo = x_ref[...] + jnp.dot(mixed.astype(bf16), wo_ref[...], preferred_element_type=f32)
    r = lax.rsqrt(jnp.mean(xo * xo, axis=-1, keepdims=True) + NORM_EPS)
    o_ref[...] = (xo * r) * fnw_ref[...]


def _out(x2, ya, proj, ob, wa, wb, wo, fnw, tm=512):
    n_rows = x2.shape[0]
    blk = lambda c: pl.BlockSpec((tm, D_MODEL), lambda i, c=c: (i, c))
    full = lambda a: pl.BlockSpec(a.shape, lambda i: (0, 0))
    return pl.pallas_call(
        _out_kernel,
        grid=(n_rows // tm,),
        in_specs=[
            blk(0),
            blk(0),
            blk(OFF_ZB // D_MODEL),
            pl.BlockSpec((tm, 2 * D_MODEL), lambda i: (i, OFF_GMIX // (2 * D_MODEL))),
            blk(0),
            full(wa), full(wb), full(wo), full(fnw),
        ],
        out_specs=pl.BlockSpec((tm, D_MODEL), lambda i: (i, 0)),
        out_shape=jax.ShapeDtypeStruct((n_rows, D_MODEL), f32),
        compiler_params=pltpu.CompilerParams(
            dimension_semantics=("parallel",), vmem_limit_bytes=VMEM_LIMIT),
        name="merge_out",
    )(x2, ya, proj, proj, ob, wa, wb, wo, fnw)


def _blockdiag_w1(w1_k, w1_v, lo):
    half = CMP_STRIDE * HEAD_DIM
    wk = w1_k[lo * half:(lo + 1) * half].reshape(CMP_STRIDE, HEAD_DIM, CMP_HID)
    wv = w1_v[lo * half:(lo + 1) * half].reshape(CMP_STRIDE, HEAD_DIM, CMP_HID)
    z = jnp.zeros_like(wk)
    top = jnp.concatenate([wk, z], axis=-1)
    bot = jnp.concatenate([z, wv], axis=-1)
    return jnp.concatenate([top, bot], axis=1).reshape(CMP_STRIDE * 2 * HEAD_DIM, 2 * CMP_HID).astype(bf16)


def _pe_row(pe_k, pe_v, lo):
    sl = slice(lo * CMP_STRIDE, (lo + 1) * CMP_STRIDE)
    return jnp.concatenate([pe_k[sl], pe_v[sl]], axis=-1).reshape(1, CMP_STRIDE * 2 * HEAD_DIM)


def kernel(x, norm_w, w_in, conv_w, conv_b, cmp_pe_k, cmp_pe_v, cmp_w1_k, cmp_w2_k, cmp_w1_v, cmp_w2_v,
           w_proj_a, w_proj_b, w_out, final_norm_w):
    batch, seq, _ = x.shape
    assert norm_w.shape[0] == 1 and seq % SEL_CHUNK == 0 and seq // CMP_STRIDE == N_CMP_PAD
    w_p = _reordered_w_in(w_in[0])
    x2 = x.reshape(batch * seq, D_MODEL)
    proj, ya = _inproj(x2, norm_w, w_p, conv_w[0], conv_b, seq)

    wlo = _blockdiag_w1(cmp_w1_k[0], cmp_w1_v[0], 0)
    whi = _blockdiag_w1(cmp_w1_k[0], cmp_w1_v[0], 1)
    pelo = _pe_row(cmp_pe_k[0], cmp_pe_v[0], 0)
    pehi = _pe_row(cmp_pe_k[0], cmp_pe_v[0], 1)
    zk = jnp.zeros_like(cmp_w2_k[0])
    w2bd = jnp.concatenate([jnp.concatenate([cmp_w2_k[0], zk], axis=1),
                            jnp.concatenate([zk, cmp_w2_v[0]], axis=1)], axis=0).astype(bf16)
    ob = _nsa(proj, wlo, whi, pelo, pehi, w2bd, batch, seq)

    out = _out(x2, ya, proj, ob, w_proj_a[0].astype(bf16), w_proj_b[0].astype(bf16),
               w_out[0].astype(bf16), final_norm_w.reshape(1, D_MODEL))
    return out.reshape(batch, seq, D_MODEL)
```

```python
import functools

import numpy as np
import jax
import jax.numpy as jnp
from jax import lax
from jax.experimental import pallas as pl
from jax.experimental.pallas import tpu as pltpu

D_MODEL = 1024
D_CONV = 1024
CONV_W = 3
N_HEADS = 16
HEAD_DIM = 64
N_KV = 4
GROUP = N_HEADS // N_KV
D_ATT = N_HEADS * HEAD_DIM
D_KV = N_KV * HEAD_DIM
CMP_LEN = 32
CMP_STRIDE = 16
CMP_HID = 128
SLC_LEN = 64
N_SEL = 8
WINDOW = 512
NORM_EPS = 1e-6

LANES = 128
N_CMP_PAD = 128
MASK_BIG = float(2.0 ** 100)
LOG2E = float(np.log2(np.e))
SEL_CHUNK = 512
WIN_TQ = 128
WIN_KEYS = WINDOW + WIN_TQ
WIN_PAD = WINDOW
CW_TQ = 512

OFF_Q = 0
OFF_ZB = 1024
OFF_GMIX = 2048
OFF_KVC = 4096
OFF_KSW = 4608
OFF_VSW = 5120
OFF_GATE = 5632
N_PROJ = 6144

L_HI = 64
L_LO = 67
L_FLAG = 70
L_SEL = 96

VMEM_LIMIT = 56 * 1024 * 1024

f32 = jnp.float32
bf16 = jnp.bfloat16


W_Q = 4 * D_CONV
W_KV = W_Q + D_ATT
W_GATE = W_KV + 6 * D_KV
W_ZB = W_GATE + 3 * N_HEADS
N_IN = W_ZB + D_ATT + 2 * D_MODEL


def _wprep_kernel(w_ref, tail_ref, o_ref):
    o_ref[:, :W_Q] = w_ref[:, :W_Q].astype(bf16)
    o_ref[:, W_Q:W_KV] = (w_ref[:, W_Q:W_KV] * (HEAD_DIM ** -0.5 * LOG2E)).astype(bf16)
    o_ref[:, W_KV:W_KV + N_IN - W_ZB] = w_ref[:, W_ZB:].astype(bf16)
    o_ref[:, W_KV + N_IN - W_ZB:] = tail_ref[...]


def _reordered_w_in(w, rows=128):
    small = w[:, W_KV:W_ZB].astype(bf16)
    kv = small[:, :6 * D_KV].reshape(D_MODEL, 6, N_KV, HEAD_DIM)
    pair = lambda a, b: jnp.stack([kv[:, a], kv[:, b]], axis=2).reshape(D_MODEL, N_KV * LANES)
    gates = jnp.pad(small[:, 6 * D_KV:].reshape(D_MODEL, N_KV, 3 * GROUP),
                    ((0, 0), (0, 0), (0, LANES - 3 * GROUP))).reshape(D_MODEL, N_KV * LANES)
    tail = jnp.concatenate([pair(0, 1), pair(2, 4), pair(3, 5), gates], axis=1)
    n_out = 4 * D_CONV + N_PROJ
    assert w.shape == (D_MODEL, N_IN) and W_KV + N_IN - W_ZB + tail.shape[1] == n_out
    return pl.pallas_call(
        _wprep_kernel,
        grid=(D_MODEL // rows,),
        in_specs=[pl.BlockSpec((rows, N_IN), lambda i: (i, 0)),
                  pl.BlockSpec((rows, tail.shape[1]), lambda i: (i, 0))],
        out_specs=pl.BlockSpec((rows, n_out), lambda i: (i, 0)),
        out_shape=jax.ShapeDtypeStruct((D_MODEL, n_out), bf16),
        compiler_params=pltpu.CompilerParams(dimension_semantics=("parallel",), vmem_limit_bytes=VMEM_LIMIT),
        name="wprep",
    )(w, tail)


def _sigmoid(x):
    return 1.0 / (1.0 + jnp.exp(-x))


def _dot_nt(a, b):
    return lax.dot_general(a, b, (((1,), (1,)), ((), ())), preferred_element_type=f32)


CARRY_ROWS = 8
ROW_BLOCK = 256


def _column_activations(lo, hi):
    silu = lambda a: a * _sigmoid(a)
    keep = lambda a: a
    segments = ((OFF_Q, OFF_ZB, keep), (OFF_ZB, OFF_GMIX, silu), (OFF_GMIX, OFF_KVC, _sigmoid),
                (OFF_KVC, OFF_GATE, keep), (OFF_GATE, N_PROJ, _sigmoid))
    return [(max(a, lo) - lo, min(b, hi) - lo, f) for a, b, f in segments if max(a, lo) < min(b, hi)]


def _inproj_kernel(x_ref, nw_ref, w1_ref, w2_ref, cw_ref, cb_ref, o_ref, ya_ref, h_ref, carry_ref,
                   *, tiles_per_seq):
    i = pl.program_id(0)
    j = pl.program_id(1)
    tm = x_ref.shape[0]
    tn = o_ref.shape[1]
    n_stored = N_PROJ // tn
    blocks = [slice(k * ROW_BLOCK, (k + 1) * ROW_BLOCK) for k in range(tm // ROW_BLOCK)]
    product = lambda rows, w: jnp.dot(h_ref[rows, :], w, preferred_element_type=f32)

    @pl.when((i == 0) & (j == 0))
    def _():
        carry_ref[...] = jnp.zeros_like(carry_ref)

    @pl.when(j == 0)
    def _():
        x = x_ref[...]
        r = lax.rsqrt(jnp.mean(x * x, axis=-1, keepdims=True) + NORM_EPS)
        h_ref[...] = ((x * r) * nw_ref[...]).astype(bf16)

    for t in range(n_stored):
        @pl.when(j == t)
        def _(t=t):
            for rows in blocks:
                acc = product(rows, w1_ref[...])
                for lo, hi, act in _column_activations(t * tn, (t + 1) * tn):
                    o_ref[rows, lo:hi] = act(acc[:, lo:hi]).astype(bf16)

    @pl.when(j == n_stored)
    def _():
        lead = carry_ref[...] * (i % tiles_per_seq != 0).astype(f32)
        cw = cw_ref[...]
        for rows in blocks:
            u = product(rows, w2_ref[:, :D_CONV]) * product(rows, w1_ref[:, :D_CONV])
            win = jnp.concatenate([lead, u], axis=0)
            u1 = pltpu.roll(win, 1, 0)[CARRY_ROWS:, :]
            u2 = pltpu.roll(win, 2, 0)[CARRY_ROWS:, :]
            y = cb_ref[...] + cw[0:1, :] * u2 + cw[1:2, :] * u1 + cw[2:3, :] * u
            z = product(rows, w2_ref[:, D_CONV:])
            ya_ref[rows, :] = (product(rows, w1_ref[:, D_CONV:]) * y * (z * _sigmoid(z))).astype(bf16)
            lead = u[ROW_BLOCK - CARRY_ROWS:, :]
        carry_ref[...] = lead


def _inproj(x2, norm_w, w_p, conv_w, conv_b, seq, tm=1024):
    n_rows = x2.shape[0]
    tn = 2 * D_CONV
    assert N_PROJ % tn == 0 and seq % tm == 0 and tm % ROW_BLOCK == 0 and w_p.shape[1] == 2 * tn + N_PROJ
    n_stored = N_PROJ // tn
    first = lambda i, j: (0, jnp.where(j < n_stored, j + 2, 0))
    second = lambda i, j: (0, 1)
    return pl.pallas_call(
        functools.partial(_inproj_kernel, tiles_per_seq=seq // tm),
        grid=(n_rows // tm, n_stored + 1),
        in_specs=[
            pl.BlockSpec((tm, D_MODEL), lambda i, j: (i, 0)),
            pl.BlockSpec((1, D_MODEL), lambda i, j: (0, 0)),
            pl.BlockSpec((D_MODEL, tn), first),
            pl.BlockSpec((D_MODEL, tn), second),
            pl.BlockSpec(conv_w.shape, lambda i, j: (0, 0)),
            pl.BlockSpec(conv_b.shape, lambda i, j: (0, 0)),
        ],
        out_specs=[
            pl.BlockSpec((tm, tn), lambda i, j: (i, jnp.minimum(j, n_stored - 1))),
            pl.BlockSpec((tm, D_CONV), lambda i, j: (i, 0)),
        ],
        out_shape=[jax.ShapeDtypeStruct((n_rows, N_PROJ), bf16), jax.ShapeDtypeStruct((n_rows, D_CONV), bf16)],
        scratch_shapes=[
            pltpu.VMEM((tm, D_MODEL), bf16),
            pltpu.VMEM((CARRY_ROWS, D_CONV), f32),
        ],
        compiler_params=pltpu.CompilerParams(
            dimension_semantics=("arbitrary", "arbitrary"), vmem_limit_bytes=VMEM_LIMIT),
        name="inproj",
    )(x2, norm_w, w_p, w_p, conv_w, conv_b)


def _softmax_step(s, v, carry):
    m, acc = carry
    m_new = jnp.maximum(m, jnp.max(s, axis=-1, keepdims=True))
    alpha = jnp.exp2(m - m_new)
    p = jnp.exp2(s - m_new)
    acc = alpha * acc + jnp.dot(p.astype(bf16), v, preferred_element_type=f32)
    return m_new, acc


def _stack_heads(q_tile, aug_rows, extra):
    lane = lax.broadcasted_iota(jnp.int32, (1, LANES), 1)
    pairs = (q_tile[:, :LANES], q_tile[:, LANES:])
    parts = []
    for r in range(GROUP):
        qh = pairs[r // 2]
        if r % 2 == 1:
            qh = pltpu.roll(qh, HEAD_DIM, 1)
        aug = aug_rows[r:r + 1, :]
        if extra is not None:
            aug = aug + extra
        parts.append(jnp.where(lane < HEAD_DIM, qh, aug).astype(bf16))
    return jnp.concatenate(parts, axis=0)


def _merge_heads(vals, lane):
    return (jnp.where(lane < HEAD_DIM, vals[0], vals[1]), jnp.where(lane < HEAD_DIM, vals[2], vals[3]))


def _nsa_kernel(q_ref, ksw_ref, vsw_ref, gate_ref, kvc_ref, wlo_ref, whi_ref, pelo_ref, pehi_ref, w2_ref,
                kaug_s_ref, kaug_w_ref, kaug_c_ref, qaug_ref, tri_ref, band_ref, ovl_ref, cend_ref,
                o_ref,
                kvcf_ref, ks_ref, kw_ref, vs_ref, vw_ref, ocw_ref, imp_ref, ns_ref, selq_ref):
    seq = q_ref.shape[0]
    lane = lax.broadcasted_iota(jnp.int32, (1, LANES), 1)
    low = lane < HEAD_DIM

    kvcf_ref[...] = kvc_ref[...].astype(f32)
    xcat = jnp.concatenate([kvcf_ref[pl.ds(l, N_CMP_PAD, stride=CMP_STRIDE), :] for l in range(CMP_STRIDE)],
                           axis=1)
    xa = (xcat + pelo_ref[...]).astype(bf16)
    xb = (xcat + pehi_ref[...]).astype(bf16)
    a = jnp.dot(xa, wlo_ref[...], preferred_element_type=f32)
    bm = jnp.dot(xb, whi_ref[...], preferred_element_type=f32)
    h = a + pltpu.roll(bm, N_CMP_PAD - 1, 0)
    hid = h * _sigmoid(h)
    kcvc = jnp.dot(hid.astype(bf16), w2_ref[...], preferred_element_type=f32)

    ksw = ksw_ref[...]
    vsw = vsw_ref[...]
    one = jnp.ones((1, LANES), bf16)
    ks_ref[...] = jnp.where(low, ksw, kaug_s_ref[...])
    kwin = pltpu.roll(ksw.astype(f32), HEAD_DIM, 1).astype(bf16)
    kw_ref[0:WIN_PAD, :] = kaug_w_ref[0:WIN_PAD, :]
    kw_ref[WIN_PAD:, :] = jnp.where(low, kwin, kaug_w_ref[WIN_PAD:, :])
    vs_ref[...] = jnp.where(low, vsw, one)
    vw_ref[0:WIN_PAD, :] = jnp.zeros((WIN_PAD, LANES), bf16)
    vw_ref[WIN_PAD:, :] = jnp.where(low, one, vsw)

    qaug = qaug_ref[0]
    kc_aug = jnp.where(low, kcvc, kaug_c_ref[...]).astype(bf16)
    vc_rows = jnp.concatenate([jnp.zeros((N_CMP_PAD, LANES), bf16), kcvc.astype(bf16)], axis=1)
    zeros_w = jnp.zeros((WIN_KEYS, LANES), bf16)
    ovl = ovl_ref[...]

    def cw_body(i, _):
        t0 = pl.multiple_of(i * CW_TQ, CW_TQ)
        sig = gate_ref[pl.ds(t0, CW_TQ), :].astype(f32)
        o_w, o_c = [], []
        n_tiles = CW_TQ // WIN_TQ

        def tile_scores(h):
            tw = pl.multiple_of(t0 + h * WIN_TQ, WIN_TQ)
            qw = _stack_heads(q_ref[pl.ds(tw, WIN_TQ), :].astype(f32), qaug[GROUP:2 * GROUP], None)
            k_all = jnp.concatenate([kw_ref[pl.ds(tw, WIN_KEYS), :], kc_aug], axis=0)
            return _dot_nt(qw, k_all)

        s_next = tile_scores(0)
        for h in range(n_tiles):
            tw = pl.multiple_of(t0 + h * WIN_TQ, WIN_TQ)
            s = s_next
            if h + 1 < n_tiles:
                s_next = tile_scores(h + 1)

            sw = jnp.concatenate([s[:, :WIN_TQ] + band_ref[:, :WIN_TQ], s[:, WIN_TQ:WINDOW],
                                  s[:, WINDOW:WIN_KEYS] + band_ref[:, WINDOW:]], axis=1)
            p_w = jnp.exp2(sw - jnp.max(sw, axis=-1, keepdims=True))

            ok = cend_ref[...] <= tw
            sm = jnp.where(ok, s[:, WIN_KEYS:], -jnp.inf)
            m = jnp.max(sm, axis=-1, keepdims=True)
            m = jnp.where(m > -jnp.inf, m, 0.0)
            e = jnp.where(ok, jnp.exp2(sm - m), 0.0)
            p_c = e / jnp.maximum(jnp.sum(e, axis=-1, keepdims=True), 1e-30)

            p_all = jnp.concatenate([p_w.astype(bf16), p_c.astype(bf16)], axis=1)
            v_all = jnp.concatenate(
                [jnp.concatenate([vw_ref[pl.ds(tw, WIN_KEYS), :], zeros_w], axis=1), vc_rows], axis=0)
            acc = jnp.dot(p_all, v_all, preferred_element_type=f32)
            acc_w = acc[:, :LANES]
            o_w.append(acc_w / jnp.maximum(acc_w[:, 0:1], 1e-30))
            o_c.append(acc[:, LANES:])

            ps = p_c[0:WIN_TQ] + p_c[WIN_TQ:2 * WIN_TQ] + p_c[2 * WIN_TQ:3 * WIN_TQ] + p_c[3 * WIN_TQ:4 * WIN_TQ]
            p1 = ps.astype(bf16)
            r1 = ps - p1.astype(f32)
            p2 = r1.astype(bf16)
            p3 = (r1 - p2.astype(f32)).astype(bf16)
            imp_ref[i * (CW_TQ // WIN_TQ) + h] = _dot_nt(ovl, p1) + _dot_nt(ovl, p2) + _dot_nt(ovl, p3)

        vals = []
        for r in range(GROUP):
            head = lambda parts: jnp.concatenate([o[r * WIN_TQ:(r + 1) * WIN_TQ, :] for o in parts], axis=0)
            a = head(o_c) * sig[:, 3 * r:3 * r + 1] + head(o_w) * sig[:, 3 * r + 2:3 * r + 3]
            vals.append(pltpu.roll(a, HEAD_DIM, 1) if r % 2 == 0 else a)
        c0, c1 = _merge_heads(vals, lane)
        ocw_ref[pl.ds(t0, CW_TQ), 0:LANES] = c0
        ocw_ref[pl.ds(t0, CW_TQ), LANES:2 * LANES] = c1
        return 0

    lax.fori_loop(0, seq // CW_TQ, cw_body, 0)

    n_slc = seq // SLC_LEN
    n_tt = seq // LANES
    shp = (n_tt, LANES)
    cur = (lax.broadcasted_iota(jnp.int32, shp, 0) * LANES + lax.broadcasted_iota(jnp.int32, shp, 1)) // SLC_LEN
    vals = []
    for j in range(n_slc):
        forced = (cur == j) | (cur == j + 1)
        vj = jnp.where(cur >= j, imp_ref[:, j, :], -jnp.inf)
        vals.append(jnp.full(shp, jnp.inf, f32) if j == 0 else jnp.where(forced, jnp.inf, vj))
    wins = [jnp.zeros(shp, f32) for _ in range(n_slc)]
    losses = [jnp.zeros(shp, f32) for _ in range(n_slc)]
    for a in range(n_slc):
        for b in range(a + 1, n_slc):
            a_first = jnp.where(vals[a] >= vals[b], 1.0, 0.0)
            wins[a] = wins[a] + a_first
            losses[b] = losses[b] + a_first
    for j in range(n_slc):
        rank = losses[j] + (float(n_slc - 1 - j) - wins[j])
        sel = (rank < float(N_SEL)) & (vals[j] > -jnp.inf)
        ns_ref[j] = jnp.where(sel, 0.0, 1.0)
    eye = (lax.broadcasted_iota(jnp.int32, (LANES, LANES), 0)
           == lax.broadcasted_iota(jnp.int32, (LANES, LANES), 1)).astype(bf16)
    zpad = jnp.zeros((L_SEL, LANES), f32)
    for tt in range(n_tt):
        z = jnp.concatenate([zpad, ns_ref[:, tt, :]], axis=0).astype(bf16)
        selq_ref[tt * LANES:(tt + 1) * LANES, :] = _dot_nt(eye, z) * (-MASK_BIG)

    half = SEL_CHUNK // 2
    half_rows = GROUP * half

    def stacked_q(i):
        parts = []
        for h in range(2):
            tok = slice(i * SEL_CHUNK + h * half, i * SEL_CHUNK + (h + 1) * half)
            parts.append(_stack_heads(q_ref[tok, :].astype(f32), qaug[0:GROUP], selq_ref[tok, :]))
        return parts

    def causal(s):
        tri = tri_ref[...]
        return jnp.concatenate([s[r * half:(r + 1) * half, :] + tri for r in range(GROUP)], axis=0)

    def scores(qs, i, c):
        k0 = c * SEL_CHUNK
        if c < i:
            return (_dot_nt(jnp.concatenate(qs, axis=0), ks_ref[k0:k0 + SEL_CHUNK, :]),)
        s_a = causal(_dot_nt(qs[0], ks_ref[k0:k0 + half, :]))
        s_b = _dot_nt(qs[1], ks_ref[k0:k0 + SEL_CHUNK, :])
        return (s_a, jnp.concatenate([s_b[:, :half], causal(s_b[:, half:])], axis=1))

    def softmax_step(s, v, carry):
        if carry is None:
            m = jnp.max(s, axis=-1, keepdims=True)
            return m, jnp.dot(jnp.exp2(s - m).astype(bf16), v, preferred_element_type=f32)
        return _softmax_step(s, v, carry)

    def finish(i, acc):
        tok = slice(i * SEL_CHUNK, (i + 1) * SEL_CHUNK)
        o_s = acc / jnp.maximum(acc[:, HEAD_DIM:HEAD_DIM + 1], 1e-30)
        sig = gate_ref[tok, :].astype(f32)
        vals = []
        for r in range(GROUP):
            a_s = jnp.concatenate([o_s[h * half_rows + r * half:h * half_rows + (r + 1) * half, :]
                                   for h in range(2)], axis=0) * sig[:, 3 * r + 1:3 * r + 2]
            vals.append(a_s if r % 2 == 0 else pltpu.roll(a_s, HEAD_DIM, 1))
        c0, c1 = _merge_heads(vals, lane)
        o_ref[tok, 0:LANES] = (ocw_ref[tok, 0:LANES] + c0).astype(bf16)
        o_ref[tok, LANES:2 * LANES] = (ocw_ref[tok, LANES:2 * LANES] + c1).astype(bf16)

    steps = [(i, c) for i in range(seq // SEL_CHUNK) for c in range(i + 1)]
    qs = stacked_q(0)
    s_next = scores(qs, 0, 0)
    carry = None
    for k, (i, c) in enumerate(steps):
        s = s_next
        if k + 1 < len(steps):
            i2, c2 = steps[k + 1]
            if i2 != i:
                qs = stacked_q(i2)
            s_next = scores(qs, i2, c2)
        k0 = c * SEL_CHUNK
        if c < i:
            carry = softmax_step(s[0], vs_ref[k0:k0 + SEL_CHUNK, :], carry)
        else:
            rows_a = None if carry is None else tuple(a[:half_rows] for a in carry)
            rows_b = None if carry is None else tuple(a[half_rows:] for a in carry)
            _, acc_a = softmax_step(s[0], vs_ref[k0:k0 + half, :], rows_a)
            _, acc_b = softmax_step(s[1], vs_ref[k0:k0 + SEL_CHUNK, :], rows_b)
            finish(i, jnp.concatenate([acc_a, acc_b], axis=0))
            carry = None


def _position_lanes(pos):
    out = np.zeros((pos.shape[0], LANES), np.float32)
    out[:, L_HI:L_HI + 3] = ((pos // 64) * 64)[:, None]
    out[:, L_LO:L_LO + 3] = (pos % 64)[:, None]
    return out


def _nsa_tables(seq):
    pos = np.arange(seq)
    kaug_s = _position_lanes(pos)
    kaug_s[pos, L_SEL + pos // SLC_LEN] = 1.0
    kaug_w = np.zeros((seq + WIN_PAD, LANES), np.float32)
    kaug_w[WIN_PAD:] = _position_lanes(pos)
    kaug_w[:WIN_PAD, L_FLAG] = 1.0
    kaug_c = _position_lanes(np.arange(N_CMP_PAD) * CMP_STRIDE + CMP_LEN - 1)
    tl = np.arange(SEL_CHUNK // 2)[:, None]
    kk = np.arange(SEL_CHUNK // 2)[None, :]
    tri = np.where(kk <= tl, 0.0, -MASK_BIG).astype(np.float32)
    tl = np.arange(WIN_TQ)[:, None]
    kk = np.arange(WIN_KEYS)[None, :]
    dist = tl + WIN_PAD - kk
    band = np.tile(np.where((dist >= 0) & (dist < WINDOW), 0.0, -MASK_BIG).astype(np.float32), (GROUP, 1))
    c = np.arange(N_CMP_PAD)[None, :]
    j = np.arange(seq // SLC_LEN)[:, None]
    c_start, c_end, s_start = c * CMP_STRIDE, c * CMP_STRIDE + CMP_LEN - 1, j * SLC_LEN
    ovl = ((c_start <= s_start + SLC_LEN - 1) & (c_end >= s_start) & (c < seq // CMP_STRIDE - 1)).astype(np.float32)
    cend = np.tile((c * CMP_STRIDE + CMP_LEN - 1) - tl, (GROUP, 1)).astype(np.int32)
    return (jnp.asarray(kaug_s, bf16), jnp.asarray(kaug_w, bf16), jnp.asarray(kaug_c, f32),
            jnp.asarray(tri), jnp.asarray(band), jnp.asarray(ovl, bf16), jnp.asarray(cend))


def _query_aug():
    slopes = jnp.exp2(-8.0 * jnp.arange(1, N_HEADS + 1, dtype=f32) / N_HEADS).reshape(N_KV, GROUP) * LOG2E
    hi =slopes.astype(bf16).astype(f32)
    mid = (slopes - hi).astype(bf16).astype(f32)
    lo = (slopes - hi - mid).astype(bf16).astype(f32)
    pieces = jnp.stack([hi, mid, lo, hi, mid, lo], axis=-1)
    base = jnp.zeros((N_KV, GROUP, LANES), f32).at[:, :, L_HI:L_HI + 6].set(pieces)
    return jnp.concatenate([base, base.at[:, :, L_FLAG].set(-MASK_BIG)], axis=1)


def _nsa(proj, wlo, whi, pelo, pehi, w2bd, batch, seq):
    kaug_s, kaug_w, kaug_c, tri, band, ovl, cend = _nsa_tables(seq)
    qaug = _query_aug()
    const2 = lambda b, g: (0, 0)
    in_specs = [
        pl.BlockSpec((seq, 2 * LANES), lambda b, g: (b, OFF_Q // (2 * LANES) + g)),
        pl.BlockSpec((seq, LANES), lambda b, g: (b, OFF_KSW // LANES + g)),
        pl.BlockSpec((seq, LANES), lambda b, g: (b, OFF_VSW // LANES + g)),
        pl.BlockSpec((seq, LANES), lambda b, g: (b, OFF_GATE // LANES + g)),
        pl.BlockSpec((seq, LANES), lambda b, g: (b, OFF_KVC // LANES + g)),
        pl.BlockSpec(wlo.shape, const2),
        pl.BlockSpec(whi.shape, const2),
        pl.BlockSpec(pelo.shape, const2),
        pl.BlockSpec(pehi.shape, const2),
        pl.BlockSpec(w2bd.shape, const2),
        pl.BlockSpec(kaug_s.shape, const2),
        pl.BlockSpec(kaug_w.shape, const2),
        pl.BlockSpec(kaug_c.shape, const2),
        pl.BlockSpec((1, 2 * GROUP, LANES), lambda b, g: (g, 0, 0)),
        pl.BlockSpec(tri.shape, const2),
        pl.BlockSpec(band.shape, const2),
        pl.BlockSpec(ovl.shape, const2),
        pl.BlockSpec(cend.shape, const2),
    ]
    return pl.pallas_call(
        _nsa_kernel,
        grid=(batch, N_KV),
        in_specs=in_specs,
        out_specs=pl.BlockSpec((seq, 2 * LANES), lambda b, g: (b, g)),
        out_shape=jax.ShapeDtypeStruct((batch * seq, D_ATT), bf16),
        scratch_shapes=[
            pltpu.VMEM((seq, LANES), f32),
            pltpu.VMEM((seq, LANES), bf16),
            pltpu.VMEM((seq + WIN_PAD, LANES), bf16),
            pltpu.VMEM((seq, LANES), bf16),
            pltpu.VMEM((seq + WIN_PAD, LANES), bf16),
            pltpu.VMEM((seq, 2 * LANES), f32),
            pltpu.VMEM((seq // LANES, seq // SLC_LEN, LANES), f32),
            pltpu.VMEM((seq // SLC_LEN, seq // LANES, LANES), f32),
            pltpu.VMEM((seq, LANES), f32),
        ],
        compiler_params=pltpu.CompilerParams(
            dimension_semantics=("parallel", "parallel"), vmem_limit_bytes=VMEM_LIMIT),
        name="nsa",
    )(proj, proj, proj, proj, proj, wlo, whi, pelo, pehi, w2bd, kaug_s, kaug_w, kaug_c, qaug, tri, band, ovl, cend)


def _out_kernel(x_ref, ya_ref, zb_ref, gm_ref, ob_ref, wa_ref, wb_ref, wo_ref, fnw_ref, o_ref):
    y_a = jnp.dot(ya_ref[...], wa_ref[...], preferred_element_type=f32)
    yb_in = ob_ref[...].astype(f32) * zb_ref[...].astype(f32)
    y_b = jnp.dot(yb_in.astype(bf16), wb_ref[...], preferred_element_type=f32)
    mixed = (gm_ref[:, 0:D_MODEL].astype(f32) * y_a + gm_ref[:, D_MODEL:2 * D_MODEL].astype(f32) * y_b)
    xo = x_ref[...] + jnp.dot(mixed.astype(bf16), wo_ref[...], preferred_element_type=f32)
    r = lax.rsqrt(jnp.mean(xo * xo, axis=-1, keepdims=True) + NORM_EPS)
    o_ref[...] = (xo * r) * fnw_ref[...]


def _out(x2, ya, proj, ob, wa, wb, wo, fnw, tm=512):
    n_rows = x2.shape[0]
    blk = lambda c: pl.BlockSpec((tm, D_MODEL), lambda i, c=c: (i, c))
    full = lambda a: pl.BlockSpec(a.shape, lambda i: (0, 0))
    return pl.pallas_call(
        _out_kernel,
        grid=(n_rows // tm,),
        in_specs=[
            blk(0),
            blk(0),
            blk(OFF_ZB // D_MODEL),
            pl.BlockSpec((tm, 2 * D_MODEL), lambda i: (i, OFF_GMIX // (2 * D_MODEL))),
            blk(0),
            full(wa), full(wb), full(wo), full(fnw),
        ],
        out_specs=pl.BlockSpec((tm, D_MODEL), lambda i: (i, 0)),
        out_shape=jax.ShapeDtypeStruct((n_rows, D_MODEL), f32),
        compiler_params=pltpu.CompilerParams(
            dimension_semantics=("parallel",), vmem_limit_bytes=VMEM_LIMIT),
        name="merge_out",
    )(x2, ya, proj, proj, ob, wa, wb, wo, fnw)


def _blockdiag_w1(w1_k, w1_v, lo):
    half = CMP_STRIDE * HEAD_DIM
    wk = w1_k[lo * half:(lo + 1) * half].reshape(CMP_STRIDE, HEAD_DIM, CMP_HID)
    wv = w1_v[lo * half:(lo + 1) * half].reshape(CMP_STRIDE, HEAD_DIM, CMP_HID)
    z = jnp.zeros_like(wk)
    top = jnp.concatenate([wk, z], axis=-1)
    bot = jnp.concatenate([z, wv], axis=-1)
    return jnp.concatenate([top, bot], axis=1).reshape(CMP_STRIDE * 2 * HEAD_DIM, 2 * CMP_HID).astype(bf16)


def _pe_row(pe_k, pe_v, lo):
    sl = slice(lo * CMP_STRIDE, (lo + 1) * CMP_STRIDE)
    return jnp.concatenate([pe_k[sl], pe_v[sl]], axis=-1).reshape(1, CMP_STRIDE * 2 * HEAD_DIM)


def kernel(x, norm_w, w_in, conv_w, conv_b, cmp_pe_k, cmp_pe_v, cmp_w1_k, cmp_w2_k, cmp_w1_v, cmp_w2_v,
           w_proj_a, w_proj_b, w_out, final_norm_w):
    batch, seq, _ = x.shape
    assert norm_w.shape[0] == 1 and seq % SEL_CHUNK == 0 and seq // CMP_STRIDE == N_CMP_PAD
    w_p = _reordered_w_in(w_in[0])
    x2 = x.reshape(batch * seq, D_MODEL)
    proj, ya = _inproj(x2, norm_w, w_p, conv_w[0], conv_b, seq)

    wlo = _blockdiag_w1(cmp_w1_k[0], cmp_w1_v[0], 0)
    whi = _blockdiag_w1(cmp_w1_k[0], cmp_w1_v[0], 1)
    pelo = _pe_row(cmp_pe_k[0], cmp_pe_v[0], 0)
    pehi = _pe_row(cmp_pe_k[0], cmp_pe_v[0], 1)
    zk = jnp.zeros_like(cmp_w2_k[0])
    w2bd = jnp.concatenate([jnp.concatenate([cmp_w2_k[0], zk], axis=1),
                            jnp.concatenate([zk, cmp_w2_v[0]], axis=1)], axis=0).astype(bf16)
    ob = _nsa(proj, wlo, whi, pelo, pehi, w2bd, batch, seq)

    out = _out(x2, ya, proj, ob, w_proj_a[0].astype(bf16), w_proj_b[0].astype(bf16),
               w_out[0].astype(bf16), final_norm_w.reshape(1, D_MODEL))
    return out.reshape(batch, seq, D_MODEL)
```

```python
import functools

import numpy as np
import jax
import jax.numpy as jnp
from jax import lax
from jax.experimental import pallas as pl
from jax.experimental.pallas import tpu as pltpu

D_MODEL = 1024
D_CONV = 1024
CONV_W = 3
N_HEADS = 16
HEAD_DIM = 64
N_KV = 4
GROUP = N_HEADS // N_KV
D_ATT = N_HEADS * HEAD_DIM
D_KV = N_KV * HEAD_DIM
CMP_LEN = 32
CMP_STRIDE = 16
CMP_HID = 128
SLC_LEN = 64
N_SEL = 8
WINDOW = 512
NORM_EPS = 1e-6

LANES = 128
N_CMP_PAD = 128
MASK_BIG = float(2.0 ** 100)
LOG2E = float(np.log2(np.e))
SEL_CHUNK = 512
WIN_TQ = 128
WIN_KEYS = WINDOW + WIN_TQ
WIN_PAD = WINDOW
CW_TQ = 512

OFF_Q = 0
OFF_ZB = 1024
OFF_GMIX = 2048
OFF_KVC = 4096
OFF_KSW = 4608
OFF_VSW = 5120
OFF_GATE = 5632
N_PROJ = 6144

L_HI = 64
L_LO = 67
L_FLAG = 70
L_SEL = 96

VMEM_LIMIT = 56 * 1024 * 1024

f32 = jnp.float32
bf16 = jnp.bfloat16


W_Q = 4 * D_CONV
W_KV = W_Q + D_ATT
W_GATE = W_KV + 6 * D_KV
W_ZB = W_GATE + 3 * N_HEADS
N_IN = W_ZB + D_ATT + 2 * D_MODEL


def _wprep_kernel(w_ref, tail_ref, o_ref):
    o_ref[:, :W_Q] = w_ref[:, :W_Q].astype(bf16)
    o_ref[:, W_Q:W_KV] = (w_ref[:, W_Q:W_KV] * (HEAD_DIM ** -0.5 * LOG2E)).astype(bf16)
    o_ref[:, W_KV:W_KV + N_IN - W_ZB] = w_ref[:, W_ZB:].astype(bf16)
    kv_off = W_KV + N_IN - W_ZB
    o_ref[:, kv_off:kv_off + tail_ref.shape[1]] = tail_ref[...]
    gate_off = kv_off + tail_ref.shape[1]
    gates = w_ref[:, W_GATE:W_GATE + LANES]
    lane = lax.broadcasted_iota(jnp.int32, (1, LANES), 1)
    for g in range(N_KV):
        shifted = gates if g == 0 else pltpu.roll(gates, LANES - 3 * GROUP * g, 1)
        o_ref[:, gate_off + g * LANES:gate_off + (g + 1) * LANES] = (
            jnp.where(lane < 3 * GROUP, shifted, 0.0).astype(bf16))


def _reordered_w_in(w, rows=128):
    kv = w[:, W_KV:W_GATE].astype(bf16).reshape(D_MODEL, 6, N_KV, HEAD_DIM)
    pair = lambda a, b: jnp.stack([kv[:, a], kv[:, b]], axis=2).reshape(D_MODEL, N_KV * LANES)
    tail = jnp.concatenate([pair(0, 1), pair(2, 4), pair(3, 5)], axis=1)
    n_out = 4 * D_CONV + N_PROJ
    assert w.shape == (D_MODEL, N_IN) and W_KV + N_IN - W_ZB + tail.shape[1] + N_KV * LANES == n_out
    return pl.pallas_call(
        _wprep_kernel,
        grid=(D_MODEL // rows,),
        in_specs=[pl.BlockSpec((rows, N_IN), lambda i: (i, 0)),
                  pl.BlockSpec((rows, tail.shape[1]), lambda i: (i, 0))],
        out_specs=pl.BlockSpec((rows, n_out), lambda i: (i, 0)),
        out_shape=jax.ShapeDtypeStruct((D_MODEL, n_out), bf16),
        compiler_params=pltpu.CompilerParams(dimension_semantics=("parallel",), vmem_limit_bytes=VMEM_LIMIT),
        name="wprep",
    )(w, tail)


def _sigmoid(x):
    return 1.0 / (1.0 + jnp.exp(-x))


def _dot_nt(a, b):
    return lax.dot_general(a, b, (((1,), (1,)), ((), ())), preferred_element_type=f32)


CARRY_ROWS = 8
ROW_BLOCK = 256


def _column_activations(lo, hi):
    silu = lambda a: a * _sigmoid(a)
    keep = lambda a: a
    segments = ((OFF_Q, OFF_ZB, keep), (OFF_ZB, OFF_GMIX, silu), (OFF_GMIX, OFF_KVC, _sigmoid),
                (OFF_KVC, OFF_GATE, keep), (OFF_GATE, N_PROJ, _sigmoid))
    return [(max(a, lo) - lo, min(b, hi) - lo, f) for a, b, f in segments if max(a, lo) < min(b, hi)]


def _inproj_kernel(x_ref, nw_ref, w1_ref, w2_ref, cw_ref, cb_ref, o_ref, ya_ref, h_ref, carry_ref,
                   *, tiles_per_seq):
    i = pl.program_id(0)
    j = pl.program_id(1)
    tm = x_ref.shape[0]
    tn = o_ref.shape[1]
    n_stored = N_PROJ // tn
    blocks = [slice(k * ROW_BLOCK, (k + 1) * ROW_BLOCK) for k in range(tm // ROW_BLOCK)]
    product = lambda rows, w: jnp.dot(h_ref[rows, :], w, preferred_element_type=f32)

    @pl.when((i == 0) & (j == 0))
    def _():
        carry_ref[...] = jnp.zeros_like(carry_ref)

    @pl.when(j == 0)
    def _():
        x = x_ref[...]
        r = lax.rsqrt(jnp.mean(x * x, axis=-1, keepdims=True) + NORM_EPS)
        h_ref[...] = ((x * r) * nw_ref[...]).astype(bf16)

    for t in range(n_stored):
        @pl.when(j == t)
        def _(t=t):
            for rows in blocks:
                acc = product(rows, w1_ref[...])
                for lo, hi, act in _column_activations(t * tn, (t + 1) * tn):
                    o_ref[rows, lo:hi] = act(acc[:, lo:hi]).astype(bf16)

    @pl.when(j == n_stored)
    def _():
        lead = carry_ref[...] * (i % tiles_per_seq != 0).astype(f32)
        cw = cw_ref[...]
        for rows in blocks:
            u = product(rows, w2_ref[:, :D_CONV]) * product(rows, w1_ref[:, :D_CONV])
            win = jnp.concatenate([lead, u], axis=0)
            u1 = pltpu.roll(win, 1, 0)[CARRY_ROWS:, :]
            u2 = pltpu.roll(win, 2, 0)[CARRY_ROWS:, :]
            y = cb_ref[...] + cw[0:1, :] * u2 + cw[1:2, :] * u1 + cw[2:3, :] * u
            z = product(rows, w2_ref[:, D_CONV:])
            ya_ref[rows, :] = (product(rows, w1_ref[:, D_CONV:]) * y * (z * _sigmoid(z))).astype(bf16)
            lead = u[ROW_BLOCK - CARRY_ROWS:, :]
        carry_ref[...] = lead


def _inproj(x2, norm_w, w_p, conv_w, conv_b, seq, tm=1024):
    n_rows = x2.shape[0]
    tn = 2 * D_CONV
    assert N_PROJ % tn == 0 and seq % tm == 0 and tm % ROW_BLOCK == 0 and w_p.shape[1] == 2 * tn + N_PROJ
    n_stored = N_PROJ // tn
    first = lambda i, j: (0, jnp.where(j < n_stored, j + 2, 0))
    second = lambda i, j: (0, 1)
    return pl.pallas_call(
        functools.partial(_inproj_kernel, tiles_per_seq=seq // tm),
        grid=(n_rows // tm, n_stored + 1),
        in_specs=[
            pl.BlockSpec((tm, D_MODEL), lambda i, j: (i, 0)),
            pl.BlockSpec((1, D_MODEL), lambda i, j: (0, 0)),
            pl.BlockSpec((D_MODEL, tn), first),
            pl.BlockSpec((D_MODEL, tn), second),
            pl.BlockSpec(conv_w.shape, lambda i, j: (0, 0)),
            pl.BlockSpec(conv_b.shape, lambda i, j: (0, 0)),
        ],
        out_specs=[
            pl.BlockSpec((tm, tn), lambda i, j: (i, jnp.minimum(j, n_stored - 1))),
            pl.BlockSpec((tm, D_CONV), lambda i, j: (i, 0)),
        ],
        out_shape=[jax.ShapeDtypeStruct((n_rows, N_PROJ), bf16), jax.ShapeDtypeStruct((n_rows, D_CONV), bf16)],
        scratch_shapes=[
            pltpu.VMEM((tm, D_MODEL), bf16),
            pltpu.VMEM((CARRY_ROWS, D_CONV), f32),
        ],
        compiler_params=pltpu.CompilerParams(
            dimension_semantics=("arbitrary", "arbitrary"), vmem_limit_bytes=VMEM_LIMIT),
        name="inproj",
    )(x2, norm_w, w_p, w_p, conv_w, conv_b)


def _softmax_step(s, v, carry):
    m, acc = carry
    m_new = jnp.maximum(m, jnp.max(s, axis=-1, keepdims=True))
    alpha = jnp.exp2(m - m_new)
    p = jnp.exp2(s - m_new)
    acc = alpha * acc + jnp.dot(p.astype(bf16), v, preferred_element_type=f32)
    return m_new, acc


def _stack_heads(q_tile, aug_rows, extra):
    lane = lax.broadcasted_iota(jnp.int32, (1, LANES), 1)
    pairs = (q_tile[:, :LANES], q_tile[:, LANES:])
    parts = []
    for r in range(GROUP):
        qh = pairs[r // 2]
        if r % 2 == 1:
            qh = pltpu.roll(qh, HEAD_DIM, 1)
        aug = aug_rows[r:r + 1, :]
        if extra is not None:
            aug = aug + extra
        parts.append(jnp.where(lane < HEAD_DIM, qh, aug).astype(bf16))
    return jnp.concatenate(parts, axis=0)


def _merge_heads(vals, lane):
    return (jnp.where(lane < HEAD_DIM, vals[0], vals[1]), jnp.where(lane < HEAD_DIM, vals[2], vals[3]))


def _nsa_kernel(q_ref, ksw_ref, vsw_ref, gate_ref, kvc_ref, wlo_ref, whi_ref, pelo_ref, pehi_ref, w2_ref,
                kaug_s_ref, kaug_w_ref, kaug_c_ref, qaug_ref, tri_ref, band_ref, ovl_ref, cend_ref,
                o_ref,
                kvcf_ref, ks_ref, kw_ref, vs_ref, vw_ref, ocw_ref, imp_ref, ns_ref, selq_ref):
    seq = q_ref.shape[0]
    lane = lax.broadcasted_iota(jnp.int32, (1, LANES), 1)
    low = lane < HEAD_DIM

    kvcf_ref[...] = kvc_ref[...].astype(f32)
    xcat = jnp.concatenate([kvcf_ref[pl.ds(l, N_CMP_PAD, stride=CMP_STRIDE), :] for l in range(CMP_STRIDE)],
                           axis=1)
    xa = (xcat + pelo_ref[...]).astype(bf16)
    xb = (xcat + pehi_ref[...]).astype(bf16)
    a = jnp.dot(xa, wlo_ref[...], preferred_element_type=f32)
    bm = jnp.dot(xb, whi_ref[...], preferred_element_type=f32)
    h = a + pltpu.roll(bm, N_CMP_PAD - 1, 0)
    hid = h * _sigmoid(h)
    kcvc = jnp.dot(hid.astype(bf16), w2_ref[...], preferred_element_type=f32)

    ksw = ksw_ref[...]
    vsw = vsw_ref[...]
    one = jnp.ones((1, LANES), bf16)
    ks_ref[...] = jnp.where(low, ksw, kaug_s_ref[...])
    kwin = pltpu.roll(ksw.astype(f32), HEAD_DIM, 1).astype(bf16)
    kw_ref[0:WIN_PAD, :] = kaug_w_ref[0:WIN_PAD, :]
    kw_ref[WIN_PAD:, :] = jnp.where(low, kwin, kaug_w_ref[WIN_PAD:, :])
    vs_ref[...] = jnp.where(low, vsw, one)
    vw_ref[0:WIN_PAD, :] = jnp.zeros((WIN_PAD, LANES), bf16)
    vw_ref[WIN_PAD:, :] = jnp.where(low, one, vsw)

    qaug = qaug_ref[0]
    kc_aug = jnp.where(low, kcvc, kaug_c_ref[...]).astype(bf16)
    vc_rows = jnp.concatenate([jnp.zeros((N_CMP_PAD, LANES), bf16), kcvc.astype(bf16)], axis=1)
    zeros_w = jnp.zeros((WIN_KEYS, LANES), bf16)
    ovl = ovl_ref[...]

    def cw_body(i, _):
        t0 = pl.multiple_of(i * CW_TQ, CW_TQ)
        sig = gate_ref[pl.ds(t0, CW_TQ), :].astype(f32)
        o_w, o_c = [], []
        n_tiles = CW_TQ // WIN_TQ

        def tile_scores(h):
            tw = pl.multiple_of(t0 + h * WIN_TQ, WIN_TQ)
            qw = _stack_heads(q_ref[pl.ds(tw, WIN_TQ), :].astype(f32), qaug[GROUP:2 * GROUP], None)
            k_all = jnp.concatenate([kw_ref[pl.ds(tw, WIN_KEYS), :], kc_aug], axis=0)
            return _dot_nt(qw, k_all)

        s_next = tile_scores(0)
        for h in range(n_tiles):
            tw = pl.multiple_of(t0 + h * WIN_TQ, WIN_TQ)
            s = s_next
            if h + 1 < n_tiles:
                s_next = tile_scores(h + 1)

            sw = jnp.concatenate([s[:, :WIN_TQ] + band_ref[:, :WIN_TQ], s[:, WIN_TQ:WINDOW],
                                  s[:, WINDOW:WIN_KEYS] + band_ref[:, WINDOW:]], axis=1)
            p_w = jnp.exp2(sw - jnp.max(sw, axis=-1, keepdims=True))

            ok = cend_ref[...] <= tw
            sm = jnp.where(ok, s[:, WIN_KEYS:], -jnp.inf)
            m = jnp.max(sm, axis=-1, keepdims=True)
            m = jnp.where(m > -jnp.inf, m, 0.0)
            e = jnp.where(ok, jnp.exp2(sm - m), 0.0)
            p_c = e / jnp.maximum(jnp.sum(e, axis=-1, keepdims=True), 1e-30)

            p_all = jnp.concatenate([p_w.astype(bf16), p_c.astype(bf16)], axis=1)
            v_all = jnp.concatenate(
                [jnp.concatenate([vw_ref[pl.ds(tw, WIN_KEYS), :], zeros_w], axis=1), vc_rows], axis=0)
            acc = jnp.dot(p_all, v_all, preferred_element_type=f32)
            acc_w = acc[:, :LANES]
            o_w.append(acc_w / jnp.maximum(acc_w[:, 0:1], 1e-30))
            o_c.append(acc[:, LANES:])

            ps = p_c[0:WIN_TQ] + p_c[WIN_TQ:2 * WIN_TQ] + p_c[2 * WIN_TQ:3 * WIN_TQ] + p_c[3 * WIN_TQ:4 * WIN_TQ]
            p1 = ps.astype(bf16)
            r1 = ps - p1.astype(f32)
            p2 = r1.astype(bf16)
            p3 = (r1 - p2.astype(f32)).astype(bf16)
            imp_ref[i * (CW_TQ // WIN_TQ) + h] = _dot_nt(ovl, p1) + _dot_nt(ovl, p2) + _dot_nt(ovl, p3)

        vals = []
        for r in range(GROUP):
            head = lambda parts: jnp.concatenate([o[r * WIN_TQ:(r + 1) * WIN_TQ, :] for o in parts], axis=0)
            a = head(o_c) * sig[:, 3 * r:3 * r + 1] + head(o_w) * sig[:, 3 * r + 2:3 * r + 3]
            vals.append(pltpu.roll(a, HEAD_DIM, 1) if r % 2 == 0 else a)
        c0, c1 = _merge_heads(vals, lane)
        ocw_ref[pl.ds(t0, CW_TQ), 0:LANES] = c0
        ocw_ref[pl.ds(t0, CW_TQ), LANES:2 * LANES] = c1
        return 0

    lax.fori_loop(0, seq // CW_TQ, cw_body, 0)

    n_slc = seq // SLC_LEN
    n_tt = seq // LANES
    shp = (n_tt, LANES)
    cur = (lax.broadcasted_iota(jnp.int32, shp, 0) * LANES + lax.broadcasted_iota(jnp.int32, shp, 1)) // SLC_LEN
    vals = []
    for j in range(n_slc):
        forced = (cur == j) | (cur == j + 1)
        vj = jnp.where(cur >= j, imp_ref[:, j, :], -jnp.inf)
        vals.append(jnp.full(shp, jnp.inf, f32) if j == 0 else jnp.where(forced, jnp.inf, vj))
    wins = [jnp.zeros(shp, f32) for _ in range(n_slc)]
    losses = [jnp.zeros(shp, f32) for _ in range(n_slc)]
    for a in range(n_slc):
        for b in range(a + 1, n_slc):
            a_first = jnp.where(vals[a] >= vals[b], 1.0, 0.0)
            wins[a] = wins[a] + a_first
            losses[b] = losses[b] + a_first
    for j in range(n_slc):
        rank = losses[j] + (float(n_slc - 1 - j) - wins[j])
        sel = (rank < float(N_SEL)) & (vals[j] > -jnp.inf)
        ns_ref[j] = jnp.where(sel, 0.0, 1.0)
    eye = (lax.broadcasted_iota(jnp.int32, (LANES, LANES), 0)
           == lax.broadcasted_iota(jnp.int32, (LANES, LANES), 1)).astype(bf16)
    zpad = jnp.zeros((L_SEL, LANES), f32)
    for tt in range(n_tt):
        z = jnp.concatenate([zpad, ns_ref[:, tt, :]], axis=0).astype(bf16)
        selq_ref[tt * LANES:(tt + 1) * LANES, :] = _dot_nt(eye, z) * (-MASK_BIG)

    half = SEL_CHUNK // 2
    half_rows = GROUP * half

    def stacked_q(i):
        parts = []
        for h in range(2):
            tok = slice(i * SEL_CHUNK + h * half, i * SEL_CHUNK + (h + 1) * half)
            parts.append(_stack_heads(q_ref[tok, :].astype(f32), qaug[0:GROUP], selq_ref[tok, :]))
        return parts

    def causal(s):
        tri = tri_ref[...]
        return jnp.concatenate([s[r * half:(r + 1) * half, :] + tri for r in range(GROUP)], axis=0)

    def scores(qs, i, c):
        k0 = c * SEL_CHUNK
        if c < i:
            return (_dot_nt(jnp.concatenate(qs, axis=0), ks_ref[k0:k0 + SEL_CHUNK, :]),)
        s_a = causal(_dot_nt(qs[0], ks_ref[k0:k0 + half, :]))
        s_b = _dot_nt(qs[1], ks_ref[k0:k0 + SEL_CHUNK, :])
        return (s_a, jnp.concatenate([s_b[:, :half], causal(s_b[:, half:])], axis=1))

    def softmax_step(s, v, carry):
        if carry is None:
            m = jnp.max(s, axis=-1, keepdims=True)
            return m, jnp.dot(jnp.exp2(s - m).astype(bf16), v, preferred_element_type=f32)
        return _softmax_step(s, v, carry)

    def finish(i, acc):
        tok = slice(i * SEL_CHUNK, (i + 1) * SEL_CHUNK)
        o_s = acc / jnp.maximum(acc[:, HEAD_DIM:HEAD_DIM + 1], 1e-30)
        sig = gate_ref[tok, :].astype(f32)
        vals = []
        for r in range(GROUP):
            a_s = jnp.concatenate([o_s[h * half_rows + r * half:h * half_rows + (r + 1) * half, :]
                                   for h in range(2)], axis=0) * sig[:, 3 * r + 1:3 * r + 2]
            vals.append(a_s if r % 2 == 0 else pltpu.roll(a_s, HEAD_DIM, 1))
        c0, c1 = _merge_heads(vals, lane)
        o_ref[tok, 0:LANES] = (ocw_ref[tok, 0:LANES] + c0).astype(bf16)
        o_ref[tok, LANES:2 * LANES] = (ocw_ref[tok, LANES:2 * LANES] + c1).astype(bf16)

    steps = [(i, c) for i in range(seq // SEL_CHUNK) for c in range(i + 1)]
    qs = stacked_q(0)
    s_next = scores(qs, 0, 0)
    carry = None
    for k, (i, c) in enumerate(steps):
        s = s_next
        if k + 1 < len(steps):
            i2, c2 = steps[k + 1]
            if i2 != i:
                qs = stacked_q(i2)
            s_next = scores(qs, i2, c2)
        k0 = c * SEL_CHUNK
        if c < i:
            carry = softmax_step(s[0], vs_ref[k0:k0 + SEL_CHUNK, :], carry)
        else:
            rows_a = None if carry is None else tuple(a[:half_rows] for a in carry)
            rows_b = None if carry is None else tuple(a[half_rows:] for a in carry)
            _, acc_a = softmax_step(s[0], vs_ref[k0:k0 + half, :], rows_a)
            _, acc_b = softmax_step(s[1], vs_ref[k0:k0 + SEL_CHUNK, :], rows_b)
            finish(i, jnp.concatenate([acc_a, acc_b], axis=0))
            carry = None


def _position_lanes(pos):
    out = np.zeros((pos.shape[0], LANES), np.float32)
    out[:, L_HI:L_HI + 3] = ((pos // 64) * 64)[:, None]
    out[:, L_LO:L_LO + 3] = (pos % 64)[:, None]
    return out


def _nsa_tables(seq):
    pos = np.arange(seq)
    kaug_s = _position_lanes(pos)
    kaug_s[pos, L_SEL + pos // SLC_LEN] = 1.0
    kaug_w = np.zeros((seq + WIN_PAD, LANES), np.float32)
    kaug_w[WIN_PAD:] = _position_lanes(pos)
    kaug_w[:WIN_PAD, L_FLAG] = 1.0
    kaug_c = _position_lanes(np.arange(N_CMP_PAD) * CMP_STRIDE + CMP_LEN - 1)
    tl = np.arange(SEL_CHUNK // 2)[:, None]
    kk = np.arange(SEL_CHUNK // 2)[None, :]
    tri = np.where(kk <= tl, 0.0, -MASK_BIG).astype(np.float32)
    tl = np.arange(WIN_TQ)[:, None]
    kk = np.arange(WIN_KEYS)[None, :]
    dist = tl + WIN_PAD - kk
    band = np.tile(np.where((dist >= 0) & (dist < WINDOW), 0.0, -MASK_BIG).astype(np.float32), (GROUP, 1))
    c = np.arange(N_CMP_PAD)[None, :]
    j = np.arange(seq // SLC_LEN)[:, None]
    c_start, c_end, s_start = c * CMP_STRIDE, c * CMP_STRIDE + CMP_LEN - 1, j * SLC_LEN
    ovl = ((c_start <= s_start + SLC_LEN - 1) & (c_end >= s_start) & (c < seq // CMP_STRIDE - 1)).astype(np.float32)
    cend = np.tile((c * CMP_STRIDE + CMP_LEN - 1) - tl, (GROUP, 1)).astype(np.int32)
    return (jnp.asarray(kaug_s, bf16), jnp.asarray(kaug_w, bf16), jnp.asarray(kaug_c, f32),
            jnp.asarray(tri), jnp.asarray(band), jnp.asarray(ovl, bf16), jnp.asarray(cend))


def _query_aug():
    slopes = jnp.exp2(-8.0 * jnp.arange(1, N_HEADS + 1, dtype=f32) / N_HEADS).reshape(N_KV, GROUP) * LOG2E
    hi =slopes.astype(bf16).astype(f32)
    mid = (slopes - hi).astype(bf16).astype(f32)
    lo = (slopes - hi - mid).astype(bf16).astype(f32)
    pieces = jnp.stack([hi, mid, lo, hi, mid, lo], axis=-1)
    base = jnp.zeros((N_KV, GROUP, LANES), f32).at[:, :, L_HI:L_HI + 6].set(pieces)
    return jnp.concatenate([base, base.at[:, :, L_FLAG].set(-MASK_BIG)], axis=1)


def _nsa(proj, wlo, whi, pelo, pehi, w2bd, batch, seq):
    kaug_s, kaug_w, kaug_c, tri, band, ovl, cend = _nsa_tables(seq)
    qaug = _query_aug()
    const2 = lambda b, g: (0, 0)
    in_specs = [
        pl.BlockSpec((seq, 2 * LANES), lambda b, g: (b, OFF_Q // (2 * LANES) + g)),
        pl.BlockSpec((seq, LANES), lambda b, g: (b, OFF_KSW // LANES + g)),
        pl.BlockSpec((seq, LANES), lambda b, g: (b, OFF_VSW // LANES + g)),
        pl.BlockSpec((seq, LANES), lambda b, g: (b, OFF_GATE // LANES + g)),
        pl.BlockSpec((seq, LANES), lambda b, g: (b, OFF_KVC // LANES + g)),
        pl.BlockSpec(wlo.shape, const2),
        pl.BlockSpec(whi.shape, const2),
        pl.BlockSpec(pelo.shape, const2),
        pl.BlockSpec(pehi.shape, const2),
        pl.BlockSpec(w2bd.shape, const2),
        pl.BlockSpec(kaug_s.shape, const2),
        pl.BlockSpec(kaug_w.shape, const2),
        pl.BlockSpec(kaug_c.shape, const2),
        pl.BlockSpec((1, 2 * GROUP, LANES), lambda b, g: (g, 0, 0)),
        pl.BlockSpec(tri.shape, const2),
        pl.BlockSpec(band.shape, const2),
        pl.BlockSpec(ovl.shape, const2),
        pl.BlockSpec(cend.shape, const2),
    ]
    return pl.pallas_call(
        _nsa_kernel,
        grid=(batch, N_KV),
        in_specs=in_specs,
        out_specs=pl.BlockSpec((seq, 2 * LANES), lambda b, g: (b, g)),
        out_shape=jax.ShapeDtypeStruct((batch * seq, D_ATT), bf16),
        scratch_shapes=[
            pltpu.VMEM((seq, LANES), f32),
            pltpu.VMEM((seq, LANES), bf16),
            pltpu.VMEM((seq + WIN_PAD, LANES), bf16),
            pltpu.VMEM((seq, LANES), bf16),
            pltpu.VMEM((seq + WIN_PAD, LANES), bf16),
            pltpu.VMEM((seq, 2 * LANES), f32),
            pltpu.VMEM((seq // LANES, seq // SLC_LEN, LANES), f32),
            pltpu.VMEM((seq // SLC_LEN, seq // LANES, LANES), f32),
            pltpu.VMEM((seq, LANES), f32),
        ],
        compiler_params=pltpu.CompilerParams(
            dimension_semantics=("parallel", "parallel"), vmem_limit_bytes=VMEM_LIMIT),
        name="nsa",
    )(proj, proj, proj, proj, proj, wlo, whi, pelo, pehi, w2bd, kaug_s, kaug_w, kaug_c, qaug, tri, band, ovl, cend)


def _out_kernel(x_ref, ya_ref, zb_ref, gm_ref, ob_ref, wa_ref, wb_ref, wo_ref, fnw_ref, o_ref):
    y_a = jnp.dot(ya_ref[...], wa_ref[...], preferred_element_type=f32)
    yb_in = ob_ref[...].astype(f32) * zb_ref[...].astype(f32)
    y_b = jnp.dot(yb_in.astype(bf16), wb_ref[...], preferred_element_type=f32)
    mixed = (gm_ref[:, 0:D_MODEL].astype(f32) * y_a + gm_ref[:, D_MODEL:2 * D_MODEL].astype(f32) * y_b)
    xo = x_ref[...] + jnp.dot(mixed.astype(bf16), wo_ref[...], preferred_element_type=f32)
    r = lax.rsqrt(jnp.mean(xo * xo, axis=-1, keepdims=True) + NORM_EPS)
    o_ref[...] = (xo * r) * fnw_ref[...]


def _out(x2, ya, proj, ob, wa, wb, wo, fnw, tm=512):
    n_rows = x2.shape[0]
    blk = lambda c: pl.BlockSpec((tm, D_MODEL), lambda i, c=c: (i, c))
    full = lambda a: pl.BlockSpec(a.shape, lambda i: (0, 0))
    return pl.pallas_call(
        _out_kernel,
        grid=(n_rows // tm,),
        in_specs=[
            blk(0),
            blk(0),
            blk(OFF_ZB // D_MODEL),
            pl.BlockSpec((tm, 2 * D_MODEL), lambda i: (i, OFF_GMIX // (2 * D_MODEL))),
            blk(0),
            full(wa), full(wb), full(wo), full(fnw),
        ],
        out_specs=pl.BlockSpec((tm, D_MODEL), lambda i: (i, 0)),
        out_shape=jax.ShapeDtypeStruct((n_rows, D_MODEL), f32),
        compiler_params=pltpu.CompilerParams(
            dimension_semantics=("parallel",), vmem_limit_bytes=VMEM_LIMIT),
        name="merge_out",
    )(x2, ya, proj, proj, ob, wa, wb, wo, fnw)


def _blockdiag_w1(w1_k, w1_v, lo):
    half = CMP_STRIDE * HEAD_DIM
    wk = w1_k[lo * half:(lo + 1) * half].reshape(CMP_STRIDE, HEAD_DIM, CMP_HID)
    wv = w1_v[lo * half:(lo + 1) * half].reshape(CMP_STRIDE, HEAD_DIM, CMP_HID)
    z = jnp.zeros_like(wk)
    top = jnp.concatenate([wk, z], axis=-1)
    bot = jnp.concatenate([z, wv], axis=-1)
    return jnp.concatenate([top, bot], axis=1).reshape(CMP_STRIDE * 2 * HEAD_DIM, 2 * CMP_HID).astype(bf16)


def _pe_row(pe_k, pe_v, lo):
    sl = slice(lo * CMP_STRIDE, (lo + 1) * CMP_STRIDE)
    return jnp.concatenate([pe_k[sl], pe_v[sl]], axis=-1).reshape(1, CMP_STRIDE * 2 * HEAD_DIM)


def kernel(x, norm_w, w_in, conv_w, conv_b, cmp_pe_k, cmp_pe_v, cmp_w1_k, cmp_w2_k, cmp_w1_v, cmp_w2_v,
           w_proj_a, w_proj_b, w_out, final_norm_w):
    batch, seq, _ = x.shape
    assert norm_w.shape[0] == 1 and seq % SEL_CHUNK == 0 and seq // CMP_STRIDE == N_CMP_PAD
    w_p = _reordered_w_in(w_in[0])
    x2 = x.reshape(batch * seq, D_MODEL)
    proj, ya = _inproj(x2, norm_w, w_p, conv_w[0], conv_b, seq)

    wlo = _blockdiag_w1(cmp_w1_k[0], cmp_w1_v[0], 0)
    whi = _blockdiag_w1(cmp_w1_k[0], cmp_w1_v[0], 1)
    pelo = _pe_row(cmp_pe_k[0], cmp_pe_v[0], 0)
    pehi = _pe_row(cmp_pe_k[0], cmp_pe_v[0], 1)
    zk = jnp.zeros_like(cmp_w2_k[0])
    w2bd = jnp.concatenate([jnp.concatenate([cmp_w2_k[0], zk], axis=1),
                            jnp.concatenate([zk, cmp_w2_v[0]], axis=1)], axis=0).astype(bf16)
    ob = _nsa(proj, wlo, whi, pelo, pehi, w2bd, batch, seq)

    out = _out(x2, ya, proj, ob, w_proj_a[0].astype(bf16), w_proj_b[0].astype(bf16),
               w_out[0].astype(bf16), final_norm_w.reshape(1, D_MODEL))
    return out.reshape(batch, seq, D_MODEL)
```

```python
import functools

import numpy as np
import jax
import jax.numpy as jnp
from jax import lax
from jax.experimental import pallas as pl
from jax.experimental.pallas import tpu as pltpu

D_MODEL = 1024
D_CONV = 1024
CONV_W = 3
N_HEADS = 16
HEAD_DIM = 64
N_KV = 4
GROUP = N_HEADS // N_KV
D_ATT = N_HEADS * HEAD_DIM
D_KV = N_KV * HEAD_DIM
CMP_LEN = 32
CMP_STRIDE = 16
CMP_HID = 128
SLC_LEN = 64
N_SEL = 8
WINDOW = 512
NORM_EPS = 1e-6

LANES = 128
N_CMP_PAD = 128
MASK_BIG = float(2.0 ** 100)
LOG2E = float(np.log2(np.e))
SEL_CHUNK = 512
WIN_TQ = 128
WIN_KEYS = WINDOW + WIN_TQ
WIN_PAD = WINDOW
CW_TQ = 512

OFF_Q = 0
OFF_ZB = 1024
OFF_GMIX = 2048
OFF_KVC = 4096
OFF_KSW = 4608
OFF_VSW = 5120
OFF_GATE = 5632
N_PROJ = 6144

L_HI = 64
L_LO = 67
L_FLAG = 70
L_SEL = 96

VMEM_LIMIT = 56 * 1024 * 1024

f32 = jnp.float32
bf16 = jnp.bfloat16


W_Q = 4 * D_CONV
W_KV = W_Q + D_ATT
W_GATE = W_KV + 6 * D_KV
W_ZB = W_GATE + 3 * N_HEADS
N_IN = W_ZB + D_ATT + 2 * D_MODEL


def _reordered_w_in(w):
    assert w.shape == (D_MODEL, N_IN)
    w_t = w.T
    kv = w_t[W_KV:W_GATE].reshape(6, N_KV, HEAD_DIM, D_MODEL)
    pair = lambda a, b: jnp.stack([kv[a], kv[b]], axis=1).reshape(N_KV * LANES, D_MODEL)
    gates = jnp.pad(w_t[W_GATE:W_ZB].reshape(N_KV, 3 * GROUP, D_MODEL),
                    ((0, 0), (0, LANES - 3 * GROUP), (0, 0))).reshape(N_KV * LANES, D_MODEL)
    rows = [w_t[:W_Q], w_t[W_Q:W_KV] * (HEAD_DIM ** -0.5 * LOG2E), w_t[W_ZB:],
            pair(0, 1), pair(2, 4), pair(3, 5), gates]
    return jnp.concatenate(rows, axis=0).astype(bf16)


def _sigmoid(x):
    return 1.0 / (1.0 + jnp.exp(-x))


def _dot_nt(a, b):
    return lax.dot_general(a, b, (((1,), (1,)), ((), ())), preferred_element_type=f32)


CARRY_ROWS = 8
ROW_BLOCK = 256


def _column_activations(lo, hi):
    silu = lambda a: a * _sigmoid(a)
    keep = lambda a: a
    segments = ((OFF_Q, OFF_ZB, keep), (OFF_ZB, OFF_GMIX, silu), (OFF_GMIX, OFF_KVC, _sigmoid),
                (OFF_KVC, OFF_GATE, keep), (OFF_GATE, N_PROJ, _sigmoid))
    return [(max(a, lo) - lo, min(b, hi) - lo, f) for a, b, f in segments if max(a, lo) < min(b, hi)]


def _inproj_kernel(x_ref, nw_ref, w1_ref, w2_ref, cw_ref, cb_ref, o_ref, ya_ref, h_ref, carry_ref,
                   *, tiles_per_seq):
    i = pl.program_id(0)
    j = pl.program_id(1)
    tm = x_ref.shape[0]
    tn = o_ref.shape[1]
    n_stored = N_PROJ // tn
    blocks = [slice(k * ROW_BLOCK, (k + 1) * ROW_BLOCK) for k in range(tm // ROW_BLOCK)]
    product = lambda rows, w_t: _dot_nt(h_ref[rows, :], w_t)

    @pl.when((i == 0) & (j == 0))
    def _():
        carry_ref[...] = jnp.zeros_like(carry_ref)

    @pl.when(j == 0)
    def _():
        x = x_ref[...]
        r = lax.rsqrt(jnp.mean(x * x, axis=-1, keepdims=True) + NORM_EPS)
        h_ref[...] = ((x * r) * nw_ref[...]).astype(bf16)

    for t in range(n_stored):
        @pl.when(j == t)
        def _(t=t):
            for rows in blocks:
                acc = product(rows, w1_ref[...])
                for lo, hi, act in _column_activations(t * tn, (t + 1) * tn):
                    o_ref[rows, lo:hi] = act(acc[:, lo:hi]).astype(bf16)

    @pl.when(j == n_stored)
    def _():
        lead = carry_ref[...] * (i % tiles_per_seq != 0).astype(f32)
        cw = cw_ref[...]
        for rows in blocks:
            u = product(rows, w2_ref[:D_CONV, :]) * product(rows, w1_ref[:D_CONV, :])
            win = jnp.concatenate([lead, u], axis=0)
            u1 = pltpu.roll(win, 1, 0)[CARRY_ROWS:, :]
            u2 = pltpu.roll(win, 2, 0)[CARRY_ROWS:, :]
            y = cb_ref[...] + cw[0:1, :] * u2 + cw[1:2, :] * u1 + cw[2:3, :] * u
            z = product(rows, w2_ref[D_CONV:, :])
            ya_ref[rows, :] = (product(rows, w1_ref[D_CONV:, :]) * y * (z * _sigmoid(z))).astype(bf16)
            lead = u[ROW_BLOCK - CARRY_ROWS:, :]
        carry_ref[...] = lead


def _inproj(x2, norm_w, w_p, conv_w, conv_b, seq, tm=1024):
    n_rows = x2.shape[0]
    tn = 2 * D_CONV
    assert N_PROJ % tn == 0 and seq % tm == 0 and tm % ROW_BLOCK == 0 and w_p.shape[0] == 2 * tn + N_PROJ
    n_stored = N_PROJ // tn
    first = lambda i, j: (jnp.where(j < n_stored, j + 2, 0), 0)
    second = lambda i, j: (1, 0)
    return pl.pallas_call(
        functools.partial(_inproj_kernel, tiles_per_seq=seq // tm),
        grid=(n_rows // tm, n_stored + 1),
        in_specs=[
            pl.BlockSpec((tm, D_MODEL), lambda i, j: (i, 0)),
            pl.BlockSpec((1, D_MODEL), lambda i, j: (0, 0)),
            pl.BlockSpec((tn, D_MODEL), first),
            pl.BlockSpec((tn, D_MODEL), second),
            pl.BlockSpec(conv_w.shape, lambda i, j: (0, 0)),
            pl.BlockSpec(conv_b.shape, lambda i, j: (0, 0)),
        ],
        out_specs=[
            pl.BlockSpec((tm, tn), lambda i, j: (i, jnp.minimum(j, n_stored - 1))),
            pl.BlockSpec((tm, D_CONV), lambda i, j: (i, 0)),
        ],
        out_shape=[jax.ShapeDtypeStruct((n_rows, N_PROJ), bf16), jax.ShapeDtypeStruct((n_rows, D_CONV), bf16)],
        scratch_shapes=[
            pltpu.VMEM((tm, D_MODEL), bf16),
            pltpu.VMEM((CARRY_ROWS, D_CONV), f32),
        ],
        compiler_params=pltpu.CompilerParams(
            dimension_semantics=("arbitrary", "arbitrary"), vmem_limit_bytes=VMEM_LIMIT),
        name="inproj",
    )(x2, norm_w, w_p, w_p, conv_w, conv_b)


def _softmax_step(s, v, carry):
    m, acc = carry
    m_new = jnp.maximum(m, jnp.max(s, axis=-1, keepdims=True))
    alpha = jnp.exp2(m - m_new)
    p = jnp.exp2(s - m_new)
    acc = alpha * acc + jnp.dot(p.astype(bf16), v, preferred_element_type=f32)
    return m_new, acc


def _stack_heads(q_tile, aug_rows, extra):
    lane = lax.broadcasted_iota(jnp.int32, (1, LANES), 1)
    pairs = (q_tile[:, :LANES], q_tile[:, LANES:])
    parts = []
    for r in range(GROUP):
        qh = pairs[r // 2]
        if r % 2 == 1:
            qh = pltpu.roll(qh, HEAD_DIM, 1)
        aug = aug_rows[r:r + 1, :]
        if extra is not None:
            aug = aug + extra
        parts.append(jnp.where(lane < HEAD_DIM, qh, aug).astype(bf16))
    return jnp.concatenate(parts, axis=0)


def _merge_heads(vals, lane):
    return (jnp.where(lane < HEAD_DIM, vals[0], vals[1]), jnp.where(lane < HEAD_DIM, vals[2], vals[3]))


def _nsa_kernel(q_ref, ksw_ref, vsw_ref, gate_ref, kvc_ref, wlo_ref, whi_ref, pelo_ref, pehi_ref, w2_ref,
                kaug_s_ref, kaug_w_ref, kaug_c_ref, qaug_ref, tri_ref, band_ref, ovl_ref, cend_ref,
                o_ref,
                kvcf_ref, ks_ref, kw_ref, vs_ref, vw_ref, ocw_ref, imp_ref, ns_ref, selq_ref):
    seq = q_ref.shape[0]
    lane = lax.broadcasted_iota(jnp.int32, (1, LANES), 1)
    low = lane < HEAD_DIM

    kvcf_ref[...] = kvc_ref[...].astype(f32)
    xcat = jnp.concatenate([kvcf_ref[pl.ds(l, N_CMP_PAD, stride=CMP_STRIDE), :] for l in range(CMP_STRIDE)],
                           axis=1)
    xa = (xcat + pelo_ref[...]).astype(bf16)
    xb = (xcat + pehi_ref[...]).astype(bf16)
    a = jnp.dot(xa, wlo_ref[...], preferred_element_type=f32)
    bm = jnp.dot(xb, whi_ref[...], preferred_element_type=f32)
    h = a + pltpu.roll(bm, N_CMP_PAD - 1, 0)
    hid = h * _sigmoid(h)
    kcvc = jnp.dot(hid.astype(bf16), w2_ref[...], preferred_element_type=f32)

    ksw = ksw_ref[...]
    vsw = vsw_ref[...]
    one = jnp.ones((1, LANES), bf16)
    ks_ref[...] = jnp.where(low, ksw, kaug_s_ref[...])
    kwin = pltpu.roll(ksw.astype(f32), HEAD_DIM, 1).astype(bf16)
    kw_ref[0:WIN_PAD, :] = kaug_w_ref[0:WIN_PAD, :]
    kw_ref[WIN_PAD:, :] = jnp.where(low, kwin, kaug_w_ref[WIN_PAD:, :])
    vs_ref[...] = jnp.where(low, vsw, one)
    vw_ref[0:WIN_PAD, :] = jnp.zeros((WIN_PAD, LANES), bf16)
    vw_ref[WIN_PAD:, :] = jnp.where(low, one, vsw)

    qaug = qaug_ref[0]
    kc_aug = jnp.where(low, kcvc, kaug_c_ref[...]).astype(bf16)
    vc_rows = jnp.concatenate([jnp.zeros((N_CMP_PAD, LANES), bf16), kcvc.astype(bf16)], axis=1)
    zeros_w = jnp.zeros((WIN_KEYS, LANES), bf16)
    ovl = ovl_ref[...]

    def cw_body(i, _):
        t0 = pl.multiple_of(i * CW_TQ, CW_TQ)
        sig = gate_ref[pl.ds(t0, CW_TQ), :].astype(f32)
        o_w, o_c = [], []
        n_tiles = CW_TQ // WIN_TQ

        def tile_scores(h):
            tw = pl.multiple_of(t0 + h * WIN_TQ, WIN_TQ)
            qw = _stack_heads(q_ref[pl.ds(tw, WIN_TQ), :].astype(f32), qaug[GROUP:2 * GROUP], None)
            k_all = jnp.concatenate([kw_ref[pl.ds(tw, WIN_KEYS), :], kc_aug], axis=0)
            return _dot_nt(qw, k_all)

        s_next = tile_scores(0)
        for h in range(n_tiles):
            tw = pl.multiple_of(t0 + h * WIN_TQ, WIN_TQ)
            s = s_next
            if h + 1 < n_tiles:
                s_next = tile_scores(h + 1)

            sw = jnp.concatenate([s[:, :WIN_TQ] + band_ref[:, :WIN_TQ], s[:, WIN_TQ:WINDOW],
                                  s[:, WINDOW:WIN_KEYS] + band_ref[:, WINDOW:]], axis=1)
            p_w = jnp.exp2(sw - jnp.max(sw, axis=-1, keepdims=True))

            ok = cend_ref[...] <= tw
            sm = jnp.where(ok, s[:, WIN_KEYS:], -jnp.inf)
            m = jnp.max(sm, axis=-1, keepdims=True)
            m = jnp.where(m > -jnp.inf, m, 0.0)
            e = jnp.where(ok, jnp.exp2(sm - m), 0.0)
            p_c = e / jnp.maximum(jnp.sum(e, axis=-1, keepdims=True), 1e-30)

            p_all = jnp.concatenate([p_w.astype(bf16), p_c.astype(bf16)], axis=1)
            v_all = jnp.concatenate(
                [jnp.concatenate([vw_ref[pl.ds(tw, WIN_KEYS), :], zeros_w], axis=1), vc_rows], axis=0)
            acc = jnp.dot(p_all, v_all, preferred_element_type=f32)
            acc_w = acc[:, :LANES]
            o_w.append(acc_w / jnp.maximum(acc_w[:, 0:1], 1e-30))
            o_c.append(acc[:, LANES:])

            ps = p_c[0:WIN_TQ] + p_c[WIN_TQ:2 * WIN_TQ] + p_c[2 * WIN_TQ:3 * WIN_TQ] + p_c[3 * WIN_TQ:4 * WIN_TQ]
            p1 = ps.astype(bf16)
            r1 = ps - p1.astype(f32)
            p2 = r1.astype(bf16)
            p3 = (r1 - p2.astype(f32)).astype(bf16)
            imp_ref[i * (CW_TQ // WIN_TQ) + h] = _dot_nt(ovl, p1) + _dot_nt(ovl, p2) + _dot_nt(ovl, p3)

        vals = []
        for r in range(GROUP):
            head = lambda parts: jnp.concatenate([o[r * WIN_TQ:(r + 1) * WIN_TQ, :] for o in parts], axis=0)
            a = head(o_c) * sig[:, 3 * r:3 * r + 1] + head(o_w) * sig[:, 3 * r + 2:3 * r + 3]
            vals.append(pltpu.roll(a, HEAD_DIM, 1) if r % 2 == 0 else a)
        c0, c1 = _merge_heads(vals, lane)
        ocw_ref[pl.ds(t0, CW_TQ), 0:LANES] = c0
        ocw_ref[pl.ds(t0, CW_TQ), LANES:2 * LANES] = c1
        return 0

    lax.fori_loop(0, seq // CW_TQ, cw_body, 0)

    n_slc = seq // SLC_LEN
    n_tt = seq // LANES
    shp = (n_tt, LANES)
    cur = (lax.broadcasted_iota(jnp.int32, shp, 0) * LANES + lax.broadcasted_iota(jnp.int32, shp, 1)) // SLC_LEN
    vals = []
    for j in range(n_slc):
        forced = (cur == j) | (cur == j + 1)
        vj = jnp.where(cur >= j, imp_ref[:, j, :], -jnp.inf)
        vals.append(jnp.full(shp, jnp.inf, f32) if j == 0 else jnp.where(forced, jnp.inf, vj))
    wins = [jnp.zeros(shp, f32) for _ in range(n_slc)]
    losses = [jnp.zeros(shp, f32) for _ in range(n_slc)]
    for a in range(n_slc):
        for b in range(a + 1, n_slc):
            a_first = jnp.where(vals[a] >= vals[b], 1.0, 0.0)
            wins[a] = wins[a] + a_first
            losses[b] = losses[b] + a_first
    for j in range(n_slc):
        rank = losses[j] + (float(n_slc - 1 - j) - wins[j])
        sel = (rank < float(N_SEL)) & (vals[j] > -jnp.inf)
        ns_ref[j] = jnp.where(sel, 0.0, 1.0)
    eye = (lax.broadcasted_iota(jnp.int32, (LANES, LANES), 0)
           == lax.broadcasted_iota(jnp.int32, (LANES, LANES), 1)).astype(bf16)
    zpad = jnp.zeros((L_SEL, LANES), f32)
    for tt in range(n_tt):
        z = jnp.concatenate([zpad, ns_ref[:, tt, :]], axis=0).astype(bf16)
        selq_ref[tt * LANES:(tt + 1) * LANES, :] = _dot_nt(eye, z) * (-MASK_BIG)

    half = SEL_CHUNK // 2
    half_rows = GROUP * half

    def stacked_q(i):
        parts = []
        for h in range(2):
            tok = slice(i * SEL_CHUNK + h * half, i * SEL_CHUNK + (h + 1) * half)
            parts.append(_stack_heads(q_ref[tok, :].astype(f32), qaug[0:GROUP], selq_ref[tok, :]))
        return parts

    def causal(s):
        tri = tri_ref[...]
        return jnp.concatenate([s[r * half:(r + 1) * half, :] + tri for r in range(GROUP)], axis=0)

    def scores(qs, i, c):
        k0 = c * SEL_CHUNK
        if c < i:
            return (_dot_nt(jnp.concatenate(qs, axis=0), ks_ref[k0:k0 + SEL_CHUNK, :]),)
        s_a = causal(_dot_nt(qs[0], ks_ref[k0:k0 + half, :]))
        s_b = _dot_nt(qs[1], ks_ref[k0:k0 + SEL_CHUNK, :])
        return (s_a, jnp.concatenate([s_b[:, :half], causal(s_b[:, half:])], axis=1))

    def softmax_step(s, v, carry):
        if carry is None:
            m = jnp.max(s, axis=-1, keepdims=True)
            return m, jnp.dot(jnp.exp2(s - m).astype(bf16), v, preferred_element_type=f32)
        return _softmax_step(s, v, carry)

    def finish(i, acc):
        tok = slice(i * SEL_CHUNK, (i + 1) * SEL_CHUNK)
        o_s = acc / jnp.maximum(acc[:, HEAD_DIM:HEAD_DIM + 1], 1e-30)
        sig = gate_ref[tok, :].astype(f32)
        vals = []
        for r in range(GROUP):
            a_s = jnp.concatenate([o_s[h * half_rows + r * half:h * half_rows + (r + 1) * half, :]
                                   for h in range(2)], axis=0) * sig[:, 3 * r + 1:3 * r + 2]
            vals.append(a_s if r % 2 == 0 else pltpu.roll(a_s, HEAD_DIM, 1))
        c0, c1 = _merge_heads(vals, lane)
        o_ref[tok, 0:LANES] = (ocw_ref[tok, 0:LANES] + c0).astype(bf16)
        o_ref[tok, LANES:2 * LANES] = (ocw_ref[tok, LANES:2 * LANES] + c1).astype(bf16)

    steps = [(i, c) for i in range(seq // SEL_CHUNK) for c in range(i + 1)]
    qs = stacked_q(0)
    s_next = scores(qs, 0, 0)
    carry = None
    for k, (i, c) in enumerate(steps):
        s = s_next
        if k + 1 < len(steps):
            i2, c2 = steps[k + 1]
            if i2 != i:
                qs = stacked_q(i2)
            s_next = scores(qs, i2, c2)
        k0 = c * SEL_CHUNK
        if c < i:
            carry = softmax_step(s[0], vs_ref[k0:k0 + SEL_CHUNK, :], carry)
        else:
            rows_a = None if carry is None else tuple(a[:half_rows] for a in carry)
            rows_b = None if carry is None else tuple(a[half_rows:] for a in carry)
            _, acc_a = softmax_step(s[0], vs_ref[k0:k0 + half, :], rows_a)
            _, acc_b = softmax_step(s[1], vs_ref[k0:k0 + SEL_CHUNK, :], rows_b)
            finish(i, jnp.concatenate([acc_a, acc_b], axis=0))
            carry = None


def _position_lanes(pos):
    out = np.zeros((pos.shape[0], LANES), np.float32)
    out[:, L_HI:L_HI + 3] = ((pos // 64) * 64)[:, None]
    out[:, L_LO:L_LO + 3] = (pos % 64)[:, None]
    return out


def _nsa_tables(seq):
    pos = np.arange(seq)
    kaug_s = _position_lanes(pos)
    kaug_s[pos, L_SEL + pos // SLC_LEN] = 1.0
    kaug_w = np.zeros((seq + WIN_PAD, LANES), np.float32)
    kaug_w[WIN_PAD:] = _position_lanes(pos)
    kaug_w[:WIN_PAD, L_FLAG] = 1.0
    kaug_c = _position_lanes(np.arange(N_CMP_PAD) * CMP_STRIDE + CMP_LEN - 1)
    tl = np.arange(SEL_CHUNK // 2)[:, None]
    kk = np.arange(SEL_CHUNK // 2)[None, :]
    tri = np.where(kk <= tl, 0.0, -MASK_BIG).astype(np.float32)
    tl = np.arange(WIN_TQ)[:, None]
    kk = np.arange(WIN_KEYS)[None, :]
    dist = tl + WIN_PAD - kk
    band = np.tile(np.where((dist >= 0) & (dist < WINDOW), 0.0, -MASK_BIG).astype(np.float32), (GROUP, 1))
    c = np.arange(N_CMP_PAD)[None, :]
    j = np.arange(seq // SLC_LEN)[:, None]
    c_start, c_end, s_start = c * CMP_STRIDE, c * CMP_STRIDE + CMP_LEN - 1, j * SLC_LEN
    ovl = ((c_start <= s_start + SLC_LEN - 1) & (c_end >= s_start) & (c < seq // CMP_STRIDE - 1)).astype(np.float32)
    cend = np.tile((c * CMP_STRIDE + CMP_LEN - 1) - tl, (GROUP, 1)).astype(np.int32)
    return (jnp.asarray(kaug_s, bf16), jnp.asarray(kaug_w, bf16), jnp.asarray(kaug_c, f32),
            jnp.asarray(tri), jnp.asarray(band), jnp.asarray(ovl, bf16), jnp.asarray(cend))


def _query_aug():
    slopes = jnp.exp2(-8.0 * jnp.arange(1, N_HEADS + 1, dtype=f32) / N_HEADS).reshape(N_KV, GROUP) * LOG2E
    hi =slopes.astype(bf16).astype(f32)
    mid = (slopes - hi).astype(bf16).astype(f32)
    lo = (slopes - hi - mid).astype(bf16).astype(f32)
    pieces = jnp.stack([hi, mid, lo, hi, mid, lo], axis=-1)
    base = jnp.zeros((N_KV, GROUP, LANES), f32).at[:, :, L_HI:L_HI + 6].set(pieces)
    return jnp.concatenate([base, base.at[:, :, L_FLAG].set(-MASK_BIG)], axis=1)


def _nsa(proj, wlo, whi, pelo, pehi, w2bd, batch, seq):
    kaug_s, kaug_w, kaug_c, tri, band, ovl, cend = _nsa_tables(seq)
    qaug = _query_aug()
    const2 = lambda b, g: (0, 0)
    in_specs = [
        pl.BlockSpec((seq, 2 * LANES), lambda b, g: (b, OFF_Q // (2 * LANES) + g)),
        pl.BlockSpec((seq, LANES), lambda b, g: (b, OFF_KSW // LANES + g)),
        pl.BlockSpec((seq, LANES), lambda b, g: (b, OFF_VSW // LANES + g)),
        pl.BlockSpec((seq, LANES), lambda b, g: (b, OFF_GATE // LANES + g)),
        pl.BlockSpec((seq, LANES), lambda b, g: (b, OFF_KVC // LANES + g)),
        pl.BlockSpec(wlo.shape, const2),
        pl.BlockSpec(whi.shape, const2),
        pl.BlockSpec(pelo.shape, const2),
        pl.BlockSpec(pehi.shape, const2),
        pl.BlockSpec(w2bd.shape, const2),
        pl.BlockSpec(kaug_s.shape, const2),
        pl.BlockSpec(kaug_w.shape, const2),
        pl.BlockSpec(kaug_c.shape, const2),
        pl.BlockSpec((1, 2 * GROUP, LANES), lambda b, g: (g, 0, 0)),
        pl.BlockSpec(tri.shape, const2),
        pl.BlockSpec(band.shape, const2),
        pl.BlockSpec(ovl.shape, const2),
        pl.BlockSpec(cend.shape, const2),
    ]
    return pl.pallas_call(
        _nsa_kernel,
        grid=(batch, N_KV),
        in_specs=in_specs,
        out_specs=pl.BlockSpec((seq, 2 * LANES), lambda b, g: (b, g)),
        out_shape=jax.ShapeDtypeStruct((batch * seq, D_ATT), bf16),
        scratch_shapes=[
            pltpu.VMEM((seq, LANES), f32),
            pltpu.VMEM((seq, LANES), bf16),
            pltpu.VMEM((seq + WIN_PAD, LANES), bf16),
            pltpu.VMEM((seq, LANES), bf16),
            pltpu.VMEM((seq + WIN_PAD, LANES), bf16),
            pltpu.VMEM((seq, 2 * LANES), f32),
            pltpu.VMEM((seq // LANES, seq // SLC_LEN, LANES), f32),
            pltpu.VMEM((seq // SLC_LEN, seq // LANES, LANES), f32),
            pltpu.VMEM((seq, LANES), f32),
        ],
        compiler_params=pltpu.CompilerParams(
            dimension_semantics=("parallel", "parallel"), vmem_limit_bytes=VMEM_LIMIT),
        name="nsa",
    )(proj, proj, proj, proj, proj, wlo, whi, pelo, pehi, w2bd, kaug_s, kaug_w, kaug_c, qaug, tri, band, ovl, cend)


def _out_kernel(x_ref, ya_ref, zb_ref, gm_ref, ob_ref, wa_ref, wb_ref, wo_ref, fnw_ref, o_ref):
    y_a = jnp.dot(ya_ref[...], wa_ref[...], preferred_element_type=f32)
    yb_in = ob_ref[...].astype(f32) * zb_ref[...].astype(f32)
    y_b = jnp.dot(yb_in.astype(bf16), wb_ref[...], preferred_element_type=f32)
    mixed = (gm_ref[:, 0:D_MODEL].astype(f32) * y_a + gm_ref[:, D_MODEL:2 * D_MODEL].astype(f32) * y_b)
    xo = x_ref[...] + jnp.dot(mixed.astype(bf16), wo_ref[...], preferred_element_type=f32)
    r = lax.rsqrt(jnp.mean(xo * xo, axis=-1, keepdims=True) + NORM_EPS)
    o_ref[...] = (xo * r) * fnw_ref[...]


def _out(x2, ya, proj, ob, wa, wb, wo, fnw, tm=512):
    n_rows = x2.shape[0]
    blk = lambda c: pl.BlockSpec((tm, D_MODEL), lambda i, c=c: (i, c))
    full = lambda a: pl.BlockSpec(a.shape, lambda i: (0, 0))
    return pl.pallas_call(
        _out_kernel,
        grid=(n_rows // tm,),
        in_specs=[
            blk(0),
            blk(0),
            blk(OFF_ZB // D_MODEL),
            pl.BlockSpec((tm, 2 * D_MODEL), lambda i: (i, OFF_GMIX // (2 * D_MODEL))),
            blk(0),
            full(wa), full(wb), full(wo), full(fnw),
        ],
        out_specs=pl.BlockSpec((tm, D_MODEL), lambda i: (i, 0)),
        out_shape=jax.ShapeDtypeStruct((n_rows, D_MODEL), f32),
        compiler_params=pltpu.CompilerParams(
            dimension_semantics=("parallel",), vmem_limit_bytes=VMEM_LIMIT),
        name="merge_out",
    )(x2, ya, proj, proj, ob, wa, wb, wo, fnw)


def _blockdiag_w1(w1_k, w1_v, lo):
    half = CMP_STRIDE * HEAD_DIM
    wk = w1_k[lo * half:(lo + 1) * half].reshape(CMP_STRIDE, HEAD_DIM, CMP_HID)
    wv = w1_v[lo * half:(lo + 1) * half].reshape(CMP_STRIDE, HEAD_DIM, CMP_HID)
    z = jnp.zeros_like(wk)
    top = jnp.concatenate([wk, z], axis=-1)
    bot = jnp.concatenate([z, wv], axis=-1)
    return jnp.concatenate([top, bot], axis=1).reshape(CMP_STRIDE * 2 * HEAD_DIM, 2 * CMP_HID).astype(bf16)


def _pe_row(pe_k, pe_v, lo):
    sl = slice(lo * CMP_STRIDE, (lo + 1) * CMP_STRIDE)
    return jnp.concatenate([pe_k[sl], pe_v[sl]], axis=-1).reshape(1, CMP_STRIDE * 2 * HEAD_DIM)


def kernel(x, norm_w, w_in, conv_w, conv_b, cmp_pe_k, cmp_pe_v, cmp_w1_k, cmp_w2_k, cmp_w1_v, cmp_w2_v,
           w_proj_a, w_proj_b, w_out, final_norm_w):
    batch, seq, _ = x.shape
    assert norm_w.shape[0] == 1 and seq % SEL_CHUNK == 0 and seq // CMP_STRIDE == N_CMP_PAD
    w_p = _reordered_w_in(w_in[0])
    x2 = x.reshape(batch * seq, D_MODEL)
    proj, ya = _inproj(x2, norm_w, w_p, conv_w[0], conv_b, seq)

    wlo = _blockdiag_w1(cmp_w1_k[0], cmp_w1_v[0], 0)
    whi = _blockdiag_w1(cmp_w1_k[0], cmp_w1_v[0], 1)
    pelo = _pe_row(cmp_pe_k[0], cmp_pe_v[0], 0)
    pehi = _pe_row(cmp_pe_k[0], cmp_pe_v[0], 1)
    zk = jnp.zeros_like(cmp_w2_k[0])
    w2bd = jnp.concatenate([jnp.concatenate([cmp_w2_k[0], zk], axis=1),
                            jnp.concatenate([zk, cmp_w2_v[0]], axis=1)], axis=0).astype(bf16)
    ob = _nsa(proj, wlo, whi, pelo, pehi, w2bd, batch, seq)

    out = _out(x2, ya, proj, ob, w_proj_a[0].astype(bf16), w_proj_b[0].astype(bf16),
               w_out[0].astype(bf16), final_norm_w.reshape(1, D_MODEL))
    return out.reshape(batch, seq, D_MODEL)
```

```python
import functools

import numpy as np
import jax
import jax.numpy as jnp
from jax import lax
from jax.experimental import pallas as pl
from jax.experimental.pallas import tpu as pltpu

D_MODEL = 1024
D_CONV = 1024
CONV_W = 3
N_HEADS = 16
HEAD_DIM = 64
N_KV = 4
GROUP = N_HEADS // N_KV
D_ATT = N_HEADS * HEAD_DIM
D_KV = N_KV * HEAD_DIM
CMP_LEN = 32
CMP_STRIDE = 16
CMP_HID = 128
SLC_LEN = 64
N_SEL = 8
WINDOW = 512
NORM_EPS = 1e-6

LANES = 128
N_CMP_PAD = 128
MASK_BIG = float(2.0 ** 100)
LOG2E = float(np.log2(np.e))
SEL_CHUNK = 512
WIN_TQ = 128
WIN_KEYS = WINDOW + WIN_TQ
WIN_PAD = WINDOW
CW_TQ = 512

OFF_Q = 0
OFF_ZB = 1024
OFF_GMIX = 2048
OFF_KVC = 4096
OFF_KSW = 4608
OFF_VSW = 5120
OFF_GATE = 5632
N_PROJ = 6144

L_HI = 64
L_LO = 67
L_FLAG = 70
L_SEL = 96

VMEM_LIMIT = 56 * 1024 * 1024

f32 = jnp.float32
bf16 = jnp.bfloat16


W_Q = 4 * D_CONV
W_KV = W_Q + D_ATT
W_GATE = W_KV + 6 * D_KV
W_ZB = W_GATE + 3 * N_HEADS
N_IN = W_ZB + D_ATT + 2 * D_MODEL


def _reordered_w_in(w):
    assert w.shape == (D_MODEL, N_IN)
    w_t = w.T
    kv = w_t[W_KV:W_GATE].reshape(6, N_KV, HEAD_DIM, D_MODEL)
    pair = lambda a, b: jnp.stack([kv[a], kv[b]], axis=1).reshape(N_KV * LANES, D_MODEL)
    gates = jnp.pad(w_t[W_GATE:W_ZB].reshape(N_KV, 3 * GROUP, D_MODEL),
                    ((0, 0), (0, LANES - 3 * GROUP), (0, 0))).reshape(N_KV * LANES, D_MODEL)
    rows = [w_t[:W_Q], w_t[W_Q:W_KV] * (HEAD_DIM ** -0.5 * LOG2E), w_t[W_ZB:],
            pair(0, 1), pair(2, 4), pair(3, 5), gates]
    return jnp.concatenate(rows, axis=0).astype(bf16)


def _sigmoid(x):
    return 1.0 / (1.0 + jnp.exp(-x))


def _dot_nt(a, b):
    return lax.dot_general(a, b, (((1,), (1,)), ((), ())), preferred_element_type=f32)


CARRY_ROWS = 8
ROW_BLOCK = 512


def _column_activations(lo, hi):
    silu = lambda a: a * _sigmoid(a)
    keep = lambda a: a
    segments = ((OFF_Q, OFF_ZB, keep), (OFF_ZB, OFF_GMIX, silu), (OFF_GMIX, OFF_KVC, _sigmoid),
                (OFF_KVC, OFF_GATE, keep), (OFF_GATE, N_PROJ, _sigmoid))
    return [(max(a, lo) - lo, min(b, hi) - lo, f) for a, b, f in segments if max(a, lo) < min(b, hi)]


def _inproj_kernel(x_ref, nw_ref, w1_ref, w2_ref, cw_ref, cb_ref, o_ref, ya_ref, h_ref, carry_ref,
                   *, tiles_per_seq):
    i = pl.program_id(0)
    j = pl.program_id(1)
    tm = x_ref.shape[0]
    tn = o_ref.shape[1]
    n_stored = N_PROJ // tn
    blocks = [slice(k * ROW_BLOCK, (k + 1) * ROW_BLOCK) for k in range(tm // ROW_BLOCK)]
    product = lambda rows, w_t: _dot_nt(h_ref[rows, :], w_t)

    @pl.when((i == 0) & (j == 0))
    def _():
        carry_ref[...] = jnp.zeros_like(carry_ref)

    @pl.when(j == 0)
    def _():
        x = x_ref[...]
        r = lax.rsqrt(jnp.mean(x * x, axis=-1, keepdims=True) + NORM_EPS)
        h_ref[...] = ((x * r) * nw_ref[...]).astype(bf16)

    for t in range(n_stored):
        @pl.when(j == t)
        def _(t=t):
            for rows in blocks:
                acc = product(rows, w1_ref[...])
                for lo, hi, act in _column_activations(t * tn, (t + 1) * tn):
                    o_ref[rows, lo:hi] = act(acc[:, lo:hi]).astype(bf16)

    @pl.when(j == n_stored)
    def _():
        lead = carry_ref[...] * (i % tiles_per_seq != 0).astype(f32)
        cw = cw_ref[...]
        for rows in blocks:
            u = product(rows, w2_ref[:D_CONV, :]) * product(rows, w1_ref[:D_CONV, :])
            win = jnp.concatenate([lead, u], axis=0)
            u1 = pltpu.roll(win, 1, 0)[CARRY_ROWS:, :]
            u2 = pltpu.roll(win, 2, 0)[CARRY_ROWS:, :]
            y = cb_ref[...] + cw[0:1, :] * u2 + cw[1:2, :] * u1 + cw[2:3, :] * u
            z = product(rows, w2_ref[D_CONV:, :])
            ya_ref[rows, :] = (product(rows, w1_ref[D_CONV:, :]) * y * (z * _sigmoid(z))).astype(bf16)
            lead = u[ROW_BLOCK - CARRY_ROWS:, :]
        carry_ref[...] = lead


def _inproj(x2, norm_w, w_p, conv_w, conv_b, seq, tm=1024):
    n_rows = x2.shape[0]
    tn = 2 * D_CONV
    assert N_PROJ % tn == 0 and seq % tm == 0 and tm % ROW_BLOCK == 0 and w_p.shape[0] == 2 * tn + N_PROJ
    n_stored = N_PROJ // tn
    first = lambda i, j: (jnp.where(j < n_stored, j + 2, 0), 0)
    second = lambda i, j: (1, 0)
    return pl.pallas_call(
        functools.partial(_inproj_kernel, tiles_per_seq=seq // tm),
        grid=(n_rows // tm, n_stored + 1),
        in_specs=[
            pl.BlockSpec((tm, D_MODEL), lambda i, j: (i, 0)),
            pl.BlockSpec((1, D_MODEL), lambda i, j: (0, 0)),
            pl.BlockSpec((tn, D_MODEL), first),
            pl.BlockSpec((tn, D_MODEL), second),
            pl.BlockSpec(conv_w.shape, lambda i, j: (0, 0)),
            pl.BlockSpec(conv_b.shape, lambda i, j: (0, 0)),
        ],
        out_specs=[
            pl.BlockSpec((tm, tn), lambda i, j: (i, jnp.minimum(j, n_stored - 1))),
            pl.BlockSpec((tm, D_CONV), lambda i, j: (i, 0)),
        ],
        out_shape=[jax.ShapeDtypeStruct((n_rows, N_PROJ), bf16), jax.ShapeDtypeStruct((n_rows, D_CONV), bf16)],
        scratch_shapes=[
            pltpu.VMEM((tm, D_MODEL), bf16),
            pltpu.VMEM((CARRY_ROWS, D_CONV), f32),
        ],
        compiler_params=pltpu.CompilerParams(
            dimension_semantics=("arbitrary", "arbitrary"), vmem_limit_bytes=VMEM_LIMIT),
        name="inproj",
    )(x2, norm_w, w_p, w_p, conv_w, conv_b)


def _softmax_step(s, v, carry):
    m, acc = carry
    m_new = jnp.maximum(m, jnp.max(s, axis=-1, keepdims=True))
    alpha = jnp.exp2(m - m_new)
    p = jnp.exp2(s - m_new)
    acc = alpha * acc + jnp.dot(p.astype(bf16), v, preferred_element_type=f32)
    return m_new, acc


def _stack_heads(q_tile, aug_rows, extra):
    lane = lax.broadcasted_iota(jnp.int32, (1, LANES), 1)
    pairs = (q_tile[:, :LANES], q_tile[:, LANES:])
    parts = []
    for r in range(GROUP):
        qh = pairs[r // 2]
        if r % 2 == 1:
            qh = pltpu.roll(qh, HEAD_DIM, 1)
        aug = aug_rows[r:r + 1, :]
        if extra is not None:
            aug = aug + extra
        parts.append(jnp.where(lane < HEAD_DIM, qh, aug).astype(bf16))
    return jnp.concatenate(parts, axis=0)


def _merge_heads(vals, lane):
    return (jnp.where(lane < HEAD_DIM, vals[0], vals[1]), jnp.where(lane < HEAD_DIM, vals[2], vals[3]))


def _nsa_kernel(q_ref, ksw_ref, vsw_ref, gate_ref, kvc_ref, wlo_ref, whi_ref, pelo_ref, pehi_ref, w2_ref,
                kaug_s_ref, kaug_w_ref, kaug_c_ref, qaug_ref, tri_ref, band_ref, ovl_ref, cend_ref,
                o_ref,
                kvcf_ref, ks_ref, kw_ref, vs_ref, vw_ref, ocw_ref, imp_ref, ns_ref, selq_ref):
    seq = q_ref.shape[0]
    lane = lax.broadcasted_iota(jnp.int32, (1, LANES), 1)
    low = lane < HEAD_DIM

    kvcf_ref[...] = kvc_ref[...].astype(f32)
    xcat = jnp.concatenate([kvcf_ref[pl.ds(l, N_CMP_PAD, stride=CMP_STRIDE), :] for l in range(CMP_STRIDE)],
                           axis=1)
    xa = (xcat + pelo_ref[...]).astype(bf16)
    xb = (xcat + pehi_ref[...]).astype(bf16)
    a = jnp.dot(xa, wlo_ref[...], preferred_element_type=f32)
    bm = jnp.dot(xb, whi_ref[...], preferred_element_type=f32)
    h = a + pltpu.roll(bm, N_CMP_PAD - 1, 0)
    hid = h * _sigmoid(h)
    kcvc = jnp.dot(hid.astype(bf16), w2_ref[...], preferred_element_type=f32)

    ksw = ksw_ref[...]
    vsw = vsw_ref[...]
    one = jnp.ones((1, LANES), bf16)
    ks_ref[...] = jnp.where(low, ksw, kaug_s_ref[...])
    kwin = pltpu.roll(ksw.astype(f32), HEAD_DIM, 1).astype(bf16)
    kw_ref[0:WIN_PAD, :] = kaug_w_ref[0:WIN_PAD, :]
    kw_ref[WIN_PAD:, :] = jnp.where(low, kwin, kaug_w_ref[WIN_PAD:, :])
    vs_ref[...] = jnp.where(low, vsw, one)
    vw_ref[0:WIN_PAD, :] = jnp.zeros((WIN_PAD, LANES), bf16)
    vw_ref[WIN_PAD:, :] = jnp.where(low, one, vsw)

    qaug = qaug_ref[0]
    kc_aug = jnp.where(low, kcvc, kaug_c_ref[...]).astype(bf16)
    vc_rows = jnp.concatenate([jnp.zeros((N_CMP_PAD, LANES), bf16), kcvc.astype(bf16)], axis=1)
    zeros_w = jnp.zeros((WIN_KEYS, LANES), bf16)
    ovl = ovl_ref[...]

    def cw_body(i, _):
        t0 = pl.multiple_of(i * CW_TQ, CW_TQ)
        sig = gate_ref[pl.ds(t0, CW_TQ), :].astype(f32)
        o_w, o_c = [], []
        n_tiles = CW_TQ // WIN_TQ

        def tile_scores(h):
            tw = pl.multiple_of(t0 + h * WIN_TQ, WIN_TQ)
            qw = _stack_heads(q_ref[pl.ds(tw, WIN_TQ), :].astype(f32), qaug[GROUP:2 * GROUP], None)
            k_all = jnp.concatenate([kw_ref[pl.ds(tw, WIN_KEYS), :], kc_aug], axis=0)
            return _dot_nt(qw, k_all)

        s_next = tile_scores(0)
        for h in range(n_tiles):
            tw = pl.multiple_of(t0 + h * WIN_TQ, WIN_TQ)
            s = s_next
            if h + 1 < n_tiles:
                s_next = tile_scores(h + 1)

            sw = jnp.concatenate([s[:, :WIN_TQ] + band_ref[:, :WIN_TQ], s[:, WIN_TQ:WINDOW],
                                  s[:, WINDOW:WIN_KEYS] + band_ref[:, WINDOW:]], axis=1)
            p_w = jnp.exp2(sw - jnp.max(sw, axis=-1, keepdims=True))

            ok = cend_ref[...] <= tw
            sm = jnp.where(ok, s[:, WIN_KEYS:], -jnp.inf)
            m = jnp.max(sm, axis=-1, keepdims=True)
            m = jnp.where(m > -jnp.inf, m, 0.0)
            e = jnp.where(ok, jnp.exp2(sm - m), 0.0)
            p_c = e / jnp.maximum(jnp.sum(e, axis=-1, keepdims=True), 1e-30)

            p_all = jnp.concatenate([p_w.astype(bf16), p_c.astype(bf16)], axis=1)
            v_all = jnp.concatenate(
                [jnp.concatenate([vw_ref[pl.ds(tw, WIN_KEYS), :], zeros_w], axis=1), vc_rows], axis=0)
            acc = jnp.dot(p_all, v_all, preferred_element_type=f32)
            acc_w = acc[:, :LANES]
            o_w.append(acc_w / jnp.maximum(acc_w[:, 0:1], 1e-30))
            o_c.append(acc[:, LANES:])

            ps = p_c[0:WIN_TQ] + p_c[WIN_TQ:2 * WIN_TQ] + p_c[2 * WIN_TQ:3 * WIN_TQ] + p_c[3 * WIN_TQ:4 * WIN_TQ]
            p1 = ps.astype(bf16)
            r1 = ps - p1.astype(f32)
            p2 = r1.astype(bf16)
            p3 = (r1 - p2.astype(f32)).astype(bf16)
            imp_ref[i * (CW_TQ // WIN_TQ) + h] = _dot_nt(ovl, p1) + _dot_nt(ovl, p2) + _dot_nt(ovl, p3)

        vals = []
        for r in range(GROUP):
            head = lambda parts: jnp.concatenate([o[r * WIN_TQ:(r + 1) * WIN_TQ, :] for o in parts], axis=0)
            a = head(o_c) * sig[:, 3 * r:3 * r + 1] + head(o_w) * sig[:, 3 * r + 2:3 * r + 3]
            vals.append(pltpu.roll(a, HEAD_DIM, 1) if r % 2 == 0 else a)
        c0, c1 = _merge_heads(vals, lane)
        ocw_ref[pl.ds(t0, CW_TQ), 0:LANES] = c0
        ocw_ref[pl.ds(t0, CW_TQ), LANES:2 * LANES] = c1
        return 0

    lax.fori_loop(0, seq // CW_TQ, cw_body, 0)

    n_slc = seq // SLC_LEN
    n_tt = seq // LANES
    shp = (n_tt, LANES)
    cur = (lax.broadcasted_iota(jnp.int32, shp, 0) * LANES + lax.broadcasted_iota(jnp.int32, shp, 1)) // SLC_LEN
    vals = []
    for j in range(n_slc):
        forced = (cur == j) | (cur == j + 1)
        vj = jnp.where(cur >= j, imp_ref[:, j, :], -jnp.inf)
        vals.append(jnp.full(shp, jnp.inf, f32) if j == 0 else jnp.where(forced, jnp.inf, vj))
    wins = [jnp.zeros(shp, f32) for _ in range(n_slc)]
    losses = [jnp.zeros(shp, f32) for _ in range(n_slc)]
    for a in range(n_slc):
        for b in range(a + 1, n_slc):
            a_first = jnp.where(vals[a] >= vals[b], 1.0, 0.0)
            wins[a] = wins[a] + a_first
            losses[b] = losses[b] + a_first
    for j in range(n_slc):
        rank = losses[j] + (float(n_slc - 1 - j) - wins[j])
        sel = (rank < float(N_SEL)) & (vals[j] > -jnp.inf)
        ns_ref[j] = jnp.where(sel, 0.0, 1.0)
    eye = (lax.broadcasted_iota(jnp.int32, (LANES, LANES), 0)
           == lax.broadcasted_iota(jnp.int32, (LANES, LANES), 1)).astype(bf16)
    zpad = jnp.zeros((L_SEL, LANES), f32)
    for tt in range(n_tt):
        z = jnp.concatenate([zpad, ns_ref[:, tt, :]], axis=0).astype(bf16)
        selq_ref[tt * LANES:(tt + 1) * LANES, :] = _dot_nt(eye, z) * (-MASK_BIG)

    half = SEL_CHUNK // 2
    half_rows = GROUP * half

    def stacked_q(i):
        parts = []
        for h in range(2):
            tok = slice(i * SEL_CHUNK + h * half, i * SEL_CHUNK + (h + 1) * half)
            parts.append(_stack_heads(q_ref[tok, :].astype(f32), qaug[0:GROUP], selq_ref[tok, :]))
        return parts

    def causal(s):
        tri = tri_ref[...]
        return jnp.concatenate([s[r * half:(r + 1) * half, :] + tri for r in range(GROUP)], axis=0)

    def scores(qs, i, c):
        k0 = c * SEL_CHUNK
        if c < i:
            return (_dot_nt(jnp.concatenate(qs, axis=0), ks_ref[k0:k0 + SEL_CHUNK, :]),)
        s_a = causal(_dot_nt(qs[0], ks_ref[k0:k0 + half, :]))
        s_b = _dot_nt(qs[1], ks_ref[k0:k0 + SEL_CHUNK, :])
        return (s_a, jnp.concatenate([s_b[:, :half], causal(s_b[:, half:])], axis=1))

    def softmax_step(s, v, carry):
        if carry is None:
            m = jnp.max(s, axis=-1, keepdims=True)
            return m, jnp.dot(jnp.exp2(s - m).astype(bf16), v, preferred_element_type=f32)
        return _softmax_step(s, v, carry)

    def finish(i, acc):
        tok = slice(i * SEL_CHUNK, (i + 1) * SEL_CHUNK)
        o_s = acc / jnp.maximum(acc[:, HEAD_DIM:HEAD_DIM + 1], 1e-30)
        sig = gate_ref[tok, :].astype(f32)
        vals = []
        for r in range(GROUP):
            a_s = jnp.concatenate([o_s[h * half_rows + r * half:h * half_rows + (r + 1) * half, :]
                                   for h in range(2)], axis=0) * sig[:, 3 * r + 1:3 * r + 2]
            vals.append(a_s if r % 2 == 0 else pltpu.roll(a_s, HEAD_DIM, 1))
        c0, c1 = _merge_heads(vals, lane)
        o_ref[tok, 0:LANES] = (ocw_ref[tok, 0:LANES] + c0).astype(bf16)
        o_ref[tok, LANES:2 * LANES] = (ocw_ref[tok, LANES:2 * LANES] + c1).astype(bf16)

    steps = [(i, c) for i in range(seq // SEL_CHUNK) for c in range(i + 1)]
    qs = stacked_q(0)
    s_next = scores(qs, 0, 0)
    carry = None
    for k, (i, c) in enumerate(steps):
        s = s_next
        if k + 1 < len(steps):
            i2, c2 = steps[k + 1]
            if i2 != i:
                qs = stacked_q(i2)
            s_next = scores(qs, i2, c2)
        k0 = c * SEL_CHUNK
        if c < i:
            carry = softmax_step(s[0], vs_ref[k0:k0 + SEL_CHUNK, :], carry)
        else:
            rows_a = None if carry is None else tuple(a[:half_rows] for a in carry)
            rows_b = None if carry is None else tuple(a[half_rows:] for a in carry)
            _, acc_a = softmax_step(s[0], vs_ref[k0:k0 + half, :], rows_a)
            _, acc_b = softmax_step(s[1], vs_ref[k0:k0 + SEL_CHUNK, :], rows_b)
            finish(i, jnp.concatenate([acc_a, acc_b], axis=0))
            carry = None


def _position_lanes(pos):
    out = np.zeros((pos.shape[0], LANES), np.float32)
    out[:, L_HI:L_HI + 3] = ((pos // 64) * 64)[:, None]
    out[:, L_LO:L_LO + 3] = (pos % 64)[:, None]
    return out


def _nsa_tables(seq):
    pos = np.arange(seq)
    kaug_s = _position_lanes(pos)
    kaug_s[pos, L_SEL + pos // SLC_LEN] = 1.0
    kaug_w = np.zeros((seq + WIN_PAD, LANES), np.float32)
    kaug_w[WIN_PAD:] = _position_lanes(pos)
    kaug_w[:WIN_PAD, L_FLAG] = 1.0
    kaug_c = _position_lanes(np.arange(N_CMP_PAD) * CMP_STRIDE + CMP_LEN - 1)
    tl = np.arange(SEL_CHUNK // 2)[:, None]
    kk = np.arange(SEL_CHUNK // 2)[None, :]
    tri = np.where(kk <= tl, 0.0, -MASK_BIG).astype(np.float32)
    tl = np.arange(WIN_TQ)[:, None]
    kk = np.arange(WIN_KEYS)[None, :]
    dist = tl + WIN_PAD - kk
    band = np.tile(np.where((dist >= 0) & (dist < WINDOW), 0.0, -MASK_BIG).astype(np.float32), (GROUP, 1))
    c = np.arange(N_CMP_PAD)[None, :]
    j = np.arange(seq // SLC_LEN)[:, None]
    c_start, c_end, s_start = c * CMP_STRIDE, c * CMP_STRIDE + CMP_LEN - 1, j * SLC_LEN
    ovl = ((c_start <= s_start + SLC_LEN - 1) & (c_end >= s_start) & (c < seq // CMP_STRIDE - 1)).astype(np.float32)
    cend = np.tile((c * CMP_STRIDE + CMP_LEN - 1) - tl, (GROUP, 1)).astype(np.int32)
    return (jnp.asarray(kaug_s, bf16), jnp.asarray(kaug_w, bf16), jnp.asarray(kaug_c, f32),
            jnp.asarray(tri), jnp.asarray(band), jnp.asarray(ovl, bf16), jnp.asarray(cend))


def _query_aug():
    slopes = jnp.exp2(-8.0 * jnp.arange(1, N_HEADS + 1, dtype=f32) / N_HEADS).reshape(N_KV, GROUP) * LOG2E
    hi =slopes.astype(bf16).astype(f32)
    mid = (slopes - hi).astype(bf16).astype(f32)
    lo = (slopes - hi - mid).astype(bf16).astype(f32)
    pieces = jnp.stack([hi, mid, lo, hi, mid, lo], axis=-1)
    base = jnp.zeros((N_KV, GROUP, LANES), f32).at[:, :, L_HI:L_HI + 6].set(pieces)
    return jnp.concatenate([base, base.at[:, :, L_FLAG].set(-MASK_BIG)], axis=1)


def _nsa(proj, wlo, whi, pelo, pehi, w2bd, batch, seq):
    kaug_s, kaug_w, kaug_c, tri, band, ovl, cend = _nsa_tables(seq)
    qaug = _query_aug()
    const2 = lambda b, g: (0, 0)
    in_specs = [
        pl.BlockSpec((seq, 2 * LANES), lambda b, g: (b, OFF_Q // (2 * LANES) + g)),
        pl.BlockSpec((seq, LANES), lambda b, g: (b, OFF_KSW // LANES + g)),
        pl.BlockSpec((seq, LANES), lambda b, g: (b, OFF_VSW // LANES + g)),
        pl.BlockSpec((seq, LANES), lambda b, g: (b, OFF_GATE // LANES + g)),
        pl.BlockSpec((seq, LANES), lambda b, g: (b, OFF_KVC // LANES + g)),
        pl.BlockSpec(wlo.shape, const2),
        pl.BlockSpec(whi.shape, const2),
        pl.BlockSpec(pelo.shape, const2),
        pl.BlockSpec(pehi.shape, const2),
        pl.BlockSpec(w2bd.shape, const2),
        pl.BlockSpec(kaug_s.shape, const2),
        pl.BlockSpec(kaug_w.shape, const2),
        pl.BlockSpec(kaug_c.shape, const2),
        pl.BlockSpec((1, 2 * GROUP, LANES), lambda b, g: (g, 0, 0)),
        pl.BlockSpec(tri.shape, const2),
        pl.BlockSpec(band.shape, const2),
        pl.BlockSpec(ovl.shape, const2),
        pl.BlockSpec(cend.shape, const2),
    ]
    return pl.pallas_call(
        _nsa_kernel,
        grid=(batch, N_KV),
        in_specs=in_specs,
        out_specs=pl.BlockSpec((seq, 2 * LANES), lambda b, g: (b, g)),
        out_shape=jax.ShapeDtypeStruct((batch * seq, D_ATT), bf16),
        scratch_shapes=[
            pltpu.VMEM((seq, LANES), f32),
            pltpu.VMEM((seq, LANES), bf16),
            pltpu.VMEM((seq + WIN_PAD, LANES), bf16),
            pltpu.VMEM((seq, LANES), bf16),
            pltpu.VMEM((seq + WIN_PAD, LANES), bf16),
            pltpu.VMEM((seq, 2 * LANES), f32),
            pltpu.VMEM((seq // LANES, seq // SLC_LEN, LANES), f32),
            pltpu.VMEM((seq // SLC_LEN, seq // LANES, LANES), f32),
            pltpu.VMEM((seq, LANES), f32),
        ],
        compiler_params=pltpu.CompilerParams(
            dimension_semantics=("parallel", "parallel"), vmem_limit_bytes=VMEM_LIMIT),
        name="nsa",
    )(proj, proj, proj, proj, proj, wlo, whi, pelo, pehi, w2bd, kaug_s, kaug_w, kaug_c, qaug, tri, band, ovl, cend)


def _out_kernel(x_ref, ya_ref, zb_ref, gm_ref, ob_ref, wa_ref, wb_ref, wo_ref, fnw_ref, o_ref):
    y_a = jnp.dot(ya_ref[...], wa_ref[...], preferred_element_type=f32)
    yb_in = ob_ref[...].astype(f32) * zb_ref[...].astype(f32)
    y_b = jnp.dot(yb_in.astype(bf16), wb_ref[...], preferred_element_type=f32)
    mixed = (gm_ref[:, 0:D_MODEL].astype(f32) * y_a + gm_ref[:, D_MODEL:2 * D_MODEL].astype(f32) * y_b)
    xo = x_ref[...] + jnp.dot(mixed.astype(bf16), wo_ref[...], preferred_element_type=f32)
    r = lax.rsqrt(jnp.mean(xo * xo, axis=-1, keepdims=True) + NORM_EPS)
    o_ref[...] = (xo * r) * fnw_ref[...]


def _out(x2, ya, proj, ob, wa, wb, wo, fnw, tm=512):
    n_rows = x2.shape[0]
    blk = lambda c: pl.BlockSpec((tm, D_MODEL), lambda i, c=c: (i, c))
    full = lambda a: pl.BlockSpec(a.shape, lambda i: (0, 0))
    return pl.pallas_call(
        _out_kernel,
        grid=(n_rows // tm,),
        in_specs=[
            blk(0),
            blk(0),
            blk(OFF_ZB // D_MODEL),
            pl.BlockSpec((tm, 2 * D_MODEL), lambda i: (i, OFF_GMIX // (2 * D_MODEL))),
            blk(0),
            full(wa), full(wb), full(wo), full(fnw),
        ],
        out_specs=pl.BlockSpec((tm, D_MODEL), lambda i: (i, 0)),
        out_shape=jax.ShapeDtypeStruct((n_rows, D_MODEL), f32),
        compiler_params=pltpu.CompilerParams(
            dimension_semantics=("parallel",), vmem_limit_bytes=VMEM_LIMIT),
        name="merge_out",
    )(x2, ya, proj, proj, ob, wa, wb, wo, fnw)


def _blockdiag_w1(w1_k, w1_v, lo):
    half = CMP_STRIDE * HEAD_DIM
    wk = w1_k[lo * half:(lo + 1) * half].reshape(CMP_STRIDE, HEAD_DIM, CMP_HID)
    wv = w1_v[lo * half:(lo + 1) * half].reshape(CMP_STRIDE, HEAD_DIM, CMP_HID)
    z = jnp.zeros_like(wk)
    top = jnp.concatenate([wk, z], axis=-1)
    bot = jnp.concatenate([z, wv], axis=-1)
    return jnp.concatenate([top, bot], axis=1).reshape(CMP_STRIDE * 2 * HEAD_DIM, 2 * CMP_HID).astype(bf16)


def _pe_row(pe_k, pe_v, lo):
    sl = slice(lo * CMP_STRIDE, (lo + 1) * CMP_STRIDE)
    return jnp.concatenate([pe_k[sl], pe_v[sl]], axis=-1).reshape(1, CMP_STRIDE * 2 * HEAD_DIM)


def kernel(x, norm_w, w_in, conv_w, conv_b, cmp_pe_k, cmp_pe_v, cmp_w1_k, cmp_w2_k, cmp_w1_v, cmp_w2_v,
           w_proj_a, w_proj_b, w_out, final_norm_w):
    batch, seq, _ = x.shape
    assert norm_w.shape[0] == 1 and seq % SEL_CHUNK == 0 and seq // CMP_STRIDE == N_CMP_PAD
    w_p = _reordered_w_in(w_in[0])
    x2 = x.reshape(batch * seq, D_MODEL)
    proj, ya = _inproj(x2, norm_w, w_p, conv_w[0], conv_b, seq)

    wlo = _blockdiag_w1(cmp_w1_k[0], cmp_w1_v[0], 0)
    whi = _blockdiag_w1(cmp_w1_k[0], cmp_w1_v[0], 1)
    pelo = _pe_row(cmp_pe_k[0], cmp_pe_v[0], 0)
    pehi = _pe_row(cmp_pe_k[0], cmp_pe_v[0], 1)
    zk = jnp.zeros_like(cmp_w2_k[0])
    w2bd = jnp.concatenate([jnp.concatenate([cmp_w2_k[0], zk], axis=1),
                            jnp.concatenate([zk, cmp_w2_v[0]], axis=1)], axis=0).astype(bf16)
    ob = _nsa(proj, wlo, whi, pelo, pehi, w2bd, batch, seq)

    out = _out(x2, ya, proj, ob, w_proj_a[0].astype(bf16), w_proj_b[0].astype(bf16),
               w_out[0].astype(bf16), final_norm_w.reshape(1, D_MODEL))
    return out.reshape(batch, seq, D_MODEL)
```

```python
import functools

import numpy as np
import jax
import jax.numpy as jnp
from jax import lax
from jax.experimental import pallas as pl
from jax.experimental.pallas import tpu as pltpu

D_MODEL = 1024
D_CONV = 1024
CONV_W = 3
N_HEADS = 16
HEAD_DIM = 64
N_KV = 4
GROUP = N_HEADS // N_KV
D_ATT = N_HEADS * HEAD_DIM
D_KV = N_KV * HEAD_DIM
CMP_LEN = 32
CMP_STRIDE = 16
CMP_HID = 128
SLC_LEN = 64
N_SEL = 8
WINDOW = 512
NORM_EPS = 1e-6

LANES = 128
N_CMP_PAD = 128
MASK_BIG = float(2.0 ** 100)
LOG2E = float(np.log2(np.e))
SEL_CHUNK = 512
WIN_TQ = 128
WIN_KEYS = WINDOW + WIN_TQ
WIN_PAD = WINDOW
CW_TQ = 512

OFF_Q = 0
OFF_ZB = 1024
OFF_GMIX = 2048
OFF_KVC = 4096
OFF_KSW = 4608
OFF_VSW = 5120
OFF_GATE = 5632
N_PROJ = 6144

L_HI = 64
L_LO = 67
L_FLAG = 70
L_SEL = 96

VMEM_LIMIT = 56 * 1024 * 1024

f32 = jnp.float32
bf16 = jnp.bfloat16


W_Q = 4 * D_CONV
W_KV = W_Q + D_ATT
W_GATE = W_KV + 6 * D_KV
W_ZB = W_GATE + 3 * N_HEADS
N_IN = W_ZB + D_ATT + 2 * D_MODEL


WP_ROWS = 512
WP_T = 256


def _wprep_kernel(a_ref, b_ref, o_ref):
    j = pl.program_id(0)
    n_plain = W_KV // WP_ROWS
    n_shift = (N_IN - W_ZB) // WP_ROWS
    off = W_ZB % WP_ROWS
    eye = (lax.broadcasted_iota(jnp.int32, (WP_T, WP_T), 0)
           == lax.broadcasted_iota(jnp.int32, (WP_T, WP_T), 1)).astype(bf16)

    def transposed(rows_bf16, kb):
        return _dot_nt(eye, rows_bf16[:, kb * WP_T:(kb + 1) * WP_T])

    def emit(rows):
        rows = rows.astype(bf16)
        for kb in range(D_MODEL // WP_T):
            o_ref[kb * WP_T:(kb + 1) * WP_T, :] = transposed(rows, kb).astype(bf16)

    def interleave(x, x_off, y, y_off):
        parts = []
        for g in range(N_KV):
            parts += [x[x_off + g * HEAD_DIM:x_off + (g + 1) * HEAD_DIM], y[y_off + g * HEAD_DIM:y_off + (g + 1) * HEAD_DIM]]
        return jnp.concatenate(parts, axis=0)

    @pl.when(j < W_Q // WP_ROWS)
    def _():
        emit(a_ref[...])

    @pl.when((j >= W_Q // WP_ROWS) & (j < n_plain))
    def _():
        emit(a_ref[...] * (HEAD_DIM ** -0.5 * LOG2E))

    @pl.when((j >= n_plain) & (j < n_plain + n_shift))
    def _():
        emit(jnp.concatenate([a_ref[off:, :], b_ref[:off, :]], axis=0))

    @pl.when(j == n_plain + n_shift)
    def _():
        a = a_ref[...]
        emit(interleave(a, 0, a, D_KV))

    @pl.when(j == n_plain + n_shift + 1)
    def _():
        emit(interleave(a_ref[...], 0, b_ref[...], 0))

    @pl.when(j == n_plain + n_shift + 2)
    def _():
        emit(interleave(a_ref[...], D_KV, b_ref[...], D_KV))

    @pl.when(j == n_plain + n_shift + 3)
    def _():
        rows = a_ref[:LANES, :].astype(bf16)
        lane = lax.broadcasted_iota(jnp.int32, (1, LANES), 1)
        for kb in range(D_MODEL // WP_T):
            t = transposed(rows, kb)
            for g in range(N_KV):
                shifted = t if g == 0 else pltpu.roll(t, LANES - 3 * GROUP * g, 1)
                o_ref[kb * WP_T:(kb + 1) * WP_T, g * LANES:(g + 1) * LANES] = (
                    jnp.where(lane < 3 * GROUP, shifted, 0.0).astype(bf16))


def _reordered_w_in(w):
    assert w.shape == (D_MODEL, N_IN) and W_KV % WP_ROWS == 0 and (N_IN - W_ZB) % WP_ROWS == 0
    assert W_GATE % WP_ROWS == 0 and 6 * D_KV == 3 * WP_ROWS and WP_ROWS == N_KV * LANES
    n_plain, n_shift = W_KV // WP_ROWS, (N_IN - W_ZB) // WP_ROWS
    first_shift = W_ZB // WP_ROWS
    kv0 = W_KV // WP_ROWS
    n_out = 4 * D_CONV + N_PROJ
    n_steps = n_out // WP_ROWS

    def block_a(j):
        shifted = first_shift + (j - n_plain)
        tail = jnp.where(j == n_plain + n_shift, kv0, jnp.where(j == n_plain + n_shift + 3, W_GATE // WP_ROWS, kv0 + 1))
        return jnp.where(j < n_plain, j, jnp.where(j < n_plain + n_shift, shifted, tail))

    def block_b(j):
        shifted = jnp.clip(first_shift + 1 + (j - n_plain), first_shift + 1, first_shift + n_shift)
        return jnp.where(j < n_plain + n_shift + 1, shifted, kv0 + 2)

    return pl.pallas_call(
        _wprep_kernel,
        grid=(n_steps,),
        in_specs=[pl.BlockSpec((WP_ROWS, D_MODEL), lambda j: (block_a(j), 0)),
                  pl.BlockSpec((WP_ROWS, D_MODEL), lambda j: (block_b(j), 0))],
        out_specs=pl.BlockSpec((D_MODEL, WP_ROWS), lambda j: (0, j)),
        out_shape=jax.ShapeDtypeStruct((D_MODEL, n_out), bf16),
        compiler_params=pltpu.CompilerParams(dimension_semantics=("parallel",), vmem_limit_bytes=VMEM_LIMIT),
        name="wprep",
    )(w.T, w.T)


def _sigmoid(x):
    return 1.0 / (1.0 + jnp.exp(-x))


def _dot_nt(a, b):
    return lax.dot_general(a, b, (((1,), (1,)), ((), ())), preferred_element_type=f32)


CARRY_ROWS = 8
ROW_BLOCK = 256


def _column_activations(lo, hi):
    silu = lambda a: a * _sigmoid(a)
    keep = lambda a: a
    segments = ((OFF_Q, OFF_ZB, keep), (OFF_ZB, OFF_GMIX, silu), (OFF_GMIX, OFF_KVC, _sigmoid),
                (OFF_KVC, OFF_GATE, keep), (OFF_GATE, N_PROJ, _sigmoid))
    return [(max(a, lo) - lo, min(b, hi) - lo, f) for a, b, f in segments if max(a, lo) < min(b, hi)]


def _inproj_kernel(x_ref, nw_ref, w1_ref, w2_ref, cw_ref, cb_ref, o_ref, ya_ref, h_ref, carry_ref,
                   *, tiles_per_seq):
    i = pl.program_id(0)
    j = pl.program_id(1)
    tm = x_ref.shape[0]
    tn = o_ref.shape[1]
    n_stored = N_PROJ // tn
    blocks = [slice(k * ROW_BLOCK, (k + 1) * ROW_BLOCK) for k in range(tm // ROW_BLOCK)]
    product = lambda rows, w: jnp.dot(h_ref[rows, :], w, preferred_element_type=f32)

    @pl.when((i == 0) & (j == 0))
    def _():
        carry_ref[...] = jnp.zeros_like(carry_ref)

    @pl.when(j == 0)
    def _():
        x = x_ref[...]
        r = lax.rsqrt(jnp.mean(x * x, axis=-1, keepdims=True) + NORM_EPS)
        h_ref[...] = ((x * r) * nw_ref[...]).astype(bf16)

    for t in range(n_stored):
        @pl.when(j == t)
        def _(t=t):
            for rows in blocks:
                acc = product(rows, w1_ref[...])
                for lo, hi, act in _column_activations(t * tn, (t + 1) * tn):
                    o_ref[rows, lo:hi] = act(acc[:, lo:hi]).astype(bf16)

    @pl.when(j == n_stored)
    def _():
        lead = carry_ref[...] * (i % tiles_per_seq != 0).astype(f32)
        cw = cw_ref[...]
        for rows in blocks:
            u = product(rows, w2_ref[:, :D_CONV]) * product(rows, w1_ref[:, :D_CONV])
            win = jnp.concatenate([lead, u], axis=0)
            u1 = pltpu.roll(win, 1, 0)[CARRY_ROWS:, :]
            u2 = pltpu.roll(win, 2, 0)[CARRY_ROWS:, :]
            y = cb_ref[...] + cw[0:1, :] * u2 + cw[1:2, :] * u1 + cw[2:3, :] * u
            z = product(rows, w2_ref[:, D_CONV:])
            ya_ref[rows, :] = (product(rows, w1_ref[:, D_CONV:]) * y * (z * _sigmoid(z))).astype(bf16)
            lead = u[ROW_BLOCK - CARRY_ROWS:, :]
        carry_ref[...] = lead


def _inproj(x2, norm_w, w_p, conv_w, conv_b, seq, tm=1024):
    n_rows = x2.shape[0]
    tn = 2 * D_CONV
    assert N_PROJ % tn == 0 and seq % tm == 0 and tm % ROW_BLOCK == 0 and w_p.shape[1] == 2 * tn + N_PROJ
    n_stored = N_PROJ // tn
    first = lambda i, j: (0, jnp.where(j < n_stored, j + 2, 0))
    second = lambda i, j: (0, 1)
    return pl.pallas_call(
        functools.partial(_inproj_kernel, tiles_per_seq=seq // tm),
        grid=(n_rows // tm, n_stored + 1),
        in_specs=[
            pl.BlockSpec((tm, D_MODEL), lambda i, j: (i, 0)),
            pl.BlockSpec((1, D_MODEL), lambda i, j: (0, 0)),
            pl.BlockSpec((D_MODEL, tn), first),
            pl.BlockSpec((D_MODEL, tn), second),
            pl.BlockSpec(conv_w.shape, lambda i, j: (0, 0)),
            pl.BlockSpec(conv_b.shape, lambda i, j: (0, 0)),
        ],
        out_specs=[
            pl.BlockSpec((tm, tn), lambda i, j: (i, jnp.minimum(j, n_stored - 1))),
            pl.BlockSpec((tm, D_CONV), lambda i, j: (i, 0)),
        ],
        out_shape=[jax.ShapeDtypeStruct((n_rows, N_PROJ), bf16), jax.ShapeDtypeStruct((n_rows, D_CONV), bf16)],
        scratch_shapes=[
            pltpu.VMEM((tm, D_MODEL), bf16),
            pltpu.VMEM((CARRY_ROWS, D_CONV), f32),
        ],
        compiler_params=pltpu.CompilerParams(
            dimension_semantics=("arbitrary", "arbitrary"), vmem_limit_bytes=VMEM_LIMIT),
        name="inproj",
    )(x2, norm_w, w_p, w_p, conv_w, conv_b)


def _softmax_step(s, v, carry):
    m, acc = carry
    m_new = jnp.maximum(m, jnp.max(s, axis=-1, keepdims=True))
    alpha = jnp.exp2(m - m_new)
    p = jnp.exp2(s - m_new)
    acc = alpha * acc + jnp.dot(p.astype(bf16), v, preferred_element_type=f32)
    return m_new, acc


def _stack_heads(q_tile, aug_rows, extra):
    lane = lax.broadcasted_iota(jnp.int32, (1, LANES), 1)
    pairs = (q_tile[:, :LANES], q_tile[:, LANES:])
    parts = []
    for r in range(GROUP):
        qh = pairs[r // 2]
        if r % 2 == 1:
            qh = pltpu.roll(qh, HEAD_DIM, 1)
        aug = aug_rows[r:r + 1, :]
        if extra is not None:
            aug = aug + extra
        parts.append(jnp.where(lane < HEAD_DIM, qh, aug).astype(bf16))
    return jnp.concatenate(parts, axis=0)


def _merge_heads(vals, lane):
    return (jnp.where(lane < HEAD_DIM, vals[0], vals[1]), jnp.where(lane < HEAD_DIM, vals[2], vals[3]))


def _nsa_kernel(q_ref, ksw_ref, vsw_ref, gate_ref, kvc_ref, wlo_ref, whi_ref, pelo_ref, pehi_ref, w2_ref,
                kaug_s_ref, kaug_w_ref, kaug_c_ref, qaug_ref, tri_ref, band_ref, ovl_ref, cend_ref,
                o_ref,
                kvcf_ref, ks_ref, kw_ref, vs_ref, vw_ref, ocw_ref, imp_ref, ns_ref, selq_ref):
    seq = q_ref.shape[0]
    lane = lax.broadcasted_iota(jnp.int32, (1, LANES), 1)
    low = lane < HEAD_DIM

    kvcf_ref[...] = kvc_ref[...].astype(f32)
    xcat = jnp.concatenate([kvcf_ref[pl.ds(l, N_CMP_PAD, stride=CMP_STRIDE), :] for l in range(CMP_STRIDE)],
                           axis=1)
    xa = (xcat + pelo_ref[...]).astype(bf16)
    xb = (xcat + pehi_ref[...]).astype(bf16)
    a = jnp.dot(xa, wlo_ref[...], preferred_element_type=f32)
    bm = jnp.dot(xb, whi_ref[...], preferred_element_type=f32)
    h = a + pltpu.roll(bm, N_CMP_PAD - 1, 0)
    hid = h * _sigmoid(h)
    kcvc = jnp.dot(hid.astype(bf16), w2_ref[...], preferred_element_type=f32)

    ksw = ksw_ref[...]
    vsw = vsw_ref[...]
    one = jnp.ones((1, LANES), bf16)
    ks_ref[...] = jnp.where(low, ksw, kaug_s_ref[...])
    kwin = pltpu.roll(ksw.astype(f32), HEAD_DIM, 1).astype(bf16)
    kw_ref[0:WIN_PAD, :] = kaug_w_ref[0:WIN_PAD, :]
    kw_ref[WIN_PAD:, :] = jnp.where(low, kwin, kaug_w_ref[WIN_PAD:, :])
    vs_ref[...] = jnp.where(low, vsw, one)
    vw_ref[0:WIN_PAD, :] = jnp.zeros((WIN_PAD, LANES), bf16)
    vw_ref[WIN_PAD:, :] = jnp.where(low, one, vsw)

    qaug = qaug_ref[0]
    kc_aug = jnp.where(low, kcvc, kaug_c_ref[...]).astype(bf16)
    vc_rows = jnp.concatenate([jnp.zeros((N_CMP_PAD, LANES), bf16), kcvc.astype(bf16)], axis=1)
    zeros_w = jnp.zeros((WIN_KEYS, LANES), bf16)
    ovl = ovl_ref[...]

    def cw_body(i, _):
        t0 = pl.multiple_of(i * CW_TQ, CW_TQ)
        sig = gate_ref[pl.ds(t0, CW_TQ), :].astype(f32)
        o_w, o_c = [], []
        n_tiles = CW_TQ // WIN_TQ

        def tile_scores(h):
            tw = pl.multiple_of(t0 + h * WIN_TQ, WIN_TQ)
            qw = _stack_heads(q_ref[pl.ds(tw, WIN_TQ), :].astype(f32), qaug[GROUP:2 * GROUP], None)
            k_all = jnp.concatenate([kw_ref[pl.ds(tw, WIN_KEYS), :], kc_aug], axis=0)
            return _dot_nt(qw, k_all)

        s_next = tile_scores(0)
        for h in range(n_tiles):
            tw = pl.multiple_of(t0 + h * WIN_TQ, WIN_TQ)
            s = s_next
            if h + 1 < n_tiles:
                s_next = tile_scores(h + 1)

            sw = jnp.concatenate([s[:, :WIN_TQ] + band_ref[:, :WIN_TQ], s[:, WIN_TQ:WINDOW],
                                  s[:, WINDOW:WIN_KEYS] + band_ref[:, WINDOW:]], axis=1)
            p_w = jnp.exp2(sw - jnp.max(sw, axis=-1, keepdims=True))

            ok = cend_ref[...] <= tw
            sm = jnp.where(ok, s[:, WIN_KEYS:], -jnp.inf)
            m = jnp.max(sm, axis=-1, keepdims=True)
            m = jnp.where(m > -jnp.inf, m, 0.0)
            e = jnp.where(ok, jnp.exp2(sm - m), 0.0)
            p_c = e / jnp.maximum(jnp.sum(e, axis=-1, keepdims=True), 1e-30)

            p_all = jnp.concatenate([p_w.astype(bf16), p_c.astype(bf16)], axis=1)
            v_all = jnp.concatenate(
                [jnp.concatenate([vw_ref[pl.ds(tw, WIN_KEYS), :], zeros_w], axis=1), vc_rows], axis=0)
            acc = jnp.dot(p_all, v_all, preferred_element_type=f32)
            acc_w = acc[:, :LANES]
            o_w.append(acc_w / jnp.maximum(acc_w[:, 0:1], 1e-30))
            o_c.append(acc[:, LANES:])

            ps = p_c[0:WIN_TQ] + p_c[WIN_TQ:2 * WIN_TQ] + p_c[2 * WIN_TQ:3 * WIN_TQ] + p_c[3 * WIN_TQ:4 * WIN_TQ]
            p1 = ps.astype(bf16)
            r1 = ps - p1.astype(f32)
            p2 = r1.astype(bf16)
            p3 = (r1 - p2.astype(f32)).astype(bf16)
            imp_ref[i * (CW_TQ // WIN_TQ) + h] = _dot_nt(ovl, p1) + _dot_nt(ovl, p2) + _dot_nt(ovl, p3)

        vals = []
        for r in range(GROUP):
            head = lambda parts: jnp.concatenate([o[r * WIN_TQ:(r + 1) * WIN_TQ, :] for o in parts], axis=0)
            a = head(o_c) * sig[:, 3 * r:3 * r + 1] + head(o_w) * sig[:, 3 * r + 2:3 * r + 3]
            vals.append(pltpu.roll(a, HEAD_DIM, 1) if r % 2 == 0 else a)
        c0, c1 = _merge_heads(vals, lane)
        ocw_ref[pl.ds(t0, CW_TQ), 0:LANES] = c0
        ocw_ref[pl.ds(t0, CW_TQ), LANES:2 * LANES] = c1
        return 0

    lax.fori_loop(0, seq // CW_TQ, cw_body, 0)

    n_slc = seq // SLC_LEN
    n_tt = seq // LANES
    shp = (n_tt, LANES)
    cur = (lax.broadcasted_iota(jnp.int32, shp, 0) * LANES + lax.broadcasted_iota(jnp.int32, shp, 1)) // SLC_LEN
    vals = []
    for j in range(n_slc):
        forced = (cur == j) | (cur == j + 1)
        vj = jnp.where(cur >= j, imp_ref[:, j, :], -jnp.inf)
        vals.append(jnp.full(shp, jnp.inf, f32) if j == 0 else jnp.where(forced, jnp.inf, vj))
    wins = [jnp.zeros(shp, f32) for _ in range(n_slc)]
    losses = [jnp.zeros(shp, f32) for _ in range(n_slc)]
    for a in range(n_slc):
        for b in range(a + 1, n_slc):
            a_first = jnp.where(vals[a] >= vals[b], 1.0, 0.0)
            wins[a] = wins[a] + a_first
            losses[b] = losses[b] + a_first
    for j in range(n_slc):
        rank = losses[j] + (float(n_slc - 1 - j) - wins[j])
        sel = (rank < float(N_SEL)) & (vals[j] > -jnp.inf)
        ns_ref[j] = jnp.where(sel, 0.0, 1.0)
    eye = (lax.broadcasted_iota(jnp.int32, (LANES, LANES), 0)
           == lax.broadcasted_iota(jnp.int32, (LANES, LANES), 1)).astype(bf16)
    zpad = jnp.zeros((L_SEL, LANES), f32)
    for tt in range(n_tt):
        z = jnp.concatenate([zpad, ns_ref[:, tt, :]], axis=0).astype(bf16)
        selq_ref[tt * LANES:(tt + 1) * LANES, :] = _dot_nt(eye, z) * (-MASK_BIG)

    half = SEL_CHUNK // 2
    half_rows = GROUP * half

    def stacked_q(i):
        parts = []
        for h in range(2):
            tok = slice(i * SEL_CHUNK + h * half, i * SEL_CHUNK + (h + 1) * half)
            parts.append(_stack_heads(q_ref[tok, :].astype(f32), qaug[0:GROUP], selq_ref[tok, :]))
        return parts

    def causal(s):
        tri = tri_ref[...]
        return jnp.concatenate([s[r * half:(r + 1) * half, :] + tri for r in range(GROUP)], axis=0)

    def scores(qs, i, c):
        k0 = c * SEL_CHUNK
        if c < i:
            return (_dot_nt(jnp.concatenate(qs, axis=0), ks_ref[k0:k0 + SEL_CHUNK, :]),)
        s_a = causal(_dot_nt(qs[0], ks_ref[k0:k0 + half, :]))
        s_b = _dot_nt(qs[1], ks_ref[k0:k0 + SEL_CHUNK, :])
        return (s_a, jnp.concatenate([s_b[:, :half], causal(s_b[:, half:])], axis=1))

    def softmax_step(s, v, carry):
        if carry is None:
            m = jnp.max(s, axis=-1, keepdims=True)
            return m, jnp.dot(jnp.exp2(s - m).astype(bf16), v, preferred_element_type=f32)
        return _softmax_step(s, v, carry)

    def finish(i, acc):
        tok = slice(i * SEL_CHUNK, (i + 1) * SEL_CHUNK)
        o_s = acc / jnp.maximum(acc[:, HEAD_DIM:HEAD_DIM + 1], 1e-30)
        sig = gate_ref[tok, :].astype(f32)
        vals = []
        for r in range(GROUP):
            a_s = jnp.concatenate([o_s[h * half_rows + r * half:h * half_rows + (r + 1) * half, :]
                                   for h in range(2)], axis=0) * sig[:, 3 * r + 1:3 * r + 2]
            vals.append(a_s if r % 2 == 0 else pltpu.roll(a_s, HEAD_DIM, 1))
        c0, c1 = _merge_heads(vals, lane)
        o_ref[tok, 0:LANES] = (ocw_ref[tok, 0:LANES] + c0).astype(bf16)
        o_ref[tok, LANES:2 * LANES] = (ocw_ref[tok, LANES:2 * LANES] + c1).astype(bf16)

    steps = [(i, c) for i in range(seq // SEL_CHUNK) for c in range(i + 1)]
    qs = stacked_q(0)
    s_next = scores(qs, 0, 0)
    carry = None
    for k, (i, c) in enumerate(steps):
        s = s_next
        if k + 1 < len(steps):
            i2, c2 = steps[k + 1]
            if i2 != i:
                qs = stacked_q(i2)
            s_next = scores(qs, i2, c2)
        k0 = c * SEL_CHUNK
        if c < i:
            carry = softmax_step(s[0], vs_ref[k0:k0 + SEL_CHUNK, :], carry)
        else:
            rows_a = None if carry is None else tuple(a[:half_rows] for a in carry)
            rows_b = None if carry is None else tuple(a[half_rows:] for a in carry)
            _, acc_a = softmax_step(s[0], vs_ref[k0:k0 + half, :], rows_a)
            _, acc_b = softmax_step(s[1], vs_ref[k0:k0 + SEL_CHUNK, :], rows_b)
            finish(i, jnp.concatenate([acc_a, acc_b], axis=0))
            carry = None


def _position_lanes(pos):
    out = np.zeros((pos.shape[0], LANES), np.float32)
    out[:, L_HI:L_HI + 3] = ((pos // 64) * 64)[:, None]
    out[:, L_LO:L_LO + 3] = (pos % 64)[:, None]
    return out


def _nsa_tables(seq):
    pos = np.arange(seq)
    kaug_s = _position_lanes(pos)
    kaug_s[pos, L_SEL + pos // SLC_LEN] = 1.0
    kaug_w = np.zeros((seq + WIN_PAD, LANES), np.float32)
    kaug_w[WIN_PAD:] = _position_lanes(pos)
    kaug_w[:WIN_PAD, L_FLAG] = 1.0
    kaug_c = _position_lanes(np.arange(N_CMP_PAD) * CMP_STRIDE + CMP_LEN - 1)
    tl = np.arange(SEL_CHUNK // 2)[:, None]
    kk = np.arange(SEL_CHUNK // 2)[None, :]
    tri = np.where(kk <= tl, 0.0, -MASK_BIG).astype(np.float32)
    tl = np.arange(WIN_TQ)[:, None]
    kk = np.arange(WIN_KEYS)[None, :]
    dist = tl + WIN_PAD - kk
    band = np.tile(np.where((dist >= 0) & (dist < WINDOW), 0.0, -MASK_BIG).astype(np.float32), (GROUP, 1))
    c = np.arange(N_CMP_PAD)[None, :]
    j = np.arange(seq // SLC_LEN)[:, None]
    c_start, c_end, s_start = c * CMP_STRIDE, c * CMP_STRIDE + CMP_LEN - 1, j * SLC_LEN
    ovl = ((c_start <= s_start + SLC_LEN - 1) & (c_end >= s_start) & (c < seq // CMP_STRIDE - 1)).astype(np.float32)
    cend = np.tile((c * CMP_STRIDE + CMP_LEN - 1) - tl, (GROUP, 1)).astype(np.int32)
    return (jnp.asarray(kaug_s, bf16), jnp.asarray(kaug_w, bf16), jnp.asarray(kaug_c, f32),
            jnp.asarray(tri), jnp.asarray(band), jnp.asarray(ovl, bf16), jnp.asarray(cend))


def _query_aug():
    slopes = jnp.exp2(-8.0 * jnp.arange(1, N_HEADS + 1, dtype=f32) / N_HEADS).reshape(N_KV, GROUP) * LOG2E
    hi =slopes.astype(bf16).astype(f32)
    mid = (slopes - hi).astype(bf16).astype(f32)
    lo = (slopes - hi - mid).astype(bf16).astype(f32)
    pieces = jnp.stack([hi, mid, lo, hi, mid, lo], axis=-1)
    base = jnp.zeros((N_KV, GROUP, LANES), f32).at[:, :, L_HI:L_HI + 6].set(pieces)
    return jnp.concatenate([base, base.at[:, :, L_FLAG].set(-MASK_BIG)], axis=1)


def _nsa(proj, wlo, whi, pelo, pehi, w2bd, batch, seq):
    kaug_s, kaug_w, kaug_c, tri, band, ovl, cend = _nsa_tables(seq)
    qaug = _query_aug()
    const2 = lambda b, g: (0, 0)
    in_specs = [
        pl.BlockSpec((seq, 2 * LANES), lambda b, g: (b, OFF_Q // (2 * LANES) + g)),
        pl.BlockSpec((seq, LANES), lambda b, g: (b, OFF_KSW // LANES + g)),
        pl.BlockSpec((seq, LANES), lambda b, g: (b, OFF_VSW // LANES + g)),
        pl.BlockSpec((seq, LANES), lambda b, g: (b, OFF_GATE // LANES + g)),
        pl.BlockSpec((seq, LANES), lambda b, g: (b, OFF_KVC // LANES + g)),
        pl.BlockSpec(wlo.shape, const2),
        pl.BlockSpec(whi.shape, const2),
        pl.BlockSpec(pelo.shape, const2),
        pl.BlockSpec(pehi.shape, const2),
        pl.BlockSpec(w2bd.shape, const2),
        pl.BlockSpec(kaug_s.shape, const2),
        pl.BlockSpec(kaug_w.shape, const2),
        pl.BlockSpec(kaug_c.shape, const2),
        pl.BlockSpec((1, 2 * GROUP, LANES), lambda b, g: (g, 0, 0)),
        pl.BlockSpec(tri.shape, const2),
        pl.BlockSpec(band.shape, const2),
        pl.BlockSpec(ovl.shape, const2),
        pl.BlockSpec(cend.shape, const2),
    ]
    return pl.pallas_call(
        _nsa_kernel,
        grid=(batch, N_KV),
        in_specs=in_specs,
        out_specs=pl.BlockSpec((seq, 2 * LANES), lambda b, g: (b, g)),
        out_shape=jax.ShapeDtypeStruct((batch * seq, D_ATT), bf16),
        scratch_shapes=[
            pltpu.VMEM((seq, LANES), f32),
            pltpu.VMEM((seq, LANES), bf16),
            pltpu.VMEM((seq + WIN_PAD, LANES), bf16),
            pltpu.VMEM((seq, LANES), bf16),
            pltpu.VMEM((seq + WIN_PAD, LANES), bf16),
            pltpu.VMEM((seq, 2 * LANES), f32),
            pltpu.VMEM((seq // LANES, seq // SLC_LEN, LANES), f32),
            pltpu.VMEM((seq // SLC_LEN, seq // LANES, LANES), f32),
            pltpu.VMEM((seq, LANES), f32),
        ],
        compiler_params=pltpu.CompilerParams(
            dimension_semantics=("parallel", "parallel"), vmem_limit_bytes=VMEM_LIMIT),
        name="nsa",
    )(proj, proj, proj, proj, proj, wlo, whi, pelo, pehi, w2bd, kaug_s, kaug_w, kaug_c, qaug, tri, band, ovl, cend)


def _out_kernel(x_ref, ya_ref, zb_ref, gm_ref, ob_ref, wa_ref, wb_ref, wo_ref, fnw_ref, o_ref):
    y_a = jnp.dot(ya_ref[...], wa_ref[...], preferred_element_type=f32)
    yb_in = ob_ref[...].astype(f32) * zb_ref[...].astype(f32)
    y_b = jnp.dot(yb_in.astype(bf16), wb_ref[...], preferred_element_type=f32)
    mixed = (gm_ref[:, 0:D_MODEL].astype(f32) * y_a + gm_ref[:, D_MODEL:2 * D_MODEL].astype(f32) * y_b)
    xo = x_ref[...] + jnp.dot(mixed.astype(bf16), wo_ref[...], preferred_element_type=f32)
    r = lax.rsqrt(jnp.mean(xo * xo, axis=-1, keepdims=True) + NORM_EPS)
    o_ref[...] = (xo * r) * fnw_ref[...]


def _out(x2, ya, proj, ob, wa, wb, wo, fnw, tm=512):
    n_rows = x2.shape[0]
    blk = lambda c: pl.BlockSpec((tm, D_MODEL), lambda i, c=c: (i, c))
    full = lambda a: pl.BlockSpec(a.shape, lambda i: (0, 0))
    return pl.pallas_call(
        _out_kernel,
        grid=(n_rows // tm,),
        in_specs=[
            blk(0),
            blk(0),
            blk(OFF_ZB // D_MODEL),
            pl.BlockSpec((tm, 2 * D_MODEL), lambda i: (i, OFF_GMIX // (2 * D_MODEL))),
            blk(0),
            full(wa), full(wb), full(wo), full(fnw),
        ],
        out_specs=pl.BlockSpec((tm, D_MODEL), lambda i: (i, 0)),
        out_shape=jax.ShapeDtypeStruct((n_rows, D_MODEL), f32),
        compiler_params=pltpu.CompilerParams(
            dimension_semantics=("parallel",), vmem_limit_bytes=VMEM_LIMIT),
        name="merge_out",
    )(x2, ya, proj, proj, ob, wa, wb, wo, fnw)


def _blockdiag_w1(w1_k, w1_v, lo):
    half = CMP_STRIDE * HEAD_DIM
    wk = w1_k[lo * half:(lo + 1) * half].reshape(CMP_STRIDE, HEAD_DIM, CMP_HID)
    wv = w1_v[lo * half:(lo + 1) * half].reshape(CMP_STRIDE, HEAD_DIM, CMP_HID)
    z = jnp.zeros_like(wk)
    top = jnp.concatenate([wk, z], axis=-1)
    bot = jnp.concatenate([z, wv], axis=-1)
    return jnp.concatenate([top, bot], axis=1).reshape(CMP_STRIDE * 2 * HEAD_DIM, 2 * CMP_HID).astype(bf16)


def _pe_row(pe_k, pe_v, lo):
    sl = slice(lo * CMP_STRIDE, (lo + 1) * CMP_STRIDE)
    return jnp.concatenate([pe_k[sl], pe_v[sl]], axis=-1).reshape(1, CMP_STRIDE * 2 * HEAD_DIM)


def kernel(x, norm_w, w_in, conv_w, conv_b, cmp_pe_k, cmp_pe_v, cmp_w1_k, cmp_w2_k, cmp_w1_v, cmp_w2_v,
           w_proj_a, w_proj_b, w_out, final_norm_w):
    batch, seq, _ = x.shape
    assert norm_w.shape[0] == 1 and seq % SEL_CHUNK == 0 and seq // CMP_STRIDE == N_CMP_PAD
    w_p = _reordered_w_in(w_in[0])
    x2 = x.reshape(batch * seq, D_MODEL)
    proj, ya = _inproj(x2, norm_w, w_p, conv_w[0], conv_b, seq)

    wlo = _blockdiag_w1(cmp_w1_k[0], cmp_w1_v[0], 0)
    whi = _blockdiag_w1(cmp_w1_k[0], cmp_w1_v[0], 1)
    pelo = _pe_row(cmp_pe_k[0], cmp_pe_v[0], 0)
    pehi = _pe_row(cmp_pe_k[0], cmp_pe_v[0], 1)
    zk = jnp.zeros_like(cmp_w2_k[0])
    w2bd = jnp.concatenate([jnp.concatenate([cmp_w2_k[0], zk], axis=1),
                            jnp.concatenate([zk, cmp_w2_v[0]], axis=1)], axis=0).astype(bf16)
    ob = _nsa(proj, wlo, whi, pelo, pehi, w2bd, batch, seq)

    out = _out(x2, ya, proj, ob, w_proj_a[0].astype(bf16), w_proj_b[0].astype(bf16),
               w_out[0].astype(bf16), final_norm_w.reshape(1, D_MODEL))
    return out.reshape(batch, seq, D_MODEL)
```

```python
import functools

import numpy as np
import jax
import jax.numpy as jnp
from jax import lax
from jax.experimental import pallas as pl
from jax.experimental.pallas import tpu as pltpu

D_MODEL = 1024
D_CONV = 1024
CONV_W = 3
N_HEADS = 16
HEAD_DIM = 64
N_KV = 4
GROUP = N_HEADS // N_KV
D_ATT = N_HEADS * HEAD_DIM
D_KV = N_KV * HEAD_DIM
CMP_LEN = 32
CMP_STRIDE = 16
CMP_HID = 128
SLC_LEN = 64
N_SEL = 8
WINDOW = 512
NORM_EPS = 1e-6

LANES = 128
N_CMP_PAD = 128
MASK_BIG = float(2.0 ** 100)
LOG2E = float(np.log2(np.e))
SEL_CHUNK = 512
WIN_TQ = 128
WIN_KEYS = WINDOW + WIN_TQ
WIN_PAD = WINDOW
CW_TQ = 512

OFF_Q = 0
OFF_ZB = 1024
OFF_GMIX = 2048
OFF_KVC = 4096
OFF_KSW = 4608
OFF_VSW = 5120
OFF_GATE = 5632
N_PROJ = 6144

L_HI = 64
L_LO = 67
L_FLAG = 70
L_SEL = 96

VMEM_LIMIT = 56 * 1024 * 1024

f32 = jnp.float32
bf16 = jnp.bfloat16


W_Q = 4 * D_CONV
W_KV = W_Q + D_ATT
W_GATE = W_KV + 6 * D_KV
W_ZB = W_GATE + 3 * N_HEADS
N_IN = W_ZB + D_ATT + 2 * D_MODEL


WP_ROWS = 512
WP_T = 256


def _wprep_kernel(a_ref, b_ref, o_ref):
    j = pl.program_id(0)
    n_plain = W_KV // WP_ROWS
    n_shift = (N_IN - W_ZB) // WP_ROWS
    off = W_ZB % WP_ROWS
    eye = (lax.broadcasted_iota(jnp.int32, (WP_T, WP_T), 0)
           == lax.broadcasted_iota(jnp.int32, (WP_T, WP_T), 1)).astype(bf16)

    def transposed(rows_bf16, kb):
        return _dot_nt(eye, rows_bf16[:, kb * WP_T:(kb + 1) * WP_T])

    def emit(rows):
        rows = rows.astype(bf16)
        for kb in range(D_MODEL // WP_T):
            o_ref[kb * WP_T:(kb + 1) * WP_T, :] = transposed(rows, kb).astype(bf16)

    def interleave(x, x_off, y, y_off):
        parts = []
        for g in range(N_KV):
            parts += [x[x_off + g * HEAD_DIM:x_off + (g + 1) * HEAD_DIM], y[y_off + g * HEAD_DIM:y_off + (g + 1) * HEAD_DIM]]
        return jnp.concatenate(parts, axis=0)

    @pl.when(j < W_Q // WP_ROWS)
    def _():
        emit(a_ref[...])

    @pl.when((j >= W_Q // WP_ROWS) & (j < n_plain))
    def _():
        emit(a_ref[...] * (HEAD_DIM ** -0.5 * LOG2E))

    @pl.when((j >= n_plain) & (j < n_plain + n_shift))
    def _():
        emit(jnp.concatenate([a_ref[off:, :], b_ref[:off, :]], axis=0))

    @pl.when(j == n_plain + n_shift)
    def _():
        a = a_ref[...]
        emit(interleave(a, 0, a, D_KV))

    @pl.when(j == n_plain + n_shift + 1)
    def _():
        emit(interleave(a_ref[...], 0, b_ref[...], 0))

    @pl.when(j == n_plain + n_shift + 2)
    def _():
        emit(interleave(a_ref[...], D_KV, b_ref[...], D_KV))

    @pl.when(j == n_plain + n_shift + 3)
    def _():
        rows = a_ref[:LANES, :].astype(bf16)
        lane = lax.broadcasted_iota(jnp.int32, (1, LANES), 1)
        for kb in range(D_MODEL // WP_T):
            t = transposed(rows, kb)
            for g in range(N_KV):
                shifted = t if g == 0 else pltpu.roll(t, LANES - 3 * GROUP * g, 1)
                o_ref[kb * WP_T:(kb + 1) * WP_T, g * LANES:(g + 1) * LANES] = (
                    jnp.where(lane < 3 * GROUP, shifted, 0.0).astype(bf16))


def _reordered_w_in(w):
    assert w.shape == (D_MODEL, N_IN) and W_KV % WP_ROWS == 0 and (N_IN - W_ZB) % WP_ROWS == 0
    assert W_GATE % WP_ROWS == 0 and 6 * D_KV == 3 * WP_ROWS and WP_ROWS == N_KV * LANES
    n_plain, n_shift = W_KV // WP_ROWS, (N_IN - W_ZB) // WP_ROWS
    first_shift = W_ZB // WP_ROWS
    kv0 = W_KV // WP_ROWS
    n_out = 4 * D_CONV + N_PROJ
    n_steps = n_out // WP_ROWS

    def block_a(j):
        shifted = first_shift + (j - n_plain)
        tail = jnp.where(j == n_plain + n_shift, kv0, jnp.where(j == n_plain + n_shift + 3, W_GATE // WP_ROWS, kv0 + 1))
        return jnp.where(j < n_plain, j, jnp.where(j < n_plain + n_shift, shifted, tail))

    def block_b(j):
        shifted = jnp.clip(first_shift + 1 + (j - n_plain), first_shift + 1, first_shift + n_shift)
        return jnp.where(j < n_plain + n_shift + 1, shifted, kv0 + 2)

    return pl.pallas_call(
        _wprep_kernel,
        grid=(n_steps,),
        in_specs=[pl.BlockSpec((WP_ROWS, D_MODEL), lambda j: (block_a(j), 0)),
                  pl.BlockSpec((WP_ROWS, D_MODEL), lambda j: (block_b(j), 0))],
        out_specs=pl.BlockSpec((D_MODEL, WP_ROWS), lambda j: (0, j)),
        out_shape=jax.ShapeDtypeStruct((D_MODEL, n_out), bf16),
        compiler_params=pltpu.CompilerParams(dimension_semantics=("parallel",), vmem_limit_bytes=VMEM_LIMIT),
        name="wprep",
    )(w.T, w.T)


def _sigmoid(x):
    return 1.0 / (1.0 + jnp.exp(-x))


def _dot_nt(a, b):
    return lax.dot_general(a, b, (((1,), (1,)), ((), ())), preferred_element_type=f32)


CARRY_ROWS = 8
ROW_BLOCK = 256


def _column_activations(lo, hi):
    silu = lambda a: a * _sigmoid(a)
    keep = lambda a: a
    segments = ((OFF_Q, OFF_ZB, keep), (OFF_ZB, OFF_GMIX, silu), (OFF_GMIX, OFF_KVC, _sigmoid),
                (OFF_KVC, OFF_GATE, keep), (OFF_GATE, N_PROJ, _sigmoid))
    return [(max(a, lo) - lo, min(b, hi) - lo, f) for a, b, f in segments if max(a, lo) < min(b, hi)]


def _inproj_kernel(x_ref, nw_ref, w1_ref, w2_ref, cw_ref, cb_ref, o_ref, ya_ref, h_ref, carry_ref,
                   *, tiles_per_seq):
    i = pl.program_id(0)
    j = pl.program_id(1)
    tm = x_ref.shape[0]
    tn = o_ref.shape[1]
    n_stored = N_PROJ // tn
    blocks = [slice(k * ROW_BLOCK, (k + 1) * ROW_BLOCK) for k in range(tm // ROW_BLOCK)]
    product = lambda rows, w: jnp.dot(h_ref[rows, :], w, preferred_element_type=f32)

    @pl.when((i == 0) & (j == 0))
    def _():
        carry_ref[...] = jnp.zeros_like(carry_ref)

    for t in range(n_stored):
        @pl.when(j == t)
        def _(t=t):
            for rows in blocks:
                if t == 0:
                    x = x_ref[rows, :]
                    r = lax.rsqrt(jnp.mean(x * x, axis=-1, keepdims=True) + NORM_EPS)
                    h_ref[rows, :] = ((x * r) * nw_ref[...]).astype(bf16)
                acc = product(rows, w1_ref[...])
                for lo, hi, act in _column_activations(t * tn, (t + 1) * tn):
                    o_ref[rows, lo:hi] = act(acc[:, lo:hi]).astype(bf16)

    @pl.when(j == n_stored)
    def _():
        lead = carry_ref[...] * (i % tiles_per_seq != 0).astype(f32)
        cw = cw_ref[...]
        for rows in blocks:
            u = product(rows, w2_ref[:, :D_CONV]) * product(rows, w1_ref[:, :D_CONV])
            win = jnp.concatenate([lead, u], axis=0)
            u1 = pltpu.roll(win, 1, 0)[CARRY_ROWS:, :]
            u2 = pltpu.roll(win, 2, 0)[CARRY_ROWS:, :]
            y = cb_ref[...] + cw[0:1, :] * u2 + cw[1:2, :] * u1 + cw[2:3, :] * u
            z = product(rows, w2_ref[:, D_CONV:])
            ya_ref[rows, :] = (product(rows, w1_ref[:, D_CONV:]) * y * (z * _sigmoid(z))).astype(bf16)
            lead = u[ROW_BLOCK - CARRY_ROWS:, :]
        carry_ref[...] = lead


def _inproj(x2, norm_w, w_p, conv_w, conv_b, seq, tm=1024):
    n_rows = x2.shape[0]
    tn = 2 * D_CONV
    assert N_PROJ % tn == 0 and seq % tm == 0 and tm % ROW_BLOCK == 0 and w_p.shape[1] == 2 * tn + N_PROJ
    n_stored = N_PROJ // tn
    first = lambda i, j: (0, jnp.where(j < n_stored, j + 2, 0))
    second = lambda i, j: (0, 1)
    return pl.pallas_call(
        functools.partial(_inproj_kernel, tiles_per_seq=seq // tm),
        grid=(n_rows // tm, n_stored + 1),
        in_specs=[
            pl.BlockSpec((tm, D_MODEL), lambda i, j: (i, 0)),
            pl.BlockSpec((1, D_MODEL), lambda i, j: (0, 0)),
            pl.BlockSpec((D_MODEL, tn), first),
            pl.BlockSpec((D_MODEL, tn), second),
            pl.BlockSpec(conv_w.shape, lambda i, j: (0, 0)),
            pl.BlockSpec(conv_b.shape, lambda i, j: (0, 0)),
        ],
        out_specs=[
            pl.BlockSpec((tm, tn), lambda i, j: (i, jnp.minimum(j, n_stored - 1))),
            pl.BlockSpec((tm, D_CONV), lambda i, j: (i, 0)),
        ],
        out_shape=[jax.ShapeDtypeStruct((n_rows, N_PROJ), bf16), jax.ShapeDtypeStruct((n_rows, D_CONV), bf16)],
        scratch_shapes=[
            pltpu.VMEM((tm, D_MODEL), bf16),
            pltpu.VMEM((CARRY_ROWS, D_CONV), f32),
        ],
        compiler_params=pltpu.CompilerParams(
            dimension_semantics=("arbitrary", "arbitrary"), vmem_limit_bytes=VMEM_LIMIT),
        name="inproj",
    )(x2, norm_w, w_p, w_p, conv_w, conv_b)


def _softmax_step(s, v, carry):
    m, acc = carry
    m_new = jnp.maximum(m, jnp.max(s, axis=-1, keepdims=True))
    alpha = jnp.exp2(m - m_new)
    p = jnp.exp2(s - m_new)
    acc = alpha * acc + jnp.dot(p.astype(bf16), v, preferred_element_type=f32)
    return m_new, acc


def _stack_heads(q_tile, aug_rows, extra):
    lane = lax.broadcasted_iota(jnp.int32, (1, LANES), 1)
    pairs = (q_tile[:, :LANES], q_tile[:, LANES:])
    parts = []
    for r in range(GROUP):
        qh = pairs[r // 2]
        if r % 2 == 1:
            qh = pltpu.roll(qh, HEAD_DIM, 1)
        aug = aug_rows[r:r + 1, :]
        if extra is not None:
            aug = aug + extra
        parts.append(jnp.where(lane < HEAD_DIM, qh, aug).astype(bf16))
    return jnp.concatenate(parts, axis=0)


def _merge_heads(vals, lane):
    return (jnp.where(lane < HEAD_DIM, vals[0], vals[1]), jnp.where(lane < HEAD_DIM, vals[2], vals[3]))


def _nsa_kernel(q_ref, ksw_ref, vsw_ref, gate_ref, kvc_ref, wlo_ref, whi_ref, pelo_ref, pehi_ref, w2_ref,
                kaug_s_ref, kaug_w_ref, kaug_c_ref, qaug_ref, tri_ref, band_ref, ovl_ref, cend_ref,
                o_ref,
                kvcf_ref, ks_ref, kw_ref, vs_ref, vw_ref, ocw_ref, imp_ref, ns_ref, selq_ref):
    seq = q_ref.shape[0]
    lane = lax.broadcasted_iota(jnp.int32, (1, LANES), 1)
    low = lane < HEAD_DIM

    kvcf_ref[...] = kvc_ref[...].astype(f32)
    xcat = jnp.concatenate([kvcf_ref[pl.ds(l, N_CMP_PAD, stride=CMP_STRIDE), :] for l in range(CMP_STRIDE)],
                           axis=1)
    xa = (xcat + pelo_ref[...]).astype(bf16)
    xb = (xcat + pehi_ref[...]).astype(bf16)
    a = jnp.dot(xa, wlo_ref[...], preferred_element_type=f32)
    bm = jnp.dot(xb, whi_ref[...], preferred_element_type=f32)
    h = a + pltpu.roll(bm, N_CMP_PAD - 1, 0)
    hid = h * _sigmoid(h)
    kcvc = jnp.dot(hid.astype(bf16), w2_ref[...], preferred_element_type=f32)

    ksw = ksw_ref[...]
    vsw = vsw_ref[...]
    one = jnp.ones((1, LANES), bf16)
    ks_ref[...] = jnp.where(low, ksw, kaug_s_ref[...])
    kwin = pltpu.roll(ksw.astype(f32), HEAD_DIM, 1).astype(bf16)
    kw_ref[0:WIN_PAD, :] = kaug_w_ref[0:WIN_PAD, :]
    kw_ref[WIN_PAD:, :] = jnp.where(low, kwin, kaug_w_ref[WIN_PAD:, :])
    vs_ref[...] = jnp.where(low, vsw, one)
    vw_ref[0:WIN_PAD, :] = jnp.zeros((WIN_PAD, LANES), bf16)
    vw_ref[WIN_PAD:, :] = jnp.where(low, one, vsw)

    qaug = qaug_ref[0]
    kc_aug = jnp.where(low, kcvc, kaug_c_ref[...]).astype(bf16)
    vc_rows = jnp.concatenate([jnp.zeros((N_CMP_PAD, LANES), bf16), kcvc.astype(bf16)], axis=1)
    zeros_w = jnp.zeros((WIN_KEYS, LANES), bf16)
    ovl = ovl_ref[...]

    def cw_body(i, _):
        t0 = pl.multiple_of(i * CW_TQ, CW_TQ)
        sig = gate_ref[pl.ds(t0, CW_TQ), :].astype(f32)
        o_w, o_c = [], []
        n_tiles = CW_TQ // WIN_TQ

        def tile_scores(h):
            tw = pl.multiple_of(t0 + h * WIN_TQ, WIN_TQ)
            qw = _stack_heads(q_ref[pl.ds(tw, WIN_TQ), :].astype(f32), qaug[GROUP:2 * GROUP], None)
            k_all = jnp.concatenate([kw_ref[pl.ds(tw, WIN_KEYS), :], kc_aug], axis=0)
            return _dot_nt(qw, k_all)

        s_next = tile_scores(0)
        for h in range(n_tiles):
            tw = pl.multiple_of(t0 + h * WIN_TQ, WIN_TQ)
            s = s_next
            if h + 1 < n_tiles:
                s_next = tile_scores(h + 1)

            sw = jnp.concatenate([s[:, :WIN_TQ] + band_ref[:, :WIN_TQ], s[:, WIN_TQ:WINDOW],
                                  s[:, WINDOW:WIN_KEYS] + band_ref[:, WINDOW:]], axis=1)
            p_w = jnp.exp2(sw - jnp.max(sw, axis=-1, keepdims=True))

            ok = cend_ref[...] <= tw
            sm = jnp.where(ok, s[:, WIN_KEYS:], -jnp.inf)
            m = jnp.max(sm, axis=-1, keepdims=True)
            m = jnp.where(m > -jnp.inf, m, 0.0)
            e = jnp.where(ok, jnp.exp2(sm - m), 0.0)
            p_c = e / jnp.maximum(jnp.sum(e, axis=-1, keepdims=True), 1e-30)

            p_all = jnp.concatenate([p_w.astype(bf16), p_c.astype(bf16)], axis=1)
            v_all = jnp.concatenate(
                [jnp.concatenate([vw_ref[pl.ds(tw, WIN_KEYS), :], zeros_w], axis=1), vc_rows], axis=0)
            acc = jnp.dot(p_all, v_all, preferred_element_type=f32)
            acc_w = acc[:, :LANES]
            o_w.append(acc_w / jnp.maximum(acc_w[:, 0:1], 1e-30))
            o_c.append(acc[:, LANES:])

            ps = p_c[0:WIN_TQ] + p_c[WIN_TQ:2 * WIN_TQ] + p_c[2 * WIN_TQ:3 * WIN_TQ] + p_c[3 * WIN_TQ:4 * WIN_TQ]
            p1 = ps.astype(bf16)
            r1 = ps - p1.astype(f32)
            p2 = r1.astype(bf16)
            p3 = (r1 - p2.astype(f32)).astype(bf16)
            imp_ref[i * (CW_TQ // WIN_TQ) + h] = _dot_nt(ovl, p1) + _dot_nt(ovl, p2) + _dot_nt(ovl, p3)

        vals = []
        for r in range(GROUP):
            head = lambda parts: jnp.concatenate([o[r * WIN_TQ:(r + 1) * WIN_TQ, :] for o in parts], axis=0)
            a = head(o_c) * sig[:, 3 * r:3 * r + 1] + head(o_w) * sig[:, 3 * r + 2:3 * r + 3]
            vals.append(pltpu.roll(a, HEAD_DIM, 1) if r % 2 == 0 else a)
        c0, c1 = _merge_heads(vals, lane)
        ocw_ref[pl.ds(t0, CW_TQ), 0:LANES] = c0
        ocw_ref[pl.ds(t0, CW_TQ), LANES:2 * LANES] = c1
        return 0

    lax.fori_loop(0, seq // CW_TQ, cw_body, 0)

    n_slc = seq // SLC_LEN
    n_tt = seq // LANES
    shp = (n_tt, LANES)
    cur = (lax.broadcasted_iota(jnp.int32, shp, 0) * LANES + lax.broadcasted_iota(jnp.int32, shp, 1)) // SLC_LEN
    vals = []
    for j in range(n_slc):
        forced = (cur == j) | (cur == j + 1)
        vj = jnp.where(cur >= j, imp_ref[:, j, :], -jnp.inf)
        vals.append(jnp.full(shp, jnp.inf, f32) if j == 0 else jnp.where(forced, jnp.inf, vj))
    wins = [jnp.zeros(shp, f32) for _ in range(n_slc)]
    losses = [jnp.zeros(shp, f32) for _ in range(n_slc)]
    for a in range(n_slc):
        for b in range(a + 1, n_slc):
            a_first = jnp.where(vals[a] >= vals[b], 1.0, 0.0)
            wins[a] = wins[a] + a_first
            losses[b] = losses[b] + a_first
    for j in range(n_slc):
        rank = losses[j] + (float(n_slc - 1 - j) - wins[j])
        sel = (rank < float(N_SEL)) & (vals[j] > -jnp.inf)
        ns_ref[j] = jnp.where(sel, 0.0, 1.0)
    eye = (lax.broadcasted_iota(jnp.int32, (LANES, LANES), 0)
           == lax.broadcasted_iota(jnp.int32, (LANES, LANES), 1)).astype(bf16)
    zpad = jnp.zeros((L_SEL, LANES), f32)
    for tt in range(n_tt):
        z = jnp.concatenate([zpad, ns_ref[:, tt, :]], axis=0).astype(bf16)
        selq_ref[tt * LANES:(tt + 1) * LANES, :] = _dot_nt(eye, z) * (-MASK_BIG)

    half = SEL_CHUNK // 2
    half_rows = GROUP * half

    def stacked_q(i):
        parts = []
        for h in range(2):
            tok = slice(i * SEL_CHUNK + h * half, i * SEL_CHUNK + (h + 1) * half)
            parts.append(_stack_heads(q_ref[tok, :].astype(f32), qaug[0:GROUP], selq_ref[tok, :]))
        return parts

    def causal(s):
        tri = tri_ref[...]
        return jnp.concatenate([s[r * half:(r + 1) * half, :] + tri for r in range(GROUP)], axis=0)

    def scores(qs, i, c):
        k0 = c * SEL_CHUNK
        if c < i:
            return (_dot_nt(jnp.concatenate(qs, axis=0), ks_ref[k0:k0 + SEL_CHUNK, :]),)
        s_a = causal(_dot_nt(qs[0], ks_ref[k0:k0 + half, :]))
        s_b = _dot_nt(qs[1], ks_ref[k0:k0 + SEL_CHUNK, :])
        return (s_a, jnp.concatenate([s_b[:, :half], causal(s_b[:, half:])], axis=1))

    def softmax_step(s, v, carry):
        if carry is None:
            m = jnp.max(s, axis=-1, keepdims=True)
            return m, jnp.dot(jnp.exp2(s - m).astype(bf16), v, preferred_element_type=f32)
        return _softmax_step(s, v, carry)

    def finish(i, acc):
        tok = slice(i * SEL_CHUNK, (i + 1) * SEL_CHUNK)
        o_s = acc / jnp.maximum(acc[:, HEAD_DIM:HEAD_DIM + 1], 1e-30)
        sig = gate_ref[tok, :].astype(f32)
        vals = []
        for r in range(GROUP):
            a_s = jnp.concatenate([o_s[h * half_rows + r * half:h * half_rows + (r + 1) * half, :]
                                   for h in range(2)], axis=0) * sig[:, 3 * r + 1:3 * r + 2]
            vals.append(a_s if r % 2 == 0 else pltpu.roll(a_s, HEAD_DIM, 1))
        c0, c1 = _merge_heads(vals, lane)
        o_ref[tok, 0:LANES] = (ocw_ref[tok, 0:LANES] + c0).astype(bf16)
        o_ref[tok, LANES:2 * LANES] = (ocw_ref[tok, LANES:2 * LANES] + c1).astype(bf16)

    steps = [(i, c) for i in range(seq // SEL_CHUNK) for c in range(i + 1)]
    qs = stacked_q(0)
    s_next = scores(qs, 0, 0)
    carry = None
    for k, (i, c) in enumerate(steps):
        s = s_next
        if k + 1 < len(steps):
            i2, c2 = steps[k + 1]
            if i2 != i:
                qs = stacked_q(i2)
            s_next = scores(qs, i2, c2)
        k0 = c * SEL_CHUNK
        if c < i:
            carry = softmax_step(s[0], vs_ref[k0:k0 + SEL_CHUNK, :], carry)
        else:
            rows_a = None if carry is None else tuple(a[:half_rows] for a in carry)
            rows_b = None if carry is None else tuple(a[half_rows:] for a in carry)
            _, acc_a = softmax_step(s[0], vs_ref[k0:k0 + half, :], rows_a)
            _, acc_b = softmax_step(s[1], vs_ref[k0:k0 + SEL_CHUNK, :], rows_b)
            finish(i, jnp.concatenate([acc_a, acc_b], axis=0))
            carry = None


def _position_lanes(pos):
    out = np.zeros((pos.shape[0], LANES), np.float32)
    out[:, L_HI:L_HI + 3] = ((pos // 64) * 64)[:, None]
    out[:, L_LO:L_LO + 3] = (pos % 64)[:, None]
    return out


def _nsa_tables(seq):
    pos = np.arange(seq)
    kaug_s = _position_lanes(pos)
    kaug_s[pos, L_SEL + pos // SLC_LEN] = 1.0
    kaug_w = np.zeros((seq + WIN_PAD, LANES), np.float32)
    kaug_w[WIN_PAD:] = _position_lanes(pos)
    kaug_w[:WIN_PAD, L_FLAG] = 1.0
    kaug_c = _position_lanes(np.arange(N_CMP_PAD) * CMP_STRIDE + CMP_LEN - 1)
    tl = np.arange(SEL_CHUNK // 2)[:, None]
    kk = np.arange(SEL_CHUNK // 2)[None, :]
    tri = np.where(kk <= tl, 0.0, -MASK_BIG).astype(np.float32)
    tl = np.arange(WIN_TQ)[:, None]
    kk = np.arange(WIN_KEYS)[None, :]
    dist = tl + WIN_PAD - kk
    band = np.tile(np.where((dist >= 0) & (dist < WINDOW), 0.0, -MASK_BIG).astype(np.float32), (GROUP, 1))
    c = np.arange(N_CMP_PAD)[None, :]
    j = np.arange(seq // SLC_LEN)[:, None]
    c_start, c_end, s_start = c * CMP_STRIDE, c * CMP_STRIDE + CMP_LEN - 1, j * SLC_LEN
    ovl = ((c_start <= s_start + SLC_LEN - 1) & (c_end >= s_start) & (c < seq // CMP_STRIDE - 1)).astype(np.float32)
    cend = np.tile((c * CMP_STRIDE + CMP_LEN - 1) - tl, (GROUP, 1)).astype(np.int32)
    return (jnp.asarray(kaug_s, bf16), jnp.asarray(kaug_w, bf16), jnp.asarray(kaug_c, f32),
            jnp.asarray(tri), jnp.asarray(band), jnp.asarray(ovl, bf16), jnp.asarray(cend))


def _query_aug():
    slopes = jnp.exp2(-8.0 * jnp.arange(1, N_HEADS + 1, dtype=f32) / N_HEADS).reshape(N_KV, GROUP) * LOG2E
    hi =slopes.astype(bf16).astype(f32)
    mid = (slopes - hi).astype(bf16).astype(f32)
    lo = (slopes - hi - mid).astype(bf16).astype(f32)
    pieces = jnp.stack([hi, mid, lo, hi, mid, lo], axis=-1)
    base = jnp.zeros((N_KV, GROUP, LANES), f32).at[:, :, L_HI:L_HI + 6].set(pieces)
    return jnp.concatenate([base, base.at[:, :, L_FLAG].set(-MASK_BIG)], axis=1)


def _nsa(proj, wlo, whi, pelo, pehi, w2bd, batch, seq):
    kaug_s, kaug_w, kaug_c, tri, band, ovl, cend = _nsa_tables(seq)
    qaug = _query_aug()
    const2 = lambda b, g: (0, 0)
    in_specs = [
        pl.BlockSpec((seq, 2 * LANES), lambda b, g: (b, OFF_Q // (2 * LANES) + g)),
        pl.BlockSpec((seq, LANES), lambda b, g: (b, OFF_KSW // LANES + g)),
        pl.BlockSpec((seq, LANES), lambda b, g: (b, OFF_VSW // LANES + g)),
        pl.BlockSpec((seq, LANES), lambda b, g: (b, OFF_GATE // LANES + g)),
        pl.BlockSpec((seq, LANES), lambda b, g: (b, OFF_KVC // LANES + g)),
        pl.BlockSpec(wlo.shape, const2),
        pl.BlockSpec(whi.shape, const2),
        pl.BlockSpec(pelo.shape, const2),
        pl.BlockSpec(pehi.shape, const2),
        pl.BlockSpec(w2bd.shape, const2),
        pl.BlockSpec(kaug_s.shape, const2),
        pl.BlockSpec(kaug_w.shape, const2),
        pl.BlockSpec(kaug_c.shape, const2),
        pl.BlockSpec((1, 2 * GROUP, LANES), lambda b, g: (g, 0, 0)),
        pl.BlockSpec(tri.shape, const2),
        pl.BlockSpec(band.shape, const2),
        pl.BlockSpec(ovl.shape, const2),
        pl.BlockSpec(cend.shape, const2),
    ]
    return pl.pallas_call(
        _nsa_kernel,
        grid=(batch, N_KV),
        in_specs=in_specs,
        out_specs=pl.BlockSpec((seq, 2 * LANES), lambda b, g: (b, g)),
        out_shape=jax.ShapeDtypeStruct((batch * seq, D_ATT), bf16),
        scratch_shapes=[
            pltpu.VMEM((seq, LANES), f32),
            pltpu.VMEM((seq, LANES), bf16),
            pltpu.VMEM((seq + WIN_PAD, LANES), bf16),
            pltpu.VMEM((seq, LANES), bf16),
            pltpu.VMEM((seq + WIN_PAD, LANES), bf16),
            pltpu.VMEM((seq, 2 * LANES), f32),
            pltpu.VMEM((seq // LANES, seq // SLC_LEN, LANES), f32),
            pltpu.VMEM((seq // SLC_LEN, seq // LANES, LANES), f32),
            pltpu.VMEM((seq, LANES), f32),
        ],
        compiler_params=pltpu.CompilerParams(
            dimension_semantics=("parallel", "parallel"), vmem_limit_bytes=VMEM_LIMIT),
        name="nsa",
    )(proj, proj, proj, proj, proj, wlo, whi, pelo, pehi, w2bd, kaug_s, kaug_w, kaug_c, qaug, tri, band, ovl, cend)


def _out_kernel(x_ref, ya_ref, zb_ref, gm_ref, ob_ref, wa_ref, wb_ref, wo_ref, fnw_ref, o_ref):
    y_a = jnp.dot(ya_ref[...], wa_ref[...], preferred_element_type=f32)
    yb_in = ob_ref[...].astype(f32) * zb_ref[...].astype(f32)
    y_b = jnp.dot(yb_in.astype(bf16), wb_ref[...], preferred_element_type=f32)
    mixed = (gm_ref[:, 0:D_MODEL].astype(f32) * y_a + gm_ref[:, D_MODEL:2 * D_MODEL].astype(f32) * y_b)
    xo = x_ref[...] + jnp.dot(mixed.astype(bf16), wo_ref[...], preferred_element_type=f32)
    r = lax.rsqrt(jnp.mean(xo * xo, axis=-1, keepdims=True) + NORM_EPS)
    o_ref[...] = (xo * r) * fnw_ref[...]


def _out(x2, ya, proj, ob, wa, wb, wo, fnw, tm=1024):
    n_rows = x2.shape[0]
    blk = lambda c: pl.BlockSpec((tm, D_MODEL), lambda i, c=c: (i, c))
    full = lambda a: pl.BlockSpec(a.shape, lambda i: (0, 0))
    return pl.pallas_call(
        _out_kernel,
        grid=(n_rows // tm,),
        in_specs=[
            blk(0),
            blk(0),
            blk(OFF_ZB // D_MODEL),
            pl.BlockSpec((tm, 2 * D_MODEL), lambda i: (i, OFF_GMIX // (2 * D_MODEL))),
            blk(0),
            full(wa), full(wb), full(wo), full(fnw),
        ],
        out_specs=pl.BlockSpec((tm, D_MODEL), lambda i: (i, 0)),
        out_shape=jax.ShapeDtypeStruct((n_rows, D_MODEL), f32),
        compiler_params=pltpu.CompilerParams(
            dimension_semantics=("parallel",), vmem_limit_bytes=VMEM_LIMIT),
        name="merge_out",
    )(x2, ya, proj, proj, ob, wa, wb, wo, fnw)


def _blockdiag_w1(w1_k, w1_v, lo):
    half = CMP_STRIDE * HEAD_DIM
    wk = w1_k[lo * half:(lo + 1) * half].reshape(CMP_STRIDE, HEAD_DIM, CMP_HID)
    wv = w1_v[lo * half:(lo + 1) * half].reshape(CMP_STRIDE, HEAD_DIM, CMP_HID)
    z = jnp.zeros_like(wk)
    top = jnp.concatenate([wk, z], axis=-1)
    bot = jnp.concatenate([z, wv], axis=-1)
    return jnp.concatenate([top, bot], axis=1).reshape(CMP_STRIDE * 2 * HEAD_DIM, 2 * CMP_HID).astype(bf16)


def _pe_row(pe_k, pe_v, lo):
    sl = slice(lo * CMP_STRIDE, (lo + 1) * CMP_STRIDE)
    return jnp.concatenate([pe_k[sl], pe_v[sl]], axis=-1).reshape(1, CMP_STRIDE * 2 * HEAD_DIM)


def kernel(x, norm_w, w_in, conv_w, conv_b, cmp_pe_k, cmp_pe_v, cmp_w1_k, cmp_w2_k, cmp_w1_v, cmp_w2_v,
           w_proj_a, w_proj_b, w_out, final_norm_w):
    batch, seq, _ = x.shape
    assert norm_w.shape[0] == 1 and seq % SEL_CHUNK == 0 and seq // CMP_STRIDE == N_CMP_PAD
    w_p = _reordered_w_in(w_in[0])
    x2 = x.reshape(batch * seq, D_MODEL)
    proj, ya = _inproj(x2, norm_w, w_p, conv_w[0], conv_b, seq)

    wlo = _blockdiag_w1(cmp_w1_k[0], cmp_w1_v[0], 0)
    whi = _blockdiag_w1(cmp_w1_k[0], cmp_w1_v[0], 1)
    pelo = _pe_row(cmp_pe_k[0], cmp_pe_v[0], 0)
    pehi = _pe_row(cmp_pe_k[0], cmp_pe_v[0], 1)
    zk = jnp.zeros_like(cmp_w2_k[0])
    w2bd = jnp.concatenate([jnp.concatenate([cmp_w2_k[0], zk], axis=1),
                            jnp.concatenate([zk, cmp_w2_v[0]], axis=1)], axis=0).astype(bf16)
    ob = _nsa(proj, wlo, whi, pelo, pehi, w2bd, batch, seq)

    out = _out(x2, ya, proj, ob, w_proj_a[0].astype(bf16), w_proj_b[0].astype(bf16),
               w_out[0].astype(bf16), final_norm_w.reshape(1, D_MODEL))
    return out.reshape(batch, seq, D_MODEL)
```

```python
import functools

import numpy as np
import jax
import jax.numpy as jnp
from jax import lax
from jax.experimental import pallas as pl
from jax.experimental.pallas import tpu as pltpu

D_MODEL = 1024
D_CONV = 1024
CONV_W = 3
N_HEADS = 16
HEAD_DIM = 64
N_KV = 4
GROUP = N_HEADS // N_KV
D_ATT = N_HEADS * HEAD_DIM
D_KV = N_KV * HEAD_DIM
CMP_LEN = 32
CMP_STRIDE = 16
CMP_HID = 128
SLC_LEN = 64
N_SEL = 8
WINDOW = 512
NORM_EPS = 1e-6

LANES = 128
N_CMP_PAD = 128
MASK_BIG = float(2.0 ** 100)
LOG2E = float(np.log2(np.e))
SEL_CHUNK = 512
WIN_TQ = 128
WIN_KEYS = WINDOW + WIN_TQ
WIN_PAD = WINDOW
CW_TQ = 256

OFF_Q = 0
OFF_ZB = 1024
OFF_GMIX = 2048
OFF_KVC = 4096
OFF_KSW = 4608
OFF_VSW = 5120
OFF_GATE = 5632
N_PROJ = 6144

L_HI = 64
L_LO = 67
L_FLAG = 70
L_SEL = 96

VMEM_LIMIT = 56 * 1024 * 1024

f32 = jnp.float32
bf16 = jnp.bfloat16


W_Q = 4 * D_CONV
W_KV = W_Q + D_ATT
W_GATE = W_KV + 6 * D_KV
W_ZB = W_GATE + 3 * N_HEADS
N_IN = W_ZB + D_ATT + 2 * D_MODEL


WP_ROWS = 512
WP_T = 256


def _wprep_kernel(a_ref, b_ref, o_ref):
    j = pl.program_id(0)
    n_plain = W_KV // WP_ROWS
    n_shift = (N_IN - W_ZB) // WP_ROWS
    off = W_ZB % WP_ROWS
    eye = (lax.broadcasted_iota(jnp.int32, (WP_T, WP_T), 0)
           == lax.broadcasted_iota(jnp.int32, (WP_T, WP_T), 1)).astype(bf16)

    def transposed(rows_bf16, kb):
        return _dot_nt(eye, rows_bf16[:, kb * WP_T:(kb + 1) * WP_T])

    def emit(rows):
        rows = rows.astype(bf16)
        for kb in range(D_MODEL // WP_T):
            o_ref[kb * WP_T:(kb + 1) * WP_T, :] = transposed(rows, kb).astype(bf16)

    def interleave(x, x_off, y, y_off):
        parts = []
        for g in range(N_KV):
            parts += [x[x_off + g * HEAD_DIM:x_off + (g + 1) * HEAD_DIM], y[y_off + g * HEAD_DIM:y_off + (g + 1) * HEAD_DIM]]
        return jnp.concatenate(parts, axis=0)

    @pl.when(j < W_Q // WP_ROWS)
    def _():
        emit(a_ref[...])

    @pl.when((j >= W_Q // WP_ROWS) & (j < n_plain))
    def _():
        emit(a_ref[...] * (HEAD_DIM ** -0.5 * LOG2E))

    @pl.when((j >= n_plain) & (j < n_plain + n_shift))
    def _():
        emit(jnp.concatenate([a_ref[off:, :], b_ref[:off, :]], axis=0))

    @pl.when(j == n_plain + n_shift)
    def _():
        a = a_ref[...]
        emit(interleave(a, 0, a, D_KV))

    @pl.when(j == n_plain + n_shift + 1)
    def _():
        emit(interleave(a_ref[...], 0, b_ref[...], 0))

    @pl.when(j == n_plain + n_shift + 2)
    def _():
        emit(interleave(a_ref[...], D_KV, b_ref[...], D_KV))

    @pl.when(j == n_plain + n_shift + 3)
    def _():
        rows = a_ref[:LANES, :].astype(bf16)
        lane = lax.broadcasted_iota(jnp.int32, (1, LANES), 1)
        for kb in range(D_MODEL // WP_T):
            t = transposed(rows, kb)
            for g in range(N_KV):
                shifted = t if g == 0 else pltpu.roll(t, LANES - 3 * GROUP * g, 1)
                o_ref[kb * WP_T:(kb + 1) * WP_T, g * LANES:(g + 1) * LANES] = (
                    jnp.where(lane < 3 * GROUP, shifted, 0.0).astype(bf16))


def _reordered_w_in(w):
    assert w.shape == (D_MODEL, N_IN) and W_KV % WP_ROWS == 0 and (N_IN - W_ZB) % WP_ROWS == 0
    assert W_GATE % WP_ROWS == 0 and 6 * D_KV == 3 * WP_ROWS and WP_ROWS == N_KV * LANES
    n_plain, n_shift = W_KV // WP_ROWS, (N_IN - W_ZB) // WP_ROWS
    first_shift = W_ZB // WP_ROWS
    kv0 = W_KV // WP_ROWS
    n_out = 4 * D_CONV + N_PROJ
    n_steps = n_out // WP_ROWS

    def block_a(j):
        shifted = first_shift + (j - n_plain)
        tail = jnp.where(j == n_plain + n_shift, kv0, jnp.where(j == n_plain + n_shift + 3, W_GATE // WP_ROWS, kv0 + 1))
        return jnp.where(j < n_plain, j, jnp.where(j < n_plain + n_shift, shifted, tail))

    def block_b(j):
        shifted = jnp.clip(first_shift + 1 + (j - n_plain), first_shift + 1, first_shift + n_shift)
        return jnp.where(j < n_plain + n_shift + 1, shifted, kv0 + 2)

    return pl.pallas_call(
        _wprep_kernel,
        grid=(n_steps,),
        in_specs=[pl.BlockSpec((WP_ROWS, D_MODEL), lambda j: (block_a(j), 0)),
                  pl.BlockSpec((WP_ROWS, D_MODEL), lambda j: (block_b(j), 0))],
        out_specs=pl.BlockSpec((D_MODEL, WP_ROWS), lambda j: (0, j)),
        out_shape=jax.ShapeDtypeStruct((D_MODEL, n_out), bf16),
        compiler_params=pltpu.CompilerParams(dimension_semantics=("parallel",), vmem_limit_bytes=VMEM_LIMIT),
        name="wprep",
    )(w.T, w.T)


def _sigmoid(x):
    return 1.0 / (1.0 + jnp.exp(-x))


def _dot_nt(a, b):
    return lax.dot_general(a, b, (((1,), (1,)), ((), ())), preferred_element_type=f32)


CARRY_ROWS = 8
ROW_BLOCK = 256


def _column_activations(lo, hi):
    silu = lambda a: a * _sigmoid(a)
    keep = lambda a: a
    segments = ((OFF_Q, OFF_ZB, keep), (OFF_ZB, OFF_GMIX, silu), (OFF_GMIX, OFF_KVC, _sigmoid),
                (OFF_KVC, OFF_GATE, keep), (OFF_GATE, N_PROJ, _sigmoid))
    return [(max(a, lo) - lo, min(b, hi) - lo, f) for a, b, f in segments if max(a, lo) < min(b, hi)]


def _inproj_kernel(x_ref, nw_ref, w1_ref, w2_ref, cw_ref, cb_ref, o_ref, ya_ref, h_ref, carry_ref,
                   *, tiles_per_seq):
    i = pl.program_id(0)
    j = pl.program_id(1)
    tm = x_ref.shape[0]
    tn = o_ref.shape[1]
    n_stored = N_PROJ // tn
    blocks = [slice(k * ROW_BLOCK, (k + 1) * ROW_BLOCK) for k in range(tm // ROW_BLOCK)]
    product = lambda rows, w: jnp.dot(h_ref[rows, :], w, preferred_element_type=f32)

    @pl.when((i == 0) & (j == 0))
    def _():
        carry_ref[...] = jnp.zeros_like(carry_ref)

    for t in range(n_stored):
        @pl.when(j == t)
        def _(t=t):
            for rows in blocks:
                if t == 0:
                    x = x_ref[rows, :]
                    r = lax.rsqrt(jnp.mean(x * x, axis=-1, keepdims=True) + NORM_EPS)
                    h_ref[rows, :] = ((x * r) * nw_ref[...]).astype(bf16)
                acc = product(rows, w1_ref[...])
                for lo, hi, act in _column_activations(t * tn, (t + 1) * tn):
                    o_ref[rows, lo:hi] = act(acc[:, lo:hi]).astype(bf16)

    @pl.when(j == n_stored)
    def _():
        lead = carry_ref[...] * (i % tiles_per_seq != 0).astype(f32)
        cw = cw_ref[...]
        for rows in blocks:
            u = product(rows, w2_ref[:, :D_CONV]) * product(rows, w1_ref[:, :D_CONV])
            win = jnp.concatenate([lead, u], axis=0)
            u1 = pltpu.roll(win, 1, 0)[CARRY_ROWS:, :]
            u2 = pltpu.roll(win, 2, 0)[CARRY_ROWS:, :]
            y = cb_ref[...] + cw[0:1, :] * u2 + cw[1:2, :] * u1 + cw[2:3, :] * u
            z = product(rows, w2_ref[:, D_CONV:])
            ya_ref[rows, :] = (product(rows, w1_ref[:, D_CONV:]) * y * (z * _sigmoid(z))).astype(bf16)
            lead = u[ROW_BLOCK - CARRY_ROWS:, :]
        carry_ref[...] = lead


def _inproj(x2, norm_w, w_p, conv_w, conv_b, seq, tm=1024):
    n_rows = x2.shape[0]
    tn = 2 * D_CONV
    assert N_PROJ % tn == 0 and seq % tm == 0 and tm % ROW_BLOCK == 0 and w_p.shape[1] == 2 * tn + N_PROJ
    n_stored = N_PROJ // tn
    first = lambda i, j: (0, jnp.where(j < n_stored, j + 2, 0))
    second = lambda i, j: (0, 1)
    return pl.pallas_call(
        functools.partial(_inproj_kernel, tiles_per_seq=seq // tm),
        grid=(n_rows // tm, n_stored + 1),
        in_specs=[
            pl.BlockSpec((tm, D_MODEL), lambda i, j: (i, 0)),
            pl.BlockSpec((1, D_MODEL), lambda i, j: (0, 0)),
            pl.BlockSpec((D_MODEL, tn), first),
            pl.BlockSpec((D_MODEL, tn), second),
            pl.BlockSpec(conv_w.shape, lambda i, j: (0, 0)),
            pl.BlockSpec(conv_b.shape, lambda i, j: (0, 0)),
        ],
        out_specs=[
            pl.BlockSpec((tm, tn), lambda i, j: (i, jnp.minimum(j, n_stored - 1))),
            pl.BlockSpec((tm, D_CONV), lambda i, j: (i, 0)),
        ],
        out_shape=[jax.ShapeDtypeStruct((n_rows, N_PROJ), bf16), jax.ShapeDtypeStruct((n_rows, D_CONV), bf16)],
        scratch_shapes=[
            pltpu.VMEM((tm, D_MODEL), bf16),
            pltpu.VMEM((CARRY_ROWS, D_CONV), f32),
        ],
        compiler_params=pltpu.CompilerParams(
            dimension_semantics=("arbitrary", "arbitrary"), vmem_limit_bytes=VMEM_LIMIT),
        name="inproj",
    )(x2, norm_w, w_p, w_p, conv_w, conv_b)


def _softmax_step(s, v, carry):
    m, acc = carry
    m_new = jnp.maximum(m, jnp.max(s, axis=-1, keepdims=True))
    alpha = jnp.exp2(m - m_new)
    p = jnp.exp2(s - m_new)
    acc = alpha * acc + jnp.dot(p.astype(bf16), v, preferred_element_type=f32)
    return m_new, acc


def _stack_heads(q_tile, aug_rows, extra):
    lane = lax.broadcasted_iota(jnp.int32, (1, LANES), 1)
    pairs = (q_tile[:, :LANES], q_tile[:, LANES:])
    parts = []
    for r in range(GROUP):
        qh = pairs[r // 2]
        if r % 2 == 1:
            qh = pltpu.roll(qh, HEAD_DIM, 1)
        aug = aug_rows[r:r + 1, :]
        if extra is not None:
            aug = aug + extra
        parts.append(jnp.where(lane < HEAD_DIM, qh, aug).astype(bf16))
    return jnp.concatenate(parts, axis=0)


def _merge_heads(vals, lane):
    return (jnp.where(lane < HEAD_DIM, vals[0], vals[1]), jnp.where(lane < HEAD_DIM, vals[2], vals[3]))


def _nsa_kernel(q_ref, ksw_ref, vsw_ref, gate_ref, kvc_ref, wlo_ref, whi_ref, pelo_ref, pehi_ref, w2_ref,
                kaug_s_ref, kaug_w_ref, kaug_c_ref, qaug_ref, tri_ref, band_ref, ovl_ref, cend_ref,
                o_ref,
                kvcf_ref, ks_ref, kw_ref, vs_ref, vw_ref, ocw_ref, imp_ref, ns_ref, selq_ref):
    seq = q_ref.shape[0]
    lane = lax.broadcasted_iota(jnp.int32, (1, LANES), 1)
    low = lane < HEAD_DIM

    kvcf_ref[...] = kvc_ref[...].astype(f32)
    xcat = jnp.concatenate([kvcf_ref[pl.ds(l, N_CMP_PAD, stride=CMP_STRIDE), :] for l in range(CMP_STRIDE)],
                           axis=1)
    xa = (xcat + pelo_ref[...]).astype(bf16)
    xb = (xcat + pehi_ref[...]).astype(bf16)
    a = jnp.dot(xa, wlo_ref[...], preferred_element_type=f32)
    bm = jnp.dot(xb, whi_ref[...], preferred_element_type=f32)
    h = a + pltpu.roll(bm, N_CMP_PAD - 1, 0)
    hid = h * _sigmoid(h)
    kcvc = jnp.dot(hid.astype(bf16), w2_ref[...], preferred_element_type=f32)

    ksw = ksw_ref[...]
    vsw = vsw_ref[...]
    one = jnp.ones((1, LANES), bf16)
    ks_ref[...] = jnp.where(low, ksw, kaug_s_ref[...])
    kwin = pltpu.roll(ksw.astype(f32), HEAD_DIM, 1).astype(bf16)
    kw_ref[0:WIN_PAD, :] = kaug_w_ref[0:WIN_PAD, :]
    kw_ref[WIN_PAD:, :] = jnp.where(low, kwin, kaug_w_ref[WIN_PAD:, :])
    vs_ref[...] = jnp.where(low, vsw, one)
    vw_ref[0:WIN_PAD, :] = jnp.zeros((WIN_PAD, LANES), bf16)
    vw_ref[WIN_PAD:, :] = jnp.where(low, one, vsw)

    qaug = qaug_ref[0]
    kc_aug = jnp.where(low, kcvc, kaug_c_ref[...]).astype(bf16)
    vc_rows = jnp.concatenate([jnp.zeros((N_CMP_PAD, LANES), bf16), kcvc.astype(bf16)], axis=1)
    zeros_w = jnp.zeros((WIN_KEYS, LANES), bf16)
    ovl = ovl_ref[...]

    def cw_body(i, _):
        t0 = pl.multiple_of(i * CW_TQ, CW_TQ)
        sig = gate_ref[pl.ds(t0, CW_TQ), :].astype(f32)
        o_w, o_c = [], []
        n_tiles = CW_TQ // WIN_TQ

        def tile_scores(h):
            tw = pl.multiple_of(t0 + h * WIN_TQ, WIN_TQ)
            qw = _stack_heads(q_ref[pl.ds(tw, WIN_TQ), :].astype(f32), qaug[GROUP:2 * GROUP], None)
            k_all = jnp.concatenate([kw_ref[pl.ds(tw, WIN_KEYS), :], kc_aug], axis=0)
            return _dot_nt(qw, k_all)

        s_next = tile_scores(0)
        for h in range(n_tiles):
            tw = pl.multiple_of(t0 + h * WIN_TQ, WIN_TQ)
            s = s_next
            if h + 1 < n_tiles:
                s_next = tile_scores(h + 1)

            sw = jnp.concatenate([s[:, :WIN_TQ] + band_ref[:, :WIN_TQ], s[:, WIN_TQ:WINDOW],
                                  s[:, WINDOW:WIN_KEYS] + band_ref[:, WINDOW:]], axis=1)
            p_w = jnp.exp2(sw - jnp.max(sw, axis=-1, keepdims=True))

            ok = cend_ref[...] <= tw
            sm = jnp.where(ok, s[:, WIN_KEYS:], -jnp.inf)
            m = jnp.max(sm, axis=-1, keepdims=True)
            m = jnp.where(m > -jnp.inf, m, 0.0)
            e = jnp.where(ok, jnp.exp2(sm - m), 0.0)
            p_c = e / jnp.maximum(jnp.sum(e, axis=-1, keepdims=True), 1e-30)

            p_all = jnp.concatenate([p_w.astype(bf16), p_c.astype(bf16)], axis=1)
            v_all = jnp.concatenate(
                [jnp.concatenate([vw_ref[pl.ds(tw, WIN_KEYS), :], zeros_w], axis=1), vc_rows], axis=0)
            acc = jnp.dot(p_all, v_all, preferred_element_type=f32)
            acc_w = acc[:, :LANES]
            o_w.append(acc_w / jnp.maximum(acc_w[:, 0:1], 1e-30))
            o_c.append(acc[:, LANES:])

            ps = p_c[0:WIN_TQ] + p_c[WIN_TQ:2 * WIN_TQ] + p_c[2 * WIN_TQ:3 * WIN_TQ] + p_c[3 * WIN_TQ:4 * WIN_TQ]
            p1 = ps.astype(bf16)
            r1 = ps - p1.astype(f32)
            p2 = r1.astype(bf16)
            p3 = (r1 - p2.astype(f32)).astype(bf16)
            imp_ref[i * (CW_TQ // WIN_TQ) + h] = _dot_nt(ovl, p1) + _dot_nt(ovl, p2) + _dot_nt(ovl, p3)

        vals = []
        for r in range(GROUP):
            head = lambda parts: jnp.concatenate([o[r * WIN_TQ:(r + 1) * WIN_TQ, :] for o in parts], axis=0)
            a = head(o_c) * sig[:, 3 * r:3 * r + 1] + head(o_w) * sig[:, 3 * r + 2:3 * r + 3]
            vals.append(pltpu.roll(a, HEAD_DIM, 1) if r % 2 == 0 else a)
        c0, c1 = _merge_heads(vals, lane)
        ocw_ref[pl.ds(t0, CW_TQ), 0:LANES] = c0
        ocw_ref[pl.ds(t0, CW_TQ), LANES:2 * LANES] = c1
        return 0

    lax.fori_loop(0, seq // CW_TQ, cw_body, 0)

    n_slc = seq // SLC_LEN
    n_tt = seq // LANES
    shp = (n_tt, LANES)
    cur = (lax.broadcasted_iota(jnp.int32, shp, 0) * LANES + lax.broadcasted_iota(jnp.int32, shp, 1)) // SLC_LEN
    vals = []
    for j in range(n_slc):
        forced = (cur == j) | (cur == j + 1)
        vj = jnp.where(cur >= j, imp_ref[:, j, :], -jnp.inf)
        vals.append(jnp.full(shp, jnp.inf, f32) if j == 0 else jnp.where(forced, jnp.inf, vj))
    wins = [jnp.zeros(shp, f32) for _ in range(n_slc)]
    losses = [jnp.zeros(shp, f32) for _ in range(n_slc)]
    for a in range(n_slc):
        for b in range(a + 1, n_slc):
            a_first = jnp.where(vals[a] >= vals[b], 1.0, 0.0)
            wins[a] = wins[a] + a_first
            losses[b] = losses[b] + a_first
    for j in range(n_slc):
        rank = losses[j] + (float(n_slc - 1 - j) - wins[j])
        sel = (rank < float(N_SEL)) & (vals[j] > -jnp.inf)
        ns_ref[j] = jnp.where(sel, 0.0, 1.0)
    eye = (lax.broadcasted_iota(jnp.int32, (LANES, LANES), 0)
           == lax.broadcasted_iota(jnp.int32, (LANES, LANES), 1)).astype(bf16)
    zpad = jnp.zeros((L_SEL, LANES), f32)
    for tt in range(n_tt):
        z = jnp.concatenate([zpad, ns_ref[:, tt, :]], axis=0).astype(bf16)
        selq_ref[tt * LANES:(tt + 1) * LANES, :] = _dot_nt(eye, z) * (-MASK_BIG)

    half = SEL_CHUNK // 2
    half_rows = GROUP * half

    def stacked_q(i):
        parts = []
        for h in range(2):
            tok = slice(i * SEL_CHUNK + h * half, i * SEL_CHUNK + (h + 1) * half)
            parts.append(_stack_heads(q_ref[tok, :].astype(f32), qaug[0:GROUP], selq_ref[tok, :]))
        return parts

    def causal(s):
        tri = tri_ref[...]
        return jnp.concatenate([s[r * half:(r + 1) * half, :] + tri for r in range(GROUP)], axis=0)

    def scores(qs, i, c):
        k0 = c * SEL_CHUNK
        if c < i:
            return (_dot_nt(jnp.concatenate(qs, axis=0), ks_ref[k0:k0 + SEL_CHUNK, :]),)
        s_a = causal(_dot_nt(qs[0], ks_ref[k0:k0 + half, :]))
        s_b = _dot_nt(qs[1], ks_ref[k0:k0 + SEL_CHUNK, :])
        return (s_a, jnp.concatenate([s_b[:, :half], causal(s_b[:, half:])], axis=1))

    def softmax_step(s, v, carry):
        if carry is None:
            m = jnp.max(s, axis=-1, keepdims=True)
            return m, jnp.dot(jnp.exp2(s - m).astype(bf16), v, preferred_element_type=f32)
        return _softmax_step(s, v, carry)

    def finish(i, acc):
        tok = slice(i * SEL_CHUNK, (i + 1) * SEL_CHUNK)
        o_s = acc / jnp.maximum(acc[:, HEAD_DIM:HEAD_DIM + 1], 1e-30)
        sig = gate_ref[tok, :].astype(f32)
        vals = []
        for r in range(GROUP):
            a_s = jnp.concatenate([o_s[h * half_rows + r * half:h * half_rows + (r + 1) * half, :]
                                   for h in range(2)], axis=0) * sig[:, 3 * r + 1:3 * r + 2]
            vals.append(a_s if r % 2 == 0 else pltpu.roll(a_s, HEAD_DIM, 1))
        c0, c1 = _merge_heads(vals, lane)
        o_ref[tok, 0:LANES] = (ocw_ref[tok, 0:LANES] + c0).astype(bf16)
        o_ref[tok, LANES:2 * LANES] = (ocw_ref[tok, LANES:2 * LANES] + c1).astype(bf16)

    steps = [(i, c) for i in range(seq // SEL_CHUNK) for c in range(i + 1)]
    qs = stacked_q(0)
    s_next = scores(qs, 0, 0)
    carry = None
    for k, (i, c) in enumerate(steps):
        s = s_next
        if k + 1 < len(steps):
            i2, c2 = steps[k + 1]
            if i2 != i:
                qs = stacked_q(i2)
            s_next = scores(qs, i2, c2)
        k0 = c * SEL_CHUNK
        if c < i:
            carry = softmax_step(s[0], vs_ref[k0:k0 + SEL_CHUNK, :], carry)
        else:
            rows_a = None if carry is None else tuple(a[:half_rows] for a in carry)
            rows_b = None if carry is None else tuple(a[half_rows:] for a in carry)
            _, acc_a = softmax_step(s[0], vs_ref[k0:k0 + half, :], rows_a)
            _, acc_b = softmax_step(s[1], vs_ref[k0:k0 + SEL_CHUNK, :], rows_b)
            finish(i, jnp.concatenate([acc_a, acc_b], axis=0))
            carry = None


def _position_lanes(pos):
    out = np.zeros((pos.shape[0], LANES), np.float32)
    out[:, L_HI:L_HI + 3] = ((pos // 64) * 64)[:, None]
    out[:, L_LO:L_LO + 3] = (pos % 64)[:, None]
    return out


def _nsa_tables(seq):
    pos = np.arange(seq)
    kaug_s = _position_lanes(pos)
    kaug_s[pos, L_SEL + pos // SLC_LEN] = 1.0
    kaug_w = np.zeros((seq + WIN_PAD, LANES), np.float32)
    kaug_w[WIN_PAD:] = _position_lanes(pos)
    kaug_w[:WIN_PAD, L_FLAG] = 1.0
    kaug_c = _position_lanes(np.arange(N_CMP_PAD) * CMP_STRIDE + CMP_LEN - 1)
    tl = np.arange(SEL_CHUNK // 2)[:, None]
    kk = np.arange(SEL_CHUNK // 2)[None, :]
    tri = np.where(kk <= tl, 0.0, -MASK_BIG).astype(np.float32)
    tl = np.arange(WIN_TQ)[:, None]
    kk = np.arange(WIN_KEYS)[None, :]
    dist = tl + WIN_PAD - kk
    band = np.tile(np.where((dist >= 0) & (dist < WINDOW), 0.0, -MASK_BIG).astype(np.float32), (GROUP, 1))
    c = np.arange(N_CMP_PAD)[None, :]
    j = np.arange(seq // SLC_LEN)[:, None]
    c_start, c_end, s_start = c * CMP_STRIDE, c * CMP_STRIDE + CMP_LEN - 1, j * SLC_LEN
    ovl = ((c_start <= s_start + SLC_LEN - 1) & (c_end >= s_start) & (c < seq // CMP_STRIDE - 1)).astype(np.float32)
    cend = np.tile((c * CMP_STRIDE + CMP_LEN - 1) - tl, (GROUP, 1)).astype(np.int32)
    return (jnp.asarray(kaug_s, bf16), jnp.asarray(kaug_w, bf16), jnp.asarray(kaug_c, f32),
            jnp.asarray(tri), jnp.asarray(band), jnp.asarray(ovl, bf16), jnp.asarray(cend))


def _query_aug():
    slopes = jnp.exp2(-8.0 * jnp.arange(1, N_HEADS + 1, dtype=f32) / N_HEADS).reshape(N_KV, GROUP) * LOG2E
    hi =slopes.astype(bf16).astype(f32)
    mid = (slopes - hi).astype(bf16).astype(f32)
    lo = (slopes - hi - mid).astype(bf16).astype(f32)
    pieces = jnp.stack([hi, mid, lo, hi, mid, lo], axis=-1)
    base = jnp.zeros((N_KV, GROUP, LANES), f32).at[:, :, L_HI:L_HI + 6].set(pieces)
    return jnp.concatenate([base, base.at[:, :, L_FLAG].set(-MASK_BIG)], axis=1)


def _nsa(proj, wlo, whi, pelo, pehi, w2bd, batch, seq):
    kaug_s, kaug_w, kaug_c, tri, band, ovl, cend = _nsa_tables(seq)
    qaug = _query_aug()
    const2 = lambda b, g: (0, 0)
    in_specs = [
        pl.BlockSpec((seq, 2 * LANES), lambda b, g: (b, OFF_Q // (2 * LANES) + g)),
        pl.BlockSpec((seq, LANES), lambda b, g: (b, OFF_KSW // LANES + g)),
        pl.BlockSpec((seq, LANES), lambda b, g: (b, OFF_VSW // LANES + g)),
        pl.BlockSpec((seq, LANES), lambda b, g: (b, OFF_GATE // LANES + g)),
        pl.BlockSpec((seq, LANES), lambda b, g: (b, OFF_KVC // LANES + g)),
        pl.BlockSpec(wlo.shape, const2),
        pl.BlockSpec(whi.shape, const2),
        pl.BlockSpec(pelo.shape, const2),
        pl.BlockSpec(pehi.shape, const2),
        pl.BlockSpec(w2bd.shape, const2),
        pl.BlockSpec(kaug_s.shape, const2),
        pl.BlockSpec(kaug_w.shape, const2),
        pl.BlockSpec(kaug_c.shape, const2),
        pl.BlockSpec((1, 2 * GROUP, LANES), lambda b, g: (g, 0, 0)),
        pl.BlockSpec(tri.shape, const2),
        pl.BlockSpec(band.shape, const2),
        pl.BlockSpec(ovl.shape, const2),
        pl.BlockSpec(cend.shape, const2),
    ]
    return pl.pallas_call(
        _nsa_kernel,
        grid=(batch, N_KV),
        in_specs=in_specs,
        out_specs=pl.BlockSpec((seq, 2 * LANES), lambda b, g: (b, g)),
        out_shape=jax.ShapeDtypeStruct((batch * seq, D_ATT), bf16),
        scratch_shapes=[
            pltpu.VMEM((seq, LANES), f32),
            pltpu.VMEM((seq, LANES), bf16),
            pltpu.VMEM((seq + WIN_PAD, LANES), bf16),
            pltpu.VMEM((seq, LANES), bf16),
            pltpu.VMEM((seq + WIN_PAD, LANES), bf16),
            pltpu.VMEM((seq, 2 * LANES), f32),
            pltpu.VMEM((seq // LANES, seq // SLC_LEN, LANES), f32),
            pltpu.VMEM((seq // SLC_LEN, seq // LANES, LANES), f32),
            pltpu.VMEM((seq, LANES), f32),
        ],
        compiler_params=pltpu.CompilerParams(
            dimension_semantics=("parallel", "parallel"), vmem_limit_bytes=VMEM_LIMIT),
        name="nsa",
    )(proj, proj, proj, proj, proj, wlo, whi, pelo, pehi, w2bd, kaug_s, kaug_w, kaug_c, qaug, tri, band, ovl, cend)


def _out_kernel(x_ref, ya_ref, zb_ref, gm_ref, ob_ref, wa_ref, wb_ref, wo_ref, fnw_ref, o_ref):
    y_a = jnp.dot(ya_ref[...], wa_ref[...], preferred_element_type=f32)
    yb_in = ob_ref[...].astype(f32) * zb_ref[...].astype(f32)
    y_b = jnp.dot(yb_in.astype(bf16), wb_ref[...], preferred_element_type=f32)
    mixed = (gm_ref[:, 0:D_MODEL].astype(f32) * y_a + gm_ref[:, D_MODEL:2 * D_MODEL].astype(f32) * y_b)
    xo = x_ref[...] + jnp.dot(mixed.astype(bf16), wo_ref[...], preferred_element_type=f32)
    r = lax.rsqrt(jnp.mean(xo * xo, axis=-1, keepdims=True) + NORM_EPS)
    o_ref[...] = (xo * r) * fnw_ref[...]


def _out(x2, ya, proj, ob, wa, wb, wo, fnw, tm=1024):
    n_rows = x2.shape[0]
    blk = lambda c: pl.BlockSpec((tm, D_MODEL), lambda i, c=c: (i, c))
    full = lambda a: pl.BlockSpec(a.shape, lambda i: (0, 0))
    return pl.pallas_call(
        _out_kernel,
        grid=(n_rows // tm,),
        in_specs=[
            blk(0),
            blk(0),
            blk(OFF_ZB // D_MODEL),
            pl.BlockSpec((tm, 2 * D_MODEL), lambda i: (i, OFF_GMIX // (2 * D_MODEL))),
            blk(0),
            full(wa), full(wb), full(wo), full(fnw),
        ],
        out_specs=pl.BlockSpec((tm, D_MODEL), lambda i: (i, 0)),
        out_shape=jax.ShapeDtypeStruct((n_rows, D_MODEL), f32),
        compiler_params=pltpu.CompilerParams(
            dimension_semantics=("parallel",), vmem_limit_bytes=VMEM_LIMIT),
        name="merge_out",
    )(x2, ya, proj, proj, ob, wa, wb, wo, fnw)


def _blockdiag_w1(w1_k, w1_v, lo):
    half = CMP_STRIDE * HEAD_DIM
    wk = w1_k[lo * half:(lo + 1) * half].reshape(CMP_STRIDE, HEAD_DIM, CMP_HID)
    wv = w1_v[lo * half:(lo + 1) * half].reshape(CMP_STRIDE, HEAD_DIM, CMP_HID)
    z = jnp.zeros_like(wk)
    top = jnp.concatenate([wk, z], axis=-1)
    bot = jnp.concatenate([z, wv], axis=-1)
    return jnp.concatenate([top, bot], axis=1).reshape(CMP_STRIDE * 2 * HEAD_DIM, 2 * CMP_HID).astype(bf16)


def _pe_row(pe_k, pe_v, lo):
    sl = slice(lo * CMP_STRIDE, (lo + 1) * CMP_STRIDE)
    return jnp.concatenate([pe_k[sl], pe_v[sl]], axis=-1).reshape(1, CMP_STRIDE * 2 * HEAD_DIM)


def kernel(x, norm_w, w_in, conv_w, conv_b, cmp_pe_k, cmp_pe_v, cmp_w1_k, cmp_w2_k, cmp_w1_v, cmp_w2_v,
           w_proj_a, w_proj_b, w_out, final_norm_w):
    batch, seq, _ = x.shape
    assert norm_w.shape[0] == 1 and seq % SEL_CHUNK == 0 and seq // CMP_STRIDE == N_CMP_PAD
    w_p = _reordered_w_in(w_in[0])
    x2 = x.reshape(batch * seq, D_MODEL)
    proj, ya = _inproj(x2, norm_w, w_p, conv_w[0], conv_b, seq)

    wlo = _blockdiag_w1(cmp_w1_k[0], cmp_w1_v[0], 0)
    whi = _blockdiag_w1(cmp_w1_k[0], cmp_w1_v[0], 1)
    pelo = _pe_row(cmp_pe_k[0], cmp_pe_v[0], 0)
    pehi = _pe_row(cmp_pe_k[0], cmp_pe_v[0], 1)
    zk = jnp.zeros_like(cmp_w2_k[0])
    w2bd = jnp.concatenate([jnp.concatenate([cmp_w2_k[0], zk], axis=1),
                            jnp.concatenate([zk, cmp_w2_v[0]], axis=1)], axis=0).astype(bf16)
    ob = _nsa(proj, wlo, whi, pelo, pehi, w2bd, batch, seq)

    out = _out(x2, ya, proj, ob, w_proj_a[0].astype(bf16), w_proj_b[0].astype(bf16),
               w_out[0].astype(bf16), final_norm_w.reshape(1, D_MODEL))
    return out.reshape(batch, seq, D_MODEL)
```

```python
import functools

import numpy as np
import jax
import jax.numpy as jnp
from jax import lax
from jax.experimental import pallas as pl
from jax.experimental.pallas import tpu as pltpu

D_MODEL = 1024
D_CONV = 1024
CONV_W = 3
N_HEADS = 16
HEAD_DIM = 64
N_KV = 4
GROUP = N_HEADS // N_KV
D_ATT = N_HEADS * HEAD_DIM
D_KV = N_KV * HEAD_DIM
CMP_LEN = 32
CMP_STRIDE = 16
CMP_HID = 128
SLC_LEN = 64
N_SEL = 8
WINDOW = 512
NORM_EPS = 1e-6

LANES = 128
N_CMP_PAD = 128
MASK_BIG = float(2.0 ** 100)
LOG2E = float(np.log2(np.e))
SEL_CHUNK = 512
WIN_TQ = 128
WIN_KEYS = WINDOW + WIN_TQ
WIN_PAD = WINDOW
CW_TQ = 512

OFF_Q = 0
OFF_ZB = 1024
OFF_GMIX = 2048
OFF_KVC = 4096
OFF_KSW = 4608
OFF_VSW = 5120
OFF_GATE = 5632
N_PROJ = 6144

L_HI = 64
L_LO = 67
L_FLAG = 70
L_SEL = 96

VMEM_LIMIT = 56 * 1024 * 1024

f32 = jnp.float32
bf16 = jnp.bfloat16


W_Q = 4 * D_CONV
W_KV = W_Q + D_ATT
W_GATE = W_KV + 6 * D_KV
W_ZB = W_GATE + 3 * N_HEADS
N_IN = W_ZB + D_ATT + 2 * D_MODEL


WP_ROWS = 512
WP_T = 256


def _wprep_kernel(a_ref, b_ref, o_ref):
    j = pl.program_id(0)
    n_plain = W_KV // WP_ROWS
    n_shift = (N_IN - W_ZB) // WP_ROWS
    off = W_ZB % WP_ROWS
    eye = (lax.broadcasted_iota(jnp.int32, (WP_T, WP_T), 0)
           == lax.broadcasted_iota(jnp.int32, (WP_T, WP_T), 1)).astype(bf16)

    def transposed(rows_bf16, kb):
        return _dot_nt(eye, rows_bf16[:, kb * WP_T:(kb + 1) * WP_T])

    def emit(rows):
        rows = rows.astype(bf16)
        for kb in range(D_MODEL // WP_T):
            o_ref[kb * WP_T:(kb + 1) * WP_T, :] = transposed(rows, kb).astype(bf16)

    def interleave(x, x_off, y, y_off):
        parts = []
        for g in range(N_KV):
            parts += [x[x_off + g * HEAD_DIM:x_off + (g + 1) * HEAD_DIM], y[y_off + g * HEAD_DIM:y_off + (g + 1) * HEAD_DIM]]
        return jnp.concatenate(parts, axis=0)

    @pl.when(j < W_Q // WP_ROWS)
    def _():
        emit(a_ref[...])

    @pl.when((j >= W_Q // WP_ROWS) & (j < n_plain))
    def _():
        emit(a_ref[...] * (HEAD_DIM ** -0.5 * LOG2E))

    @pl.when((j >= n_plain) & (j < n_plain + n_shift))
    def _():
        emit(jnp.concatenate([a_ref[off:, :], b_ref[:off, :]], axis=0))

    @pl.when(j == n_plain + n_shift)
    def _():
        a = a_ref[...]
        emit(interleave(a, 0, a, D_KV))

    @pl.when(j == n_plain + n_shift + 1)
    def _():
        emit(interleave(a_ref[...], 0, b_ref[...], 0))

    @pl.when(j == n_plain + n_shift + 2)
    def _():
        emit(interleave(a_ref[...], D_KV, b_ref[...], D_KV))

    @pl.when(j == n_plain + n_shift + 3)
    def _():
        rows = a_ref[:LANES, :].astype(bf16)
        lane = lax.broadcasted_iota(jnp.int32, (1, LANES), 1)
        for kb in range(D_MODEL // WP_T):
            t = transposed(rows, kb)
            for g in range(N_KV):
                shifted = t if g == 0 else pltpu.roll(t, LANES - 3 * GROUP * g, 1)
                o_ref[kb * WP_T:(kb + 1) * WP_T, g * LANES:(g + 1) * LANES] = (
                    jnp.where(lane < 3 * GROUP, shifted, 0.0).astype(bf16))


def _reordered_w_in(w):
    assert w.shape == (D_MODEL, N_IN) and W_KV % WP_ROWS == 0 and (N_IN - W_ZB) % WP_ROWS == 0
    assert W_GATE % WP_ROWS == 0 and 6 * D_KV == 3 * WP_ROWS and WP_ROWS == N_KV * LANES
    n_plain, n_shift = W_KV // WP_ROWS, (N_IN - W_ZB) // WP_ROWS
    first_shift = W_ZB // WP_ROWS
    kv0 = W_KV // WP_ROWS
    n_out = 4 * D_CONV + N_PROJ
    n_steps = n_out // WP_ROWS

    def block_a(j):
        shifted = first_shift + (j - n_plain)
        tail = jnp.where(j == n_plain + n_shift, kv0, jnp.where(j == n_plain + n_shift + 3, W_GATE // WP_ROWS, kv0 + 1))
        return jnp.where(j < n_plain, j, jnp.where(j < n_plain + n_shift, shifted, tail))

    def block_b(j):
        shifted = jnp.clip(first_shift + 1 + (j - n_plain), first_shift + 1, first_shift + n_shift)
        return jnp.where(j < n_plain + n_shift + 1, shifted, kv0 + 2)

    return pl.pallas_call(
        _wprep_kernel,
        grid=(n_steps,),
        in_specs=[pl.BlockSpec((WP_ROWS, D_MODEL), lambda j: (block_a(j), 0)),
                  pl.BlockSpec((WP_ROWS, D_MODEL), lambda j: (block_b(j), 0))],
        out_specs=pl.BlockSpec((D_MODEL, WP_ROWS), lambda j: (0, j)),
        out_shape=jax.ShapeDtypeStruct((D_MODEL, n_out), bf16),
        compiler_params=pltpu.CompilerParams(dimension_semantics=("parallel",), vmem_limit_bytes=VMEM_LIMIT),
        name="wprep",
    )(w.T, w.T)


def _sigmoid(x):
    return 1.0 / (1.0 + jnp.exp(-x))


def _dot_nt(a, b):
    return lax.dot_general(a, b, (((1,), (1,)), ((), ())), preferred_element_type=f32)


CARRY_ROWS = 8
ROW_BLOCK = 256


def _column_activations(lo, hi):
    silu = lambda a: a * _sigmoid(a)
    keep = lambda a: a
    segments = ((OFF_Q, OFF_ZB, keep), (OFF_ZB, OFF_GMIX, silu), (OFF_GMIX, OFF_KVC, _sigmoid),
                (OFF_KVC, OFF_GATE, keep), (OFF_GATE, N_PROJ, _sigmoid))
    return [(max(a, lo) - lo, min(b, hi) - lo, f) for a, b, f in segments if max(a, lo) < min(b, hi)]


def _inproj_kernel(x_ref, nw_ref, w1_ref, w2_ref, cw_ref, cb_ref, o_ref, ya_ref, h_ref, carry_ref,
                   *, tiles_per_seq):
    i = pl.program_id(0)
    j = pl.program_id(1)
    tm = x_ref.shape[0]
    tn = o_ref.shape[1]
    n_stored = N_PROJ // tn
    blocks = [slice(k * ROW_BLOCK, (k + 1) * ROW_BLOCK) for k in range(tm // ROW_BLOCK)]
    product = lambda rows, w: jnp.dot(h_ref[rows, :], w, preferred_element_type=f32)

    @pl.when((i == 0) & (j == 0))
    def _():
        carry_ref[...] = jnp.zeros_like(carry_ref)

    for t in range(n_stored):
        @pl.when(j == t)
        def _(t=t):
            for rows in blocks:
                if t == 0:
                    x = x_ref[rows, :]
                    r = lax.rsqrt(jnp.mean(x * x, axis=-1, keepdims=True) + NORM_EPS)
                    h_ref[rows, :] = ((x * r) * nw_ref[...]).astype(bf16)
                acc = product(rows, w1_ref[...])
                for lo, hi, act in _column_activations(t * tn, (t + 1) * tn):
                    o_ref[rows, lo:hi] = act(acc[:, lo:hi]).astype(bf16)

    @pl.when(j == n_stored)
    def _():
        lead = carry_ref[...] * (i % tiles_per_seq != 0).astype(f32)
        cw = cw_ref[...]
        for rows in blocks:
            u = product(rows, w2_ref[:, :D_CONV]) * product(rows, w1_ref[:, :D_CONV])
            win = jnp.concatenate([lead, u], axis=0)
            u1 = pltpu.roll(win, 1, 0)[CARRY_ROWS:, :]
            u2 = pltpu.roll(win, 2, 0)[CARRY_ROWS:, :]
            y = cb_ref[...] + cw[0:1, :] * u2 + cw[1:2, :] * u1 + cw[2:3, :] * u
            z = product(rows, w2_ref[:, D_CONV:])
            ya_ref[rows, :] = (product(rows, w1_ref[:, D_CONV:]) * y * (z * _sigmoid(z))).astype(bf16)
            lead = u[ROW_BLOCK - CARRY_ROWS:, :]
        carry_ref[...] = lead


def _inproj(x2, norm_w, w_p, conv_w, conv_b, seq, tm=1024):
    n_rows = x2.shape[0]
    tn = 2 * D_CONV
    assert N_PROJ % tn == 0 and seq % tm == 0 and tm % ROW_BLOCK == 0 and w_p.shape[1] == 2 * tn + N_PROJ
    n_stored = N_PROJ // tn
    first = lambda i, j: (0, jnp.where(j < n_stored, j + 2, 0))
    second = lambda i, j: (0, 1)
    return pl.pallas_call(
        functools.partial(_inproj_kernel, tiles_per_seq=seq // tm),
        grid=(n_rows // tm, n_stored + 1),
        in_specs=[
            pl.BlockSpec((tm, D_MODEL), lambda i, j: (i, 0)),
            pl.BlockSpec((1, D_MODEL), lambda i, j: (0, 0)),
            pl.BlockSpec((D_MODEL, tn), first),
            pl.BlockSpec((D_MODEL, tn), second),
            pl.BlockSpec(conv_w.shape, lambda i, j: (0, 0)),
            pl.BlockSpec(conv_b.shape, lambda i, j: (0, 0)),
        ],
        out_specs=[
            pl.BlockSpec((tm, tn), lambda i, j: (i, jnp.minimum(j, n_stored - 1))),
            pl.BlockSpec((tm, D_CONV), lambda i, j: (i, 0)),
        ],
        out_shape=[jax.ShapeDtypeStruct((n_rows, N_PROJ), bf16), jax.ShapeDtypeStruct((n_rows, D_CONV), bf16)],
        scratch_shapes=[
            pltpu.VMEM((tm, D_MODEL), bf16),
            pltpu.VMEM((CARRY_ROWS, D_CONV), f32),
        ],
        compiler_params=pltpu.CompilerParams(
            dimension_semantics=("arbitrary", "arbitrary"), vmem_limit_bytes=VMEM_LIMIT),
        name="inproj",
    )(x2, norm_w, w_p, w_p, conv_w, conv_b)


def _softmax_step(s, v, carry):
    m, acc = carry
    m_new = jnp.maximum(m, jnp.max(s, axis=-1, keepdims=True))
    alpha = jnp.exp2(m - m_new)
    p = jnp.exp2(s - m_new)
    acc = alpha * acc + jnp.dot(p.astype(bf16), v, preferred_element_type=f32)
    return m_new, acc


def _stack_heads(q_tile, aug_rows, extra):
    lane = lax.broadcasted_iota(jnp.int32, (1, LANES), 1)
    pairs = (q_tile[:, :LANES], q_tile[:, LANES:])
    parts = []
    for r in range(GROUP):
        qh = pairs[r // 2]
        if r % 2 == 1:
            qh = pltpu.roll(qh, HEAD_DIM, 1)
        aug = aug_rows[r:r + 1, :]
        if extra is not None:
            aug = aug + extra
        parts.append(jnp.where(lane < HEAD_DIM, qh, aug).astype(bf16))
    return jnp.concatenate(parts, axis=0)


def _merge_heads(vals, lane):
    return (jnp.where(lane < HEAD_DIM, vals[0], vals[1]), jnp.where(lane < HEAD_DIM, vals[2], vals[3]))


def _nsa_kernel(q_ref, ksw_ref, vsw_ref, gate_ref, kvc_ref, zb_ref, wlo_ref, whi_ref, pelo_ref, pehi_ref, w2_ref,
                kaug_s_ref, kaug_w_ref, kaug_c_ref, qaug_ref, tri_ref, band_ref, ovl_ref, cend_ref,
                o_ref,
                kvcf_ref, ks_ref, kw_ref, vs_ref, vw_ref, ocw_ref, imp_ref, ns_ref, selq_ref):
    seq = q_ref.shape[0]
    lane = lax.broadcasted_iota(jnp.int32, (1, LANES), 1)
    low = lane < HEAD_DIM

    kvcf_ref[...] = kvc_ref[...].astype(f32)
    xcat = jnp.concatenate([kvcf_ref[pl.ds(l, N_CMP_PAD, stride=CMP_STRIDE), :] for l in range(CMP_STRIDE)],
                           axis=1)
    xa = (xcat + pelo_ref[...]).astype(bf16)
    xb = (xcat + pehi_ref[...]).astype(bf16)
    a = jnp.dot(xa, wlo_ref[...], preferred_element_type=f32)
    bm = jnp.dot(xb, whi_ref[...], preferred_element_type=f32)
    h = a + pltpu.roll(bm, N_CMP_PAD - 1, 0)
    hid = h * _sigmoid(h)
    kcvc = jnp.dot(hid.astype(bf16), w2_ref[...], preferred_element_type=f32)

    ksw = ksw_ref[...]
    vsw = vsw_ref[...]
    one = jnp.ones((1, LANES), bf16)
    ks_ref[...] = jnp.where(low, ksw, kaug_s_ref[...])
    kwin = pltpu.roll(ksw.astype(f32), HEAD_DIM, 1).astype(bf16)
    kw_ref[0:WIN_PAD, :] = kaug_w_ref[0:WIN_PAD, :]
    kw_ref[WIN_PAD:, :] = jnp.where(low, kwin, kaug_w_ref[WIN_PAD:, :])
    vs_ref[...] = jnp.where(low, vsw, one)
    vw_ref[0:WIN_PAD, :] = jnp.zeros((WIN_PAD, LANES), bf16)
    vw_ref[WIN_PAD:, :] = jnp.where(low, one, vsw)

    qaug = qaug_ref[0]
    kc_aug = jnp.where(low, kcvc, kaug_c_ref[...]).astype(bf16)
    vc_rows = jnp.concatenate([jnp.zeros((N_CMP_PAD, LANES), bf16), kcvc.astype(bf16)], axis=1)
    zeros_w = jnp.zeros((WIN_KEYS, LANES), bf16)
    ovl = ovl_ref[...]

    def cw_body(i, _):
        t0 = pl.multiple_of(i * CW_TQ, CW_TQ)
        sig = gate_ref[pl.ds(t0, CW_TQ), :].astype(f32)
        o_w, o_c = [], []
        n_tiles = CW_TQ // WIN_TQ

        def tile_scores(h):
            tw = pl.multiple_of(t0 + h * WIN_TQ, WIN_TQ)
            qw = _stack_heads(q_ref[pl.ds(tw, WIN_TQ), :].astype(f32), qaug[GROUP:2 * GROUP], None)
            k_all = jnp.concatenate([kw_ref[pl.ds(tw, WIN_KEYS), :], kc_aug], axis=0)
            return _dot_nt(qw, k_all)

        s_next = tile_scores(0)
        for h in range(n_tiles):
            tw = pl.multiple_of(t0 + h * WIN_TQ, WIN_TQ)
            s = s_next
            if h + 1 < n_tiles:
                s_next = tile_scores(h + 1)

            sw = jnp.concatenate([s[:, :WIN_TQ] + band_ref[:, :WIN_TQ], s[:, WIN_TQ:WINDOW],
                                  s[:, WINDOW:WIN_KEYS] + band_ref[:, WINDOW:]], axis=1)
            p_w = jnp.exp2(sw - jnp.max(sw, axis=-1, keepdims=True))

            ok = cend_ref[...] <= tw
            sm = jnp.where(ok, s[:, WIN_KEYS:], -jnp.inf)
            m = jnp.max(sm, axis=-1, keepdims=True)
            m = jnp.where(m > -jnp.inf, m, 0.0)
            e = jnp.where(ok, jnp.exp2(sm - m), 0.0)
            p_c = e / jnp.maximum(jnp.sum(e, axis=-1, keepdims=True), 1e-30)

            p_all = jnp.concatenate([p_w.astype(bf16), p_c.astype(bf16)], axis=1)
            v_all = jnp.concatenate(
                [jnp.concatenate([vw_ref[pl.ds(tw, WIN_KEYS), :], zeros_w], axis=1), vc_rows], axis=0)
            acc = jnp.dot(p_all, v_all, preferred_element_type=f32)
            acc_w = acc[:, :LANES]
            o_w.append(acc_w / jnp.maximum(acc_w[:, 0:1], 1e-30))
            o_c.append(acc[:, LANES:])

            ps = p_c[0:WIN_TQ] + p_c[WIN_TQ:2 * WIN_TQ] + p_c[2 * WIN_TQ:3 * WIN_TQ] + p_c[3 * WIN_TQ:4 * WIN_TQ]
            p1 = ps.astype(bf16)
            r1 = ps - p1.astype(f32)
            p2 = r1.astype(bf16)
            p3 = (r1 - p2.astype(f32)).astype(bf16)
            imp_ref[i * (CW_TQ // WIN_TQ) + h] = _dot_nt(ovl, p1) + _dot_nt(ovl, p2) + _dot_nt(ovl, p3)

        vals = []
        for r in range(GROUP):
            head = lambda parts: jnp.concatenate([o[r * WIN_TQ:(r + 1) * WIN_TQ, :] for o in parts], axis=0)
            a = head(o_c) * sig[:, 3 * r:3 * r + 1] + head(o_w) * sig[:, 3 * r + 2:3 * r + 3]
            vals.append(pltpu.roll(a, HEAD_DIM, 1) if r % 2 == 0 else a)
        c0, c1 = _merge_heads(vals, lane)
        ocw_ref[pl.ds(t0, CW_TQ), 0:LANES] = c0
        ocw_ref[pl.ds(t0, CW_TQ), LANES:2 * LANES] = c1
        return 0

    lax.fori_loop(0, seq // CW_TQ, cw_body, 0)

    n_slc = seq // SLC_LEN
    n_tt = seq // LANES
    shp = (n_tt, LANES)
    cur = (lax.broadcasted_iota(jnp.int32, shp, 0) * LANES + lax.broadcasted_iota(jnp.int32, shp, 1)) // SLC_LEN
    vals = []
    for j in range(n_slc):
        forced = (cur == j) | (cur == j + 1)
        vj = jnp.where(cur >= j, imp_ref[:, j, :], -jnp.inf)
        vals.append(jnp.full(shp, jnp.inf, f32) if j == 0 else jnp.where(forced, jnp.inf, vj))
    wins = [jnp.zeros(shp, f32) for _ in range(n_slc)]
    losses = [jnp.zeros(shp, f32) for _ in range(n_slc)]
    for a in range(n_slc):
        for b in range(a + 1, n_slc):
            a_first = jnp.where(vals[a] >= vals[b], 1.0, 0.0)
            wins[a] = wins[a] + a_first
            losses[b] = losses[b] + a_first
    for j in range(n_slc):
        rank = losses[j] + (float(n_slc - 1 - j) - wins[j])
        sel = (rank < float(N_SEL)) & (vals[j] > -jnp.inf)
        ns_ref[j] = jnp.where(sel, 0.0, 1.0)
    eye = (lax.broadcasted_iota(jnp.int32, (LANES, LANES), 0)
           == lax.broadcasted_iota(jnp.int32, (LANES, LANES), 1)).astype(bf16)
    zpad = jnp.zeros((L_SEL, LANES), f32)
    for tt in range(n_tt):
        z = jnp.concatenate([zpad, ns_ref[:, tt, :]], axis=0).astype(bf16)
        selq_ref[tt * LANES:(tt + 1) * LANES, :] = _dot_nt(eye, z) * (-MASK_BIG)

    half = SEL_CHUNK // 2
    half_rows = GROUP * half

    def stacked_q(i):
        parts = []
        for h in range(2):
            tok = slice(i * SEL_CHUNK + h * half, i * SEL_CHUNK + (h + 1) * half)
            parts.append(_stack_heads(q_ref[tok, :].astype(f32), qaug[0:GROUP], selq_ref[tok, :]))
        return parts

    def causal(s):
        tri = tri_ref[...]
        return jnp.concatenate([s[r * half:(r + 1) * half, :] + tri for r in range(GROUP)], axis=0)

    def scores(qs, i, c):
        k0 = c * SEL_CHUNK
        if c < i:
            return (_dot_nt(jnp.concatenate(qs, axis=0), ks_ref[k0:k0 + SEL_CHUNK, :]),)
        s_a = causal(_dot_nt(qs[0], ks_ref[k0:k0 + half, :]))
        s_b = _dot_nt(qs[1], ks_ref[k0:k0 + SEL_CHUNK, :])
        return (s_a, jnp.concatenate([s_b[:, :half], causal(s_b[:, half:])], axis=1))

    def softmax_step(s, v, carry):
        if carry is None:
            m = jnp.max(s, axis=-1, keepdims=True)
            return m, jnp.dot(jnp.exp2(s - m).astype(bf16), v, preferred_element_type=f32)
        return _softmax_step(s, v, carry)

    def finish(i, acc):
        tok = slice(i * SEL_CHUNK, (i + 1) * SEL_CHUNK)
        o_s = acc / jnp.maximum(acc[:, HEAD_DIM:HEAD_DIM + 1], 1e-30)
        sig = gate_ref[tok, :].astype(f32)
        vals = []
        for r in range(GROUP):
            a_s = jnp.concatenate([o_s[h * half_rows + r * half:h * half_rows + (r + 1) * half, :]
                                   for h in range(2)], axis=0) * sig[:, 3 * r + 1:3 * r + 2]
            vals.append(a_s if r % 2 == 0 else pltpu.roll(a_s, HEAD_DIM, 1))
        c0, c1 = _merge_heads(vals, lane)
        o_ref[tok, 0:LANES] = ((ocw_ref[tok, 0:LANES] + c0) * zb_ref[tok, 0:LANES].astype(f32)).astype(bf16)
        o_ref[tok, LANES:2 * LANES] = (
            (ocw_ref[tok, LANES:2 * LANES] + c1) * zb_ref[tok, LANES:2 * LANES].astype(f32)).astype(bf16)

    steps = [(i, c) for i in range(seq // SEL_CHUNK) for c in range(i + 1)]
    qs = stacked_q(0)
    s_next = scores(qs, 0, 0)
    carry = None
    for k, (i, c) in enumerate(steps):
        s = s_next
        if k + 1 < len(steps):
            i2, c2 = steps[k + 1]
            if i2 != i:
                qs = stacked_q(i2)
            s_next = scores(qs, i2, c2)
        k0 = c * SEL_CHUNK
        if c < i:
            carry = softmax_step(s[0], vs_ref[k0:k0 + SEL_CHUNK, :], carry)
        else:
            rows_a = None if carry is None else tuple(a[:half_rows] for a in carry)
            rows_b = None if carry is None else tuple(a[half_rows:] for a in carry)
            _, acc_a = softmax_step(s[0], vs_ref[k0:k0 + half, :], rows_a)
            _, acc_b = softmax_step(s[1], vs_ref[k0:k0 + SEL_CHUNK, :], rows_b)
            finish(i, jnp.concatenate([acc_a, acc_b], axis=0))
            carry = None


def _position_lanes(pos):
    out = np.zeros((pos.shape[0], LANES), np.float32)
    out[:, L_HI:L_HI + 3] = ((pos // 64) * 64)[:, None]
    out[:, L_LO:L_LO + 3] = (pos % 64)[:, None]
    return out


def _nsa_tables(seq):
    pos = np.arange(seq)
    kaug_s = _position_lanes(pos)
    kaug_s[pos, L_SEL + pos // SLC_LEN] = 1.0
    kaug_w = np.zeros((seq + WIN_PAD, LANES), np.float32)
    kaug_w[WIN_PAD:] = _position_lanes(pos)
    kaug_w[:WIN_PAD, L_FLAG] = 1.0
    kaug_c = _position_lanes(np.arange(N_CMP_PAD) * CMP_STRIDE + CMP_LEN - 1)
    tl = np.arange(SEL_CHUNK // 2)[:, None]
    kk = np.arange(SEL_CHUNK // 2)[None, :]
    tri = np.where(kk <= tl, 0.0, -MASK_BIG).astype(np.float32)
    tl = np.arange(WIN_TQ)[:, None]
    kk = np.arange(WIN_KEYS)[None, :]
    dist = tl + WIN_PAD - kk
    band = np.tile(np.where((dist >= 0) & (dist < WINDOW), 0.0, -MASK_BIG).astype(np.float32), (GROUP, 1))
    c = np.arange(N_CMP_PAD)[None, :]
    j = np.arange(seq // SLC_LEN)[:, None]
    c_start, c_end, s_start = c * CMP_STRIDE, c * CMP_STRIDE + CMP_LEN - 1, j * SLC_LEN
    ovl = ((c_start <= s_start + SLC_LEN - 1) & (c_end >= s_start) & (c < seq // CMP_STRIDE - 1)).astype(np.float32)
    cend = np.tile((c * CMP_STRIDE + CMP_LEN - 1) - tl, (GROUP, 1)).astype(np.int32)
    return (jnp.asarray(kaug_s, bf16), jnp.asarray(kaug_w, bf16), jnp.asarray(kaug_c, f32),
            jnp.asarray(tri), jnp.asarray(band), jnp.asarray(ovl, bf16), jnp.asarray(cend))


def _query_aug():
    slopes = jnp.exp2(-8.0 * jnp.arange(1, N_HEADS + 1, dtype=f32) / N_HEADS).reshape(N_KV, GROUP) * LOG2E
    hi =slopes.astype(bf16).astype(f32)
    mid = (slopes - hi).astype(bf16).astype(f32)
    lo = (slopes - hi - mid).astype(bf16).astype(f32)
    pieces = jnp.stack([hi, mid, lo, hi, mid, lo], axis=-1)
    base = jnp.zeros((N_KV, GROUP, LANES), f32).at[:, :, L_HI:L_HI + 6].set(pieces)
    return jnp.concatenate([base, base.at[:, :, L_FLAG].set(-MASK_BIG)], axis=1)


def _nsa(proj, wlo, whi, pelo, pehi, w2bd, batch, seq):
    kaug_s, kaug_w, kaug_c, tri, band, ovl, cend = _nsa_tables(seq)
    qaug = _query_aug()
    const2 = lambda b, g: (0, 0)
    in_specs = [
        pl.BlockSpec((seq, 2 * LANES), lambda b, g: (b, OFF_Q // (2 * LANES) + g)),
        pl.BlockSpec((seq, LANES), lambda b, g: (b, OFF_KSW // LANES + g)),
        pl.BlockSpec((seq, LANES), lambda b, g: (b, OFF_VSW // LANES + g)),
        pl.BlockSpec((seq, LANES), lambda b, g: (b, OFF_GATE // LANES + g)),
        pl.BlockSpec((seq, LANES), lambda b, g: (b, OFF_KVC // LANES + g)),
        pl.BlockSpec((seq, 2 * LANES), lambda b, g: (b, OFF_ZB // (2 * LANES) + g)),
        pl.BlockSpec(wlo.shape, const2),
        pl.BlockSpec(whi.shape, const2),
        pl.BlockSpec(pelo.shape, const2),
        pl.BlockSpec(pehi.shape, const2),
        pl.BlockSpec(w2bd.shape, const2),
        pl.BlockSpec(kaug_s.shape, const2),
        pl.BlockSpec(kaug_w.shape, const2),
        pl.BlockSpec(kaug_c.shape, const2),
        pl.BlockSpec((1, 2 * GROUP, LANES), lambda b, g: (g, 0, 0)),
        pl.BlockSpec(tri.shape, const2),
        pl.BlockSpec(band.shape, const2),
        pl.BlockSpec(ovl.shape, const2),
        pl.BlockSpec(cend.shape, const2),
    ]
    return pl.pallas_call(
        _nsa_kernel,
        grid=(batch, N_KV),
        in_specs=in_specs,
        out_specs=pl.BlockSpec((seq, 2 * LANES), lambda b, g: (b, g)),
        out_shape=jax.ShapeDtypeStruct((batch * seq, D_ATT), bf16),
        scratch_shapes=[
            pltpu.VMEM((seq, LANES), f32),
            pltpu.VMEM((seq, LANES), bf16),
            pltpu.VMEM((seq + WIN_PAD, LANES), bf16),
            pltpu.VMEM((seq, LANES), bf16),
            pltpu.VMEM((seq + WIN_PAD, LANES), bf16),
            pltpu.VMEM((seq, 2 * LANES), f32),
            pltpu.VMEM((seq // LANES, seq // SLC_LEN, LANES), f32),
            pltpu.VMEM((seq // SLC_LEN, seq // LANES, LANES), f32),
            pltpu.VMEM((seq, LANES), f32),
        ],
        compiler_params=pltpu.CompilerParams(
            dimension_semantics=("parallel", "parallel"), vmem_limit_bytes=VMEM_LIMIT),
        name="nsa",
    )(proj, proj, proj, proj, proj, proj, wlo, whi, pelo, pehi, w2bd, kaug_s, kaug_w, kaug_c, qaug, tri, band, ovl, cend)


def _out_kernel(x_ref, ya_ref, gm_ref, ob_ref, wa_ref, wb_ref, wo_ref, fnw_ref, o_ref):
    y_a = jnp.dot(ya_ref[...], wa_ref[...], preferred_element_type=f32)
    y_b = jnp.dot(ob_ref[...], wb_ref[...], preferred_element_type=f32)
    mixed = (gm_ref[:, 0:D_MODEL].astype(f32) * y_a + gm_ref[:, D_MODEL:2 * D_MODEL].astype(f32) * y_b)
    xo = x_ref[...] + jnp.dot(mixed.astype(bf16), wo_ref[...], preferred_element_type=f32)
    r = lax.rsqrt(jnp.mean(xo * xo, axis=-1, keepdims=True) + NORM_EPS)
    o_ref[...] = (xo * r) * fnw_ref[...]


def _out(x2, ya, proj, ob, wa, wb, wo, fnw, tm=1024):
    n_rows = x2.shape[0]
    blk = lambda c: pl.BlockSpec((tm, D_MODEL), lambda i, c=c: (i, c))
    full = lambda a: pl.BlockSpec(a.shape, lambda i: (0, 0))
    return pl.pallas_call(
        _out_kernel,
        grid=(n_rows // tm,),
        in_specs=[
            blk(0),
            blk(0),
            pl.BlockSpec((tm, 2 * D_MODEL), lambda i: (i, OFF_GMIX // (2 * D_MODEL))),
            blk(0),
            full(wa), full(wb), full(wo), full(fnw),
        ],
        out_specs=pl.BlockSpec((tm, D_MODEL), lambda i: (i, 0)),
        out_shape=jax.ShapeDtypeStruct((n_rows, D_MODEL), f32),
        compiler_params=pltpu.CompilerParams(
            dimension_semantics=("parallel",), vmem_limit_bytes=VMEM_LIMIT),
        name="merge_out",
    )(x2, ya, proj, ob, wa, wb, wo, fnw)


def _blockdiag_w1(w1_k, w1_v, lo):
    half = CMP_STRIDE * HEAD_DIM
    wk = w1_k[lo * half:(lo + 1) * half].reshape(CMP_STRIDE, HEAD_DIM, CMP_HID)
    wv = w1_v[lo * half:(lo + 1) * half].reshape(CMP_STRIDE, HEAD_DIM, CMP_HID)
    z = jnp.zeros_like(wk)
    top = jnp.concatenate([wk, z], axis=-1)
    bot = jnp.concatenate([z, wv], axis=-1)
    return jnp.concatenate([top, bot], axis=1).reshape(CMP_STRIDE * 2 * HEAD_DIM, 2 * CMP_HID).astype(bf16)


def _pe_row(pe_k, pe_v, lo):
    sl = slice(lo * CMP_STRIDE, (lo + 1) * CMP_STRIDE)
    return jnp.concatenate([pe_k[sl], pe_v[sl]], axis=-1).reshape(1, CMP_STRIDE * 2 * HEAD_DIM)


def kernel(x, norm_w, w_in, conv_w, conv_b, cmp_pe_k, cmp_pe_v, cmp_w1_k, cmp_w2_k, cmp_w1_v, cmp_w2_v,
           w_proj_a, w_proj_b, w_out, final_norm_w):
    batch, seq, _ = x.shape
    assert norm_w.shape[0] == 1 and seq % SEL_CHUNK == 0 and seq // CMP_STRIDE == N_CMP_PAD
    w_p = _reordered_w_in(w_in[0])
    x2 = x.reshape(batch * seq, D_MODEL)
    proj, ya = _inproj(x2, norm_w, w_p, conv_w[0], conv_b, seq)

    wlo = _blockdiag_w1(cmp_w1_k[0], cmp_w1_v[0], 0)
    whi = _blockdiag_w1(cmp_w1_k[0], cmp_w1_v[0], 1)
    pelo = _pe_row(cmp_pe_k[0], cmp_pe_v[0], 0)
    pehi = _pe_row(cmp_pe_k[0], cmp_pe_v[0], 1)
    zk = jnp.zeros_like(cmp_w2_k[0])
    w2bd = jnp.concatenate([jnp.concatenate([cmp_w2_k[0], zk], axis=1),
                            jnp.concatenate([zk, cmp_w2_v[0]], axis=1)], axis=0).astype(bf16)
    ob = _nsa(proj, wlo, whi, pelo, pehi, w2bd, batch, seq)

    out = _out(x2, ya, proj, ob, w_proj_a[0].astype(bf16), w_proj_b[0].astype(bf16),
               w_out[0].astype(bf16), final_norm_w.reshape(1, D_MODEL))
    return out.reshape(batch, seq, D_MODEL)
```

```python
import functools

import numpy as np
import jax
import jax.numpy as jnp
from jax import lax
from jax.experimental import pallas as pl
from jax.experimental.pallas import tpu as pltpu

D_MODEL = 1024
D_CONV = 1024
CONV_W = 3
N_HEADS = 16
HEAD_DIM = 64
N_KV = 4
GROUP = N_HEADS // N_KV
D_ATT = N_HEADS * HEAD_DIM
D_KV = N_KV * HEAD_DIM
CMP_LEN = 32
CMP_STRIDE = 16
CMP_HID = 128
SLC_LEN = 64
N_SEL = 8
WINDOW = 512
NORM_EPS = 1e-6

LANES = 128
N_CMP_PAD = 128
MASK_BIG = float(2.0 ** 100)
LOG2E = float(np.log2(np.e))
SEL_CHUNK = 512
WIN_TQ = 128
WIN_KEYS = WINDOW + WIN_TQ
WIN_PAD = WINDOW
CW_TQ = 512

OFF_Q = 0
OFF_ZB = 1024
OFF_GMIX = 2048
OFF_KVC = 4096
OFF_KSW = 4608
OFF_VSW = 5120
OFF_GATE = 5632
N_PROJ = 6144

L_HI = 64
L_LO = 67
L_FLAG = 70
L_SEL = 96

VMEM_LIMIT = 56 * 1024 * 1024

f32 = jnp.float32
bf16 = jnp.bfloat16


W_Q = 4 * D_CONV
W_KV = W_Q + D_ATT
W_GATE = W_KV + 6 * D_KV
W_ZB = W_GATE + 3 * N_HEADS
N_IN = W_ZB + D_ATT + 2 * D_MODEL


WP_ROWS = 512
WP_T = 256


def _wprep_kernel(a_ref, b_ref, o_ref):
    j = pl.program_id(0)
    n_plain = W_KV // WP_ROWS
    n_shift = (N_IN - W_ZB) // WP_ROWS
    off = W_ZB % WP_ROWS
    eye = (lax.broadcasted_iota(jnp.int32, (WP_T, WP_T), 0)
           == lax.broadcasted_iota(jnp.int32, (WP_T, WP_T), 1)).astype(bf16)

    def transposed(rows_bf16, kb):
        return _dot_nt(eye, rows_bf16[:, kb * WP_T:(kb + 1) * WP_T])

    def emit(rows):
        rows = rows.astype(bf16)
        for kb in range(D_MODEL // WP_T):
            o_ref[kb * WP_T:(kb + 1) * WP_T, :] = transposed(rows, kb).astype(bf16)

    def interleave(x, x_off, y, y_off):
        parts = []
        for g in range(N_KV):
            parts += [x[x_off + g * HEAD_DIM:x_off + (g + 1) * HEAD_DIM], y[y_off + g * HEAD_DIM:y_off + (g + 1) * HEAD_DIM]]
        return jnp.concatenate(parts, axis=0)

    @pl.when(j < W_Q // WP_ROWS)
    def _():
        emit(a_ref[...])

    @pl.when((j >= W_Q // WP_ROWS) & (j < n_plain))
    def _():
        emit(a_ref[...] * (HEAD_DIM ** -0.5 * LOG2E))

    @pl.when((j >= n_plain) & (j < n_plain + n_shift))
    def _():
        emit(jnp.concatenate([a_ref[off:, :], b_ref[:off, :]], axis=0))

    @pl.when(j == n_plain + n_shift)
    def _():
        a = a_ref[...]
        emit(interleave(a, 0, a, D_KV))

    @pl.when(j == n_plain + n_shift + 1)
    def _():
        emit(interleave(a_ref[...], 0, b_ref[...], 0))

    @pl.when(j == n_plain + n_shift + 2)
    def _():
        emit(interleave(a_ref[...], D_KV, b_ref[...], D_KV))

    @pl.when(j == n_plain + n_shift + 3)
    def _():
        rows = a_ref[:LANES, :].astype(bf16)
        lane = lax.broadcasted_iota(jnp.int32, (1, LANES), 1)
        for kb in range(D_MODEL // WP_T):
            t = transposed(rows, kb)
            for g in range(N_KV):
                shifted = t if g == 0 else pltpu.roll(t, LANES - 3 * GROUP * g, 1)
                o_ref[kb * WP_T:(kb + 1) * WP_T, g * LANES:(g + 1) * LANES] = (
                    jnp.where(lane < 3 * GROUP, shifted, 0.0).astype(bf16))


def _reordered_w_in(w):
    assert w.shape == (D_MODEL, N_IN) and W_KV % WP_ROWS == 0 and (N_IN - W_ZB) % WP_ROWS == 0
    assert W_GATE % WP_ROWS == 0 and 6 * D_KV == 3 * WP_ROWS and WP_ROWS == N_KV * LANES
    n_plain, n_shift = W_KV // WP_ROWS, (N_IN - W_ZB) // WP_ROWS
    first_shift = W_ZB // WP_ROWS
    kv0 = W_KV // WP_ROWS
    n_out = 4 * D_CONV + N_PROJ
    n_steps = n_out // WP_ROWS

    def block_a(j):
        shifted = first_shift + (j - n_plain)
        tail = jnp.where(j == n_plain + n_shift, kv0, jnp.where(j == n_plain + n_shift + 3, W_GATE // WP_ROWS, kv0 + 1))
        return jnp.where(j < n_plain, j, jnp.where(j < n_plain + n_shift, shifted, tail))

    def block_b(j):
        shifted = jnp.clip(first_shift + 1 + (j - n_plain), first_shift + 1, first_shift + n_shift)
        return jnp.where(j < n_plain + n_shift + 1, shifted, kv0 + 2)

    return pl.pallas_call(
        _wprep_kernel,
        grid=(n_steps,),
        in_specs=[pl.BlockSpec((WP_ROWS, D_MODEL), lambda j: (block_a(j), 0)),
                  pl.BlockSpec((WP_ROWS, D_MODEL), lambda j: (block_b(j), 0))],
        out_specs=pl.BlockSpec((D_MODEL, WP_ROWS), lambda j: (0, j)),
        out_shape=jax.ShapeDtypeStruct((D_MODEL, n_out), bf16),
        compiler_params=pltpu.CompilerParams(dimension_semantics=("parallel",), vmem_limit_bytes=VMEM_LIMIT),
        name="wprep",
    )(w.T, w.T)


def _sigmoid(x):
    return 1.0 / (1.0 + jnp.exp(-x))


def _dot_nt(a, b):
    return lax.dot_general(a, b, (((1,), (1,)), ((), ())), preferred_element_type=f32)


CARRY_ROWS = 8
ROW_BLOCK = 512


def _column_activations(lo, hi):
    silu = lambda a: a * _sigmoid(a)
    keep = lambda a: a
    segments = ((OFF_Q, OFF_ZB, keep), (OFF_ZB, OFF_GMIX, silu), (OFF_GMIX, OFF_KVC, _sigmoid),
                (OFF_KVC, OFF_GATE, keep), (OFF_GATE, N_PROJ, _sigmoid))
    return [(max(a, lo) - lo, min(b, hi) - lo, f) for a, b, f in segments if max(a, lo) < min(b, hi)]


def _inproj_kernel(x_ref, nw_ref, w1_ref, w2_ref, cw_ref, cb_ref, o_ref, ya_ref, h_ref, carry_ref,
                   *, tiles_per_seq):
    i = pl.program_id(0)
    j = pl.program_id(1)
    tm = x_ref.shape[0]
    tn = o_ref.shape[1]
    n_stored = N_PROJ // tn
    blocks = [slice(k * ROW_BLOCK, (k + 1) * ROW_BLOCK) for k in range(tm // ROW_BLOCK)]
    product = lambda rows, w: jnp.dot(h_ref[rows, :], w, preferred_element_type=f32)

    @pl.when((i == 0) & (j == 0))
    def _():
        carry_ref[...] = jnp.zeros_like(carry_ref)

    for t in range(n_stored):
        @pl.when(j == t)
        def _(t=t):
            for rows in blocks:
                if t == 0:
                    x = x_ref[rows, :]
                    r = lax.rsqrt(jnp.mean(x * x, axis=-1, keepdims=True) + NORM_EPS)
                    h_ref[rows, :] = ((x * r) * nw_ref[...]).astype(bf16)
                acc = product(rows, w1_ref[...])
                for lo, hi, act in _column_activations(t * tn, (t + 1) * tn):
                    o_ref[rows, lo:hi] = act(acc[:, lo:hi]).astype(bf16)

    @pl.when(j == n_stored)
    def _():
        lead = carry_ref[...] * (i % tiles_per_seq != 0).astype(f32)
        cw = cw_ref[...]
        for rows in blocks:
            u = product(rows, w2_ref[:, :D_CONV]) * product(rows, w1_ref[:, :D_CONV])
            win = jnp.concatenate([lead, u], axis=0)
            u1 = pltpu.roll(win, 1, 0)[CARRY_ROWS:, :]
            u2 = pltpu.roll(win, 2, 0)[CARRY_ROWS:, :]
            y = cb_ref[...] + cw[0:1, :] * u2 + cw[1:2, :] * u1 + cw[2:3, :] * u
            z = product(rows, w2_ref[:, D_CONV:])
            ya_ref[rows, :] = (product(rows, w1_ref[:, D_CONV:]) * y * (z * _sigmoid(z))).astype(bf16)
            lead = u[ROW_BLOCK - CARRY_ROWS:, :]
        carry_ref[...] = lead


def _inproj(x2, norm_w, w_p, conv_w, conv_b, seq, tm=1024):
    n_rows = x2.shape[0]
    tn = 2 * D_CONV
    assert N_PROJ % tn == 0 and seq % tm == 0 and tm % ROW_BLOCK == 0 and w_p.shape[1] == 2 * tn + N_PROJ
    n_stored = N_PROJ // tn
    first = lambda i, j: (0, jnp.where(j < n_stored, j + 2, 0))
    second = lambda i, j: (0, 1)
    return pl.pallas_call(
        functools.partial(_inproj_kernel, tiles_per_seq=seq // tm),
        grid=(n_rows // tm, n_stored + 1),
        in_specs=[
            pl.BlockSpec((tm, D_MODEL), lambda i, j: (i, 0)),
            pl.BlockSpec((1, D_MODEL), lambda i, j: (0, 0)),
            pl.BlockSpec((D_MODEL, tn), first),
            pl.BlockSpec((D_MODEL, tn), second),
            pl.BlockSpec(conv_w.shape, lambda i, j: (0, 0)),
            pl.BlockSpec(conv_b.shape, lambda i, j: (0, 0)),
        ],
        out_specs=[
            pl.BlockSpec((tm, tn), lambda i, j: (i, jnp.minimum(j, n_stored - 1))),
            pl.BlockSpec((tm, D_CONV), lambda i, j: (i, 0)),
        ],
        out_shape=[jax.ShapeDtypeStruct((n_rows, N_PROJ), bf16), jax.ShapeDtypeStruct((n_rows, D_CONV), bf16)],
        scratch_shapes=[
            pltpu.VMEM((tm, D_MODEL), bf16),
            pltpu.VMEM((CARRY_ROWS, D_CONV), f32),
        ],
        compiler_params=pltpu.CompilerParams(
            dimension_semantics=("arbitrary", "arbitrary"), vmem_limit_bytes=VMEM_LIMIT),
        name="inproj",
    )(x2, norm_w, w_p, w_p, conv_w, conv_b)


def _softmax_step(s, v, carry):
    m, acc = carry
    m_new = jnp.maximum(m, jnp.max(s, axis=-1, keepdims=True))
    alpha = jnp.exp2(m - m_new)
    p = jnp.exp2(s - m_new)
    acc = alpha * acc + jnp.dot(p.astype(bf16), v, preferred_element_type=f32)
    return m_new, acc


def _stack_heads(q_tile, aug_rows, extra):
    lane = lax.broadcasted_iota(jnp.int32, (1, LANES), 1)
    pairs = (q_tile[:, :LANES], q_tile[:, LANES:])
    parts = []
    for r in range(GROUP):
        qh = pairs[r // 2]
        if r % 2 == 1:
            qh = pltpu.roll(qh, HEAD_DIM, 1)
        aug = aug_rows[r:r + 1, :]
        if extra is not None:
            aug = aug + extra
        parts.append(jnp.where(lane < HEAD_DIM, qh, aug).astype(bf16))
    return jnp.concatenate(parts, axis=0)


def _merge_heads(vals, lane):
    return (jnp.where(lane < HEAD_DIM, vals[0], vals[1]), jnp.where(lane < HEAD_DIM, vals[2], vals[3]))


def _nsa_kernel(q_ref, ksw_ref, vsw_ref, gate_ref, kvc_ref, zb_ref, wlo_ref, whi_ref, pelo_ref, pehi_ref, w2_ref,
                kaug_s_ref, kaug_w_ref, kaug_c_ref, qaug_ref, tri_ref, band_ref, ovl_ref, cend_ref,
                o_ref,
                kvcf_ref, ks_ref, kw_ref, vs_ref, vw_ref, ocw_ref, imp_ref, ns_ref, selq_ref):
    seq = q_ref.shape[0]
    lane = lax.broadcasted_iota(jnp.int32, (1, LANES), 1)
    low = lane < HEAD_DIM

    kvcf_ref[...] = kvc_ref[...].astype(f32)
    xcat = jnp.concatenate([kvcf_ref[pl.ds(l, N_CMP_PAD, stride=CMP_STRIDE), :] for l in range(CMP_STRIDE)],
                           axis=1)
    xa = (xcat + pelo_ref[...]).astype(bf16)
    xb = (xcat + pehi_ref[...]).astype(bf16)
    a = jnp.dot(xa, wlo_ref[...], preferred_element_type=f32)
    bm = jnp.dot(xb, whi_ref[...], preferred_element_type=f32)
    h = a + pltpu.roll(bm, N_CMP_PAD - 1, 0)
    hid = h * _sigmoid(h)
    kcvc = jnp.dot(hid.astype(bf16), w2_ref[...], preferred_element_type=f32)

    ksw = ksw_ref[...]
    vsw = vsw_ref[...]
    one = jnp.ones((1, LANES), bf16)
    ks_ref[...] = jnp.where(low, ksw, kaug_s_ref[...])
    kwin = pltpu.roll(ksw.astype(f32), HEAD_DIM, 1).astype(bf16)
    kw_ref[0:WIN_PAD, :] = kaug_w_ref[0:WIN_PAD, :]
    kw_ref[WIN_PAD:, :] = jnp.where(low, kwin, kaug_w_ref[WIN_PAD:, :])
    vs_ref[...] = jnp.where(low, vsw, one)
    vw_ref[0:WIN_PAD, :] = jnp.zeros((WIN_PAD, LANES), bf16)
    vw_ref[WIN_PAD:, :] = jnp.where(low, one, vsw)

    qaug = qaug_ref[0]
    kc_aug = jnp.where(low, kcvc, kaug_c_ref[...]).astype(bf16)
    vc_rows = jnp.concatenate([jnp.zeros((N_CMP_PAD, LANES), bf16), kcvc.astype(bf16)], axis=1)
    zeros_w = jnp.zeros((WIN_KEYS, LANES), bf16)
    ovl = ovl_ref[...]

    def cw_body(i, _):
        t0 = pl.multiple_of(i * CW_TQ, CW_TQ)
        sig = gate_ref[pl.ds(t0, CW_TQ), :].astype(f32)
        o_w, o_c = [], []
        n_tiles = CW_TQ // WIN_TQ

        def tile_scores(h):
            tw = pl.multiple_of(t0 + h * WIN_TQ, WIN_TQ)
            qw = _stack_heads(q_ref[pl.ds(tw, WIN_TQ), :].astype(f32), qaug[GROUP:2 * GROUP], None)
            k_all = jnp.concatenate([kw_ref[pl.ds(tw, WIN_KEYS), :], kc_aug], axis=0)
            return _dot_nt(qw, k_all)

        s_next = tile_scores(0)
        for h in range(n_tiles):
            tw = pl.multiple_of(t0 + h * WIN_TQ, WIN_TQ)
            s = s_next
            if h + 1 < n_tiles:
                s_next = tile_scores(h + 1)

            sw = jnp.concatenate([s[:, :WIN_TQ] + band_ref[:, :WIN_TQ], s[:, WIN_TQ:WINDOW],
                                  s[:, WINDOW:WIN_KEYS] + band_ref[:, WINDOW:]], axis=1)
            p_w = jnp.exp2(sw - jnp.max(sw, axis=-1, keepdims=True))

            ok = cend_ref[...] <= tw
            sm = jnp.where(ok, s[:, WIN_KEYS:], -jnp.inf)
            m = jnp.max(sm, axis=-1, keepdims=True)
            m = jnp.where(m > -jnp.inf, m, 0.0)
            e = jnp.where(ok, jnp.exp2(sm - m), 0.0)
            p_c = e / jnp.maximum(jnp.sum(e, axis=-1, keepdims=True), 1e-30)

            p_all = jnp.concatenate([p_w.astype(bf16), p_c.astype(bf16)], axis=1)
            v_all = jnp.concatenate(
                [jnp.concatenate([vw_ref[pl.ds(tw, WIN_KEYS), :], zeros_w], axis=1), vc_rows], axis=0)
            acc = jnp.dot(p_all, v_all, preferred_element_type=f32)
            acc_w = acc[:, :LANES]
            o_w.append(acc_w / jnp.maximum(acc_w[:, 0:1], 1e-30))
            o_c.append(acc[:, LANES:])

            ps = p_c[0:WIN_TQ] + p_c[WIN_TQ:2 * WIN_TQ] + p_c[2 * WIN_TQ:3 * WIN_TQ] + p_c[3 * WIN_TQ:4 * WIN_TQ]
            p1 = ps.astype(bf16)
            r1 = ps - p1.astype(f32)
            p2 = r1.astype(bf16)
            p3 = (r1 - p2.astype(f32)).astype(bf16)
            imp_ref[i * (CW_TQ // WIN_TQ) + h] = _dot_nt(ovl, p1) + _dot_nt(ovl, p2) + _dot_nt(ovl, p3)

        vals = []
        for r in range(GROUP):
            head = lambda parts: jnp.concatenate([o[r * WIN_TQ:(r + 1) * WIN_TQ, :] for o in parts], axis=0)
            a = head(o_c) * sig[:, 3 * r:3 * r + 1] + head(o_w) * sig[:, 3 * r + 2:3 * r + 3]
            vals.append(pltpu.roll(a, HEAD_DIM, 1) if r % 2 == 0 else a)
        c0, c1 = _merge_heads(vals, lane)
        ocw_ref[pl.ds(t0, CW_TQ), 0:LANES] = c0
        ocw_ref[pl.ds(t0, CW_TQ), LANES:2 * LANES] = c1
        return 0

    lax.fori_loop(0, seq // CW_TQ, cw_body, 0)

    n_slc = seq // SLC_LEN
    n_tt = seq // LANES
    shp = (n_tt, LANES)
    cur = (lax.broadcasted_iota(jnp.int32, shp, 0) * LANES + lax.broadcasted_iota(jnp.int32, shp, 1)) // SLC_LEN
    vals = []
    for j in range(n_slc):
        forced = (cur == j) | (cur == j + 1)
        vj = jnp.where(cur >= j, imp_ref[:, j, :], -jnp.inf)
        vals.append(jnp.full(shp, jnp.inf, f32) if j == 0 else jnp.where(forced, jnp.inf, vj))
    wins = [jnp.zeros(shp, f32) for _ in range(n_slc)]
    losses = [jnp.zeros(shp, f32) for _ in range(n_slc)]
    for a in range(n_slc):
        for b in range(a + 1, n_slc):
            a_first = jnp.where(vals[a] >= vals[b], 1.0, 0.0)
            wins[a] = wins[a] + a_first
            losses[b] = losses[b] + a_first
    for j in range(n_slc):
        rank = losses[j] + (float(n_slc - 1 - j) - wins[j])
        sel = (rank < float(N_SEL)) & (vals[j] > -jnp.inf)
        ns_ref[j] = jnp.where(sel, 0.0, 1.0)
    eye = (lax.broadcasted_iota(jnp.int32, (LANES, LANES), 0)
           == lax.broadcasted_iota(jnp.int32, (LANES, LANES), 1)).astype(bf16)
    zpad = jnp.zeros((L_SEL, LANES), f32)
    for tt in range(n_tt):
        z = jnp.concatenate([zpad, ns_ref[:, tt, :]], axis=0).astype(bf16)
        selq_ref[tt * LANES:(tt + 1) * LANES, :] = _dot_nt(eye, z) * (-MASK_BIG)

    half = SEL_CHUNK // 2
    half_rows = GROUP * half

    def stacked_q(i):
        parts = []
        for h in range(2):
            tok = slice(i * SEL_CHUNK + h * half, i * SEL_CHUNK + (h + 1) * half)
            parts.append(_stack_heads(q_ref[tok, :].astype(f32), qaug[0:GROUP], selq_ref[tok, :]))
        return parts

    def causal(s):
        tri = tri_ref[...]
        return jnp.concatenate([s[r * half:(r + 1) * half, :] + tri for r in range(GROUP)], axis=0)

    def scores(qs, i, c):
        k0 = c * SEL_CHUNK
        if c < i:
            return (_dot_nt(jnp.concatenate(qs, axis=0), ks_ref[k0:k0 + SEL_CHUNK, :]),)
        s_a = causal(_dot_nt(qs[0], ks_ref[k0:k0 + half, :]))
        s_b = _dot_nt(qs[1], ks_ref[k0:k0 + SEL_CHUNK, :])
        return (s_a, jnp.concatenate([s_b[:, :half], causal(s_b[:, half:])], axis=1))

    def softmax_step(s, v, carry):
        if carry is None:
            m = jnp.max(s, axis=-1, keepdims=True)
            return m, jnp.dot(jnp.exp2(s - m).astype(bf16), v, preferred_element_type=f32)
        return _softmax_step(s, v, carry)

    def finish(i, acc):
        tok = slice(i * SEL_CHUNK, (i + 1) * SEL_CHUNK)
        o_s = acc / jnp.maximum(acc[:, HEAD_DIM:HEAD_DIM + 1], 1e-30)
        sig = gate_ref[tok, :].astype(f32)
        vals = []
        for r in range(GROUP):
            a_s = jnp.concatenate([o_s[h * half_rows + r * half:h * half_rows + (r + 1) * half, :]
                                   for h in range(2)], axis=0) * sig[:, 3 * r + 1:3 * r + 2]
            vals.append(a_s if r % 2 == 0 else pltpu.roll(a_s, HEAD_DIM, 1))
        c0, c1 = _merge_heads(vals, lane)
        o_ref[tok, 0:LANES] = ((ocw_ref[tok, 0:LANES] + c0) * zb_ref[tok, 0:LANES].astype(f32)).astype(bf16)
        o_ref[tok, LANES:2 * LANES] = (
            (ocw_ref[tok, LANES:2 * LANES] + c1) * zb_ref[tok, LANES:2 * LANES].astype(f32)).astype(bf16)

    steps = [(i, c) for i in range(seq // SEL_CHUNK) for c in range(i + 1)]
    qs = stacked_q(0)
    s_next = scores(qs, 0, 0)
    carry = None
    for k, (i, c) in enumerate(steps):
        s = s_next
        if k + 1 < len(steps):
            i2, c2 = steps[k + 1]
            if i2 != i:
                qs = stacked_q(i2)
            s_next = scores(qs, i2, c2)
        k0 = c * SEL_CHUNK
        if c < i:
            carry = softmax_step(s[0], vs_ref[k0:k0 + SEL_CHUNK, :], carry)
        else:
            rows_a = None if carry is None else tuple(a[:half_rows] for a in carry)
            rows_b = None if carry is None else tuple(a[half_rows:] for a in carry)
            _, acc_a = softmax_step(s[0], vs_ref[k0:k0 + half, :], rows_a)
            _, acc_b = softmax_step(s[1], vs_ref[k0:k0 + SEL_CHUNK, :], rows_b)
            finish(i, jnp.concatenate([acc_a, acc_b], axis=0))
            carry = None


def _position_lanes(pos):
    out = np.zeros((pos.shape[0], LANES), np.float32)
    out[:, L_HI:L_HI + 3] = ((pos // 64) * 64)[:, None]
    out[:, L_LO:L_LO + 3] = (pos % 64)[:, None]
    return out


def _nsa_tables(seq):
    pos = np.arange(seq)
    kaug_s = _position_lanes(pos)
    kaug_s[pos, L_SEL + pos // SLC_LEN] = 1.0
    kaug_w = np.zeros((seq + WIN_PAD, LANES), np.float32)
    kaug_w[WIN_PAD:] = _position_lanes(pos)
    kaug_w[:WIN_PAD, L_FLAG] = 1.0
    kaug_c = _position_lanes(np.arange(N_CMP_PAD) * CMP_STRIDE + CMP_LEN - 1)
    tl = np.arange(SEL_CHUNK // 2)[:, None]
    kk = np.arange(SEL_CHUNK // 2)[None, :]
    tri = np.where(kk <= tl, 0.0, -MASK_BIG).astype(np.float32)
    tl = np.arange(WIN_TQ)[:, None]
    kk = np.arange(WIN_KEYS)[None, :]
    dist = tl + WIN_PAD - kk
    band = np.tile(np.where((dist >= 0) & (dist < WINDOW), 0.0, -MASK_BIG).astype(np.float32), (GROUP, 1))
    c = np.arange(N_CMP_PAD)[None, :]
    j = np.arange(seq // SLC_LEN)[:, None]
    c_start, c_end, s_start = c * CMP_STRIDE, c * CMP_STRIDE + CMP_LEN - 1, j * SLC_LEN
    ovl = ((c_start <= s_start + SLC_LEN - 1) & (c_end >= s_start) & (c < seq // CMP_STRIDE - 1)).astype(np.float32)
    cend = np.tile((c * CMP_STRIDE + CMP_LEN - 1) - tl, (GROUP, 1)).astype(np.int32)
    return (jnp.asarray(kaug_s, bf16), jnp.asarray(kaug_w, bf16), jnp.asarray(kaug_c, f32),
            jnp.asarray(tri), jnp.asarray(band), jnp.asarray(ovl, bf16), jnp.asarray(cend))


def _query_aug():
    slopes = jnp.exp2(-8.0 * jnp.arange(1, N_HEADS + 1, dtype=f32) / N_HEADS).reshape(N_KV, GROUP) * LOG2E
    hi =slopes.astype(bf16).astype(f32)
    mid = (slopes - hi).astype(bf16).astype(f32)
    lo = (slopes - hi - mid).astype(bf16).astype(f32)
    pieces = jnp.stack([hi, mid, lo, hi, mid, lo], axis=-1)
    base = jnp.zeros((N_KV, GROUP, LANES), f32).at[:, :, L_HI:L_HI + 6].set(pieces)
    return jnp.concatenate([base, base.at[:, :, L_FLAG].set(-MASK_BIG)], axis=1)


def _nsa(proj, wlo, whi, pelo, pehi, w2bd, batch, seq):
    kaug_s, kaug_w, kaug_c, tri, band, ovl, cend = _nsa_tables(seq)
    qaug = _query_aug()
    const2 = lambda b, g: (0, 0)
    in_specs = [
        pl.BlockSpec((seq, 2 * LANES), lambda b, g: (b, OFF_Q // (2 * LANES) + g)),
        pl.BlockSpec((seq, LANES), lambda b, g: (b, OFF_KSW // LANES + g)),
        pl.BlockSpec((seq, LANES), lambda b, g: (b, OFF_VSW // LANES + g)),
        pl.BlockSpec((seq, LANES), lambda b, g: (b, OFF_GATE // LANES + g)),
        pl.BlockSpec((seq, LANES), lambda b, g: (b, OFF_KVC // LANES + g)),
        pl.BlockSpec((seq, 2 * LANES), lambda b, g: (b, OFF_ZB // (2 * LANES) + g)),
        pl.BlockSpec(wlo.shape, const2),
        pl.BlockSpec(whi.shape, const2),
        pl.BlockSpec(pelo.shape, const2),
        pl.BlockSpec(pehi.shape, const2),
        pl.BlockSpec(w2bd.shape, const2),
        pl.BlockSpec(kaug_s.shape, const2),
        pl.BlockSpec(kaug_w.shape, const2),
        pl.BlockSpec(kaug_c.shape, const2),
        pl.BlockSpec((1, 2 * GROUP, LANES), lambda b, g: (g, 0, 0)),
        pl.BlockSpec(tri.shape, const2),
        pl.BlockSpec(band.shape, const2),
        pl.BlockSpec(ovl.shape, const2),
        pl.BlockSpec(cend.shape, const2),
    ]
    return pl.pallas_call(
        _nsa_kernel,
        grid=(batch, N_KV),
        in_specs=in_specs,
        out_specs=pl.BlockSpec((seq, 2 * LANES), lambda b, g: (b, g)),
        out_shape=jax.ShapeDtypeStruct((batch * seq, D_ATT), bf16),
        scratch_shapes=[
            pltpu.VMEM((seq, LANES), f32),
            pltpu.VMEM((seq, LANES), bf16),
            pltpu.VMEM((seq + WIN_PAD, LANES), bf16),
            pltpu.VMEM((seq, LANES), bf16),
            pltpu.VMEM((seq + WIN_PAD, LANES), bf16),
            pltpu.VMEM((seq, 2 * LANES), f32),
            pltpu.VMEM((seq // LANES, seq // SLC_LEN, LANES), f32),
            pltpu.VMEM((seq // SLC_LEN, seq // LANES, LANES), f32),
            pltpu.VMEM((seq, LANES), f32),
        ],
        compiler_params=pltpu.CompilerParams(
            dimension_semantics=("parallel", "parallel"), vmem_limit_bytes=VMEM_LIMIT),
        name="nsa",
    )(proj, proj, proj, proj, proj, proj, wlo, whi, pelo, pehi, w2bd, kaug_s, kaug_w, kaug_c, qaug, tri, band, ovl, cend)


def _out_kernel(x_ref, ya_ref, gm_ref, ob_ref, wa_ref, wb_ref, wo_ref, fnw_ref, o_ref):
    y_a = jnp.dot(ya_ref[...], wa_ref[...], preferred_element_type=f32)
    y_b = jnp.dot(ob_ref[...], wb_ref[...], preferred_element_type=f32)
    mixed = (gm_ref[:, 0:D_MODEL].astype(f32) * y_a + gm_ref[:, D_MODEL:2 * D_MODEL].astype(f32) * y_b)
    xo = x_ref[...] + jnp.dot(mixed.astype(bf16), wo_ref[...], preferred_element_type=f32)
    r = lax.rsqrt(jnp.mean(xo * xo, axis=-1, keepdims=True) + NORM_EPS)
    o_ref[...] = (xo * r) * fnw_ref[...]


def _out(x2, ya, proj, ob, wa, wb, wo, fnw, tm=1024):
    n_rows = x2.shape[0]
    blk = lambda c: pl.BlockSpec((tm, D_MODEL), lambda i, c=c: (i, c))
    full = lambda a: pl.BlockSpec(a.shape, lambda i: (0, 0))
    return pl.pallas_call(
        _out_kernel,
        grid=(n_rows // tm,),
        in_specs=[
            blk(0),
            blk(0),
            pl.BlockSpec((tm, 2 * D_MODEL), lambda i: (i, OFF_GMIX // (2 * D_MODEL))),
            blk(0),
            full(wa), full(wb), full(wo), full(fnw),
        ],
        out_specs=pl.BlockSpec((tm, D_MODEL), lambda i: (i, 0)),
        out_shape=jax.ShapeDtypeStruct((n_rows, D_MODEL), f32),
        compiler_params=pltpu.CompilerParams(
            dimension_semantics=("parallel",), vmem_limit_bytes=VMEM_LIMIT),
        name="merge_out",
    )(x2, ya, proj, ob, wa, wb, wo, fnw)


def _blockdiag_w1(w1_k, w1_v, lo):
    half = CMP_STRIDE * HEAD_DIM
    wk = w1_k[lo * half:(lo + 1) * half].reshape(CMP_STRIDE, HEAD_DIM, CMP_HID)
    wv = w1_v[lo * half:(lo + 1) * half].reshape(CMP_STRIDE, HEAD_DIM, CMP_HID)
    z = jnp.zeros_like(wk)
    top = jnp.concatenate([wk, z], axis=-1)
    bot = jnp.concatenate([z, wv], axis=-1)
    return jnp.concatenate([top, bot], axis=1).reshape(CMP_STRIDE * 2 * HEAD_DIM, 2 * CMP_HID).astype(bf16)


def _pe_row(pe_k, pe_v, lo):
    sl = slice(lo * CMP_STRIDE, (lo + 1) * CMP_STRIDE)
    return jnp.concatenate([pe_k[sl], pe_v[sl]], axis=-1).reshape(1, CMP_STRIDE * 2 * HEAD_DIM)


def kernel(x, norm_w, w_in, conv_w, conv_b, cmp_pe_k, cmp_pe_v, cmp_w1_k, cmp_w2_k, cmp_w1_v, cmp_w2_v,
           w_proj_a, w_proj_b, w_out, final_norm_w):
    batch, seq, _ = x.shape
    assert norm_w.shape[0] == 1 and seq % SEL_CHUNK == 0 and seq // CMP_STRIDE == N_CMP_PAD
    w_p = _reordered_w_in(w_in[0])
    x2 = x.reshape(batch * seq, D_MODEL)
    proj, ya = _inproj(x2, norm_w, w_p, conv_w[0], conv_b, seq)

    wlo = _blockdiag_w1(cmp_w1_k[0], cmp_w1_v[0], 0)
    whi = _blockdiag_w1(cmp_w1_k[0], cmp_w1_v[0], 1)
    pelo = _pe_row(cmp_pe_k[0], cmp_pe_v[0], 0)
    pehi = _pe_row(cmp_pe_k[0], cmp_pe_v[0], 1)
    zk = jnp.zeros_like(cmp_w2_k[0])
    w2bd = jnp.concatenate([jnp.concatenate([cmp_w2_k[0], zk], axis=1),
                            jnp.concatenate([zk, cmp_w2_v[0]], axis=1)], axis=0).astype(bf16)
    ob = _nsa(proj, wlo, whi, pelo, pehi, w2bd, batch, seq)

    out = _out(x2, ya, proj, ob, w_proj_a[0].astype(bf16), w_proj_b[0].astype(bf16),
               w_out[0].astype(bf16), final_norm_w.reshape(1, D_MODEL))
    return out.reshape(batch, seq, D_MODEL)
```

```python
import functools

import numpy as np
import jax
import jax.numpy as jnp
from jax import lax
from jax.experimental import pallas as pl
from jax.experimental.pallas import tpu as pltpu

D_MODEL = 1024
D_CONV = 1024
CONV_W = 3
N_HEADS = 16
HEAD_DIM = 64
N_KV = 4
GROUP = N_HEADS // N_KV
D_ATT = N_HEADS * HEAD_DIM
D_KV = N_KV * HEAD_DIM
CMP_LEN = 32
CMP_STRIDE = 16
CMP_HID = 128
SLC_LEN = 64
N_SEL = 8
WINDOW = 512
NORM_EPS = 1e-6

LANES = 128
N_CMP_PAD = 128
MASK_BIG = float(2.0 ** 100)
LOG2E = float(np.log2(np.e))
SEL_CHUNK = 512
WIN_TQ = 128
WIN_KEYS = WINDOW + WIN_TQ
WIN_PAD = WINDOW
CW_TQ = 512

OFF_Q = 0
OFF_ZB = 1024
OFF_GMIX = 2048
OFF_KVC = 4096
OFF_KSW = 4608
OFF_VSW = 5120
OFF_GATE = 5632
N_PROJ = 6144

L_HI = 64
L_LO = 67
L_FLAG = 70
L_SEL = 96

VMEM_LIMIT = 56 * 1024 * 1024

f32 = jnp.float32
bf16 = jnp.bfloat16


W_Q = 4 * D_CONV
W_KV = W_Q + D_ATT
W_GATE = W_KV + 6 * D_KV
W_ZB = W_GATE + 3 * N_HEADS
N_IN = W_ZB + D_ATT + 2 * D_MODEL


WP_ROWS = 512
WP_T = 256


def _wprep_kernel(a_ref, b_ref, o_ref):
    j = pl.program_id(0)
    n_plain = W_KV // WP_ROWS
    n_shift = (N_IN - W_ZB) // WP_ROWS
    off = W_ZB % WP_ROWS
    eye = (lax.broadcasted_iota(jnp.int32, (WP_T, WP_T), 0)
           == lax.broadcasted_iota(jnp.int32, (WP_T, WP_T), 1)).astype(bf16)

    def transposed(rows_bf16, kb):
        return _dot_nt(eye, rows_bf16[:, kb * WP_T:(kb + 1) * WP_T])

    def emit(rows):
        rows = rows.astype(bf16)
        for kb in range(D_MODEL // WP_T):
            o_ref[kb * WP_T:(kb + 1) * WP_T, :] = transposed(rows, kb).astype(bf16)

    def interleave(x, x_off, y, y_off):
        parts = []
        for g in range(N_KV):
            parts += [x[x_off + g * HEAD_DIM:x_off + (g + 1) * HEAD_DIM], y[y_off + g * HEAD_DIM:y_off + (g + 1) * HEAD_DIM]]
        return jnp.concatenate(parts, axis=0)

    @pl.when(j < W_Q // WP_ROWS)
    def _():
        emit(a_ref[...])

    @pl.when((j >= W_Q // WP_ROWS) & (j < n_plain))
    def _():
        emit(a_ref[...] * (HEAD_DIM ** -0.5 * LOG2E))

    @pl.when((j >= n_plain) & (j < n_plain + n_shift))
    def _():
        emit(jnp.concatenate([a_ref[off:, :], b_ref[:off, :]], axis=0))

    @pl.when(j == n_plain + n_shift)
    def _():
        a = a_ref[...]
        emit(interleave(a, 0, a, D_KV))

    @pl.when(j == n_plain + n_shift + 1)
    def _():
        emit(interleave(a_ref[...], 0, b_ref[...], 0))

    @pl.when(j == n_plain + n_shift + 2)
    def _():
        emit(interleave(a_ref[...], D_KV, b_ref[...], D_KV))

    @pl.when(j == n_plain + n_shift + 3)
    def _():
        rows = a_ref[:LANES, :].astype(bf16)
        lane = lax.broadcasted_iota(jnp.int32, (1, LANES), 1)
        for kb in range(D_MODEL // WP_T):
            t = transposed(rows, kb)
            for g in range(N_KV):
                shifted = t if g == 0 else pltpu.roll(t, LANES - 3 * GROUP * g, 1)
                o_ref[kb * WP_T:(kb + 1) * WP_T, g * LANES:(g + 1) * LANES] = (
                    jnp.where(lane < 3 * GROUP, shifted, 0.0).astype(bf16))


def _reordered_w_in(w):
    assert w.shape == (D_MODEL, N_IN) and W_KV % WP_ROWS == 0 and (N_IN - W_ZB) % WP_ROWS == 0
    assert W_GATE % WP_ROWS == 0 and 6 * D_KV == 3 * WP_ROWS and WP_ROWS == N_KV * LANES
    n_plain, n_shift = W_KV // WP_ROWS, (N_IN - W_ZB) // WP_ROWS
    first_shift = W_ZB // WP_ROWS
    kv0 = W_KV // WP_ROWS
    n_out = 4 * D_CONV + N_PROJ
    n_steps = n_out // WP_ROWS

    def block_a(j):
        shifted = first_shift + (j - n_plain)
        tail = jnp.where(j == n_plain + n_shift, kv0, jnp.where(j == n_plain + n_shift + 3, W_GATE // WP_ROWS, kv0 + 1))
        return jnp.where(j < n_plain, j, jnp.where(j < n_plain + n_shift, shifted, tail))

    def block_b(j):
        shifted = jnp.clip(first_shift + 1 + (j - n_plain), first_shift + 1, first_shift + n_shift)
        return jnp.where(j < n_plain + n_shift + 1, shifted, kv0 + 2)

    return pl.pallas_call(
        _wprep_kernel,
        grid=(n_steps,),
        in_specs=[pl.BlockSpec((WP_ROWS, D_MODEL), lambda j: (block_a(j), 0)),
                  pl.BlockSpec((WP_ROWS, D_MODEL), lambda j: (block_b(j), 0))],
        out_specs=pl.BlockSpec((D_MODEL, WP_ROWS), lambda j: (0, j)),
        out_shape=jax.ShapeDtypeStruct((D_MODEL, n_out), bf16),
        compiler_params=pltpu.CompilerParams(dimension_semantics=("parallel",), vmem_limit_bytes=VMEM_LIMIT),
        name="wprep",
    )(w.T, w.T)


def _sigmoid(x):
    return 1.0 / (1.0 + jnp.exp(-x))


def _dot_nt(a, b):
    return lax.dot_general(a, b, (((1,), (1,)), ((), ())), preferred_element_type=f32)


CARRY_ROWS = 8
ROW_BLOCK = 128


def _column_activations(lo, hi):
    silu = lambda a: a * _sigmoid(a)
    keep = lambda a: a
    segments = ((OFF_Q, OFF_ZB, keep), (OFF_ZB, OFF_GMIX, silu), (OFF_GMIX, OFF_KVC, _sigmoid),
                (OFF_KVC, OFF_GATE, keep), (OFF_GATE, N_PROJ, _sigmoid))
    return [(max(a, lo) - lo, min(b, hi) - lo, f) for a, b, f in segments if max(a, lo) < min(b, hi)]


def _inproj_kernel(x_ref, nw_ref, w1_ref, w2_ref, cw_ref, cb_ref, o_ref, ya_ref, h_ref, carry_ref,
                   *, tiles_per_seq):
    i = pl.program_id(0)
    j = pl.program_id(1)
    tm = x_ref.shape[0]
    tn = o_ref.shape[1]
    n_stored = N_PROJ // tn
    blocks = [slice(k * ROW_BLOCK, (k + 1) * ROW_BLOCK) for k in range(tm // ROW_BLOCK)]
    product = lambda rows, w: jnp.dot(h_ref[rows, :], w, preferred_element_type=f32)

    @pl.when((i == 0) & (j == 0))
    def _():
        carry_ref[...] = jnp.zeros_like(carry_ref)

    for t in range(n_stored):
        @pl.when(j == t)
        def _(t=t):
            for rows in blocks:
                if t == 0:
                    x = x_ref[rows, :]
                    r = lax.rsqrt(jnp.mean(x * x, axis=-1, keepdims=True) + NORM_EPS)
                    h_ref[rows, :] = ((x * r) * nw_ref[...]).astype(bf16)
                acc = product(rows, w1_ref[...])
                for lo, hi, act in _column_activations(t * tn, (t + 1) * tn):
                    o_ref[rows, lo:hi] = act(acc[:, lo:hi]).astype(bf16)

    @pl.when(j == n_stored)
    def _():
        lead = carry_ref[...] * (i % tiles_per_seq != 0).astype(f32)
        cw = cw_ref[...]
        for rows in blocks:
            u = product(rows, w2_ref[:, :D_CONV]) * product(rows, w1_ref[:, :D_CONV])
            win = jnp.concatenate([lead, u], axis=0)
            u1 = pltpu.roll(win, 1, 0)[CARRY_ROWS:, :]
            u2 = pltpu.roll(win, 2, 0)[CARRY_ROWS:, :]
            y = cb_ref[...] + cw[0:1, :] * u2 + cw[1:2, :] * u1 + cw[2:3, :] * u
            z = product(rows, w2_ref[:, D_CONV:])
            ya_ref[rows, :] = (product(rows, w1_ref[:, D_CONV:]) * y * (z * _sigmoid(z))).astype(bf16)
            lead = u[ROW_BLOCK - CARRY_ROWS:, :]
        carry_ref[...] = lead


def _inproj(x2, norm_w, w_p, conv_w, conv_b, seq, tm=1024):
    n_rows = x2.shape[0]
    tn = 2 * D_CONV
    assert N_PROJ % tn == 0 and seq % tm == 0 and tm % ROW_BLOCK == 0 and w_p.shape[1] == 2 * tn + N_PROJ
    n_stored = N_PROJ // tn
    first = lambda i, j: (0, jnp.where(j < n_stored, j + 2, 0))
    second = lambda i, j: (0, 1)
    return pl.pallas_call(
        functools.partial(_inproj_kernel, tiles_per_seq=seq // tm),
        grid=(n_rows // tm, n_stored + 1),
        in_specs=[
            pl.BlockSpec((tm, D_MODEL), lambda i, j: (i, 0)),
            pl.BlockSpec((1, D_MODEL), lambda i, j: (0, 0)),
            pl.BlockSpec((D_MODEL, tn), first),
            pl.BlockSpec((D_MODEL, tn), second),
            pl.BlockSpec(conv_w.shape, lambda i, j: (0, 0)),
            pl.BlockSpec(conv_b.shape, lambda i, j: (0, 0)),
        ],
        out_specs=[
            pl.BlockSpec((tm, tn), lambda i, j: (i, jnp.minimum(j, n_stored - 1))),
            pl.BlockSpec((tm, D_CONV), lambda i, j: (i, 0)),
        ],
        out_shape=[jax.ShapeDtypeStruct((n_rows, N_PROJ), bf16), jax.ShapeDtypeStruct((n_rows, D_CONV), bf16)],
        scratch_shapes=[
            pltpu.VMEM((tm, D_MODEL), bf16),
            pltpu.VMEM((CARRY_ROWS, D_CONV), f32),
        ],
        compiler_params=pltpu.CompilerParams(
            dimension_semantics=("arbitrary", "arbitrary"), vmem_limit_bytes=VMEM_LIMIT),
        name="inproj",
    )(x2, norm_w, w_p, w_p, conv_w, conv_b)


def _softmax_step(s, v, carry):
    m, acc = carry
    m_new = jnp.maximum(m, jnp.max(s, axis=-1, keepdims=True))
    alpha = jnp.exp2(m - m_new)
    p = jnp.exp2(s - m_new)
    acc = alpha * acc + jnp.dot(p.astype(bf16), v, preferred_element_type=f32)
    return m_new, acc


def _stack_heads(q_tile, aug_rows, extra):
    lane = lax.broadcasted_iota(jnp.int32, (1, LANES), 1)
    pairs = (q_tile[:, :LANES], q_tile[:, LANES:])
    parts = []
    for r in range(GROUP):
        qh = pairs[r // 2]
        if r % 2 == 1:
            qh = pltpu.roll(qh, HEAD_DIM, 1)
        aug = aug_rows[r:r + 1, :]
        if extra is not None:
            aug = aug + extra
        parts.append(jnp.where(lane < HEAD_DIM, qh, aug).astype(bf16))
    return jnp.concatenate(parts, axis=0)


def _merge_heads(vals, lane):
    return (jnp.where(lane < HEAD_DIM, vals[0], vals[1]), jnp.where(lane < HEAD_DIM, vals[2], vals[3]))


def _nsa_kernel(q_ref, ksw_ref, vsw_ref, gate_ref, kvc_ref, zb_ref, wlo_ref, whi_ref, pelo_ref, pehi_ref, w2_ref,
                kaug_s_ref, kaug_w_ref, kaug_c_ref, qaug_ref, tri_ref, band_ref, ovl_ref, cend_ref,
                o_ref,
                kvcf_ref, ks_ref, kw_ref, vs_ref, vw_ref, ocw_ref, imp_ref, ns_ref, selq_ref):
    seq = q_ref.shape[0]
    lane = lax.broadcasted_iota(jnp.int32, (1, LANES), 1)
    low = lane < HEAD_DIM

    kvcf_ref[...] = kvc_ref[...].astype(f32)
    xcat = jnp.concatenate([kvcf_ref[pl.ds(l, N_CMP_PAD, stride=CMP_STRIDE), :] for l in range(CMP_STRIDE)],
                           axis=1)
    xa = (xcat + pelo_ref[...]).astype(bf16)
    xb = (xcat + pehi_ref[...]).astype(bf16)
    a = jnp.dot(xa, wlo_ref[...], preferred_element_type=f32)
    bm = jnp.dot(xb, whi_ref[...], preferred_element_type=f32)
    h = a + pltpu.roll(bm, N_CMP_PAD - 1, 0)
    hid = h * _sigmoid(h)
    kcvc = jnp.dot(hid.astype(bf16), w2_ref[...], preferred_element_type=f32)

    ksw = ksw_ref[...]
    vsw = vsw_ref[...]
    one = jnp.ones((1, LANES), bf16)
    ks_ref[...] = jnp.where(low, ksw, kaug_s_ref[...])
    kwin = pltpu.roll(ksw.astype(f32), HEAD_DIM, 1).astype(bf16)
    kw_ref[0:WIN_PAD, :] = kaug_w_ref[0:WIN_PAD, :]
    kw_ref[WIN_PAD:, :] = jnp.where(low, kwin, kaug_w_ref[WIN_PAD:, :])
    vs_ref[...] = jnp.where(low, vsw, one)
    vw_ref[0:WIN_PAD, :] = jnp.zeros((WIN_PAD, LANES), bf16)
    vw_ref[WIN_PAD:, :] = jnp.where(low, one, vsw)

    qaug = qaug_ref[0]
    kc_aug = jnp.where(low, kcvc, kaug_c_ref[...]).astype(bf16)
    vc_rows = jnp.concatenate([jnp.zeros((N_CMP_PAD, LANES), bf16), kcvc.astype(bf16)], axis=1)
    zeros_w = jnp.zeros((WIN_KEYS, LANES), bf16)
    ovl = ovl_ref[...]

    def cw_body(i, _):
        t0 = pl.multiple_of(i * CW_TQ, CW_TQ)
        sig = gate_ref[pl.ds(t0, CW_TQ), :].astype(f32)
        o_w, o_c = [], []
        n_tiles = CW_TQ // WIN_TQ

        def tile_scores(h):
            tw = pl.multiple_of(t0 + h * WIN_TQ, WIN_TQ)
            qw = _stack_heads(q_ref[pl.ds(tw, WIN_TQ), :].astype(f32), qaug[GROUP:2 * GROUP], None)
            k_all = jnp.concatenate([kw_ref[pl.ds(tw, WIN_KEYS), :], kc_aug], axis=0)
            return _dot_nt(qw, k_all)

        s_next = tile_scores(0)
        for h in range(n_tiles):
            tw = pl.multiple_of(t0 + h * WIN_TQ, WIN_TQ)
            s = s_next
            if h + 1 < n_tiles:
                s_next = tile_scores(h + 1)

            sw = jnp.concatenate([s[:, :WIN_TQ] + band_ref[:, :WIN_TQ], s[:, WIN_TQ:WINDOW],
                                  s[:, WINDOW:WIN_KEYS] + band_ref[:, WINDOW:]], axis=1)
            p_w = jnp.exp2(sw - jnp.max(sw, axis=-1, keepdims=True))

            ok = cend_ref[...] <= tw
            sm = jnp.where(ok, s[:, WIN_KEYS:], -jnp.inf)
            m = jnp.max(sm, axis=-1, keepdims=True)
            m = jnp.where(m > -jnp.inf, m, 0.0)
            e = jnp.where(ok, jnp.exp2(sm - m), 0.0)
            p_c = e / jnp.maximum(jnp.sum(e, axis=-1, keepdims=True), 1e-30)

            p_all = jnp.concatenate([p_w.astype(bf16), p_c.astype(bf16)], axis=1)
            v_all = jnp.concatenate(
                [jnp.concatenate([vw_ref[pl.ds(tw, WIN_KEYS), :], zeros_w], axis=1), vc_rows], axis=0)
            acc = jnp.dot(p_all, v_all, preferred_element_type=f32)
            acc_w = acc[:, :LANES]
            o_w.append(acc_w / jnp.maximum(acc_w[:, 0:1], 1e-30))
            o_c.append(acc[:, LANES:])

            ps = p_c[0:WIN_TQ] + p_c[WIN_TQ:2 * WIN_TQ] + p_c[2 * WIN_TQ:3 * WIN_TQ] + p_c[3 * WIN_TQ:4 * WIN_TQ]
            p1 = ps.astype(bf16)
            r1 = ps - p1.astype(f32)
            p2 = r1.astype(bf16)
            p3 = (r1 - p2.astype(f32)).astype(bf16)
            imp_ref[i * (CW_TQ // WIN_TQ) + h] = _dot_nt(ovl, p1) + _dot_nt(ovl, p2) + _dot_nt(ovl, p3)

        vals = []
        for r in range(GROUP):
            head = lambda parts: jnp.concatenate([o[r * WIN_TQ:(r + 1) * WIN_TQ, :] for o in parts], axis=0)
            a = head(o_c) * sig[:, 3 * r:3 * r + 1] + head(o_w) * sig[:, 3 * r + 2:3 * r + 3]
            vals.append(pltpu.roll(a, HEAD_DIM, 1) if r % 2 == 0 else a)
        c0, c1 = _merge_heads(vals, lane)
        ocw_ref[pl.ds(t0, CW_TQ), 0:LANES] = c0
        ocw_ref[pl.ds(t0, CW_TQ), LANES:2 * LANES] = c1
        return 0

    lax.fori_loop(0, seq // CW_TQ, cw_body, 0)

    n_slc = seq // SLC_LEN
    n_tt = seq // LANES
    shp = (n_tt, LANES)
    cur = (lax.broadcasted_iota(jnp.int32, shp, 0) * LANES + lax.broadcasted_iota(jnp.int32, shp, 1)) // SLC_LEN
    vals = []
    for j in range(n_slc):
        forced = (cur == j) | (cur == j + 1)
        vj = jnp.where(cur >= j, imp_ref[:, j, :], -jnp.inf)
        vals.append(jnp.full(shp, jnp.inf, f32) if j == 0 else jnp.where(forced, jnp.inf, vj))
    wins = [jnp.zeros(shp, f32) for _ in range(n_slc)]
    losses = [jnp.zeros(shp, f32) for _ in range(n_slc)]
    for a in range(n_slc):
        for b in range(a + 1, n_slc):
            a_first = jnp.where(vals[a] >= vals[b], 1.0, 0.0)
            wins[a] = wins[a] + a_first
            losses[b] = losses[b] + a_first
    for j in range(n_slc):
        rank = losses[j] + (float(n_slc - 1 - j) - wins[j])
        sel = (rank < float(N_SEL)) & (vals[j] > -jnp.inf)
        ns_ref[j] = jnp.where(sel, 0.0, 1.0)
    eye = (lax.broadcasted_iota(jnp.int32, (LANES, LANES), 0)
           == lax.broadcasted_iota(jnp.int32, (LANES, LANES), 1)).astype(bf16)
    zpad = jnp.zeros((L_SEL, LANES), f32)
    for tt in range(n_tt):
        z = jnp.concatenate([zpad, ns_ref[:, tt, :]], axis=0).astype(bf16)
        selq_ref[tt * LANES:(tt + 1) * LANES, :] = _dot_nt(eye, z) * (-MASK_BIG)

    half = SEL_CHUNK // 2
    half_rows = GROUP * half

    def stacked_q(i):
        parts = []
        for h in range(2):
            tok = slice(i * SEL_CHUNK + h * half, i * SEL_CHUNK + (h + 1) * half)
            parts.append(_stack_heads(q_ref[tok, :].astype(f32), qaug[0:GROUP], selq_ref[tok, :]))
        return parts

    def causal(s):
        tri = tri_ref[...]
        return jnp.concatenate([s[r * half:(r + 1) * half, :] + tri for r in range(GROUP)], axis=0)

    def scores(qs, i, c):
        k0 = c * SEL_CHUNK
        if c < i:
            return (_dot_nt(jnp.concatenate(qs, axis=0), ks_ref[k0:k0 + SEL_CHUNK, :]),)
        s_a = causal(_dot_nt(qs[0], ks_ref[k0:k0 + half, :]))
        s_b = _dot_nt(qs[1], ks_ref[k0:k0 + SEL_CHUNK, :])
        return (s_a, jnp.concatenate([s_b[:, :half], causal(s_b[:, half:])], axis=1))

    def softmax_step(s, v, carry):
        if carry is None:
            m = jnp.max(s, axis=-1, keepdims=True)
            return m, jnp.dot(jnp.exp2(s - m).astype(bf16), v, preferred_element_type=f32)
        return _softmax_step(s, v, carry)

    def finish(i, acc):
        tok = slice(i * SEL_CHUNK, (i + 1) * SEL_CHUNK)
        o_s = acc / jnp.maximum(acc[:, HEAD_DIM:HEAD_DIM + 1], 1e-30)
        sig = gate_ref[tok, :].astype(f32)
        vals = []
        for r in range(GROUP):
            a_s = jnp.concatenate([o_s[h * half_rows + r * half:h * half_rows + (r + 1) * half, :]
                                   for h in range(2)], axis=0) * sig[:, 3 * r + 1:3 * r + 2]
            vals.append(a_s if r % 2 == 0 else pltpu.roll(a_s, HEAD_DIM, 1))
        c0, c1 = _merge_heads(vals, lane)
        o_ref[tok, 0:LANES] = ((ocw_ref[tok, 0:LANES] + c0) * zb_ref[tok, 0:LANES].astype(f32)).astype(bf16)
        o_ref[tok, LANES:2 * LANES] = (
            (ocw_ref[tok, LANES:2 * LANES] + c1) * zb_ref[tok, LANES:2 * LANES].astype(f32)).astype(bf16)

    steps = [(i, c) for i in range(seq // SEL_CHUNK) for c in range(i + 1)]
    qs = stacked_q(0)
    s_next = scores(qs, 0, 0)
    carry = None
    for k, (i, c) in enumerate(steps):
        s = s_next
        if k + 1 < len(steps):
            i2, c2 = steps[k + 1]
            if i2 != i:
                qs = stacked_q(i2)
            s_next = scores(qs, i2, c2)
        k0 = c * SEL_CHUNK
        if c < i:
            carry = softmax_step(s[0], vs_ref[k0:k0 + SEL_CHUNK, :], carry)
        else:
            rows_a = None if carry is None else tuple(a[:half_rows] for a in carry)
            rows_b = None if carry is None else tuple(a[half_rows:] for a in carry)
            _, acc_a = softmax_step(s[0], vs_ref[k0:k0 + half, :], rows_a)
            _, acc_b = softmax_step(s[1], vs_ref[k0:k0 + SEL_CHUNK, :], rows_b)
            finish(i, jnp.concatenate([acc_a, acc_b], axis=0))
            carry = None


def _position_lanes(pos):
    out = np.zeros((pos.shape[0], LANES), np.float32)
    out[:, L_HI:L_HI + 3] = ((pos // 64) * 64)[:, None]
    out[:, L_LO:L_LO + 3] = (pos % 64)[:, None]
    return out


def _nsa_tables(seq):
    pos = np.arange(seq)
    kaug_s = _position_lanes(pos)
    kaug_s[pos, L_SEL + pos // SLC_LEN] = 1.0
    kaug_w = np.zeros((seq + WIN_PAD, LANES), np.float32)
    kaug_w[WIN_PAD:] = _position_lanes(pos)
    kaug_w[:WIN_PAD, L_FLAG] = 1.0
    kaug_c = _position_lanes(np.arange(N_CMP_PAD) * CMP_STRIDE + CMP_LEN - 1)
    tl = np.arange(SEL_CHUNK // 2)[:, None]
    kk = np.arange(SEL_CHUNK // 2)[None, :]
    tri = np.where(kk <= tl, 0.0, -MASK_BIG).astype(np.float32)
    tl = np.arange(WIN_TQ)[:, None]
    kk = np.arange(WIN_KEYS)[None, :]
    dist = tl + WIN_PAD - kk
    band = np.tile(np.where((dist >= 0) & (dist < WINDOW), 0.0, -MASK_BIG).astype(np.float32), (GROUP, 1))
    c = np.arange(N_CMP_PAD)[None, :]
    j = np.arange(seq // SLC_LEN)[:, None]
    c_start, c_end, s_start = c * CMP_STRIDE, c * CMP_STRIDE + CMP_LEN - 1, j * SLC_LEN
    ovl = ((c_start <= s_start + SLC_LEN - 1) & (c_end >= s_start) & (c < seq // CMP_STRIDE - 1)).astype(np.float32)
    cend = np.tile((c * CMP_STRIDE + CMP_LEN - 1) - tl, (GROUP, 1)).astype(np.int32)
    return (jnp.asarray(kaug_s, bf16), jnp.asarray(kaug_w, bf16), jnp.asarray(kaug_c, f32),
            jnp.asarray(tri), jnp.asarray(band), jnp.asarray(ovl, bf16), jnp.asarray(cend))


def _query_aug():
    slopes = jnp.exp2(-8.0 * jnp.arange(1, N_HEADS + 1, dtype=f32) / N_HEADS).reshape(N_KV, GROUP) * LOG2E
    hi =slopes.astype(bf16).astype(f32)
    mid = (slopes - hi).astype(bf16).astype(f32)
    lo = (slopes - hi - mid).astype(bf16).astype(f32)
    pieces = jnp.stack([hi, mid, lo, hi, mid, lo], axis=-1)
    base = jnp.zeros((N_KV, GROUP, LANES), f32).at[:, :, L_HI:L_HI + 6].set(pieces)
    return jnp.concatenate([base, base.at[:, :, L_FLAG].set(-MASK_BIG)], axis=1)


def _nsa(proj, wlo, whi, pelo, pehi, w2bd, batch, seq):
    kaug_s, kaug_w, kaug_c, tri, band, ovl, cend = _nsa_tables(seq)
    qaug = _query_aug()
    const2 = lambda b, g: (0, 0)
    in_specs = [
        pl.BlockSpec((seq, 2 * LANES), lambda b, g: (b, OFF_Q // (2 * LANES) + g)),
        pl.BlockSpec((seq, LANES), lambda b, g: (b, OFF_KSW // LANES + g)),
        pl.BlockSpec((seq, LANES), lambda b, g: (b, OFF_VSW // LANES + g)),
        pl.BlockSpec((seq, LANES), lambda b, g: (b, OFF_GATE // LANES + g)),
        pl.BlockSpec((seq, LANES), lambda b, g: (b, OFF_KVC // LANES + g)),
        pl.BlockSpec((seq, 2 * LANES), lambda b, g: (b, OFF_ZB // (2 * LANES) + g)),
        pl.BlockSpec(wlo.shape, const2),
        pl.BlockSpec(whi.shape, const2),
        pl.BlockSpec(pelo.shape, const2),
        pl.BlockSpec(pehi.shape, const2),
        pl.BlockSpec(w2bd.shape, const2),
        pl.BlockSpec(kaug_s.shape, const2),
        pl.BlockSpec(kaug_w.shape, const2),
        pl.BlockSpec(kaug_c.shape, const2),
        pl.BlockSpec((1, 2 * GROUP, LANES), lambda b, g: (g, 0, 0)),
        pl.BlockSpec(tri.shape, const2),
        pl.BlockSpec(band.shape, const2),
        pl.BlockSpec(ovl.shape, const2),
        pl.BlockSpec(cend.shape, const2),
    ]
    return pl.pallas_call(
        _nsa_kernel,
        grid=(batch, N_KV),
        in_specs=in_specs,
        out_specs=pl.BlockSpec((seq, 2 * LANES), lambda b, g: (b, g)),
        out_shape=jax.ShapeDtypeStruct((batch * seq, D_ATT), bf16),
        scratch_shapes=[
            pltpu.VMEM((seq, LANES), f32),
            pltpu.VMEM((seq, LANES), bf16),
            pltpu.VMEM((seq + WIN_PAD, LANES), bf16),
            pltpu.VMEM((seq, LANES), bf16),
            pltpu.VMEM((seq + WIN_PAD, LANES), bf16),
            pltpu.VMEM((seq, 2 * LANES), f32),
            pltpu.VMEM((seq // LANES, seq // SLC_LEN, LANES), f32),
            pltpu.VMEM((seq // SLC_LEN, seq // LANES, LANES), f32),
            pltpu.VMEM((seq, LANES), f32),
        ],
        compiler_params=pltpu.CompilerParams(
            dimension_semantics=("parallel", "parallel"), vmem_limit_bytes=VMEM_LIMIT),
        name="nsa",
    )(proj, proj, proj, proj, proj, proj, wlo, whi, pelo, pehi, w2bd, kaug_s, kaug_w, kaug_c, qaug, tri, band, ovl, cend)


def _out_kernel(x_ref, ya_ref, gm_ref, ob_ref, wa_ref, wb_ref, wo_ref, fnw_ref, o_ref):
    y_a = jnp.dot(ya_ref[...], wa_ref[...], preferred_element_type=f32)
    y_b = jnp.dot(ob_ref[...], wb_ref[...], preferred_element_type=f32)
    mixed = (gm_ref[:, 0:D_MODEL].astype(f32) * y_a + gm_ref[:, D_MODEL:2 * D_MODEL].astype(f32) * y_b)
    xo = x_ref[...] + jnp.dot(mixed.astype(bf16), wo_ref[...], preferred_element_type=f32)
    r = lax.rsqrt(jnp.mean(xo * xo, axis=-1, keepdims=True) + NORM_EPS)
    o_ref[...] = (xo * r) * fnw_ref[...]


def _out(x2, ya, proj, ob, wa, wb, wo, fnw, tm=1024):
    n_rows = x2.shape[0]
    blk = lambda c: pl.BlockSpec((tm, D_MODEL), lambda i, c=c: (i, c))
    full = lambda a: pl.BlockSpec(a.shape, lambda i: (0, 0))
    return pl.pallas_call(
        _out_kernel,
        grid=(n_rows // tm,),
        in_specs=[
            blk(0),
            blk(0),
            pl.BlockSpec((tm, 2 * D_MODEL), lambda i: (i, OFF_GMIX // (2 * D_MODEL))),
            blk(0),
            full(wa), full(wb), full(wo), full(fnw),
        ],
        out_specs=pl.BlockSpec((tm, D_MODEL), lambda i: (i, 0)),
        out_shape=jax.ShapeDtypeStruct((n_rows, D_MODEL), f32),
        compiler_params=pltpu.CompilerParams(
            dimension_semantics=("parallel",), vmem_limit_bytes=VMEM_LIMIT),
        name="merge_out",
    )(x2, ya, proj, ob, wa, wb, wo, fnw)


def _blockdiag_w1(w1_k, w1_v, lo):
    half = CMP_STRIDE * HEAD_DIM
    wk = w1_k[lo * half:(lo + 1) * half].reshape(CMP_STRIDE, HEAD_DIM, CMP_HID)
    wv = w1_v[lo * half:(lo + 1) * half].reshape(CMP_STRIDE, HEAD_DIM, CMP_HID)
    z = jnp.zeros_like(wk)
    top = jnp.concatenate([wk, z], axis=-1)
    bot = jnp.concatenate([z, wv], axis=-1)
    return jnp.concatenate([top, bot], axis=1).reshape(CMP_STRIDE * 2 * HEAD_DIM, 2 * CMP_HID).astype(bf16)


def _pe_row(pe_k, pe_v, lo):
    sl = slice(lo * CMP_STRIDE, (lo + 1) * CMP_STRIDE)
    return jnp.concatenate([pe_k[sl], pe_v[sl]], axis=-1).reshape(1, CMP_STRIDE * 2 * HEAD_DIM)


def kernel(x, norm_w, w_in, conv_w, conv_b, cmp_pe_k, cmp_pe_v, cmp_w1_k, cmp_w2_k, cmp_w1_v, cmp_w2_v,
           w_proj_a, w_proj_b, w_out, final_norm_w):
    batch, seq, _ = x.shape
    assert norm_w.shape[0] == 1 and seq % SEL_CHUNK == 0 and seq // CMP_STRIDE == N_CMP_PAD
    w_p = _reordered_w_in(w_in[0])
    x2 = x.reshape(batch * seq, D_MODEL)
    proj, ya = _inproj(x2, norm_w, w_p, conv_w[0], conv_b, seq)

    wlo = _blockdiag_w1(cmp_w1_k[0], cmp_w1_v[0], 0)
    whi = _blockdiag_w1(cmp_w1_k[0], cmp_w1_v[0], 1)
    pelo = _pe_row(cmp_pe_k[0], cmp_pe_v[0], 0)
    pehi = _pe_row(cmp_pe_k[0], cmp_pe_v[0], 1)
    zk = jnp.zeros_like(cmp_w2_k[0])
    w2bd = jnp.concatenate([jnp.concatenate([cmp_w2_k[0], zk], axis=1),
                            jnp.concatenate([zk, cmp_w2_v[0]], axis=1)], axis=0).astype(bf16)
    ob = _nsa(proj, wlo, whi, pelo, pehi, w2bd, batch, seq)

    out = _out(x2, ya, proj, ob, w_proj_a[0].astype(bf16), w_proj_b[0].astype(bf16),
               w_out[0].astype(bf16), final_norm_w.reshape(1, D_MODEL))
    return out.reshape(batch, seq, D_MODEL)
```

```python
import functools

import numpy as np
import jax
import jax.numpy as jnp
from jax import lax
from jax.experimental import pallas as pl
from jax.experimental.pallas import tpu as pltpu

D_MODEL = 1024
D_CONV = 1024
CONV_W = 3
N_HEADS = 16
HEAD_DIM = 64
N_KV = 4
GROUP = N_HEADS // N_KV
D_ATT = N_HEADS * HEAD_DIM
D_KV = N_KV * HEAD_DIM
CMP_LEN = 32
CMP_STRIDE = 16
CMP_HID = 128
SLC_LEN = 64
N_SEL = 8
WINDOW = 512
NORM_EPS = 1e-6

LANES = 128
N_CMP_PAD = 128
MASK_BIG = float(2.0 ** 100)
LOG2E = float(np.log2(np.e))
SEL_CHUNK = 512
WIN_TQ = 128
WIN_KEYS = WINDOW + WIN_TQ
WIN_PAD = WINDOW
CW_TQ = 512

OFF_Q = 0
OFF_ZB = 1024
OFF_GMIX = 2048
OFF_KVC = 4096
OFF_KSW = 4608
OFF_VSW = 5120
OFF_GATE = 5632
N_PROJ = 6144

L_HI = 64
L_LO = 67
L_FLAG = 70
L_SEL = 96

VMEM_LIMIT = 56 * 1024 * 1024

f32 = jnp.float32
bf16 = jnp.bfloat16


W_Q = 4 * D_CONV
W_KV = W_Q + D_ATT
W_GATE = W_KV + 6 * D_KV
W_ZB = W_GATE + 3 * N_HEADS
N_IN = W_ZB + D_ATT + 2 * D_MODEL


WP_ROWS = 512
WP_T = 256


def _wprep_kernel(a_ref, b_ref, o_ref):
    j = pl.program_id(0)
    n_plain = W_KV // WP_ROWS
    n_shift = (N_IN - W_ZB) // WP_ROWS
    off = W_ZB % WP_ROWS
    eye = (lax.broadcasted_iota(jnp.int32, (WP_T, WP_T), 0)
           == lax.broadcasted_iota(jnp.int32, (WP_T, WP_T), 1)).astype(bf16)

    def transposed(rows_bf16, kb):
        return _dot_nt(eye, rows_bf16[:, kb * WP_T:(kb + 1) * WP_T])

    def emit(rows):
        rows = rows.astype(bf16)
        for kb in range(D_MODEL // WP_T):
            o_ref[kb * WP_T:(kb + 1) * WP_T, :] = transposed(rows, kb).astype(bf16)

    def interleave(x, x_off, y, y_off):
        parts = []
        for g in range(N_KV):
            parts += [x[x_off + g * HEAD_DIM:x_off + (g + 1) * HEAD_DIM], y[y_off + g * HEAD_DIM:y_off + (g + 1) * HEAD_DIM]]
        return jnp.concatenate(parts, axis=0)

    @pl.when(j < W_Q // WP_ROWS)
    def _():
        emit(a_ref[...])

    @pl.when((j >= W_Q // WP_ROWS) & (j < n_plain))
    def _():
        emit(a_ref[...] * (HEAD_DIM ** -0.5 * LOG2E))

    @pl.when((j >= n_plain) & (j < n_plain + n_shift))
    def _():
        emit(jnp.concatenate([a_ref[off:, :], b_ref[:off, :]], axis=0))

    @pl.when(j == n_plain + n_shift)
    def _():
        a = a_ref[...]
        emit(interleave(a, 0, a, D_KV))

    @pl.when(j == n_plain + n_shift + 1)
    def _():
        emit(interleave(a_ref[...], 0, b_ref[...], 0))

    @pl.when(j == n_plain + n_shift + 2)
    def _():
        emit(interleave(a_ref[...], D_KV, b_ref[...], D_KV))

    @pl.when(j == n_plain + n_shift + 3)
    def _():
        rows = a_ref[:LANES, :].astype(bf16)
        lane = lax.broadcasted_iota(jnp.int32, (1, LANES), 1)
        for kb in range(D_MODEL // WP_T):
            t = transposed(rows, kb)
            for g in range(N_KV):
                shifted = t if g == 0 else pltpu.roll(t, LANES - 3 * GROUP * g, 1)
                o_ref[kb * WP_T:(kb + 1) * WP_T, g * LANES:(g + 1) * LANES] = (
                    jnp.where(lane < 3 * GROUP, shifted, 0.0).astype(bf16))


def _reordered_w_in(w):
    assert w.shape == (D_MODEL, N_IN) and W_KV % WP_ROWS == 0 and (N_IN - W_ZB) % WP_ROWS == 0
    assert W_GATE % WP_ROWS == 0 and 6 * D_KV == 3 * WP_ROWS and WP_ROWS == N_KV * LANES
    n_plain, n_shift = W_KV // WP_ROWS, (N_IN - W_ZB) // WP_ROWS
    first_shift = W_ZB // WP_ROWS
    kv0 = W_KV // WP_ROWS
    n_out = 4 * D_CONV + N_PROJ
    n_steps = n_out // WP_ROWS

    def block_a(j):
        shifted = first_shift + (j - n_plain)
        tail = jnp.where(j == n_plain + n_shift, kv0, jnp.where(j == n_plain + n_shift + 3, W_GATE // WP_ROWS, kv0 + 1))
        return jnp.where(j < n_plain, j, jnp.where(j < n_plain + n_shift, shifted, tail))

    def block_b(j):
        shifted = jnp.clip(first_shift + 1 + (j - n_plain), first_shift + 1, first_shift + n_shift)
        return jnp.where(j < n_plain + n_shift + 1, shifted, kv0 + 2)

    return pl.pallas_call(
        _wprep_kernel,
        grid=(n_steps,),
        in_specs=[pl.BlockSpec((WP_ROWS, D_MODEL), lambda j: (block_a(j), 0)),
                  pl.BlockSpec((WP_ROWS, D_MODEL), lambda j: (block_b(j), 0))],
        out_specs=pl.BlockSpec((D_MODEL, WP_ROWS), lambda j: (0, j)),
        out_shape=jax.ShapeDtypeStruct((D_MODEL, n_out), bf16),
        compiler_params=pltpu.CompilerParams(dimension_semantics=("parallel",), vmem_limit_bytes=VMEM_LIMIT),
        name="wprep",
    )(w.T, w.T)


def _sigmoid(x):
    return 1.0 / (1.0 + jnp.exp(-x))


def _dot_nt(a, b):
    return lax.dot_general(a, b, (((1,), (1,)), ((), ())), preferred_element_type=f32)


CARRY_ROWS = 8
ROW_BLOCK = 256


def _column_activations(lo, hi):
    silu = lambda a: a * _sigmoid(a)
    keep = lambda a: a
    segments = ((OFF_Q, OFF_ZB, keep), (OFF_ZB, OFF_GMIX, silu), (OFF_GMIX, OFF_KVC, _sigmoid),
                (OFF_KVC, OFF_GATE, keep), (OFF_GATE, N_PROJ, _sigmoid))
    return [(max(a, lo) - lo, min(b, hi) - lo, f) for a, b, f in segments if max(a, lo) < min(b, hi)]


def _inproj_kernel(x_ref, nw_ref, w1_ref, w2_ref, cw_ref, cb_ref, o_ref, ya_ref, h_ref, carry_ref,
                   *, tiles_per_seq):
    i = pl.program_id(0)
    j = pl.program_id(1)
    tm = x_ref.shape[0]
    tn = o_ref.shape[1]
    n_stored = N_PROJ // tn
    blocks = [slice(k * ROW_BLOCK, (k + 1) * ROW_BLOCK) for k in range(tm // ROW_BLOCK)]
    product = lambda rows, w: jnp.dot(h_ref[rows, :], w, preferred_element_type=f32)

    @pl.when((i == 0) & (j == 0))
    def _():
        carry_ref[...] = jnp.zeros_like(carry_ref)

    for t in range(n_stored):
        @pl.when(j == t)
        def _(t=t):
            for rows in blocks:
                if t == 0:
                    x = x_ref[rows, :]
                    r = lax.rsqrt(jnp.mean(x * x, axis=-1, keepdims=True) + NORM_EPS)
                    h_ref[rows, :] = ((x * r) * nw_ref[...]).astype(bf16)
                acc = product(rows, w1_ref[...])
                for lo, hi, act in _column_activations(t * tn, (t + 1) * tn):
                    o_ref[rows, lo:hi] = act(acc[:, lo:hi]).astype(bf16)

    @pl.when(j == n_stored)
    def _():
        lead = carry_ref[...] * (i % tiles_per_seq != 0).astype(f32)
        cw = cw_ref[...]
        for rows in blocks:
            u = product(rows, w2_ref[:, :D_CONV]) * product(rows, w1_ref[:, :D_CONV])
            win = jnp.concatenate([lead, u], axis=0)
            u1 = pltpu.roll(win, 1, 0)[CARRY_ROWS:, :]
            u2 = pltpu.roll(win, 2, 0)[CARRY_ROWS:, :]
            y = cb_ref[...] + cw[0:1, :] * u2 + cw[1:2, :] * u1 + cw[2:3, :] * u
            z = product(rows, w2_ref[:, D_CONV:])
            ya_ref[rows, :] = (product(rows, w1_ref[:, D_CONV:]) * y * (z * _sigmoid(z))).astype(bf16)
            lead = u[ROW_BLOCK - CARRY_ROWS:, :]
        carry_ref[...] = lead


def _inproj(x2, norm_w, w_p, conv_w, conv_b, seq, tm=1024):
    n_rows = x2.shape[0]
    tn = 2 * D_CONV
    assert N_PROJ % tn == 0 and seq % tm == 0 and tm % ROW_BLOCK == 0 and w_p.shape[1] == 2 * tn + N_PROJ
    n_stored = N_PROJ // tn
    first = lambda i, j: (0, jnp.where(j < n_stored, j + 2, 0))
    second = lambda i, j: (0, 1)
    return pl.pallas_call(
        functools.partial(_inproj_kernel, tiles_per_seq=seq // tm),
        grid=(n_rows // tm, n_stored + 1),
        in_specs=[
            pl.BlockSpec((tm, D_MODEL), lambda i, j: (i, 0)),
            pl.BlockSpec((1, D_MODEL), lambda i, j: (0, 0)),
            pl.BlockSpec((D_MODEL, tn), first),
            pl.BlockSpec((D_MODEL, tn), second),
            pl.BlockSpec(conv_w.shape, lambda i, j: (0, 0)),
            pl.BlockSpec(conv_b.shape, lambda i, j: (0, 0)),
        ],
        out_specs=[
            pl.BlockSpec((tm, tn), lambda i, j: (i, jnp.minimum(j, n_stored - 1))),
            pl.BlockSpec((tm, D_CONV), lambda i, j: (i, 0)),
        ],
        out_shape=[jax.ShapeDtypeStruct((n_rows, N_PROJ), bf16), jax.ShapeDtypeStruct((n_rows, D_CONV), bf16)],
        scratch_shapes=[
            pltpu.VMEM((tm, D_MODEL), bf16),
            pltpu.VMEM((CARRY_ROWS, D_CONV), f32),
        ],
        compiler_params=pltpu.CompilerParams(
            dimension_semantics=("arbitrary", "arbitrary"), vmem_limit_bytes=VMEM_LIMIT),
        name="inproj",
    )(x2, norm_w, w_p, w_p, conv_w, conv_b)


def _softmax_step(s, v, carry):
    m, acc = carry
    m_new = jnp.maximum(m, jnp.max(s, axis=-1, keepdims=True))
    alpha = jnp.exp2(m - m_new)
    p = jnp.exp2(s - m_new)
    acc = alpha * acc + jnp.dot(p.astype(bf16), v, preferred_element_type=f32)
    return m_new, acc


def _stack_heads(q_tile, aug_rows, extra):
    lane = lax.broadcasted_iota(jnp.int32, (1, LANES), 1)
    pairs = (q_tile[:, :LANES], q_tile[:, LANES:])
    parts = []
    for r in range(GROUP):
        qh = pairs[r // 2]
        if r % 2 == 1:
            qh = pltpu.roll(qh, HEAD_DIM, 1)
        aug = aug_rows[r:r + 1, :]
        if extra is not None:
            aug = aug + extra
        parts.append(jnp.where(lane < HEAD_DIM, qh, aug).astype(bf16))
    return jnp.concatenate(parts, axis=0)


def _merge_heads(vals, lane):
    return (jnp.where(lane < HEAD_DIM, vals[0], vals[1]), jnp.where(lane < HEAD_DIM, vals[2], vals[3]))


def _nsa_kernel(q_ref, ksw_ref, vsw_ref, gate_ref, kvc_ref, wlo_ref, whi_ref, pelo_ref, pehi_ref, w2_ref,
                kaug_s_ref, kaug_w_ref, kaug_c_ref, qaug_ref, tri_ref, band_ref, ovl_ref, cend_ref,
                o_ref,
                kvcf_ref, ks_ref, kw_ref, vs_ref, vw_ref, ocw_ref, imp_ref, ns_ref, selq_ref):
    seq = q_ref.shape[0]
    lane = lax.broadcasted_iota(jnp.int32, (1, LANES), 1)
    low = lane < HEAD_DIM

    kvcf_ref[...] = kvc_ref[...].astype(f32)
    xcat = jnp.concatenate([kvcf_ref[pl.ds(l, N_CMP_PAD, stride=CMP_STRIDE), :] for l in range(CMP_STRIDE)],
                           axis=1)
    xa = (xcat + pelo_ref[...]).astype(bf16)
    xb = (xcat + pehi_ref[...]).astype(bf16)
    a = jnp.dot(xa, wlo_ref[...], preferred_element_type=f32)
    bm = jnp.dot(xb, whi_ref[...], preferred_element_type=f32)
    h = a + pltpu.roll(bm, N_CMP_PAD - 1, 0)
    hid = h * _sigmoid(h)
    kcvc = jnp.dot(hid.astype(bf16), w2_ref[...], preferred_element_type=f32)

    ksw = ksw_ref[...]
    vsw = vsw_ref[...]
    one = jnp.ones((1, LANES), bf16)
    ks_ref[...] = jnp.where(low, ksw, kaug_s_ref[...])
    kwin = pltpu.roll(ksw.astype(f32), HEAD_DIM, 1).astype(bf16)
    kw_ref[0:WIN_PAD, :] = kaug_w_ref[0:WIN_PAD, :]
    kw_ref[WIN_PAD:, :] = jnp.where(low, kwin, kaug_w_ref[WIN_PAD:, :])
    vs_ref[...] = jnp.where(low, vsw, one)
    vw_ref[0:WIN_PAD, :] = jnp.zeros((WIN_PAD, LANES), bf16)
    vw_ref[WIN_PAD:, :] = jnp.where(low, one, vsw)

    qaug = qaug_ref[0]
    kc_aug = jnp.where(low, kcvc, kaug_c_ref[...]).astype(bf16)
    vc_rows = jnp.concatenate([jnp.zeros((N_CMP_PAD, LANES), bf16), kcvc.astype(bf16)], axis=1)
    zeros_w = jnp.zeros((WIN_KEYS, LANES), bf16)
    ovl = ovl_ref[...]

    def cw_body(i, _):
        t0 = pl.multiple_of(i * CW_TQ, CW_TQ)
        sig = gate_ref[pl.ds(t0, CW_TQ), :].astype(f32)
        n_tiles = CW_TQ // WIN_TQ

        def tile_scores(h):
            tw = pl.multiple_of(t0 + h * WIN_TQ, WIN_TQ)
            qw = _stack_heads(q_ref[pl.ds(tw, WIN_TQ), :].astype(f32), qaug[GROUP:2 * GROUP], None)
            k_all = jnp.concatenate([kw_ref[pl.ds(tw, WIN_KEYS), :], kc_aug], axis=0)
            return _dot_nt(qw, k_all)

        s_next = tile_scores(0)
        for h in range(n_tiles):
            tw = pl.multiple_of(t0 + h * WIN_TQ, WIN_TQ)
            s = s_next
            if h + 1 < n_tiles:
                s_next = tile_scores(h + 1)

            sw = jnp.concatenate([s[:, :WIN_TQ] + band_ref[:, :WIN_TQ], s[:, WIN_TQ:WINDOW],
                                  s[:, WINDOW:WIN_KEYS] + band_ref[:, WINDOW:]], axis=1)
            p_w = jnp.exp2(sw - jnp.max(sw, axis=-1, keepdims=True))

            ok = cend_ref[...] <= tw
            sm = jnp.where(ok, s[:, WIN_KEYS:], -jnp.inf)
            m = jnp.max(sm, axis=-1, keepdims=True)
            m = jnp.where(m > -jnp.inf, m, 0.0)
            e = jnp.where(ok, jnp.exp2(sm - m), 0.0)
            p_c = e / jnp.maximum(jnp.sum(e, axis=-1, keepdims=True), 1e-30)

            p_all = jnp.concatenate([p_w.astype(bf16), p_c.astype(bf16)], axis=1)
            v_all = jnp.concatenate(
                [jnp.concatenate([vw_ref[pl.ds(tw, WIN_KEYS), :], zeros_w], axis=1), vc_rows], axis=0)
            acc = jnp.dot(p_all, v_all, preferred_element_type=f32)
            acc_w = acc[:, :LANES]
            o_w = acc_w / jnp.maximum(acc_w[:, 0:1], 1e-30)
            o_c = acc[:, LANES:]
            sig_t = sig[h * WIN_TQ:(h + 1) * WIN_TQ, :]
            vals = []
            for r in range(GROUP):
                rows_r = slice(r * WIN_TQ, (r + 1) * WIN_TQ)
                a = o_c[rows_r, :] * sig_t[:, 3 * r:3 * r + 1] + o_w[rows_r, :] * sig_t[:, 3 * r + 2:3 * r + 3]
                vals.append(pltpu.roll(a, HEAD_DIM, 1) if r % 2 == 0 else a)
            c0, c1 = _merge_heads(vals, lane)
            ocw_ref[pl.ds(tw, WIN_TQ), 0:LANES] = c0
            ocw_ref[pl.ds(tw, WIN_TQ), LANES:2 * LANES] = c1

            ps = p_c[0:WIN_TQ] + p_c[WIN_TQ:2 * WIN_TQ] + p_c[2 * WIN_TQ:3 * WIN_TQ] + p_c[3 * WIN_TQ:4 * WIN_TQ]
            p1 = ps.astype(bf16)
            r1 = ps - p1.astype(f32)
            p2 = r1.astype(bf16)
            p3 = (r1 - p2.astype(f32)).astype(bf16)
            imp_ref[i * (CW_TQ // WIN_TQ) + h] = _dot_nt(ovl, p1) + _dot_nt(ovl, p2) + _dot_nt(ovl, p3)

        return 0

    lax.fori_loop(0, seq // CW_TQ, cw_body, 0)

    n_slc = seq // SLC_LEN
    n_tt = seq // LANES
    shp = (n_tt, LANES)
    cur = (lax.broadcasted_iota(jnp.int32, shp, 0) * LANES + lax.broadcasted_iota(jnp.int32, shp, 1)) // SLC_LEN
    vals = []
    for j in range(n_slc):
        forced = (cur == j) | (cur == j + 1)
        vj = jnp.where(cur >= j, imp_ref[:, j, :], -jnp.inf)
        vals.append(jnp.full(shp, jnp.inf, f32) if j == 0 else jnp.where(forced, jnp.inf, vj))
    wins = [jnp.zeros(shp, f32) for _ in range(n_slc)]
    losses = [jnp.zeros(shp, f32) for _ in range(n_slc)]
    for a in range(n_slc):
        for b in range(a + 1, n_slc):
            a_first = jnp.where(vals[a] >= vals[b], 1.0, 0.0)
            wins[a] = wins[a] + a_first
            losses[b] = losses[b] + a_first
    for j in range(n_slc):
        rank = losses[j] + (float(n_slc - 1 - j) - wins[j])
        sel = (rank < float(N_SEL)) & (vals[j] > -jnp.inf)
        ns_ref[j] = jnp.where(sel, 0.0, 1.0)
    eye = (lax.broadcasted_iota(jnp.int32, (LANES, LANES), 0)
           == lax.broadcasted_iota(jnp.int32, (LANES, LANES), 1)).astype(bf16)
    zpad = jnp.zeros((L_SEL, LANES), f32)
    for tt in range(n_tt):
        z = jnp.concatenate([zpad, ns_ref[:, tt, :]], axis=0).astype(bf16)
        selq_ref[tt * LANES:(tt + 1) * LANES, :] = _dot_nt(eye, z) * (-MASK_BIG)

    half = SEL_CHUNK // 2
    half_rows = GROUP * half

    def stacked_q(i):
        parts = []
        for h in range(2):
            tok = slice(i * SEL_CHUNK + h * half, i * SEL_CHUNK + (h + 1) * half)
            parts.append(_stack_heads(q_ref[tok, :].astype(f32), qaug[0:GROUP], selq_ref[tok, :]))
        return parts

    def causal(s):
        tri = tri_ref[...]
        return jnp.concatenate([s[r * half:(r + 1) * half, :] + tri for r in range(GROUP)], axis=0)

    def scores(qs, i, c):
        k0 = c * SEL_CHUNK
        if c < i:
            return (_dot_nt(jnp.concatenate(qs, axis=0), ks_ref[k0:k0 + SEL_CHUNK, :]),)
        s_a = causal(_dot_nt(qs[0], ks_ref[k0:k0 + half, :]))
        s_b = _dot_nt(qs[1], ks_ref[k0:k0 + SEL_CHUNK, :])
        return (s_a, jnp.concatenate([s_b[:, :half], causal(s_b[:, half:])], axis=1))

    def softmax_step(s, v, carry):
        if carry is None:
            m = jnp.max(s, axis=-1, keepdims=True)
            return m, jnp.dot(jnp.exp2(s - m).astype(bf16), v, preferred_element_type=f32)
        return _softmax_step(s, v, carry)

    def finish(i, acc):
        tok = slice(i * SEL_CHUNK, (i + 1) * SEL_CHUNK)
        o_s = acc / jnp.maximum(acc[:, HEAD_DIM:HEAD_DIM + 1], 1e-30)
        sig = gate_ref[tok, :].astype(f32)
        vals = []
        for r in range(GROUP):
            a_s = jnp.concatenate([o_s[h * half_rows + r * half:h * half_rows + (r + 1) * half, :]
                                   for h in range(2)], axis=0) * sig[:, 3 * r + 1:3 * r + 2]
            vals.append(a_s if r % 2 == 0 else pltpu.roll(a_s, HEAD_DIM, 1))
        c0, c1 = _merge_heads(vals, lane)
        o_ref[tok, 0:LANES] = (ocw_ref[tok, 0:LANES] + c0).astype(bf16)
        o_ref[tok, LANES:2 * LANES] = (ocw_ref[tok, LANES:2 * LANES] + c1).astype(bf16)

    steps = [(i, c) for i in range(seq // SEL_CHUNK) for c in range(i + 1)]
    qs = stacked_q(0)
    s_next = scores(qs, 0, 0)
    carry = None
    for k, (i, c) in enumerate(steps):
        s = s_next
        if k + 1 < len(steps):
            i2, c2 = steps[k + 1]
            if i2 != i:
                qs = stacked_q(i2)
            s_next = scores(qs, i2, c2)
        k0 = c * SEL_CHUNK
        if c < i:
            carry = softmax_step(s[0], vs_ref[k0:k0 + SEL_CHUNK, :], carry)
        else:
            rows_a = None if carry is None else tuple(a[:half_rows] for a in carry)
            rows_b = None if carry is None else tuple(a[half_rows:] for a in carry)
            _, acc_a = softmax_step(s[0], vs_ref[k0:k0 + half, :], rows_a)
            _, acc_b = softmax_step(s[1], vs_ref[k0:k0 + SEL_CHUNK, :], rows_b)
            finish(i, jnp.concatenate([acc_a, acc_b], axis=0))
            carry = None


def _position_lanes(pos):
    out = np.zeros((pos.shape[0], LANES), np.float32)
    out[:, L_HI:L_HI + 3] = ((pos // 64) * 64)[:, None]
    out[:, L_LO:L_LO + 3] = (pos % 64)[:, None]
    return out


def _nsa_tables(seq):
    pos = np.arange(seq)
    kaug_s = _position_lanes(pos)
    kaug_s[pos, L_SEL + pos // SLC_LEN] = 1.0
    kaug_w = np.zeros((seq + WIN_PAD, LANES), np.float32)
    kaug_w[WIN_PAD:] = _position_lanes(pos)
    kaug_w[:WIN_PAD, L_FLAG] = 1.0
    kaug_c = _position_lanes(np.arange(N_CMP_PAD) * CMP_STRIDE + CMP_LEN - 1)
    tl = np.arange(SEL_CHUNK // 2)[:, None]
    kk = np.arange(SEL_CHUNK // 2)[None, :]
    tri = np.where(kk <= tl, 0.0, -MASK_BIG).astype(np.float32)
    tl = np.arange(WIN_TQ)[:, None]
    kk = np.arange(WIN_KEYS)[None, :]
    dist = tl + WIN_PAD - kk
    band = np.tile(np.where((dist >= 0) & (dist < WINDOW), 0.0, -MASK_BIG).astype(np.float32), (GROUP, 1))
    c = np.arange(N_CMP_PAD)[None, :]
    j = np.arange(seq // SLC_LEN)[:, None]
    c_start, c_end, s_start = c * CMP_STRIDE, c * CMP_STRIDE + CMP_LEN - 1, j * SLC_LEN
    ovl = ((c_start <= s_start + SLC_LEN - 1) & (c_end >= s_start) & (c < seq // CMP_STRIDE - 1)).astype(np.float32)
    cend = np.tile((c * CMP_STRIDE + CMP_LEN - 1) - tl, (GROUP, 1)).astype(np.int32)
    return (jnp.asarray(kaug_s, bf16), jnp.asarray(kaug_w, bf16), jnp.asarray(kaug_c, f32),
            jnp.asarray(tri), jnp.asarray(band), jnp.asarray(ovl, bf16), jnp.asarray(cend))


def _query_aug():
    slopes = jnp.exp2(-8.0 * jnp.arange(1, N_HEADS + 1, dtype=f32) / N_HEADS).reshape(N_KV, GROUP) * LOG2E
    hi =slopes.astype(bf16).astype(f32)
    mid = (slopes - hi).astype(bf16).astype(f32)
    lo = (slopes - hi - mid).astype(bf16).astype(f32)
    pieces = jnp.stack([hi, mid, lo, hi, mid, lo], axis=-1)
    base = jnp.zeros((N_KV, GROUP, LANES), f32).at[:, :, L_HI:L_HI + 6].set(pieces)
    return jnp.concatenate([base, base.at[:, :, L_FLAG].set(-MASK_BIG)], axis=1)


def _nsa(proj, wlo, whi, pelo, pehi, w2bd, batch, seq):
    kaug_s, kaug_w, kaug_c, tri, band, ovl, cend = _nsa_tables(seq)
    qaug = _query_aug()
    const2 = lambda b, g: (0, 0)
    in_specs = [
        pl.BlockSpec((seq, 2 * LANES), lambda b, g: (b, OFF_Q // (2 * LANES) + g)),
        pl.BlockSpec((seq, LANES), lambda b, g: (b, OFF_KSW // LANES + g)),
        pl.BlockSpec((seq, LANES), lambda b, g: (b, OFF_VSW // LANES + g)),
        pl.BlockSpec((seq, LANES), lambda b, g: (b, OFF_GATE // LANES + g)),
        pl.BlockSpec((seq, LANES), lambda b, g: (b, OFF_KVC // LANES + g)),
        pl.BlockSpec(wlo.shape, const2),
        pl.BlockSpec(whi.shape, const2),
        pl.BlockSpec(pelo.shape, const2),
        pl.BlockSpec(pehi.shape, const2),
        pl.BlockSpec(w2bd.shape, const2),
        pl.BlockSpec(kaug_s.shape, const2),
        pl.BlockSpec(kaug_w.shape, const2),
        pl.BlockSpec(kaug_c.shape, const2),
        pl.BlockSpec((1, 2 * GROUP, LANES), lambda b, g: (g, 0, 0)),
        pl.BlockSpec(tri.shape, const2),
        pl.BlockSpec(band.shape, const2),
        pl.BlockSpec(ovl.shape, const2),
        pl.BlockSpec(cend.shape, const2),
    ]
    return pl.pallas_call(
        _nsa_kernel,
        grid=(batch, N_KV),
        in_specs=in_specs,
        out_specs=pl.BlockSpec((seq, 2 * LANES), lambda b, g: (b, g)),
        out_shape=jax.ShapeDtypeStruct((batch * seq, D_ATT), bf16),
        scratch_shapes=[
            pltpu.VMEM((seq, LANES), f32),
            pltpu.VMEM((seq, LANES), bf16),
            pltpu.VMEM((seq + WIN_PAD, LANES), bf16),
            pltpu.VMEM((seq, LANES), bf16),
            pltpu.VMEM((seq + WIN_PAD, LANES), bf16),
            pltpu.VMEM((seq, 2 * LANES), f32),
            pltpu.VMEM((seq // LANES, seq // SLC_LEN, LANES), f32),
            pltpu.VMEM((seq // SLC_LEN, seq // LANES, LANES), f32),
            pltpu.VMEM((seq, LANES), f32),
        ],
        compiler_params=pltpu.CompilerParams(
            dimension_semantics=("parallel", "parallel"), vmem_limit_bytes=VMEM_LIMIT),
        name="nsa",
    )(proj, proj, proj, proj, proj, wlo, whi, pelo, pehi, w2bd, kaug_s, kaug_w, kaug_c, qaug, tri, band, ovl, cend)


def _out_kernel(x_ref, ya_ref, zb_ref, gm_ref, ob_ref, wa_ref, wb_ref, wo_ref, fnw_ref, o_ref):
    y_a = jnp.dot(ya_ref[...], wa_ref[...], preferred_element_type=f32)
    yb_in = ob_ref[...].astype(f32) * zb_ref[...].astype(f32)
    y_b = jnp.dot(yb_in.astype(bf16), wb_ref[...], preferred_element_type=f32)
    mixed = (gm_ref[:, 0:D_MODEL].astype(f32) * y_a + gm_ref[:, D_MODEL:2 * D_MODEL].astype(f32) * y_b)
    xo = x_ref[...] + jnp.dot(mixed.astype(bf16), wo_ref[...], preferred_element_type=f32)
    r = lax.rsqrt(jnp.mean(xo * xo, axis=-1, keepdims=True) + NORM_EPS)
    o_ref[...] = (xo * r) * fnw_ref[...]


def _out(x2, ya, proj, ob, wa, wb, wo, fnw, tm=1024):
    n_rows = x2.shape[0]
    blk = lambda c: pl.BlockSpec((tm, D_MODEL), lambda i, c=c: (i, c))
    full = lambda a: pl.BlockSpec(a.shape, lambda i: (0, 0))
    return pl.pallas_call(
        _out_kernel,
        grid=(n_rows // tm,),
        in_specs=[
            blk(0),
            blk(0),
            blk(OFF_ZB // D_MODEL),
            pl.BlockSpec((tm, 2 * D_MODEL), lambda i: (i, OFF_GMIX // (2 * D_MODEL))),
            blk(0),
            full(wa), full(wb), full(wo), full(fnw),
        ],
        out_specs=pl.BlockSpec((tm, D_MODEL), lambda i: (i, 0)),
        out_shape=jax.ShapeDtypeStruct((n_rows, D_MODEL), f32),
        compiler_params=pltpu.CompilerParams(
            dimension_semantics=("parallel",), vmem_limit_bytes=VMEM_LIMIT),
        name="merge_out",
    )(x2, ya, proj, proj, ob, wa, wb, wo, fnw)


def _blockdiag_w1(w1_k, w1_v, lo):
    half = CMP_STRIDE * HEAD_DIM
    wk = w1_k[lo * half:(lo + 1) * half].reshape(CMP_STRIDE, HEAD_DIM, CMP_HID)
    wv = w1_v[lo * half:(lo + 1) * half].reshape(CMP_STRIDE, HEAD_DIM, CMP_HID)
    z = jnp.zeros_like(wk)
    top = jnp.concatenate([wk, z], axis=-1)
    bot = jnp.concatenate([z, wv], axis=-1)
    return jnp.concatenate([top, bot], axis=1).reshape(CMP_STRIDE * 2 * HEAD_DIM, 2 * CMP_HID).astype(bf16)


def _pe_row(pe_k, pe_v, lo):
    sl = slice(lo * CMP_STRIDE, (lo + 1) * CMP_STRIDE)
    return jnp.concatenate([pe_k[sl], pe_v[sl]], axis=-1).reshape(1, CMP_STRIDE * 2 * HEAD_DIM)


def kernel(x, norm_w, w_in, conv_w, conv_b, cmp_pe_k, cmp_pe_v, cmp_w1_k, cmp_w2_k, cmp_w1_v, cmp_w2_v,
           w_proj_a, w_proj_b, w_out, final_norm_w):
    batch, seq, _ = x.shape
    assert norm_w.shape[0] == 1 and seq % SEL_CHUNK == 0 and seq // CMP_STRIDE == N_CMP_PAD
    w_p = _reordered_w_in(w_in[0])
    x2 = x.reshape(batch * seq, D_MODEL)
    proj, ya = _inproj(x2, norm_w, w_p, conv_w[0], conv_b, seq)

    wlo = _blockdiag_w1(cmp_w1_k[0], cmp_w1_v[0], 0)
    whi = _blockdiag_w1(cmp_w1_k[0], cmp_w1_v[0], 1)
    pelo = _pe_row(cmp_pe_k[0], cmp_pe_v[0], 0)
    pehi = _pe_row(cmp_pe_k[0], cmp_pe_v[0], 1)
    zk = jnp.zeros_like(cmp_w2_k[0])
    w2bd = jnp.concatenate([jnp.concatenate([cmp_w2_k[0], zk], axis=1),
                            jnp.concatenate([zk, cmp_w2_v[0]], axis=1)], axis=0).astype(bf16)
    ob = _nsa(proj, wlo, whi, pelo, pehi, w2bd, batch, seq)

    out = _out(x2, ya, proj, ob, w_proj_a[0].astype(bf16), w_proj_b[0].astype(bf16),
               w_out[0].astype(bf16), final_norm_w.reshape(1, D_MODEL))
    return out.reshape(batch, seq, D_MODEL)
```

```python
import functools

import numpy as np
import jax
import jax.numpy as jnp
from jax import lax
from jax.experimental import pallas as pl
from jax.experimental.pallas import tpu as pltpu

D_MODEL = 1024
D_CONV = 1024
CONV_W = 3
N_HEADS = 16
HEAD_DIM = 64
N_KV = 4
GROUP = N_HEADS // N_KV
D_ATT = N_HEADS * HEAD_DIM
D_KV = N_KV * HEAD_DIM
CMP_LEN = 32
CMP_STRIDE = 16
CMP_HID = 128
SLC_LEN = 64
N_SEL = 8
WINDOW = 512
NORM_EPS = 1e-6

LANES = 128
N_CMP_PAD = 128
MASK_BIG = float(2.0 ** 100)
LOG2E = float(np.log2(np.e))
SEL_CHUNK = 512
WIN_TQ = 128
WIN_KEYS = WINDOW + WIN_TQ
WIN_PAD = WINDOW
CW_TQ = 512

OFF_Q = 0
OFF_ZB = 1024
OFF_GMIX = 2048
OFF_KVC = 4096
OFF_KSW = 4608
OFF_VSW = 5120
OFF_GATE = 5632
N_PROJ = 6144

L_HI = 64
L_LO = 67
L_FLAG = 70
L_SEL = 96

VMEM_LIMIT = 56 * 1024 * 1024

f32 = jnp.float32
bf16 = jnp.bfloat16


W_Q = 4 * D_CONV
W_KV = W_Q + D_ATT
W_GATE = W_KV + 6 * D_KV
W_ZB = W_GATE + 3 * N_HEADS
N_IN = W_ZB + D_ATT + 2 * D_MODEL


WP_ROWS = 512
WP_T = 256


def _wprep_kernel(a_ref, b_ref, o_ref):
    j = pl.program_id(0)
    n_plain = W_KV // WP_ROWS
    n_shift = (N_IN - W_ZB) // WP_ROWS
    off = W_ZB % WP_ROWS
    eye = (lax.broadcasted_iota(jnp.int32, (WP_T, WP_T), 0)
           == lax.broadcasted_iota(jnp.int32, (WP_T, WP_T), 1)).astype(bf16)

    def transposed(rows_bf16, kb):
        return _dot_nt(eye, rows_bf16[:, kb * WP_T:(kb + 1) * WP_T])

    def emit(rows):
        rows = rows.astype(bf16)
        for kb in range(D_MODEL // WP_T):
            o_ref[kb * WP_T:(kb + 1) * WP_T, :] = transposed(rows, kb).astype(bf16)

    def interleave(x, x_off, y, y_off):
        parts = []
        for g in range(N_KV):
            parts += [x[x_off + g * HEAD_DIM:x_off + (g + 1) * HEAD_DIM], y[y_off + g * HEAD_DIM:y_off + (g + 1) * HEAD_DIM]]
        return jnp.concatenate(parts, axis=0)

    @pl.when(j < W_Q // WP_ROWS)
    def _():
        emit(a_ref[...])

    @pl.when((j >= W_Q // WP_ROWS) & (j < n_plain))
    def _():
        emit(a_ref[...] * (HEAD_DIM ** -0.5 * LOG2E))

    @pl.when((j >= n_plain) & (j < n_plain + n_shift))
    def _():
        emit(jnp.concatenate([a_ref[off:, :], b_ref[:off, :]], axis=0))

    @pl.when(j == n_plain + n_shift)
    def _():
        a = a_ref[...]
        emit(interleave(a, 0, a, D_KV))

    @pl.when(j == n_plain + n_shift + 1)
    def _():
        emit(interleave(a_ref[...], 0, b_ref[...], 0))

    @pl.when(j == n_plain + n_shift + 2)
    def _():
        emit(interleave(a_ref[...], D_KV, b_ref[...], D_KV))

    @pl.when(j == n_plain + n_shift + 3)
    def _():
        rows = a_ref[:LANES, :].astype(bf16)
        lane = lax.broadcasted_iota(jnp.int32, (1, LANES), 1)
        for kb in range(D_MODEL // WP_T):
            t = transposed(rows, kb)
            for g in range(N_KV):
                shifted = t if g == 0 else pltpu.roll(t, LANES - 3 * GROUP * g, 1)
                o_ref[kb * WP_T:(kb + 1) * WP_T, g * LANES:(g + 1) * LANES] = (
                    jnp.where(lane < 3 * GROUP, shifted, 0.0).astype(bf16))


def _reordered_w_in(w):
    assert w.shape == (D_MODEL, N_IN) and W_KV % WP_ROWS == 0 and (N_IN - W_ZB) % WP_ROWS == 0
    assert W_GATE % WP_ROWS == 0 and 6 * D_KV == 3 * WP_ROWS and WP_ROWS == N_KV * LANES
    n_plain, n_shift = W_KV // WP_ROWS, (N_IN - W_ZB) // WP_ROWS
    first_shift = W_ZB // WP_ROWS
    kv0 = W_KV // WP_ROWS
    n_out = 4 * D_CONV + N_PROJ
    n_steps = n_out // WP_ROWS

    def block_a(j):
        shifted = first_shift + (j - n_plain)
        tail = jnp.where(j == n_plain + n_shift, kv0, jnp.where(j == n_plain + n_shift + 3, W_GATE // WP_ROWS, kv0 + 1))
        return jnp.where(j < n_plain, j, jnp.where(j < n_plain + n_shift, shifted, tail))

    def block_b(j):
        shifted = jnp.clip(first_shift + 1 + (j - n_plain), first_shift + 1, first_shift + n_shift)
        return jnp.where(j < n_plain + n_shift + 1, shifted, kv0 + 2)

    return pl.pallas_call(
        _wprep_kernel,
        grid=(n_steps,),
        in_specs=[pl.BlockSpec((WP_ROWS, D_MODEL), lambda j: (block_a(j), 0)),
                  pl.BlockSpec((WP_ROWS, D_MODEL), lambda j: (block_b(j), 0))],
        out_specs=pl.BlockSpec((D_MODEL, WP_ROWS), lambda j: (0, j)),
        out_shape=jax.ShapeDtypeStruct((D_MODEL, n_out), bf16),
        compiler_params=pltpu.CompilerParams(dimension_semantics=("parallel",), vmem_limit_bytes=VMEM_LIMIT),
        name="wprep",
    )(w.T, w.T)


def _sigmoid(x):
    return 1.0 / (1.0 + jnp.exp(-x))


def _dot_nt(a, b):
    return lax.dot_general(a, b, (((1,), (1,)), ((), ())), preferred_element_type=f32)


CARRY_ROWS = 8
ROW_BLOCK = 256


def _column_activations(lo, hi):
    silu = lambda a: a * _sigmoid(a)
    keep = lambda a: a
    segments = ((OFF_Q, OFF_ZB, keep), (OFF_ZB, OFF_GMIX, silu), (OFF_GMIX, OFF_KVC, _sigmoid),
                (OFF_KVC, OFF_GATE, keep), (OFF_GATE, N_PROJ, _sigmoid))
    return [(max(a, lo) - lo, min(b, hi) - lo, f) for a, b, f in segments if max(a, lo) < min(b, hi)]


def _inproj_kernel(x_ref, nw_ref, w1_ref, w2_ref, cw_ref, cb_ref, o_ref, ya_ref, h_ref, carry_ref,
                   *, tiles_per_seq):
    i = pl.program_id(0)
    j = pl.program_id(1)
    tm = x_ref.shape[0]
    tn = o_ref.shape[1]
    n_stored = N_PROJ // tn
    blocks = [slice(k * ROW_BLOCK, (k + 1) * ROW_BLOCK) for k in range(tm // ROW_BLOCK)]
    product = lambda rows, w: jnp.dot(h_ref[rows, :], w, preferred_element_type=f32)

    @pl.when((i == 0) & (j == 0))
    def _():
        carry_ref[...] = jnp.zeros_like(carry_ref)

    for t in range(n_stored):
        @pl.when(j == t)
        def _(t=t):
            for rows in blocks:
                if t == 0:
                    x = x_ref[rows, :]
                    r = lax.rsqrt(jnp.mean(x * x, axis=-1, keepdims=True) + NORM_EPS)
                    h_ref[rows, :] = ((x * r) * nw_ref[...]).astype(bf16)
                acc = product(rows, w1_ref[...])
                for lo, hi, act in _column_activations(t * tn, (t + 1) * tn):
                    o_ref[rows, lo:hi] = act(acc[:, lo:hi]).astype(bf16)

    @pl.when(j == n_stored)
    def _():
        lead = carry_ref[...] * (i % tiles_per_seq != 0).astype(f32)
        cw = cw_ref[...]
        for rows in blocks:
            u = product(rows, w2_ref[:, :D_CONV]) * product(rows, w1_ref[:, :D_CONV])
            win = jnp.concatenate([lead, u], axis=0)
            u1 = pltpu.roll(win, 1, 0)[CARRY_ROWS:, :]
            u2 = pltpu.roll(win, 2, 0)[CARRY_ROWS:, :]
            y = cb_ref[...] + cw[0:1, :] * u2 + cw[1:2, :] * u1 + cw[2:3, :] * u
            z = product(rows, w2_ref[:, D_CONV:])
            ya_ref[rows, :] = (product(rows, w1_ref[:, D_CONV:]) * y * (z * _sigmoid(z))).astype(bf16)
            lead = u[ROW_BLOCK - CARRY_ROWS:, :]
        carry_ref[...] = lead


def _inproj(x2, norm_w, w_p, conv_w, conv_b, seq, tm=1024):
    n_rows = x2.shape[0]
    tn = 2 * D_CONV
    assert N_PROJ % tn == 0 and seq % tm == 0 and tm % ROW_BLOCK == 0 and w_p.shape[1] == 2 * tn + N_PROJ
    n_stored = N_PROJ // tn
    first = lambda i, j: (0, jnp.where(j < n_stored, j + 2, 0))
    second = lambda i, j: (0, 1)
    return pl.pallas_call(
        functools.partial(_inproj_kernel, tiles_per_seq=seq // tm),
        grid=(n_rows // tm, n_stored + 1),
        in_specs=[
            pl.BlockSpec((tm, D_MODEL), lambda i, j: (i, 0)),
            pl.BlockSpec((1, D_MODEL), lambda i, j: (0, 0)),
            pl.BlockSpec((D_MODEL, tn), first),
            pl.BlockSpec((D_MODEL, tn), second),
            pl.BlockSpec(conv_w.shape, lambda i, j: (0, 0)),
            pl.BlockSpec(conv_b.shape, lambda i, j: (0, 0)),
        ],
        out_specs=[
            pl.BlockSpec((tm, tn), lambda i, j: (i, jnp.minimum(j, n_stored - 1))),
            pl.BlockSpec((tm, D_CONV), lambda i, j: (i, 0)),
        ],
        out_shape=[jax.ShapeDtypeStruct((n_rows, N_PROJ), bf16), jax.ShapeDtypeStruct((n_rows, D_CONV), bf16)],
        scratch_shapes=[
            pltpu.VMEM((tm, D_MODEL), bf16),
            pltpu.VMEM((CARRY_ROWS, D_CONV), f32),
        ],
        compiler_params=pltpu.CompilerParams(
            dimension_semantics=("arbitrary", "arbitrary"), vmem_limit_bytes=VMEM_LIMIT),
        name="inproj",
    )(x2, norm_w, w_p, w_p, conv_w, conv_b)


def _softmax_step(s, v, carry):
    m, acc = carry
    m_new = jnp.maximum(m, jnp.max(s, axis=-1, keepdims=True))
    alpha = jnp.exp2(m - m_new)
    p = jnp.exp2(s - m_new)
    acc = alpha * acc + jnp.dot(p.astype(bf16), v, preferred_element_type=f32)
    return m_new, acc


def _stack_heads(q_tile, aug_rows, extra):
    lane = lax.broadcasted_iota(jnp.int32, (1, LANES), 1)
    pairs = (q_tile[:, :LANES], q_tile[:, LANES:])
    parts = []
    for r in range(GROUP):
        qh = pairs[r // 2]
        if r % 2 == 1:
            qh = pltpu.roll(qh, HEAD_DIM, 1)
        aug = aug_rows[r:r + 1, :]
        if extra is not None:
            aug = aug + extra
        parts.append(jnp.where(lane < HEAD_DIM, qh, aug).astype(bf16))
    return jnp.concatenate(parts, axis=0)


def _merge_heads(vals, lane):
    return (jnp.where(lane < HEAD_DIM, vals[0], vals[1]), jnp.where(lane < HEAD_DIM, vals[2], vals[3]))


def _nsa_kernel(q_ref, ksw_ref, vsw_ref, gate_ref, kvc_ref, wlo_ref, whi_ref, pelo_ref, pehi_ref, w2_ref,
                kaug_s_ref, kaug_w_ref, kaug_c_ref, qaug_ref, tri_ref, band_ref, ovl_ref, cend_ref,
                o_ref,
                kvcf_ref, ks_ref, kw_ref, vs_ref, vw_ref, ocw_ref, imp_ref, ns_ref, selq_ref):
    seq = q_ref.shape[0]
    lane = lax.broadcasted_iota(jnp.int32, (1, LANES), 1)
    low = lane < HEAD_DIM

    kvcf_ref[...] = kvc_ref[...].astype(f32)
    xcat = jnp.concatenate([kvcf_ref[pl.ds(l, N_CMP_PAD, stride=CMP_STRIDE), :] for l in range(CMP_STRIDE)],
                           axis=1)
    xa = (xcat + pelo_ref[...]).astype(bf16)
    xb = (xcat + pehi_ref[...]).astype(bf16)
    a = jnp.dot(xa, wlo_ref[...], preferred_element_type=f32)
    bm = jnp.dot(xb, whi_ref[...], preferred_element_type=f32)
    h = a + pltpu.roll(bm, N_CMP_PAD - 1, 0)
    hid = h * _sigmoid(h)
    kcvc = jnp.dot(hid.astype(bf16), w2_ref[...], preferred_element_type=f32)

    ksw = ksw_ref[...]
    vsw = vsw_ref[...]
    one = jnp.ones((1, LANES), bf16)
    ks_ref[...] = jnp.where(low, ksw, kaug_s_ref[...])
    kwin = pltpu.roll(ksw.astype(f32), HEAD_DIM, 1).astype(bf16)
    kw_ref[0:WIN_PAD, :] = kaug_w_ref[0:WIN_PAD, :]
    kw_ref[WIN_PAD:, :] = jnp.where(low, kwin, kaug_w_ref[WIN_PAD:, :])
    vs_ref[...] = jnp.where(low, vsw, one)
    vw_ref[0:WIN_PAD, :] = jnp.zeros((WIN_PAD, LANES), bf16)
    vw_ref[WIN_PAD:, :] = jnp.where(low, one, vsw)

    qaug = qaug_ref[0]
    kc_aug = jnp.where(low, kcvc, kaug_c_ref[...]).astype(bf16)
    vc_rows = jnp.concatenate([jnp.zeros((N_CMP_PAD, LANES), bf16), kcvc.astype(bf16)], axis=1)
    zeros_w = jnp.zeros((WIN_KEYS, LANES), bf16)
    ovl = ovl_ref[...]

    def cw_body(i, _):
        t0 = pl.multiple_of(i * CW_TQ, CW_TQ)
        sig = gate_ref[pl.ds(t0, CW_TQ), :].astype(f32)
        o_w, o_c, p_sum = [], [], []
        n_tiles = CW_TQ // WIN_TQ

        def tile_scores(h):
            tw = pl.multiple_of(t0 + h * WIN_TQ, WIN_TQ)
            qw = _stack_heads(q_ref[pl.ds(tw, WIN_TQ), :].astype(f32), qaug[GROUP:2 * GROUP], None)
            k_all = jnp.concatenate([kw_ref[pl.ds(tw, WIN_KEYS), :], kc_aug], axis=0)
            return _dot_nt(qw, k_all)

        s_next = tile_scores(0)
        for h in range(n_tiles):
            tw = pl.multiple_of(t0 + h * WIN_TQ, WIN_TQ)
            s = s_next
            if h + 1 < n_tiles:
                s_next = tile_scores(h + 1)

            sw = jnp.concatenate([s[:, :WIN_TQ] + band_ref[:, :WIN_TQ], s[:, WIN_TQ:WINDOW],
                                  s[:, WINDOW:WIN_KEYS] + band_ref[:, WINDOW:]], axis=1)
            p_w = jnp.exp2(sw - jnp.max(sw, axis=-1, keepdims=True))

            ok = cend_ref[...] <= tw
            sm = jnp.where(ok, s[:, WIN_KEYS:], -jnp.inf)
            m = jnp.max(sm, axis=-1, keepdims=True)
            m = jnp.where(m > -jnp.inf, m, 0.0)
            e = jnp.where(ok, jnp.exp2(sm - m), 0.0)
            p_c = e / jnp.maximum(jnp.sum(e, axis=-1, keepdims=True), 1e-30)

            p_all = jnp.concatenate([p_w.astype(bf16), p_c.astype(bf16)], axis=1)
            v_all = jnp.concatenate(
                [jnp.concatenate([vw_ref[pl.ds(tw, WIN_KEYS), :], zeros_w], axis=1), vc_rows], axis=0)
            acc = jnp.dot(p_all, v_all, preferred_element_type=f32)
            acc_w = acc[:, :LANES]
            o_w.append(acc_w / jnp.maximum(acc_w[:, 0:1], 1e-30))
            o_c.append(acc[:, LANES:])

            p_sum.append(p_c[0:WIN_TQ] + p_c[WIN_TQ:2 * WIN_TQ] + p_c[2 * WIN_TQ:3 * WIN_TQ] + p_c[3 * WIN_TQ:4 * WIN_TQ])

        ps = jnp.concatenate(p_sum, axis=0)
        p1 = ps.astype(bf16)
        r1 = ps - p1.astype(f32)
        p2 = r1.astype(bf16)
        p3 = (r1 - p2.astype(f32)).astype(bf16)
        imp = _dot_nt(ovl, p1) + _dot_nt(ovl, p2) + _dot_nt(ovl, p3)
        for h in range(n_tiles):
            imp_ref[i * n_tiles + h] = imp[:, h * WIN_TQ:(h + 1) * WIN_TQ]

        vals = []
        for r in range(GROUP):
            head = lambda parts: jnp.concatenate([o[r * WIN_TQ:(r + 1) * WIN_TQ, :] for o in parts], axis=0)
            a = head(o_c) * sig[:, 3 * r:3 * r + 1] + head(o_w) * sig[:, 3 * r + 2:3 * r + 3]
            vals.append(pltpu.roll(a, HEAD_DIM, 1) if r % 2 == 0 else a)
        c0, c1 = _merge_heads(vals, lane)
        ocw_ref[pl.ds(t0, CW_TQ), 0:LANES] = c0
        ocw_ref[pl.ds(t0, CW_TQ), LANES:2 * LANES] = c1
        return 0

    lax.fori_loop(0, seq // CW_TQ, cw_body, 0)

    n_slc = seq // SLC_LEN
    n_tt = seq // LANES
    shp = (n_tt, LANES)
    cur = (lax.broadcasted_iota(jnp.int32, shp, 0) * LANES + lax.broadcasted_iota(jnp.int32, shp, 1)) // SLC_LEN
    vals = []
    for j in range(n_slc):
        forced = (cur == j) | (cur == j + 1)
        vj = jnp.where(cur >= j, imp_ref[:, j, :], -jnp.inf)
        vals.append(jnp.full(shp, jnp.inf, f32) if j == 0 else jnp.where(forced, jnp.inf, vj))
    wins = [jnp.zeros(shp, f32) for _ in range(n_slc)]
    losses = [jnp.zeros(shp, f32) for _ in range(n_slc)]
    for a in range(n_slc):
        for b in range(a + 1, n_slc):
            a_first = jnp.where(vals[a] >= vals[b], 1.0, 0.0)
            wins[a] = wins[a] + a_first
            losses[b] = losses[b] + a_first
    for j in range(n_slc):
        rank = losses[j] + (float(n_slc - 1 - j) - wins[j])
        sel = (rank < float(N_SEL)) & (vals[j] > -jnp.inf)
        ns_ref[j] = jnp.where(sel, 0.0, 1.0)
    eye = (lax.broadcasted_iota(jnp.int32, (LANES, LANES), 0)
           == lax.broadcasted_iota(jnp.int32, (LANES, LANES), 1)).astype(bf16)
    zpad = jnp.zeros((L_SEL, LANES), f32)
    for tt in range(n_tt):
        z = jnp.concatenate([zpad, ns_ref[:, tt, :]], axis=0).astype(bf16)
        selq_ref[tt * LANES:(tt + 1) * LANES, :] = _dot_nt(eye, z) * (-MASK_BIG)

    half = SEL_CHUNK // 2
    half_rows = GROUP * half

    def stacked_q(i):
        parts = []
        for h in range(2):
            tok = slice(i * SEL_CHUNK + h * half, i * SEL_CHUNK + (h + 1) * half)
            parts.append(_stack_heads(q_ref[tok, :].astype(f32), qaug[0:GROUP], selq_ref[tok, :]))
        return parts

    def causal(s):
        tri = tri_ref[...]
        return jnp.concatenate([s[r * half:(r + 1) * half, :] + tri for r in range(GROUP)], axis=0)

    def scores(qs, i, c):
        k0 = c * SEL_CHUNK
        if c < i:
            return (_dot_nt(jnp.concatenate(qs, axis=0), ks_ref[k0:k0 + SEL_CHUNK, :]),)
        s_a = causal(_dot_nt(qs[0], ks_ref[k0:k0 + half, :]))
        s_b = _dot_nt(qs[1], ks_ref[k0:k0 + SEL_CHUNK, :])
        return (s_a, jnp.concatenate([s_b[:, :half], causal(s_b[:, half:])], axis=1))

    def softmax_step(s, v, carry):
        if carry is None:
            m = jnp.max(s, axis=-1, keepdims=True)
            return m, jnp.dot(jnp.exp2(s - m).astype(bf16), v, preferred_element_type=f32)
        return _softmax_step(s, v, carry)

    def finish(i, acc):
        tok = slice(i * SEL_CHUNK, (i + 1) * SEL_CHUNK)
        o_s = acc / jnp.maximum(acc[:, HEAD_DIM:HEAD_DIM + 1], 1e-30)
        sig = gate_ref[tok, :].astype(f32)
        vals = []
        for r in range(GROUP):
            a_s = jnp.concatenate([o_s[h * half_rows + r * half:h * half_rows + (r + 1) * half, :]
                                   for h in range(2)], axis=0) * sig[:, 3 * r + 1:3 * r + 2]
            vals.append(a_s if r % 2 == 0 else pltpu.roll(a_s, HEAD_DIM, 1))
        c0, c1 = _merge_heads(vals, lane)
        o_ref[tok, 0:LANES] = (ocw_ref[tok, 0:LANES] + c0).astype(bf16)
        o_ref[tok, LANES:2 * LANES] = (ocw_ref[tok, LANES:2 * LANES] + c1).astype(bf16)

    steps = [(i, c) for i in range(seq // SEL_CHUNK) for c in range(i + 1)]
    qs = stacked_q(0)
    s_next = scores(qs, 0, 0)
    carry = None
    for k, (i, c) in enumerate(steps):
        s = s_next
        if k + 1 < len(steps):
            i2, c2 = steps[k + 1]
            if i2 != i:
                qs = stacked_q(i2)
            s_next = scores(qs, i2, c2)
        k0 = c * SEL_CHUNK
        if c < i:
            carry = softmax_step(s[0], vs_ref[k0:k0 + SEL_CHUNK, :], carry)
        else:
            rows_a = None if carry is None else tuple(a[:half_rows] for a in carry)
            rows_b = None if carry is None else tuple(a[half_rows:] for a in carry)
            _, acc_a = softmax_step(s[0], vs_ref[k0:k0 + half, :], rows_a)
            _, acc_b = softmax_step(s[1], vs_ref[k0:k0 + SEL_CHUNK, :], rows_b)
            finish(i, jnp.concatenate([acc_a, acc_b], axis=0))
            carry = None


def _position_lanes(pos):
    out = np.zeros((pos.shape[0], LANES), np.float32)
    out[:, L_HI:L_HI + 3] = ((pos // 64) * 64)[:, None]
    out[:, L_LO:L_LO + 3] = (pos % 64)[:, None]
    return out


def _nsa_tables(seq):
    pos = np.arange(seq)
    kaug_s = _position_lanes(pos)
    kaug_s[pos, L_SEL + pos // SLC_LEN] = 1.0
    kaug_w = np.zeros((seq + WIN_PAD, LANES), np.float32)
    kaug_w[WIN_PAD:] = _position_lanes(pos)
    kaug_w[:WIN_PAD, L_FLAG] = 1.0
    kaug_c = _position_lanes(np.arange(N_CMP_PAD) * CMP_STRIDE + CMP_LEN - 1)
    tl = np.arange(SEL_CHUNK // 2)[:, None]
    kk = np.arange(SEL_CHUNK // 2)[None, :]
    tri = np.where(kk <= tl, 0.0, -MASK_BIG).astype(np.float32)
    tl = np.arange(WIN_TQ)[:, None]
    kk = np.arange(WIN_KEYS)[None, :]
    dist = tl + WIN_PAD - kk
    band = np.tile(np.where((dist >= 0) & (dist < WINDOW), 0.0, -MASK_BIG).astype(np.float32), (GROUP, 1))
    c = np.arange(N_CMP_PAD)[None, :]
    j = np.arange(seq // SLC_LEN)[:, None]
    c_start, c_end, s_start = c * CMP_STRIDE, c * CMP_STRIDE + CMP_LEN - 1, j * SLC_LEN
    ovl = ((c_start <= s_start + SLC_LEN - 1) & (c_end >= s_start) & (c < seq // CMP_STRIDE - 1)).astype(np.float32)
    cend = np.tile((c * CMP_STRIDE + CMP_LEN - 1) - tl, (GROUP, 1)).astype(np.int32)
    return (jnp.asarray(kaug_s, bf16), jnp.asarray(kaug_w, bf16), jnp.asarray(kaug_c, f32),
            jnp.asarray(tri), jnp.asarray(band), jnp.asarray(ovl, bf16), jnp.asarray(cend))


def _query_aug():
    slopes = jnp.exp2(-8.0 * jnp.arange(1, N_HEADS + 1, dtype=f32) / N_HEADS).reshape(N_KV, GROUP) * LOG2E
    hi =slopes.astype(bf16).astype(f32)
    mid = (slopes - hi).astype(bf16).astype(f32)
    lo = (slopes - hi - mid).astype(bf16).astype(f32)
    pieces = jnp.stack([hi, mid, lo, hi, mid, lo], axis=-1)
    base = jnp.zeros((N_KV, GROUP, LANES), f32).at[:, :, L_HI:L_HI + 6].set(pieces)
    return jnp.concatenate([base, base.at[:, :, L_FLAG].set(-MASK_BIG)], axis=1)


def _nsa(proj, wlo, whi, pelo, pehi, w2bd, batch, seq):
    kaug_s, kaug_w, kaug_c, tri, band, ovl, cend = _nsa_tables(seq)
    qaug = _query_aug()
    const2 = lambda b, g: (0, 0)
    in_specs = [
        pl.BlockSpec((seq, 2 * LANES), lambda b, g: (b, OFF_Q // (2 * LANES) + g)),
        pl.BlockSpec((seq, LANES), lambda b, g: (b, OFF_KSW // LANES + g)),
        pl.BlockSpec((seq, LANES), lambda b, g: (b, OFF_VSW // LANES + g)),
        pl.BlockSpec((seq, LANES), lambda b, g: (b, OFF_GATE // LANES + g)),
        pl.BlockSpec((seq, LANES), lambda b, g: (b, OFF_KVC // LANES + g)),
        pl.BlockSpec(wlo.shape, const2),
        pl.BlockSpec(whi.shape, const2),
        pl.BlockSpec(pelo.shape, const2),
        pl.BlockSpec(pehi.shape, const2),
        pl.BlockSpec(w2bd.shape, const2),
        pl.BlockSpec(kaug_s.shape, const2),
        pl.BlockSpec(kaug_w.shape, const2),
        pl.BlockSpec(kaug_c.shape, const2),
        pl.BlockSpec((1, 2 * GROUP, LANES), lambda b, g: (g, 0, 0)),
        pl.BlockSpec(tri.shape, const2),
        pl.BlockSpec(band.shape, const2),
        pl.BlockSpec(ovl.shape, const2),
        pl.BlockSpec(cend.shape, const2),
    ]
    return pl.pallas_call(
        _nsa_kernel,
        grid=(batch, N_KV),
        in_specs=in_specs,
        out_specs=pl.BlockSpec((seq, 2 * LANES), lambda b, g: (b, g)),
        out_shape=jax.ShapeDtypeStruct((batch * seq, D_ATT), bf16),
        scratch_shapes=[
            pltpu.VMEM((seq, LANES), f32),
            pltpu.VMEM((seq, LANES), bf16),
            pltpu.VMEM((seq + WIN_PAD, LANES), bf16),
            pltpu.VMEM((seq, LANES), bf16),
            pltpu.VMEM((seq + WIN_PAD, LANES), bf16),
            pltpu.VMEM((seq, 2 * LANES), f32),
            pltpu.VMEM((seq // LANES, seq // SLC_LEN, LANES), f32),
            pltpu.VMEM((seq // SLC_LEN, seq // LANES, LANES), f32),
            pltpu.VMEM((seq, LANES), f32),
        ],
        compiler_params=pltpu.CompilerParams(
            dimension_semantics=("parallel", "parallel"), vmem_limit_bytes=VMEM_LIMIT),
        name="nsa",
    )(proj, proj, proj, proj, proj, wlo, whi, pelo, pehi, w2bd, kaug_s, kaug_w, kaug_c, qaug, tri, band, ovl, cend)


def _out_kernel(x_ref, ya_ref, zb_ref, gm_ref, ob_ref, wa_ref, wb_ref, wo_ref, fnw_ref, o_ref):
    y_a = jnp.dot(ya_ref[...], wa_ref[...], preferred_element_type=f32)
    yb_in = ob_ref[...].astype(f32) * zb_ref[...].astype(f32)
    y_b = jnp.dot(yb_in.astype(bf16), wb_ref[...], preferred_element_type=f32)
    mixed = (gm_ref[:, 0:D_MODEL].astype(f32) * y_a + gm_ref[:, D_MODEL:2 * D_MODEL].astype(f32) * y_b)
    xo = x_ref[...] + jnp.dot(mixed.astype(bf16), wo_ref[...], preferred_element_type=f32)
    r = lax.rsqrt(jnp.mean(xo * xo, axis=-1, keepdims=True) + NORM_EPS)
    o_ref[...] = (xo * r) * fnw_ref[...]


def _out(x2, ya, proj, ob, wa, wb, wo, fnw, tm=1024):
    n_rows = x2.shape[0]
    blk = lambda c: pl.BlockSpec((tm, D_MODEL), lambda i, c=c: (i, c))
    full = lambda a: pl.BlockSpec(a.shape, lambda i: (0, 0))
    return pl.pallas_call(
        _out_kernel,
        grid=(n_rows // tm,),
        in_specs=[
            blk(0),
            blk(0),
            blk(OFF_ZB // D_MODEL),
            pl.BlockSpec((tm, 2 * D_MODEL), lambda i: (i, OFF_GMIX // (2 * D_MODEL))),
            blk(0),
            full(wa), full(wb), full(wo), full(fnw),
        ],
        out_specs=pl.BlockSpec((tm, D_MODEL), lambda i: (i, 0)),
        out_shape=jax.ShapeDtypeStruct((n_rows, D_MODEL), f32),
        compiler_params=pltpu.CompilerParams(
            dimension_semantics=("parallel",), vmem_limit_bytes=VMEM_LIMIT),
        name="merge_out",
    )(x2, ya, proj, proj, ob, wa, wb, wo, fnw)


def _blockdiag_w1(w1_k, w1_v, lo):
    half = CMP_STRIDE * HEAD_DIM
    wk = w1_k[lo * half:(lo + 1) * half].reshape(CMP_STRIDE, HEAD_DIM, CMP_HID)
    wv = w1_v[lo * half:(lo + 1) * half].reshape(CMP_STRIDE, HEAD_DIM, CMP_HID)
    z = jnp.zeros_like(wk)
    top = jnp.concatenate([wk, z], axis=-1)
    bot = jnp.concatenate([z, wv], axis=-1)
    return jnp.concatenate([top, bot], axis=1).reshape(CMP_STRIDE * 2 * HEAD_DIM, 2 * CMP_HID).astype(bf16)


def _pe_row(pe_k, pe_v, lo):
    sl = slice(lo * CMP_STRIDE, (lo + 1) * CMP_STRIDE)
    return jnp.concatenate([pe_k[sl], pe_v[sl]], axis=-1).reshape(1, CMP_STRIDE * 2 * HEAD_DIM)


def kernel(x, norm_w, w_in, conv_w, conv_b, cmp_pe_k, cmp_pe_v, cmp_w1_k, cmp_w2_k, cmp_w1_v, cmp_w2_v,
           w_proj_a, w_proj_b, w_out, final_norm_w):
    batch, seq, _ = x.shape
    assert norm_w.shape[0] == 1 and seq % SEL_CHUNK == 0 and seq // CMP_STRIDE == N_CMP_PAD
    w_p = _reordered_w_in(w_in[0])
    x2 = x.reshape(batch * seq, D_MODEL)
    proj, ya = _inproj(x2, norm_w, w_p, conv_w[0], conv_b, seq)

    wlo = _blockdiag_w1(cmp_w1_k[0], cmp_w1_v[0], 0)
    whi = _blockdiag_w1(cmp_w1_k[0], cmp_w1_v[0], 1)
    pelo = _pe_row(cmp_pe_k[0], cmp_pe_v[0], 0)
    pehi = _pe_row(cmp_pe_k[0], cmp_pe_v[0], 1)
    zk = jnp.zeros_like(cmp_w2_k[0])
    w2bd = jnp.concatenate([jnp.concatenate([cmp_w2_k[0], zk], axis=1),
                            jnp.concatenate([zk, cmp_w2_v[0]], axis=1)], axis=0).astype(bf16)
    ob = _nsa(proj, wlo, whi, pelo, pehi, w2bd, batch, seq)

    out = _out(x2, ya, proj, ob, w_proj_a[0].astype(bf16), w_proj_b[0].astype(bf16),
               w_out[0].astype(bf16), final_norm_w.reshape(1, D_MODEL))
    return out.reshape(batch, seq, D_MODEL)
```
